```python
import math
import jax, jax.numpy as jnp
from jax import lax
import numpy as np

D_MODEL = 1024
BATCH = 8
SEQ = 4096
DEPTH = 2
DEC_BATCH = 128
DEC_SEQ = 1
PAST_LEN = 16384
PAGE_SIZE = 128

HEAD_DIM = 64
BLOCK = 128
POOL_WINDOWS = (2, 4, 8, 16)
N_POOL_GROUPS = 4
POOL_GROUP = D_MODEL // 16
W_POOL = N_POOL_GROUPS * POOL_GROUP
POOL_BUF = 15
DIL_CONFIGS = ((128, 1), (512, 4), (2048, 16))
DIL_MAX = 2048
N_HEADS_DIL = 6
W_DIL = N_HEADS_DIL * HEAD_DIM
W_CONV = D_MODEL // 4
CONV_WIDTH = 3
CONV_BUF = CONV_WIDTH - 1
N_HEADS_SWA = 4
N_KV_SWA = 2
GQA_GROUP = N_HEADS_SWA // N_KV_SWA
W_SWA = N_HEADS_SWA * HEAD_DIM
W_SWA_KV = N_KV_SWA * HEAD_DIM
SWA_WINDOW = 128
D_MIX = W_POOL + W_DIL + W_CONV + W_SWA
PROJ_SIZES = (W_POOL, W_POOL,
              W_DIL, W_DIL, W_DIL, W_DIL,
              W_CONV, W_CONV, W_CONV, W_CONV,
              W_SWA, W_SWA_KV, W_SWA_KV, W_SWA)
D_PROJ = 2 * W_POOL + 4 * W_DIL + 4 * W_CONV + 2 * W_SWA + 2 * W_SWA_KV
RMS_EPS = 1e-6

kernel_name = "hymba_pool_dilated_conv_swa_step"


def _split_proj(z):
    out, off = [], 0
    for n in PROJ_SIZES:
        out.append(z[..., off:off + n])
        off += n
    return out


def _rmsnorm(x, g):
    xf = x.astype(jnp.float32)
    y = xf * lax.rsqrt(jnp.mean(xf * xf, axis=-1, keepdims=True) + RMS_EPS)
    return y.astype(x.dtype) * g


def _alibi_slopes(n):
    return jnp.exp2(-8.0 * jnp.arange(1, n + 1, dtype=jnp.float32) / n)


def _band_attention(q, k, v, slopes, dist_scale, sink=None):
    n, l, hk, g, dh = q.shape
    nb = -(-l // BLOCK)
    pad = nb * BLOCK - l
    qb = jnp.pad(q, ((0, 0), (0, pad), (0, 0), (0, 0), (0, 0))).reshape(n, nb, BLOCK, hk, g, dh)
    kp = jnp.pad(k, ((0, 0), (BLOCK, pad), (0, 0), (0, 0))).reshape(n, nb + 1, BLOCK, hk, dh)
    vp = jnp.pad(v, ((0, 0), (BLOCK, pad), (0, 0), (0, 0))).reshape(n, nb + 1, BLOCK, hk, dh)
    kw = jnp.concatenate([kp[:, :-1], kp[:, 1:]], axis=2)
    vw = jnp.concatenate([vp[:, :-1], vp[:, 1:]], axis=2)
    s = jnp.einsum("nbqhgd,nbkhd->nbhgqk", qb, kw,
                   preferred_element_type=jnp.float32) * (1.0 / math.sqrt(dh))
    qi = jnp.arange(BLOCK)[:, None]
    kj = jnp.arange(2 * BLOCK)[None, :] - BLOCK
    off = qi - kj
    kpos = (jnp.arange(nb) * BLOCK)[:, None, None] + kj[None]
    valid = (off >= 0) & (off <= BLOCK) & (kpos >= 0)
    bias = -(slopes * dist_scale)[:, :, None, None] * off.astype(jnp.float32)
    s = jnp.where(valid[None, :, None, None], s + bias, -jnp.inf)
    m = jnp.max(s, axis=-1)
    if sink is not None:
        m = jnp.maximum(m, sink[:, :, None])
    p = jnp.exp(s - m[..., None])
    den = jnp.sum(p, axis=-1)
    if sink is not None:
        den = den + jnp.exp(sink[:, :, None] - m)
    o = jnp.einsum("nbhgqk,nbkhd->nbqhgd", p.astype(v.dtype), vw,
                   preferred_element_type=jnp.float32)
    o = o / jnp.moveaxis(den, -1, 2)[..., None]
    lse = jnp.moveaxis(m + jnp.log(den), -1, 2)
    o = o.reshape(n, nb * BLOCK, hk, g, dh)[:, :l].astype(q.dtype)
    lse = lse.reshape(n, nb * BLOCK, hk, g)[:, :l]
    return o, lse


def _gather_attention(q, kc, vc, dists, slopes, sink=None):
    n, t, hk, g, dh = q.shape
    r = kc.shape[1]
    idx = (r - t + jnp.arange(t))[:, None] - dists[None, :]
    valid = idx >= 0
    idx = jnp.clip(idx, 0, r - 1)
    kg = kc[:, idx]
    vg = vc[:, idx]
    s = jnp.einsum("nthgd,ntjhd->nthgj", q, kg,
                   preferred_element_type=jnp.float32) * (1.0 / math.sqrt(dh))
    s = s - slopes[:, :, None] * dists.astype(jnp.float32)
    s = jnp.where(valid[None, :, None, None, :], s, -jnp.inf)
    m = jnp.max(s, axis=-1)
    if sink is not None:
        m = jnp.maximum(m, sink)
    p = jnp.exp(s - m[..., None])
    den = jnp.sum(p, axis=-1)
    if sink is not None:
        den = den + jnp.exp(sink - m)
    o = jnp.einsum("nthgj,ntjhd->nthgd", p.astype(vc.dtype), vg,
                   preferred_element_type=jnp.float32) / den[..., None]
    return o.astype(q.dtype), m + jnp.log(den)


def _mix_dilations(outs, lses):
    w = jax.nn.softmax(jnp.stack(lses, axis=0), axis=0)
    return jnp.einsum("cnlh,cnlhd->nlhd", w, jnp.stack(outs, axis=0).astype(jnp.float32))


def _fold(a, d):
    n, l, h, dh = a.shape
    return a.reshape(n, l // d, d, h, dh).transpose(0, 2, 1, 3, 4).reshape(n * d, l // d, h, dh)


def _dilated_prompt(q, k, v, slopes):
    n, l, h, dh = q.shape
    outs, lses = [], []
    for window, d in DIL_CONFIGS:
        o, lse = _band_attention(_fold(q, d)[:, :, :, None], _fold(k, d), _fold(v, d),
                                 slopes[:, None], d)
        outs.append(o[:, :, :, 0].reshape(n, d, l // d, h, dh).transpose(0, 2, 1, 3, 4)
                    .reshape(n, l, h, dh))
        lses.append(lse[..., 0].reshape(n, d, l // d, h).transpose(0, 2, 1, 3).reshape(n, l, h))
    return _mix_dilations(outs, lses)


def _dilated_decode(q, kc, vc, slopes):
    outs, lses = [], []
    for window, d in DIL_CONFIGS:
        dists = d * jnp.arange(window // d + 1)
        o, lse = _gather_attention(q[:, :, :, None], kc, vc, dists, slopes[:, None])
        outs.append(o[:, :, :, 0])
        lses.append(lse[..., 0])
    return _mix_dilations(outs, lses)


def _pool_mix(u_ext, t, w_pool, pool_scale):
    n, r, _ = u_ext.shape
    uf = u_ext.astype(jnp.float32).reshape(n, r, N_POOL_GROUPS, POOL_GROUP)
    cs = jnp.concatenate([jnp.zeros_like(uf[:, :1]), jnp.cumsum(uf, axis=1)], axis=1)
    i = jnp.arange(r - t, r)
    means = []
    for gi, w in enumerate(POOL_WINDOWS):
        csg = cs[:, :, gi]
        s = csg[:, i + 1] - csg[:, jnp.maximum(i + 1 - w, 0)]
        means.append(s / jnp.minimum(i + 1, w).astype(jnp.float32)[None, :, None])
    diff = (jnp.stack(means, axis=2) - uf[:, r - t:]).astype(u_ext.dtype)
    y = jnp.einsum("ntgc,gce->ntge", diff, w_pool).reshape(n, t, W_POOL)
    return y * pool_scale


def _short_conv(z_ext, t, conv_w):
    acc = conv_w[0] * z_ext[:, 0:t]
    for j in range(1, CONV_WIDTH):
        acc = acc + conv_w[j] * z_ext[:, j:j + t]
    return acc


def _pre(x, c, norm_g, w_ada, b_ada, w_in):
    mod = jnp.einsum("nd,de->ne", jax.nn.silu(c), w_ada) + b_ada
    shift, scale, gate = jnp.split(mod[:, None, :], 3, axis=-1)
    h = _rmsnorm(x, norm_g) * (1.0 + scale) + shift
    z = jnp.einsum("nld,de->nle", h, w_in)
    return _split_proj(z), gate


def _post(x, gate, branches, gate_paths, w_out):
    y = jnp.concatenate([b.astype(x.dtype) * jax.nn.silu(gp)
                         for b, gp in zip(branches, gate_paths)], axis=-1)
    return x + gate * jnp.einsum("nle,ed->nld", y, w_out)


def setup_inputs(seed: int = 0) -> dict:
    key = jax.random.key(seed)
    ks = jax.random.split(key, 24)
    f32 = jnp.float32
    wb = min(DIL_MAX, PAST_LEN)
    wd = min(SWA_WINDOW, PAST_LEN)
    nrm = lambda k, shape, s=1.0: jax.random.normal(k, shape, f32) * s
    return {
        "x_prompt": nrm(ks[0], (BATCH, SEQ, D_MODEL)),
        "x_sample": nrm(ks[1], (DEC_BATCH, DEC_SEQ, D_MODEL)),
        "c_prompt": nrm(ks[2], (BATCH, D_MODEL)),
        "c_sample": nrm(ks[3], (DEC_BATCH, D_MODEL)),
        "state_pool": nrm(ks[4], (DEPTH, DEC_BATCH, POOL_BUF, W_POOL)),
        "cache_dil_k": nrm(ks[5], (DEPTH, DEC_BATCH, wb, N_HEADS_DIL, HEAD_DIM)),
        "cache_dil_v": nrm(ks[6], (DEPTH, DEC_BATCH, wb, N_HEADS_DIL, HEAD_DIM)),
        "state_conv": nrm(ks[7], (DEPTH, DEC_BATCH, CONV_BUF, W_CONV)),
        "cache_swa_k": nrm(ks[8], (DEPTH, DEC_BATCH, wd, N_KV_SWA, HEAD_DIM)),
        "cache_swa_v": nrm(ks[9], (DEPTH, DEC_BATCH, wd, N_KV_SWA, HEAD_DIM)),
        "norm_g": 1.0 + nrm(ks[10], (DEPTH, D_MODEL), 0.02),
        "w_ada": nrm(ks[11], (DEPTH, D_MODEL, 3 * D_MODEL), 0.5 * D_MODEL ** -0.5),
        "b_ada": nrm(ks[12], (DEPTH, 3 * D_MODEL), 0.02),
        "w_in": nrm(ks[13], (DEPTH, D_MODEL, D_PROJ), D_MODEL ** -0.5),
        "w_pool": nrm(ks[14], (DEPTH, N_POOL_GROUPS, POOL_GROUP, POOL_GROUP), POOL_GROUP ** -0.5),
        "pool_scale": 1.0 + nrm(ks[15], (DEPTH, W_POOL), 0.1),
        "conv_w": nrm(ks[16], (DEPTH, CONV_WIDTH, W_CONV), CONV_WIDTH ** -0.5),
        "swa_sink": nrm(ks[17], (DEPTH, N_HEADS_SWA), 0.5),
        "w_out": nrm(ks[18], (DEPTH, D_MIX, D_MODEL), D_MIX ** -0.5),
        "final_g": 1.0 + nrm(ks[19], (D_MODEL,), 0.02),
    }


def reference(x_prompt, x_sample, c_prompt, c_sample, state_pool, cache_dil_k, cache_dil_v,
              state_conv, cache_swa_k, cache_swa_v, norm_g, w_ada, b_ada, w_in, w_pool,
              pool_scale, conv_w, swa_sink, w_out, final_g):
    dil_slopes = _alibi_slopes(N_HEADS_DIL)
    swa_slopes = _alibi_slopes(N_HEADS_SWA).reshape(N_KV_SWA, GQA_GROUP)
    swa_dists = jnp.arange(SWA_WINDOW + 1)
    xp, xs = x_prompt, x_sample
    pool_p, pool_s, dkp, dvp, dks, dvs = [], [], [], [], [], []
    conv_p, conv_s, skp, svp, sks, svs = [], [], [], [], [], []
    for i in range(DEPTH):
        sink = swa_sink[i].astype(jnp.float32).reshape(N_KV_SWA, GQA_GROUP)

        (pu, pg, dq, dk, dv, dg, ch, cb, cc, cg, sq, sk, sv, sg), gate = _pre(
            xp, c_prompt, norm_g[i], w_ada[i], b_ada[i], w_in[i])
        n, l = xp.shape[:2]
        a_out = _pool_mix(pu, l, w_pool[i], pool_scale[i])
        dk4 = dk.reshape(n, l, N_HEADS_DIL, HEAD_DIM)
        dv4 = dv.reshape(n, l, N_HEADS_DIL, HEAD_DIM)
        b_out = _dilated_prompt(dq.reshape(n, l, N_HEADS_DIL, HEAD_DIM), dk4, dv4,
                                dil_slopes).reshape(n, l, W_DIL)
        zc = cc * ch
        c_out = cb * _short_conv(jnp.pad(zc, ((0, 0), (CONV_BUF, 0), (0, 0))), l, conv_w[i])
        sk4 = sk.reshape(n, l, N_KV_SWA, HEAD_DIM)
        sv4 = sv.reshape(n, l, N_KV_SWA, HEAD_DIM)
        so, _ = _band_attention(sq.reshape(n, l, N_KV_SWA, GQA_GROUP, HEAD_DIM), sk4, sv4,
                                swa_slopes, 1, sink)
        d_out = so.reshape(n, l, W_SWA)
        xp = _post(xp, gate, (a_out, b_out, c_out, d_out), (pg, dg, cg, sg), w_out[i])
        pool_p.append(pu[:, -POOL_BUF:])
        dkp.append(dk4[:, -min(DIL_MAX, l):])
        dvp.append(dv4[:, -min(DIL_MAX, l):])
        conv_p.append(zc[:, -CONV_BUF:])
        skp.append(sk4[:, -min(SWA_WINDOW, l):])
        svp.append(sv4[:, -min(SWA_WINDOW, l):])

        (pu, pg, dq, dk, dv, dg, ch, cb, cc, cg, sq, sk, sv, sg), gate = _pre(
            xs, c_sample, norm_g[i], w_ada[i], b_ada[i], w_in[i])
        n, t = xs.shape[:2]
        u_ext = jnp.concatenate([state_pool[i], pu], axis=1)
        a_out = _pool_mix(u_ext, t, w_pool[i], pool_scale[i])
        dk4 = dk.reshape(n, t, N_HEADS_DIL, HEAD_DIM)
        dv4 = dv.reshape(n, t, N_HEADS_DIL, HEAD_DIM)
        b_out = _dilated_decode(dq.reshape(n, t, N_HEADS_DIL, HEAD_DIM),
                                jnp.concatenate([cache_dil_k[i], dk4], axis=1),
                                jnp.concatenate([cache_dil_v[i], dv4], axis=1),
                                dil_slopes).reshape(n, t, W_DIL)
        z_ext = jnp.concatenate([state_conv[i], cc * ch], axis=1)
        c_out = cb * _short_conv(z_ext, t, conv_w[i])
        sk4 = sk.reshape(n, t, N_KV_SWA, HEAD_DIM)
        sv4 = sv.reshape(n, t, N_KV_SWA, HEAD_DIM)
        so, _ = _gather_attention(sq.reshape(n, t, N_KV_SWA, GQA_GROUP, HEAD_DIM),
                                  jnp.concatenate([cache_swa_k[i], sk4], axis=1),
                                  jnp.concatenate([cache_swa_v[i], sv4], axis=1),
                                  swa_dists, swa_slopes, sink)
        d_out = so.reshape(n, t, W_SWA)
        xs = _post(xs, gate, (a_out, b_out, c_out, d_out), (pg, dg, cg, sg), w_out[i])
        pool_s.append(u_ext[:, -POOL_BUF:])
        dks.append(dk4)
        dvs.append(dv4)
        conv_s.append(z_ext[:, -CONV_BUF:])
        sks.append(sk4)
        svs.append(sv4)

    y_prompt = _rmsnorm(xp, final_g)
    y_sample = _rmsnorm(xs, final_g)
    return (y_prompt, y_sample,
            jnp.stack(pool_p), jnp.stack(pool_s),
            jnp.stack(dkp), jnp.stack(dvp), jnp.stack(dks), jnp.stack(dvs),
            jnp.stack(conv_p), jnp.stack(conv_s),
            jnp.stack(skp), jnp.stack(svp), jnp.stack(sks), jnp.stack(svs))
```

```python
import functools
import math

import numpy as np
import jax
import jax.numpy as jnp
from jax import lax
from jax.experimental import pallas as pl
from jax.experimental.pallas import tpu as pltpu

F32 = jnp.float32
BF16 = jnp.bfloat16

D_MODEL = 1024
HEAD_DIM = 64
BLOCK = 128
POOL_WINDOWS = (2, 4, 8, 16)
POOL_GROUP = 64
W_POOL = 256
POOL_BUF = 15
DIL_CONFIGS = ((128, 1), (512, 4), (2048, 16))
DIL_MAX = 2048
N_HEADS_DIL = 6
N_PAIRS_DIL = 3
W_DIL = 384
W_CONV = 256
CONV_BUF = 2
N_HEADS_SWA = 4
W_SWA = 256
W_SWA_KV = 128
SWA_WINDOW = 128
D_MIX = 1152
D_PROJ = 3840
RMS_EPS = 1e-6
QK_SCALE = 1.0 / math.sqrt(HEAD_DIM)

OFF_POOL = 0
OFF_DIL = 512
OFF_CONV = 2048
OFF_SWA = 3072
SWA_HEAD_PERM = (0, 3, 1, 2)

SEQ_TILE = 512
CHUNK = 2048
DEC_TILE = 2
VMEM_LIMIT = 56 * 1024 * 1024

NEG_INF = float("-inf")


def _silu(v):
    return v * jax.nn.sigmoid(v)


def _dot(a, b):
    return jnp.dot(a, b, preferred_element_type=F32)


def _dot_nt(a, b):
    return lax.dot_general(a, b, (((1,), (1,)), ((), ())), preferred_element_type=F32)


def _alibi_slopes(n):
    return [2.0 ** (-8.0 * (h + 1) / n) for h in range(n)]


def _band_bias(slope_lo, slope_hi, dist_scale, variant):
    qi = np.arange(BLOCK)[:, None]
    kj = np.arange(2 * BLOCK)[None, :]
    if variant == 1:
        off = qi - kj
    else:
        off = qi - kj + BLOCK
    valid = (off >= 0) & (off <= BLOCK)
    if variant == 2:
        valid = valid & (kj >= BLOCK)
    out = []
    for s in (slope_lo, slope_hi):
        b = np.where(valid, -(np.float32(s) * np.float32(dist_scale)) * off.astype(np.float32),
                     np.float32(NEG_INF))
        out.append(b.astype(np.float32))
    return np.concatenate(out, axis=0)


def _compiler_params(sem):
    return pltpu.CompilerParams(dimension_semantics=sem, vmem_limit_bytes=VMEM_LIMIT)


def _ada_kernel(c_ref, w_ref, b_ref, o_ref):
    s = _silu(c_ref[...]).astype(BF16)
    o_ref[0] = _dot(s, w_ref[0]) + b_ref[0]


def _ada(c_all, w_ada_b, b_ada):
    depth = w_ada_b.shape[0]
    rows = c_all.shape[0]
    return pl.pallas_call(
        _ada_kernel,
        grid=(depth, 3),
        in_specs=[
            pl.BlockSpec((rows, D_MODEL), lambda i, j: (0, 0)),
            pl.BlockSpec((1, D_MODEL, D_MODEL), lambda i, j: (i, 0, j)),
            pl.BlockSpec((1, 1, D_MODEL), lambda i, j: (i, 0, j)),
        ],
        out_specs=pl.BlockSpec((1, rows, D_MODEL), lambda i, j: (i, 0, j)),
        out_shape=jax.ShapeDtypeStruct((depth, rows, 3 * D_MODEL), F32),
        compiler_params=_compiler_params(("arbitrary", "arbitrary")),
        name="ada",
    )(c_all, w_ada_b, b_ada.reshape(depth, 1, 3 * D_MODEL))


def _modulated_norm(x, g, mod_row):
    ms = jnp.mean(x * x, axis=-1, keepdims=True)
    y = x * lax.rsqrt(ms + RMS_EPS) * g
    shift = mod_row[:, 0:D_MODEL]
    scale = mod_row[:, D_MODEL:2 * D_MODEL]
    return y * (1.0 + scale) + shift


def _pool_select(sums, lane):
    grp = lane // POOL_GROUP
    sel = jnp.where(grp == 0, sums[2], jnp.where(grp == 1, sums[4],
                                                 jnp.where(grp == 2, sums[8], sums[16])))
    win = jnp.where(grp == 0, 2, jnp.where(grp == 1, 4, jnp.where(grp == 2, 8, 16)))
    return sel, win


def _pa_kernel(sink_ref, x_ref, mod_ref, g_ref, w_ref, wbd_ref, ps_ref, cw_ref, bias_ref,
               qd_ref, kd_ref, vd_ref, gd_ref, kc_ref, vc_ref, y_ref,
               pst_ref, cst_ref, skc_ref, svc_ref,
               uext, zcext, kext, vext):
    tl = SEQ_TILE
    t = pl.program_id(1)

    @pl.when(t == 0)
    def _():
        uext[0:16, :] = jnp.zeros((16, W_POOL), F32)
        zcext[0:8, :] = jnp.zeros((8, W_CONV), F32)
        kext[0:BLOCK, :] = jnp.zeros((BLOCK, W_SWA_KV), BF16)
        vext[0:BLOCK, :] = jnp.zeros((BLOCK, W_SWA_KV), BF16)

    hb = _modulated_norm(x_ref[0], g_ref[...], mod_ref[0]).astype(BF16)

    zp = _dot(hb, w_ref[:, OFF_POOL:OFF_POOL + 2 * W_POOL])
    pu = zp[:, 0:W_POOL]
    pg = zp[:, W_POOL:2 * W_POOL]
    uext[16:16 + tl, :] = pu
    acc = pu
    sums = {}
    for j in range(1, 16):
        acc = acc + uext[16 - j:16 - j + tl, :]
        if j + 1 in POOL_WINDOWS:
            sums[j + 1] = acc
    lane = lax.broadcasted_iota(jnp.int32, (tl, W_POOL), 1)
    gpos = lax.broadcasted_iota(jnp.int32, (tl, W_POOL), 0) + t * tl
    sel, win = _pool_select(sums, lane)
    cnt = jnp.minimum(gpos + 1, win).astype(F32)
    diff = sel / cnt - pu
    a_out = _dot(diff.astype(BF16), wbd_ref[...]) * ps_ref[...]
    y_ref[0, :, 0:W_POOL] = (a_out * _silu(pg)).astype(BF16)
    pst_ref[0] = uext[tl + 1:tl + 16, :]
    uext[0:16, :] = uext[tl:tl + 16, :]

    zc4 = _dot(hb, w_ref[:, OFF_CONV:OFF_CONV + 4 * W_CONV])
    ch = zc4[:, 0:W_CONV]
    cb = zc4[:, W_CONV:2 * W_CONV]
    cc = zc4[:, 2 * W_CONV:3 * W_CONV]
    cg = zc4[:, 3 * W_CONV:4 * W_CONV]
    zc = cc * ch
    zcext[8:8 + tl, :] = zc
    conv = (cw_ref[0:1, :] * zcext[6:6 + tl, :] + cw_ref[1:2, :] * zcext[7:7 + tl, :]
            + cw_ref[2:3, :] * zc)
    y_ref[0, :, W_POOL:W_POOL + W_CONV] = (cb * conv * _silu(cg)).astype(BF16)
    cst_ref[0] = zcext[tl + 6:tl + 8, :]
    zcext[0:8, :] = zcext[tl:tl + 8, :]

    zd = _dot(hb, w_ref[:, OFF_DIL:OFF_DIL + 4 * W_DIL])
    for hp in range(N_PAIRS_DIL):
        lo = hp * BLOCK
        qd_ref[0, hp] = (zd[:, lo:lo + BLOCK] * QK_SCALE).astype(BF16)
        kd_ref[0, hp] = zd[:, W_DIL + lo:W_DIL + lo + BLOCK].astype(BF16)
        vd_ref[0, hp] = zd[:, 2 * W_DIL + lo:2 * W_DIL + lo + BLOCK].astype(BF16)
        gd_ref[0, hp] = _silu(zd[:, 3 * W_DIL + lo:3 * W_DIL + lo + BLOCK]).astype(BF16)
    kc_ref[0] = zd[:, W_DIL:2 * W_DIL].T
    vc_ref[0] = zd[:, 2 * W_DIL:3 * W_DIL].T

    zs = _dot(hb, w_ref[:, OFF_SWA:OFF_SWA + 3 * W_SWA])
    sk = zs[:, W_SWA:W_SWA + W_SWA_KV]
    sv = zs[:, W_SWA + W_SWA_KV:W_SWA + 2 * W_SWA_KV]
    kext[BLOCK:BLOCK + tl, :] = sk.astype(BF16)
    vext[BLOCK:BLOCK + tl, :] = sv.astype(BF16)
    skc_ref[0] = sk[tl - SWA_WINDOW:tl, :].T
    svc_ref[0] = sv[tl - SWA_WINDOW:tl, :].T
    first = jnp.where(t == 0, 1, 0)
    lane_q = lax.broadcasted_iota(jnp.int32, (BLOCK, BLOCK), 1)
    row_s = lax.broadcasted_iota(jnp.int32, (2 * BLOCK, 1), 0)
    for jb in range(tl // BLOCK):
        r0 = jb * BLOCK
        kblk = kext[r0:r0 + 2 * BLOCK, :]
        vblk = vext[r0:r0 + 2 * BLOCK, :]
        for grp in range(2):
            q = (zs[r0:r0 + BLOCK, grp * BLOCK:(grp + 1) * BLOCK] * QK_SCALE).astype(BF16)
            lhs = jnp.concatenate([jnp.where(lane_q < HEAD_DIM, q, jnp.zeros_like(q)),
                                   jnp.where(lane_q >= HEAD_DIM, q, jnp.zeros_like(q))], axis=0)
            s = _dot_nt(lhs, kblk)
            if jb == 0:
                s = s + bias_ref[grp, first]
            else:
                s = s + bias_ref[grp, 0]
            sink = jnp.where(row_s < BLOCK, sink_ref[SWA_HEAD_PERM[2 * grp]],
                             sink_ref[SWA_HEAD_PERM[2 * grp + 1]])
            m = jnp.maximum(jnp.max(s, axis=-1, keepdims=True), sink)
            p = jnp.exp(s - m)
            den = jnp.sum(p, axis=-1, keepdims=True) + jnp.exp(sink - m)
            o = _dot(p.astype(BF16), vblk) / den
            od = jnp.where(lane_q < HEAD_DIM, o[0:BLOCK], o[BLOCK:2 * BLOCK])
            sg = zs[r0:r0 + BLOCK, 2 * W_SWA + grp * BLOCK:2 * W_SWA + (grp + 1) * BLOCK]
            c0 = W_POOL + W_CONV + grp * BLOCK
            y_ref[0, r0:r0 + BLOCK, c0:c0 + BLOCK] = (od * _silu(sg)).astype(BF16)
    kext[0:BLOCK, :] = kext[tl:tl + BLOCK, :]
    vext[0:BLOCK, :] = vext[tl:tl + BLOCK, :]


def _prompt_a(x, mod_p, norm_g, w_in_b, wbd, pool_scale, conv_w, sink, swa_bias):
    nb, l, _ = x.shape
    tl = SEQ_TILE
    nt = l // tl
    cache_t0 = (l - DIL_MAX) // tl
    pair_spec = pl.BlockSpec((1, N_PAIRS_DIL, tl, BLOCK), lambda n, t: (n, 0, t, 0))
    pair_shape = jax.ShapeDtypeStruct((nb, N_PAIRS_DIL, l, BLOCK), BF16)
    cache_spec = pl.BlockSpec((1, W_DIL, tl), lambda n, t: (n, 0, jnp.maximum(t - cache_t0, 0)))
    cache_shape = jax.ShapeDtypeStruct((nb, W_DIL, DIL_MAX), F32)
    full = lambda *shape: pl.BlockSpec(shape, lambda n, t: (0,) * len(shape))
    return pl.pallas_call(
        _pa_kernel,
        grid=(nb, nt),
        in_specs=[
            pl.BlockSpec(memory_space=pltpu.SMEM),
            pl.BlockSpec((1, tl, D_MODEL), lambda n, t: (n, t, 0)),
            pl.BlockSpec((1, 1, 3 * D_MODEL), lambda n, t: (n, 0, 0)),
            full(1, D_MODEL),
            full(D_MODEL, D_PROJ),
            full(W_POOL, W_POOL),
            full(1, W_POOL),
            full(3, W_CONV),
            full(2, 2, 2 * BLOCK, 2 * BLOCK),
        ],
        out_specs=[
            pair_spec, pair_spec, pair_spec, pair_spec, cache_spec, cache_spec,
            pl.BlockSpec((1, tl, 3 * W_POOL), lambda n, t: (n, t, 0)),
            pl.BlockSpec((1, POOL_BUF, W_POOL), lambda n, t: (n, 0, 0)),
            pl.BlockSpec((1, CONV_BUF, W_CONV), lambda n, t: (n, 0, 0)),
            pl.BlockSpec((1, SWA_WINDOW, W_SWA_KV), lambda n, t: (n, 0, 0)),
            pl.BlockSpec((1, SWA_WINDOW, W_SWA_KV), lambda n, t: (n, 0, 0)),
        ],
        out_shape=[
            pair_shape, pair_shape, pair_shape, pair_shape, cache_shape, cache_shape,
            jax.ShapeDtypeStruct((nb, l, 3 * W_POOL), BF16),
            jax.ShapeDtypeStruct((nb, POOL_BUF, W_POOL), F32),
            jax.ShapeDtypeStruct((nb, CONV_BUF, W_CONV), F32),
            jax.ShapeDtypeStruct((nb, SWA_WINDOW, W_SWA_KV), F32),
            jax.ShapeDtypeStruct((nb, SWA_WINDOW, W_SWA_KV), F32),
        ],
        scratch_shapes=[
            pltpu.VMEM((16 + tl, W_POOL), F32),
            pltpu.VMEM((8 + tl, W_CONV), F32),
            pltpu.VMEM((BLOCK + tl, W_SWA_KV), BF16),
            pltpu.VMEM((BLOCK + tl, W_SWA_KV), BF16),
        ],
        compiler_params=_compiler_params(("arbitrary", "arbitrary")),
        name="prompt_a",
    )(sink, x, mod_p, norm_g, w_in_b, wbd, pool_scale, conv_w, swa_bias)


def _band_pair(q, kblk, vblk, bias, lane_q):
    lhs = jnp.concatenate([jnp.where(lane_q < HEAD_DIM, q, jnp.zeros_like(q)),
                           jnp.where(lane_q >= HEAD_DIM, q, jnp.zeros_like(q))], axis=0)
    s = _dot_nt(lhs, kblk) + bias
    m = jnp.max(s, axis=-1, keepdims=True)
    p = jnp.exp(s - m)
    den = jnp.sum(p, axis=-1, keepdims=True)
    o = _dot(p.astype(BF16), vblk) / den
    lse = m + jnp.log(den)
    od = jnp.where(lane_q < HEAD_DIM, o[0:BLOCK], o[BLOCK:2 * BLOCK])
    ld = jnp.where(lane_q < HEAD_DIM, lse[0:BLOCK], lse[BLOCK:2 * BLOCK])
    return od, ld


def _pb_kernel(bias_ref, q1, q4, q16, k1, k4, k16, v1, v4, v16, g16, out16,
               o1s, l1s, o4s, l4s, o16s, l16s):
    c = pl.program_id(2)
    first = jnp.where(c == 0, 1, 0)
    lane_q = lax.broadcasted_iota(jnp.int32, (BLOCK, BLOCK), 1)
    views = ((1, q1, k1, v1, o1s, l1s), (4, q4, k4, v4, o4s, l4s), (16, q16, k16, v16, o16s, l16s))
    for cfg, (d, qr, kr, vr, osr, lsr) in enumerate(views):
        nblk = CHUNK // (d * BLOCK)
        for r in range(d):
            c0 = r * BLOCK
            for j in range(nblk):
                r0 = j * BLOCK
                if j == 0:
                    start = jnp.maximum(c * nblk - 1, 0) * BLOCK
                    bias = bias_ref[0, cfg, first]
                else:
                    start = (c * nblk + (j - 1)) * BLOCK
                    bias = bias_ref[0, cfg, 0]
                start = pl.multiple_of(start, BLOCK)
                q = qr[0, 0, r0:r0 + BLOCK, c0:c0 + BLOCK]
                kblk = kr[0, 0, pl.ds(start, 2 * BLOCK), c0:c0 + BLOCK]
                vblk = vr[0, 0, pl.ds(start, 2 * BLOCK), c0:c0 + BLOCK]
                od, ld = _band_pair(q, kblk, vblk, bias, lane_q)
                osr[r, r0:r0 + BLOCK, :] = od
                lsr[r, r0:r0 + BLOCK, :] = ld

    for r in range(16):
        c0 = r * BLOCK
        oa = o1s[0, pl.ds(r, BLOCK, stride=16), :]
        la = l1s[0, pl.ds(r, BLOCK, stride=16), :]
        ob = o4s[r % 4, pl.ds(r // 4, BLOCK, stride=4), :]
        lb = l4s[r % 4, pl.ds(r // 4, BLOCK, stride=4), :]
        oc = o16s[r]
        lc = l16s[r]
        mx = jnp.maximum(jnp.maximum(la, lb), lc)
        wa = jnp.exp(la - mx)
        wb = jnp.exp(lb - mx)
        wc = jnp.exp(lc - mx)
        mix = (wa * oa + wb * ob + wc * oc) / (wa + wb + wc)
        out16[0, 0, :, c0:c0 + BLOCK] = (mix * g16[0, 0, :, c0:c0 + BLOCK].astype(F32)).astype(BF16)


def _prompt_b(qd, kd, vd, gd, dil_bias):
    nb, npair, l, _ = qd.shape
    nc = l // CHUNK
    views = []
    for arr in (qd, kd, vd):
        views.append([arr.reshape(nb, npair, l // d, d * BLOCK) for d in (1, 4, 16)])
    q_specs = [pl.BlockSpec((1, 1, CHUNK // d, d * BLOCK), lambda n, h, c: (n, h, c, 0))
               for d in (1, 4, 16)]
    kv_specs = [pl.BlockSpec((1, 1, l // d, d * BLOCK), lambda n, h, c: (n, h, 0, 0))
                for d in (1, 4, 16)]
    f16_spec = pl.BlockSpec((1, 1, CHUNK // 16, 16 * BLOCK), lambda n, h, c: (n, h, c, 0))
    out = pl.pallas_call(
        _pb_kernel,
        grid=(nb, npair, nc),
        in_specs=[pl.BlockSpec((1, 3, 2, 2 * BLOCK, 2 * BLOCK), lambda n, h, c: (h, 0, 0, 0, 0))]
        + q_specs + kv_specs + kv_specs + [f16_spec],
        out_specs=f16_spec,
        out_shape=jax.ShapeDtypeStruct((nb, npair, l // 16, 16 * BLOCK), BF16),
        scratch_shapes=[
            pltpu.VMEM((1, CHUNK, BLOCK), F32), pltpu.VMEM((1, CHUNK, BLOCK), F32),
            pltpu.VMEM((4, CHUNK // 4, BLOCK), F32), pltpu.VMEM((4, CHUNK // 4, BLOCK), F32),
            pltpu.VMEM((16, CHUNK // 16, BLOCK), F32), pltpu.VMEM((16, CHUNK // 16, BLOCK), F32),
        ],
        compiler_params=_compiler_params(("arbitrary", "arbitrary", "arbitrary")),
        name="prompt_b",
    )(dil_bias, *views[0], *views[1], *views[2], gd.reshape(nb, npair, l // 16, 16 * BLOCK))
    return out.reshape(nb, npair, l, BLOCK)


def _mix_out(y, x, gate, w_ref, fg_ref, final):
    xn = x + gate * _dot(y, w_ref[...])
    if final:
        ms = jnp.mean(xn * xn, axis=-1, keepdims=True)
        xn = xn * lax.rsqrt(ms + RMS_EPS) * fg_ref[...]
    return xn


def _pc_kernel(x_ref, mod_ref, yacd_ref, yb_ref, w_ref, fg_ref, o_ref, *, final):
    yacd = yacd_ref[0]
    y = jnp.concatenate([yacd[:, 0:W_POOL], yb_ref[0, 0], yb_ref[0, 1], yb_ref[0, 2],
                         yacd[:, W_POOL:3 * W_POOL]], axis=-1)
    gate = mod_ref[0][:, 2 * D_MODEL:3 * D_MODEL]
    o_ref[0] = _mix_out(y, x_ref[0], gate, w_ref, fg_ref, final)


def _prompt_c(x, mod_p, yacd, yb, w_out_b, final_g, final):
    nb, l, _ = x.shape
    tl = SEQ_TILE
    full = lambda *shape: pl.BlockSpec(shape, lambda n, t: (0,) * len(shape))
    return pl.pallas_call(
        functools.partial(_pc_kernel, final=final),
        grid=(nb, l // tl),
        in_specs=[
            pl.BlockSpec((1, tl, D_MODEL), lambda n, t: (n, t, 0)),
            pl.BlockSpec((1, 1, 3 * D_MODEL), lambda n, t: (n, 0, 0)),
            pl.BlockSpec((1, tl, 3 * W_POOL), lambda n, t: (n, t, 0)),
            pl.BlockSpec((1, N_PAIRS_DIL, tl, BLOCK), lambda n, t: (n, 0, t, 0)),
            full(D_MIX, D_MODEL),
            full(1, D_MODEL),
        ],
        out_specs=pl.BlockSpec((1, tl, D_MODEL), lambda n, t: (n, t, 0)),
        out_shape=jax.ShapeDtypeStruct((nb, l, D_MODEL), F32),
        compiler_params=_compiler_params(("arbitrary", "arbitrary")),
        name="prompt_c",
    )(x, mod_p, yacd, yb, w_out_b, final_g)


def _sa_kernel(x_ref, mod_ref, g_ref, w_ref, wbd_ref, ps_ref, cw_ref, sp_ref, sc_ref,
               qkv_ref, sw_ref, yac_ref, gates_ref, pst_ref, cst_ref):
    hb = _modulated_norm(x_ref[...], g_ref[...], mod_ref[...]).astype(BF16)
    ns = hb.shape[0]

    zp = _dot(hb, w_ref[:, OFF_POOL:OFF_POOL + 2 * W_POOL])
    pu = zp[:, 0:W_POOL]
    pg = zp[:, W_POOL:2 * W_POOL]
    acc = pu
    sums = {}
    for j in range(1, 16):
        acc = acc + sp_ref[POOL_BUF - j]
        if j + 1 in POOL_WINDOWS:
            sums[j + 1] = acc
    lane = lax.broadcasted_iota(jnp.int32, (ns, W_POOL), 1)
    sel, win = _pool_select(sums, lane)
    diff = sel / win.astype(F32) - pu
    a_out = _dot(diff.astype(BF16), wbd_ref[...]) * ps_ref[...]
    yac_ref[:, 0:W_POOL] = a_out * _silu(pg)
    pst_ref[0:POOL_BUF - 1] = sp_ref[1:POOL_BUF]
    pst_ref[POOL_BUF - 1] = pu

    zc4 = _dot(hb, w_ref[:, OFF_CONV:OFF_CONV + 4 * W_CONV])
    ch = zc4[:, 0:W_CONV]
    cb = zc4[:, W_CONV:2 * W_CONV]
    cc = zc4[:, 2 * W_CONV:3 * W_CONV]
    cg = zc4[:, 3 * W_CONV:4 * W_CONV]
    zc = cc * ch
    conv = (cw_ref[0:1, :] * sc_ref[:, 0:W_CONV] + cw_ref[1:2, :] * sc_ref[:, W_CONV:2 * W_CONV]
            + cw_ref[2:3, :] * zc)
    yac_ref[:, W_POOL:W_POOL + W_CONV] = cb * conv * _silu(cg)
    cst_ref[:, 0:W_CONV] = sc_ref[:, W_CONV:2 * W_CONV]
    cst_ref[:, W_CONV:2 * W_CONV] = zc

    zd = _dot(hb, w_ref[:, OFF_DIL:OFF_DIL + 4 * W_DIL])
    qkv_ref[:, 0:W_DIL] = zd[:, 0:W_DIL] * QK_SCALE
    qkv_ref[:, W_DIL:3 * W_DIL] = zd[:, W_DIL:3 * W_DIL]
    gates_ref[:, 0:W_DIL] = _silu(zd[:, 3 * W_DIL:4 * W_DIL])

    zs = _dot(hb, w_ref[:, OFF_SWA:OFF_SWA + 3 * W_SWA])
    sw_ref[:, 0:W_SWA] = zs[:, 0:W_SWA] * QK_SCALE
    sw_ref[:, W_SWA:2 * W_SWA] = zs[:, W_SWA:2 * W_SWA]
    gates_ref[:, W_DIL:W_DIL + W_SWA] = _silu(zs[:, 2 * W_SWA:3 * W_SWA])


def _sample_a(xs, mod_s, norm_g, w_in_b, wbd, pool_scale, conv_w, sp, sc):
    ns = xs.shape[0]
    shapes = [
        jax.ShapeDtypeStruct((ns, 3 * W_DIL), F32),
        jax.ShapeDtypeStruct((ns, 2 * W_SWA), F32),
        jax.ShapeDtypeStruct((ns, W_POOL + W_CONV), F32),
        jax.ShapeDtypeStruct((ns, W_DIL + W_SWA), F32),
        jax.ShapeDtypeStruct((POOL_BUF, ns, W_POOL), F32),
        jax.ShapeDtypeStruct((ns, CONV_BUF * W_CONV), F32),
    ]
    return pl.pallas_call(
        _sa_kernel,
        out_shape=shapes,
        compiler_params=pltpu.CompilerParams(vmem_limit_bytes=VMEM_LIMIT),
        name="sample_a",
    )(xs, mod_s, norm_g, w_in_b, wbd, pool_scale, conv_w, sp, sc)


def _sb_kernel(dq_ref, sq_ref, dbias_ref, mult_ref, sbias_ref, sink_ref, kt_ref, vt_ref,
               ck_ref, cv_ref, bo_ref, do_ref):
    row_d = lax.broadcasted_iota(jnp.int32, (8, DIL_MAX), 0)
    row_o = lax.broadcasted_iota(jnp.int32, (8, HEAD_DIM), 0)
    row_s = lax.broadcasted_iota(jnp.int32, (8, SWA_WINDOW), 0)
    for j in range(DEC_TILE):
        q = dq_ref[j, 0]
        qb = q.astype(BF16)
        s = jnp.zeros((8, DIL_MAX), F32)
        for h in range(N_HEADS_DIL):
            s = jnp.where(row_d == h, _dot(qb, kt_ref[0, j, h].astype(BF16)), s)
        s = s + dbias_ref[...]
        s_self = jnp.sum(q * dq_ref[j, 1], axis=-1, keepdims=True)
        m = jnp.maximum(jnp.max(s, axis=-1, keepdims=True), s_self)
        p = jnp.exp(s - m) * mult_ref[...]
        p_self = float(len(DIL_CONFIGS)) * jnp.exp(s_self - m)
        den = jnp.sum(p, axis=-1, keepdims=True) + p_self
        pb = p.astype(BF16)
        acc = jnp.zeros((8, HEAD_DIM), F32)
        for h in range(N_HEADS_DIL):
            acc = jnp.where(row_o == h, _dot_nt(pb, vt_ref[0, j, h].astype(BF16)), acc)
        bo_ref[j] = (acc + p_self * dq_ref[j, 2]) / den

        q = sq_ref[j, 0]
        qb = q.astype(BF16)
        s = jnp.where(row_s % 2 == 0, _dot(qb, ck_ref[0, j, 0].astype(BF16)),
                      _dot(qb, ck_ref[0, j, 1].astype(BF16))) + sbias_ref[...]
        w_self = jnp.sum(q * sq_ref[j, 1], axis=-1, keepdims=True)
        sink = sink_ref[...][:, 0:1]
        m = jnp.maximum(jnp.maximum(jnp.max(s, axis=-1, keepdims=True), w_self), sink)
        p = jnp.exp(s - m)
        pw = jnp.exp(w_self - m)
        den = jnp.sum(p, axis=-1, keepdims=True) + pw + jnp.exp(sink - m)
        pb = p.astype(BF16)
        acc = jnp.where(row_o % 2 == 0, _dot_nt(pb, cv_ref[0, j, 0].astype(BF16)),
                        _dot_nt(pb, cv_ref[0, j, 1].astype(BF16)))
        do_ref[j] = (acc + pw * sq_ref[j, 2]) / den


def _sample_b(layer, dq3, sq3, dbias, mult, sbias, sink_rows, kt, vt, ckt, cvt):
    ns = dq3.shape[0]
    bt = DEC_TILE
    r = kt.shape[-1]
    swr = ckt.shape[-1]
    row_spec = pl.BlockSpec((bt, 3, 8, HEAD_DIM), lambda i: (i, 0, 0, 0))
    out_spec = pl.BlockSpec((bt, 8, HEAD_DIM), lambda i: (i, 0, 0))
    full = lambda *shape: pl.BlockSpec(shape, lambda i: (0,) * len(shape))
    dil_spec = pl.BlockSpec((1, bt, N_HEADS_DIL, HEAD_DIM, r), lambda i: (layer, i, 0, 0, 0))
    swa_spec = pl.BlockSpec((1, bt, 2, HEAD_DIM, swr), lambda i: (layer, i, 0, 0, 0))
    return pl.pallas_call(
        _sb_kernel,
        grid=(ns // bt,),
        in_specs=[row_spec, row_spec, full(8, r), full(1, r), full(8, swr), full(8, BLOCK),
                  dil_spec, dil_spec, swa_spec, swa_spec],
        out_specs=[out_spec, out_spec],
        out_shape=[jax.ShapeDtypeStruct((ns, 8, HEAD_DIM), F32),
                   jax.ShapeDtypeStruct((ns, 8, HEAD_DIM), F32)],
        compiler_params=_compiler_params(("arbitrary",)),
        name="sample_b",
    )(dq3, sq3, dbias, mult, sbias, sink_rows, kt, vt, ckt, cvt)


def _sc_kernel(x_ref, mod_ref, yac_ref, gates_ref, bo_ref, do_ref, w_ref, fg_ref, o_ref, *, final):
    yac = yac_ref[...]
    gates = gates_ref[...]
    y = jnp.concatenate([yac[:, 0:W_POOL], bo_ref[...] * gates[:, 0:W_DIL],
                         yac[:, W_POOL:W_POOL + W_CONV], do_ref[...] * gates[:, W_DIL:W_DIL + W_SWA]],
                        axis=-1).astype(BF16)
    gate = mod_ref[...][:, 2 * D_MODEL:3 * D_MODEL]
    o_ref[...] = _mix_out(y, x_ref[...], gate, w_ref, fg_ref, final)


def _sample_c(xs, mod_s, yac, gates, bo, do, w_out_b, final_g, final):
    return pl.pallas_call(
        functools.partial(_sc_kernel, final=final),
        out_shape=jax.ShapeDtypeStruct(xs.shape, F32),
        compiler_params=pltpu.CompilerParams(vmem_limit_bytes=VMEM_LIMIT),
        name="sample_c",
    )(xs, mod_s, yac, gates, bo, do, w_out_b, final_g)


def _swa_perm_cols():
    return np.concatenate([np.arange(h * HEAD_DIM, (h + 1) * HEAD_DIM) for h in SWA_HEAD_PERM])


def _prompt_bias_tables():
    dil = _alibi_slopes(N_HEADS_DIL)
    dil_bias = np.stack([
        np.stack([np.stack([_band_bias(dil[2 * hp], dil[2 * hp + 1], d, var) for var in (0, 1)])
                  for _, d in DIL_CONFIGS]) for hp in range(N_PAIRS_DIL)])
    swa = _alibi_slopes(N_HEADS_SWA)
    swa_bias = np.stack([
        np.stack([_band_bias(swa[SWA_HEAD_PERM[2 * g]], swa[SWA_HEAD_PERM[2 * g + 1]], 1, var)
                  for var in (0, 2)]) for g in range(2)])
    return jnp.asarray(dil_bias), jnp.asarray(swa_bias)


def _sample_bias_tables(r):
    dil = _alibi_slopes(N_HEADS_DIL)
    dist = (r - np.arange(r)).astype(np.float32)
    mult = np.zeros((1, r), np.float32)
    for window, d in DIL_CONFIGS:
        mult[0] += ((dist <= window) & (dist % d == 0)).astype(np.float32)
    dbias = np.zeros((8, r), np.float32)
    for h in range(N_HEADS_DIL):
        dbias[h] = np.where(mult[0] > 0, -np.float32(dil[h]) * dist, np.float32(NEG_INF))
    swa = _alibi_slopes(N_HEADS_SWA)
    sdist = (SWA_WINDOW - np.arange(SWA_WINDOW)).astype(np.float32)
    sbias = np.zeros((8, SWA_WINDOW), np.float32)
    for row, h in enumerate(SWA_HEAD_PERM):
        sbias[row] = -np.float32(swa[h]) * sdist
    return jnp.asarray(dbias), jnp.asarray(mult), jnp.asarray(sbias)


def kernel(x_prompt, x_sample, c_prompt, c_sample, state_pool, cache_dil_k, cache_dil_v, state_conv, cache_swa_k, cache_swa_v, norm_g, w_ada, b_ada, w_in, w_pool, pool_scale, conv_w, swa_sink, w_out, final_g):
    depth = w_in.shape[0]
    nb, l, _ = x_prompt.shape
    ns = x_sample.shape[0]
    assert x_sample.shape[1] == 1 and l % CHUNK == 0 and l >= DIL_MAX and ns % DEC_TILE == 0
    assert cache_dil_k.shape[2] == DIL_MAX and cache_swa_k.shape[2] == SWA_WINDOW

    perm = _swa_perm_cols()
    in_cols = np.arange(D_PROJ)
    in_cols[OFF_SWA:OFF_SWA + W_SWA] = OFF_SWA + perm
    in_cols[OFF_SWA + 2 * W_SWA:OFF_SWA + 3 * W_SWA] = OFF_SWA + 2 * W_SWA + perm
    out_rows = np.arange(D_MIX)
    out_rows[D_MIX - W_SWA:] = D_MIX - W_SWA + perm
    w_in_b = w_in[:, :, in_cols].astype(BF16)
    w_out_b = w_out[:, out_rows, :].astype(BF16)
    w_ada_b = w_ada.astype(BF16)
    eye = jnp.eye(len(POOL_WINDOWS), dtype=F32)
    wbd = jnp.einsum("dgce,gh->dgche", w_pool, eye).reshape(depth, W_POOL, W_POOL).astype(BF16)
    sink_rows = jnp.zeros((depth, 8), F32).at[:, 0:N_HEADS_SWA].set(swa_sink[:, list(SWA_HEAD_PERM)])
    sink_rows = jnp.broadcast_to(sink_rows[:, :, None], (depth, 8, BLOCK))

    dil_bias, swa_bias = _prompt_bias_tables()
    dbias, mult, sbias = _sample_bias_tables(cache_dil_k.shape[2])

    mod = _ada(jnp.concatenate([c_prompt, c_sample], axis=0), w_ada_b, b_ada)
    fg = final_g.reshape(1, D_MODEL)

    xp = x_prompt
    xs = x_sample.reshape(ns, D_MODEL)
    kt = jnp.transpose(cache_dil_k, (0, 1, 3, 4, 2))
    vt = jnp.transpose(cache_dil_v, (0, 1, 3, 4, 2))
    ckt = jnp.transpose(cache_swa_k, (0, 1, 3, 4, 2))
    cvt = jnp.transpose(cache_swa_v, (0, 1, 3, 4, 2))
    sp_all = jnp.transpose(state_pool, (0, 2, 1, 3))
    sc_all = state_conv.reshape(depth, ns, CONV_BUF * W_CONV)
    pad_heads = lambda a: jnp.pad(a, ((0, 0), (0, 0), (0, 8 - a.shape[2]), (0, 0)))
    outs = [[] for _ in range(12)]
    for i in range(depth):
        final = i == depth - 1
        mod_p = mod[i, 0:nb].reshape(nb, 1, 3 * D_MODEL)
        mod_s = mod[i, nb:nb + ns]
        g = norm_g[i].reshape(1, D_MODEL)
        ps = pool_scale[i].reshape(1, W_POOL)

        qd, kd, vd, gd, kc, vc, yacd, pst, cst, skc, svc = _prompt_a(
            xp, mod_p, g, w_in_b[i], wbd[i], ps, conv_w[i], swa_sink[i], swa_bias)
        yb = _prompt_b(qd, kd, vd, gd, dil_bias)
        xp = _prompt_c(xp, mod_p, yacd, yb, w_out_b[i], fg, final)

        qkv, sw, yac, gates, pst_s, cst_s = _sample_a(
            xs, mod_s, g, w_in_b[i], wbd[i], ps, conv_w[i], sp_all[i], sc_all[i])
        dq3 = pad_heads(qkv.reshape(ns, 3, N_HEADS_DIL, HEAD_DIM))
        sq4 = sw[:, 0:W_SWA].reshape(ns, 1, N_HEADS_SWA, HEAD_DIM)
        skv = jnp.tile(sw[:, W_SWA:W_SWA + 2 * W_SWA_KV].reshape(ns, 2, 2, HEAD_DIM), (1, 1, 2, 1))
        sq3 = pad_heads(jnp.concatenate([sq4, skv], axis=1))
        bo, do = _sample_b(i, dq3, sq3, dbias, mult, sbias, sink_rows[i], kt, vt, ckt, cvt)
        bo = bo[:, 0:N_HEADS_DIL].reshape(ns, W_DIL)
        do = do[:, 0:N_HEADS_SWA].reshape(ns, W_SWA)
        xs = _sample_c(xs, mod_s, yac, gates, bo, do, w_out_b[i], fg, final)

        unfold = lambda a, h: jnp.transpose(a.reshape(nb, h, HEAD_DIM, a.shape[-1]), (0, 3, 1, 2))
        kc, vc, skc, svc = unfold(kc, N_HEADS_DIL), unfold(vc, N_HEADS_DIL), unfold(skc, 2), unfold(svc, 2)
        outs[0].append(pst)
        outs[1].append(jnp.transpose(pst_s, (1, 0, 2)))
        outs[2].append(kc)
        outs[3].append(vc)
        outs[4].append(qkv[:, W_DIL:2 * W_DIL].reshape(ns, 1, N_HEADS_DIL, HEAD_DIM))
        outs[5].append(qkv[:, 2 * W_DIL:3 * W_DIL].reshape(ns, 1, N_HEADS_DIL, HEAD_DIM))
        outs[6].append(cst)
        outs[7].append(cst_s.reshape(ns, CONV_BUF, W_CONV))
        outs[8].append(skc)
        outs[9].append(svc)
        outs[10].append(sw[:, W_SWA:W_SWA + W_SWA_KV].reshape(ns, 1, 2, HEAD_DIM))
        outs[11].append(sw[:, W_SWA + W_SWA_KV:W_SWA + 2 * W_SWA_KV].reshape(ns, 1, 2, HEAD_DIM))

    return (xp, xs.reshape(ns, 1, D_MODEL)) + tuple(jnp.stack(o) for o in outs)
```

```python
import functools
import math

import numpy as np
import jax
import jax.numpy as jnp
from jax import lax
from jax.experimental import pallas as pl
from jax.experimental.pallas import tpu as pltpu

F32 = jnp.float32
BF16 = jnp.bfloat16

D_MODEL = 1024
HEAD_DIM = 64
BLOCK = 128
POOL_WINDOWS = (2, 4, 8, 16)
POOL_GROUP = 64
W_POOL = 256
POOL_BUF = 15
DIL_CONFIGS = ((128, 1), (512, 4), (2048, 16))
DIL_MAX = 2048
N_HEADS_DIL = 6
N_PAIRS_DIL = 3
N_CLASSES = 16
W_DIL = 384
W_CONV = 256
CONV_BUF = 2
N_HEADS_SWA = 4
W_SWA = 256
W_SWA_KV = 128
SWA_WINDOW = 128
D_MIX = 1152
D_PROJ = 3840
RMS_EPS = 1e-6
QK_SCALE = 1.0 / math.sqrt(HEAD_DIM)

OFF_POOL = 0
OFF_DIL = 512
OFF_CONV = 2048
OFF_SWA = 3072
SWA_HEAD_PERM = (0, 3, 1, 2)

SEQ_TILE = 512
CHUNK = 2048
DEC_TILE = 2
VMEM_LIMIT = 56 * 1024 * 1024

NEG_INF = float("-inf")


def _silu(v):
    return v * jax.nn.sigmoid(v)


def _dot(a, b):
    return jnp.dot(a, b, preferred_element_type=F32)


def _dot_nt(a, b):
    return lax.dot_general(a, b, (((1,), (1,)), ((), ())), preferred_element_type=F32)


def _alibi_slopes(n):
    return [2.0 ** (-8.0 * (h + 1) / n) for h in range(n)]


def _band_bias(slope_lo, slope_hi, dist_scale, variant):
    qi = np.arange(BLOCK)[:, None]
    kj = np.arange(2 * BLOCK)[None, :]
    if variant == 1:
        off = qi - kj
    else:
        off = qi - kj + BLOCK
    valid = (off >= 0) & (off <= BLOCK)
    if variant == 2:
        valid = valid & (kj >= BLOCK)
    out = []
    for s in (slope_lo, slope_hi):
        b = np.where(valid, -(np.float32(s) * np.float32(dist_scale)) * off.astype(np.float32),
                     np.float32(NEG_INF))
        out.append(b.astype(np.float32))
    return np.concatenate(out, axis=0)


def _compiler_params(sem):
    return pltpu.CompilerParams(dimension_semantics=sem, vmem_limit_bytes=VMEM_LIMIT)


def _ada_kernel(c_ref, w_ref, b_ref, o_ref):
    s = _silu(c_ref[...]).astype(BF16)
    o_ref[0] = _dot(s, w_ref[0].astype(BF16)) + b_ref[0]


def _ada(c_all, w_ada_b, b_ada):
    depth = w_ada_b.shape[0]
    rows = c_all.shape[0]
    return pl.pallas_call(
        _ada_kernel,
        grid=(depth, 3),
        in_specs=[
            pl.BlockSpec((rows, D_MODEL), lambda i, j: (0, 0)),
            pl.BlockSpec((1, D_MODEL, D_MODEL), lambda i, j: (i, 0, j)),
            pl.BlockSpec((1, 1, D_MODEL), lambda i, j: (i, 0, j)),
        ],
        out_specs=pl.BlockSpec((1, rows, D_MODEL), lambda i, j: (i, 0, j)),
        out_shape=jax.ShapeDtypeStruct((depth, rows, 3 * D_MODEL), F32),
        compiler_params=_compiler_params(("arbitrary", "arbitrary")),
        name="ada",
    )(c_all, w_ada_b, b_ada.reshape(depth, 1, 3 * D_MODEL))


def _modulated_norm(x, g, mod_row):
    ms = jnp.mean(x * x, axis=-1, keepdims=True)
    y = x * lax.rsqrt(ms + RMS_EPS) * g
    shift = mod_row[:, 0:D_MODEL]
    scale = mod_row[:, D_MODEL:2 * D_MODEL]
    return y * (1.0 + scale) + shift


def _pool_select(sums, lane):
    grp = lane // POOL_GROUP
    sel = jnp.where(grp == 0, sums[2], jnp.where(grp == 1, sums[4],
                                                 jnp.where(grp == 2, sums[8], sums[16])))
    win = jnp.where(grp == 0, 2, jnp.where(grp == 1, 4, jnp.where(grp == 2, 8, 16)))
    return sel, win


def _pa_kernel(sink_ref, x_ref, mod_ref, g_ref, w_ref, wbd_ref, ps_ref, cw_ref, bias_ref,
               q1_ref, k1_ref, v1_ref, qm_ref, km_ref, vm_ref, gm_ref, kc_ref, vc_ref, y_ref,
               pst_ref, cst_ref, skc_ref, svc_ref,
               uext, zcext, kext, vext, zds):
    tl = SEQ_TILE
    t = pl.program_id(1)

    @pl.when(t == 0)
    def _():
        uext[0:16, :] = jnp.zeros((16, W_POOL), F32)
        zcext[0:8, :] = jnp.zeros((8, W_CONV), F32)
        kext[0:BLOCK, :] = jnp.zeros((BLOCK, W_SWA_KV), BF16)
        vext[0:BLOCK, :] = jnp.zeros((BLOCK, W_SWA_KV), BF16)

    hb = _modulated_norm(x_ref[0], g_ref[...], mod_ref[0]).astype(BF16)

    zp = _dot(hb, w_ref[:, OFF_POOL:OFF_POOL + 2 * W_POOL])
    pu = zp[:, 0:W_POOL]
    pg = zp[:, W_POOL:2 * W_POOL]
    uext[16:16 + tl, :] = pu
    acc = pu
    sums = {}
    for j in range(1, 16):
        acc = acc + uext[16 - j:16 - j + tl, :]
        if j + 1 in POOL_WINDOWS:
            sums[j + 1] = acc
    lane = lax.broadcasted_iota(jnp.int32, (tl, W_POOL), 1)
    gpos = lax.broadcasted_iota(jnp.int32, (tl, W_POOL), 0) + t * tl
    sel, win = _pool_select(sums, lane)
    cnt = jnp.minimum(gpos + 1, win).astype(F32)
    diff = sel / cnt - pu
    a_out = _dot(diff.astype(BF16), wbd_ref[...]) * ps_ref[...]
    y_ref[0, :, 0:W_POOL] = (a_out * _silu(pg)).astype(BF16)
    pst_ref[0] = uext[tl + 1:tl + 16, :]
    uext[0:16, :] = uext[tl:tl + 16, :]

    zc4 = _dot(hb, w_ref[:, OFF_CONV:OFF_CONV + 4 * W_CONV])
    ch = zc4[:, 0:W_CONV]
    cb = zc4[:, W_CONV:2 * W_CONV]
    cc = zc4[:, 2 * W_CONV:3 * W_CONV]
    cg = zc4[:, 3 * W_CONV:4 * W_CONV]
    zc = cc * ch
    zcext[8:8 + tl, :] = zc
    conv = (cw_ref[0:1, :] * zcext[6:6 + tl, :] + cw_ref[1:2, :] * zcext[7:7 + tl, :]
            + cw_ref[2:3, :] * zc)
    y_ref[0, :, W_POOL:W_POOL + W_CONV] = (cb * conv * _silu(cg)).astype(BF16)
    cst_ref[0] = zcext[tl + 6:tl + 8, :]
    zcext[0:8, :] = zcext[tl:tl + 8, :]

    zd = _dot(hb, w_ref[:, OFF_DIL:OFF_DIL + 4 * W_DIL])
    for hp in range(N_PAIRS_DIL):
        lo = hp * BLOCK
        zds[hp] = zd[:, lo:lo + BLOCK] * QK_SCALE
        zds[3 + hp] = zd[:, W_DIL + lo:W_DIL + lo + BLOCK]
        zds[6 + hp] = zd[:, 2 * W_DIL + lo:2 * W_DIL + lo + BLOCK]
        zds[9 + hp] = _silu(zd[:, 3 * W_DIL + lo:3 * W_DIL + lo + BLOCK])
        q1_ref[0, hp] = zds[hp].astype(BF16)
        k1_ref[0, hp] = zds[3 + hp].astype(BF16)
        v1_ref[0, hp] = zds[6 + hp].astype(BF16)
        for c in range(N_CLASSES):
            rows = pl.ds(c, tl // N_CLASSES, stride=N_CLASSES)
            qm_ref[0, hp, c] = zds[hp, rows, :].astype(BF16)
            km_ref[0, hp, c] = zds[3 + hp, rows, :].astype(BF16)
            vm_ref[0, hp, c] = zds[6 + hp, rows, :].astype(BF16)
            gm_ref[0, hp, c] = zds[9 + hp, rows, :].astype(BF16)
    kc_ref[0] = zd[:, W_DIL:2 * W_DIL].T
    vc_ref[0] = zd[:, 2 * W_DIL:3 * W_DIL].T

    zs = _dot(hb, w_ref[:, OFF_SWA:OFF_SWA + 3 * W_SWA])
    sk = zs[:, W_SWA:W_SWA + W_SWA_KV]
    sv = zs[:, W_SWA + W_SWA_KV:W_SWA + 2 * W_SWA_KV]
    kext[BLOCK:BLOCK + tl, :] = sk.astype(BF16)
    vext[BLOCK:BLOCK + tl, :] = sv.astype(BF16)
    skc_ref[0] = sk[tl - SWA_WINDOW:tl, :].T
    svc_ref[0] = sv[tl - SWA_WINDOW:tl, :].T
    first = jnp.where(t == 0, 1, 0)
    lane_q = lax.broadcasted_iota(jnp.int32, (BLOCK, BLOCK), 1)
    row_s = lax.broadcasted_iota(jnp.int32, (2 * BLOCK, 1), 0)
    for jb in range(tl // BLOCK):
        r0 = jb * BLOCK
        kblk = kext[r0:r0 + 2 * BLOCK, :]
        vblk = vext[r0:r0 + 2 * BLOCK, :]
        for grp in range(2):
            q = (zs[r0:r0 + BLOCK, grp * BLOCK:(grp + 1) * BLOCK] * QK_SCALE).astype(BF16)
            lhs = jnp.concatenate([jnp.where(lane_q < HEAD_DIM, q, jnp.zeros_like(q)),
                                   jnp.where(lane_q >= HEAD_DIM, q, jnp.zeros_like(q))], axis=0)
            s = _dot_nt(lhs, kblk)
            if jb == 0:
                s = s + bias_ref[grp, first]
            else:
                s = s + bias_ref[grp, 0]
            sink = jnp.where(row_s < BLOCK, sink_ref[SWA_HEAD_PERM[2 * grp]],
                             sink_ref[SWA_HEAD_PERM[2 * grp + 1]])
            m = jnp.maximum(jnp.max(s, axis=-1, keepdims=True), sink)
            p = jnp.exp(s - m)
            den = jnp.sum(p, axis=-1, keepdims=True) + jnp.exp(sink - m)
            o = _dot(p.astype(BF16), vblk) / den
            od = jnp.where(lane_q < HEAD_DIM, o[0:BLOCK], o[BLOCK:2 * BLOCK])
            sg = zs[r0:r0 + BLOCK, 2 * W_SWA + grp * BLOCK:2 * W_SWA + (grp + 1) * BLOCK]
            c0 = W_POOL + W_CONV + grp * BLOCK
            y_ref[0, r0:r0 + BLOCK, c0:c0 + BLOCK] = (od * _silu(sg)).astype(BF16)
    kext[0:BLOCK, :] = kext[tl:tl + BLOCK, :]
    vext[0:BLOCK, :] = vext[tl:tl + BLOCK, :]


def _prompt_a(x, mod_p, norm_g, w_in_b, wbd, pool_scale, conv_w, sink, swa_bias):
    nb, l, _ = x.shape
    tl = SEQ_TILE
    nt = l // tl
    cache_t0 = (l - DIL_MAX) // tl
    pair_spec = pl.BlockSpec((1, N_PAIRS_DIL, tl, BLOCK), lambda n, t: (n, 0, t, 0))
    pair_shape = jax.ShapeDtypeStruct((nb, N_PAIRS_DIL, l, BLOCK), BF16)
    cm_spec = pl.BlockSpec((1, N_PAIRS_DIL, N_CLASSES, tl // N_CLASSES, BLOCK),
                           lambda n, t: (n, 0, 0, t, 0))
    cm_shape = jax.ShapeDtypeStruct((nb, N_PAIRS_DIL, N_CLASSES, l // N_CLASSES, BLOCK), BF16)
    cache_spec = pl.BlockSpec((1, W_DIL, tl), lambda n, t: (n, 0, jnp.maximum(t - cache_t0, 0)))
    cache_shape = jax.ShapeDtypeStruct((nb, W_DIL, DIL_MAX), F32)
    full = lambda *shape: pl.BlockSpec(shape, lambda n, t: (0,) * len(shape))
    return pl.pallas_call(
        _pa_kernel,
        grid=(nb, nt),
        in_specs=[
            pl.BlockSpec(memory_space=pltpu.SMEM),
            pl.BlockSpec((1, tl, D_MODEL), lambda n, t: (n, t, 0)),
            pl.BlockSpec((1, 1, 3 * D_MODEL), lambda n, t: (n, 0, 0)),
            full(1, D_MODEL),
            full(D_MODEL, D_PROJ),
            full(W_POOL, W_POOL),
            full(1, W_POOL),
            full(3, W_CONV),
            full(2, 2, 2 * BLOCK, 2 * BLOCK),
        ],
        out_specs=[
            pair_spec, pair_spec, pair_spec, cm_spec, cm_spec, cm_spec, cm_spec,
            cache_spec, cache_spec,
            pl.BlockSpec((1, tl, 3 * W_POOL), lambda n, t: (n, t, 0)),
            pl.BlockSpec((1, POOL_BUF, W_POOL), lambda n, t: (n, 0, 0)),
            pl.BlockSpec((1, CONV_BUF, W_CONV), lambda n, t: (n, 0, 0)),
            pl.BlockSpec((1, SWA_WINDOW, W_SWA_KV), lambda n, t: (n, 0, 0)),
            pl.BlockSpec((1, SWA_WINDOW, W_SWA_KV), lambda n, t: (n, 0, 0)),
        ],
        out_shape=[
            pair_shape, pair_shape, pair_shape, cm_shape, cm_shape, cm_shape, cm_shape,
            cache_shape, cache_shape,
            jax.ShapeDtypeStruct((nb, l, 3 * W_POOL), BF16),
            jax.ShapeDtypeStruct((nb, POOL_BUF, W_POOL), F32),
            jax.ShapeDtypeStruct((nb, CONV_BUF, W_CONV), F32),
            jax.ShapeDtypeStruct((nb, SWA_WINDOW, W_SWA_KV), F32),
            jax.ShapeDtypeStruct((nb, SWA_WINDOW, W_SWA_KV), F32),
        ],
        scratch_shapes=[
            pltpu.VMEM((16 + tl, W_POOL), F32),
            pltpu.VMEM((8 + tl, W_CONV), F32),
            pltpu.VMEM((BLOCK + tl, W_SWA_KV), BF16),
            pltpu.VMEM((BLOCK + tl, W_SWA_KV), BF16),
            pltpu.VMEM((4 * N_PAIRS_DIL, tl, BLOCK), F32),
        ],
        compiler_params=_compiler_params(("arbitrary", "arbitrary")),
        name="prompt_a",
    )(sink, x, mod_p, norm_g, w_in_b, wbd, pool_scale, conv_w, swa_bias)


def _band_pair(q, kblk, vblk, bias, lane_q):
    lhs = jnp.concatenate([jnp.where(lane_q < HEAD_DIM, q, jnp.zeros_like(q)),
                           jnp.where(lane_q >= HEAD_DIM, q, jnp.zeros_like(q))], axis=0)
    s = _dot_nt(lhs, kblk) + bias
    m = jnp.max(s, axis=-1, keepdims=True)
    p = jnp.exp(s - m)
    den = jnp.sum(p, axis=-1, keepdims=True)
    o = _dot(p.astype(BF16), vblk) / den
    lse = m + jnp.log(den)
    od = jnp.where(lane_q < HEAD_DIM, o[0:BLOCK], o[BLOCK:2 * BLOCK])
    ld = jnp.where(lane_q < HEAD_DIM, lse[0:BLOCK], lse[BLOCK:2 * BLOCK])
    return od, ld


def _pb_kernel(bias_ref, q1, k1, v1, qm, km, vm, gm, outm, o1s, l1s, o4s, l4s, o16s, l16s):
    c = pl.program_id(2)
    first = jnp.where(c == 0, 1, 0)
    lane_q = lax.broadcasted_iota(jnp.int32, (BLOCK, BLOCK), 1)
    sub = BLOCK // 4

    for j in range(CHUNK // BLOCK):
        r0 = j * BLOCK
        if j == 0:
            start = jnp.maximum(c * (CHUNK // BLOCK) - 1, 0) * BLOCK
            bias = bias_ref[0, 0, first]
        else:
            start = (c * (CHUNK // BLOCK) + (j - 1)) * BLOCK
            bias = bias_ref[0, 0, 0]
        start = pl.multiple_of(start, BLOCK)
        od, ld = _band_pair(q1[0, 0, r0:r0 + BLOCK, :], k1[0, 0, pl.ds(start, 2 * BLOCK), :],
                            v1[0, 0, pl.ds(start, 2 * BLOCK), :], bias, lane_q)
        o1s[0, r0:r0 + BLOCK, :] = od
        l1s[0, r0:r0 + BLOCK, :] = ld

    for c4 in range(4):
        for j in range(4):
            i0 = j * sub
            if j == 0:
                istart = jnp.maximum(c * (CHUNK // N_CLASSES) - sub, 0)
                bias = bias_ref[0, 1, first]
            else:
                istart = c * (CHUNK // N_CLASSES) + i0 - sub
                bias = bias_ref[0, 1, 0]
            istart = pl.multiple_of(istart, sub)
            classes = [4 * cc + c4 for cc in range(4)]
            q = jnp.concatenate([qm[0, 0, cl, i0:i0 + sub, :] for cl in classes], axis=0)
            kblk = jnp.concatenate([km[0, 0, cl, pl.ds(istart, 2 * sub), :] for cl in classes], axis=0)
            vblk = jnp.concatenate([vm[0, 0, cl, pl.ds(istart, 2 * sub), :] for cl in classes], axis=0)
            od, ld = _band_pair(q, kblk, vblk, bias, lane_q)
            for cc, cl in enumerate(classes):
                o4s[cl, i0:i0 + sub, :] = od[cc * sub:(cc + 1) * sub]
                l4s[cl, i0:i0 + sub, :] = ld[cc * sub:(cc + 1) * sub]

    start = pl.multiple_of(jnp.maximum(c - 1, 0) * BLOCK, BLOCK)
    for cl in range(N_CLASSES):
        od, ld = _band_pair(qm[0, 0, cl], km[0, 0, cl, pl.ds(start, 2 * BLOCK), :],
                            vm[0, 0, cl, pl.ds(start, 2 * BLOCK), :], bias_ref[0, 2, first], lane_q)
        o16s[cl] = od
        l16s[cl] = ld

    for cl in range(N_CLASSES):
        oa = o1s[0, pl.ds(cl, BLOCK, stride=N_CLASSES), :]
        la = l1s[0, pl.ds(cl, BLOCK, stride=N_CLASSES), :]
        ob = o4s[cl]
        lb = l4s[cl]
        oc = o16s[cl]
        lc = l16s[cl]
        mx = jnp.maximum(jnp.maximum(la, lb), lc)
        wa = jnp.exp(la - mx)
        wb = jnp.exp(lb - mx)
        wc = jnp.exp(lc - mx)
        mix = (wa * oa + wb * ob + wc * oc) / (wa + wb + wc)
        outm[0, 0, cl] = (mix * gm[0, 0, cl].astype(F32)).astype(BF16)


def _prompt_b(q1, k1, v1, qm, km, vm, gm, dil_bias):
    nb, npair, l, _ = q1.shape
    nc = l // CHUNK
    li = l // N_CLASSES
    ci = CHUNK // N_CLASSES
    q1_spec = pl.BlockSpec((1, 1, CHUNK, BLOCK), lambda n, h, c: (n, h, c, 0))
    kv1_spec = pl.BlockSpec((1, 1, l, BLOCK), lambda n, h, c: (n, h, 0, 0))
    cm_spec = pl.BlockSpec((1, 1, N_CLASSES, ci, BLOCK), lambda n, h, c: (n, h, 0, c, 0))
    kvm_spec = pl.BlockSpec((1, 1, N_CLASSES, li, BLOCK), lambda n, h, c: (n, h, 0, 0, 0))
    cm_scratch = pltpu.VMEM((N_CLASSES, ci, BLOCK), F32)
    return pl.pallas_call(
        _pb_kernel,
        grid=(nb, npair, nc),
        in_specs=[pl.BlockSpec((1, 3, 2, 2 * BLOCK, 2 * BLOCK), lambda n, h, c: (h, 0, 0, 0, 0)),
                  q1_spec, kv1_spec, kv1_spec, cm_spec, kvm_spec, kvm_spec, cm_spec],
        out_specs=cm_spec,
        out_shape=jax.ShapeDtypeStruct((nb, npair, N_CLASSES, li, BLOCK), BF16),
        scratch_shapes=[
            pltpu.VMEM((1, CHUNK, BLOCK), F32), pltpu.VMEM((1, CHUNK, BLOCK), F32),
            cm_scratch, cm_scratch, cm_scratch, cm_scratch,
        ],
        compiler_params=_compiler_params(("arbitrary", "arbitrary", "arbitrary")),
        name="prompt_b",
    )(dil_bias, q1, k1, v1, qm, km, vm, gm)


def _mix_out(y, x, gate, w_ref, fg_ref, final):
    xn = x + gate * _dot(y, w_ref[...])
    if final:
        ms = jnp.mean(xn * xn, axis=-1, keepdims=True)
        xn = xn * lax.rsqrt(ms + RMS_EPS) * fg_ref[...]
    return xn


def _pc_kernel(x_ref, mod_ref, yacd_ref, yb_ref, w_ref, fg_ref, o_ref, ybs, *, final):
    for hp in range(N_PAIRS_DIL):
        for c in range(N_CLASSES):
            ybs[hp, pl.ds(c, SEQ_TILE // N_CLASSES, stride=N_CLASSES), :] = yb_ref[0, hp, c].astype(F32)
    yacd = yacd_ref[0]
    y = jnp.concatenate([yacd[:, 0:W_POOL], ybs[0].astype(BF16), ybs[1].astype(BF16),
                         ybs[2].astype(BF16), yacd[:, W_POOL:3 * W_POOL]], axis=-1)
    gate = mod_ref[0][:, 2 * D_MODEL:3 * D_MODEL]
    o_ref[0] = _mix_out(y, x_ref[0], gate, w_ref, fg_ref, final)


def _prompt_c(x, mod_p, yacd, yb, w_out_b, final_g, final):
    nb, l, _ = x.shape
    tl = SEQ_TILE
    full = lambda *shape: pl.BlockSpec(shape, lambda n, t: (0,) * len(shape))
    return pl.pallas_call(
        functools.partial(_pc_kernel, final=final),
        grid=(nb, l // tl),
        in_specs=[
            pl.BlockSpec((1, tl, D_MODEL), lambda n, t: (n, t, 0)),
            pl.BlockSpec((1, 1, 3 * D_MODEL), lambda n, t: (n, 0, 0)),
            pl.BlockSpec((1, tl, 3 * W_POOL), lambda n, t: (n, t, 0)),
            pl.BlockSpec((1, N_PAIRS_DIL, N_CLASSES, tl // N_CLASSES, BLOCK),
                         lambda n, t: (n, 0, 0, t, 0)),
            full(D_MIX, D_MODEL),
            full(1, D_MODEL),
        ],
        out_specs=pl.BlockSpec((1, tl, D_MODEL), lambda n, t: (n, t, 0)),
        out_shape=jax.ShapeDtypeStruct((nb, l, D_MODEL), F32),
        scratch_shapes=[pltpu.VMEM((N_PAIRS_DIL, tl, BLOCK), F32)],
        compiler_params=_compiler_params(("arbitrary", "arbitrary")),
        name="prompt_c",
    )(x, mod_p, yacd, yb, w_out_b, final_g)


def _sa_kernel(x_ref, mod_ref, g_ref, w_ref, wbd_ref, ps_ref, cw_ref, sp_ref, sc_ref,
               qkv_ref, sw_ref, yac_ref, gates_ref, pst_ref, cst_ref):
    hb = _modulated_norm(x_ref[...], g_ref[...], mod_ref[...]).astype(BF16)
    ns = hb.shape[0]

    zp = _dot(hb, w_ref[:, OFF_POOL:OFF_POOL + 2 * W_POOL])
    pu = zp[:, 0:W_POOL]
    pg = zp[:, W_POOL:2 * W_POOL]
    acc = pu
    sums = {}
    for j in range(1, 16):
        acc = acc + sp_ref[POOL_BUF - j]
        if j + 1 in POOL_WINDOWS:
            sums[j + 1] = acc
    lane = lax.broadcasted_iota(jnp.int32, (ns, W_POOL), 1)
    sel, win = _pool_select(sums, lane)
    diff = sel / win.astype(F32) - pu
    a_out = _dot(diff.astype(BF16), wbd_ref[...]) * ps_ref[...]
    yac_ref[:, 0:W_POOL] = a_out * _silu(pg)
    pst_ref[0:POOL_BUF - 1] = sp_ref[1:POOL_BUF]
    pst_ref[POOL_BUF - 1] = pu

    zc4 = _dot(hb, w_ref[:, OFF_CONV:OFF_CONV + 4 * W_CONV])
    ch = zc4[:, 0:W_CONV]
    cb = zc4[:, W_CONV:2 * W_CONV]
    cc = zc4[:, 2 * W_CONV:3 * W_CONV]
    cg = zc4[:, 3 * W_CONV:4 * W_CONV]
    zc = cc * ch
    conv = (cw_ref[0:1, :] * sc_ref[:, 0:W_CONV] + cw_ref[1:2, :] * sc_ref[:, W_CONV:2 * W_CONV]
            + cw_ref[2:3, :] * zc)
    yac_ref[:, W_POOL:W_POOL + W_CONV] = cb * conv * _silu(cg)
    cst_ref[:, 0:W_CONV] = sc_ref[:, W_CONV:2 * W_CONV]
    cst_ref[:, W_CONV:2 * W_CONV] = zc

    zd = _dot(hb, w_ref[:, OFF_DIL:OFF_DIL + 4 * W_DIL])
    qkv_ref[:, 0:W_DIL] = zd[:, 0:W_DIL] * QK_SCALE
    qkv_ref[:, W_DIL:3 * W_DIL] = zd[:, W_DIL:3 * W_DIL]
    gates_ref[:, 0:W_DIL] = _silu(zd[:, 3 * W_DIL:4 * W_DIL])

    zs = _dot(hb, w_ref[:, OFF_SWA:OFF_SWA + 3 * W_SWA])
    sw_ref[:, 0:W_SWA] = zs[:, 0:W_SWA] * QK_SCALE
    sw_ref[:, W_SWA:2 * W_SWA] = zs[:, W_SWA:2 * W_SWA]
    gates_ref[:, W_DIL:W_DIL + W_SWA] = _silu(zs[:, 2 * W_SWA:3 * W_SWA])


def _sample_a(xs, mod_s, norm_g, w_in_b, wbd, pool_scale, conv_w, sp, sc):
    ns = xs.shape[0]
    shapes = [
        jax.ShapeDtypeStruct((ns, 3 * W_DIL), F32),
        jax.ShapeDtypeStruct((ns, 2 * W_SWA), F32),
        jax.ShapeDtypeStruct((ns, W_POOL + W_CONV), F32),
        jax.ShapeDtypeStruct((ns, W_DIL + W_SWA), F32),
        jax.ShapeDtypeStruct((POOL_BUF, ns, W_POOL), F32),
        jax.ShapeDtypeStruct((ns, CONV_BUF * W_CONV), F32),
    ]
    return pl.pallas_call(
        _sa_kernel,
        out_shape=shapes,
        compiler_params=pltpu.CompilerParams(vmem_limit_bytes=VMEM_LIMIT),
        name="sample_a",
    )(xs, mod_s, norm_g, w_in_b, wbd, pool_scale, conv_w, sp, sc)


def _sb_kernel(dq_ref, sq_ref, dbias_ref, mult_ref, sbias_ref, sink_ref, kt_ref, vt_ref,
               ck_ref, cv_ref, bo_ref, do_ref):
    row_d = lax.broadcasted_iota(jnp.int32, (8, DIL_MAX), 0)
    row_o = lax.broadcasted_iota(jnp.int32, (8, HEAD_DIM), 0)
    row_s = lax.broadcasted_iota(jnp.int32, (8, SWA_WINDOW), 0)
    for j in range(DEC_TILE):
        q = dq_ref[j, 0]
        qb = q.astype(BF16)
        s = jnp.zeros((8, DIL_MAX), F32)
        for h in range(N_HEADS_DIL):
            s = jnp.where(row_d == h, _dot(qb, kt_ref[0, j, h].astype(BF16)), s)
        s = s + dbias_ref[...]
        s_self = jnp.sum(q * dq_ref[j, 1], axis=-1, keepdims=True)
        m = jnp.maximum(jnp.max(s, axis=-1, keepdims=True), s_self)
        p = jnp.exp(s - m) * mult_ref[...]
        p_self = float(len(DIL_CONFIGS)) * jnp.exp(s_self - m)
        den = jnp.sum(p, axis=-1, keepdims=True) + p_self
        pb = p.astype(BF16)
        acc = jnp.zeros((8, HEAD_DIM), F32)
        for h in range(N_HEADS_DIL):
            acc = jnp.where(row_o == h, _dot_nt(pb, vt_ref[0, j, h].astype(BF16)), acc)
        bo_ref[j] = (acc + p_self * dq_ref[j, 2]) / den

        q = sq_ref[j, 0]
        qb = q.astype(BF16)
        s = jnp.where(row_s % 2 == 0, _dot(qb, ck_ref[0, j, 0].astype(BF16)),
                      _dot(qb, ck_ref[0, j, 1].astype(BF16))) + sbias_ref[...]
        w_self = jnp.sum(q * sq_ref[j, 1], axis=-1, keepdims=True)
        sink = sink_ref[...][:, 0:1]
        m = jnp.maximum(jnp.maximum(jnp.max(s, axis=-1, keepdims=True), w_self), sink)
        p = jnp.exp(s - m)
        pw = jnp.exp(w_self - m)
        den = jnp.sum(p, axis=-1, keepdims=True) + pw + jnp.exp(sink - m)
        pb = p.astype(BF16)
        acc = jnp.where(row_o % 2 == 0, _dot_nt(pb, cv_ref[0, j, 0].astype(BF16)),
                        _dot_nt(pb, cv_ref[0, j, 1].astype(BF16)))
        do_ref[j] = (acc + pw * sq_ref[j, 2]) / den


def _sample_b(layer, dq3, sq3, dbias, mult, sbias, sink_rows, kt, vt, ckt, cvt):
    ns = dq3.shape[0]
    bt = DEC_TILE
    r = kt.shape[-1]
    swr = ckt.shape[-1]
    row_spec = pl.BlockSpec((bt, 3, 8, HEAD_DIM), lambda i: (i, 0, 0, 0))
    out_spec = pl.BlockSpec((bt, 8, HEAD_DIM), lambda i: (i, 0, 0))
    full = lambda *shape: pl.BlockSpec(shape, lambda i: (0,) * len(shape))
    dil_spec = pl.BlockSpec((1, bt, N_HEADS_DIL, HEAD_DIM, r), lambda i: (layer, i, 0, 0, 0))
    swa_spec = pl.BlockSpec((1, bt, 2, HEAD_DIM, swr), lambda i: (layer, i, 0, 0, 0))
    return pl.pallas_call(
        _sb_kernel,
        grid=(ns // bt,),
        in_specs=[row_spec, row_spec, full(8, r), full(1, r), full(8, swr), full(8, BLOCK),
                  dil_spec, dil_spec, swa_spec, swa_spec],
        out_specs=[out_spec, out_spec],
        out_shape=[jax.ShapeDtypeStruct((ns, 8, HEAD_DIM), F32),
                   jax.ShapeDtypeStruct((ns, 8, HEAD_DIM), F32)],
        compiler_params=_compiler_params(("arbitrary",)),
        name="sample_b",
    )(dq3, sq3, dbias, mult, sbias, sink_rows, kt, vt, ckt, cvt)


def _sc_kernel(x_ref, mod_ref, yac_ref, gates_ref, bo_ref, do_ref, w_ref, fg_ref, o_ref, *, final):
    yac = yac_ref[...]
    gates = gates_ref[...]
    y = jnp.concatenate([yac[:, 0:W_POOL], bo_ref[...] * gates[:, 0:W_DIL],
                         yac[:, W_POOL:W_POOL + W_CONV], do_ref[...] * gates[:, W_DIL:W_DIL + W_SWA]],
                        axis=-1).astype(BF16)
    gate = mod_ref[...][:, 2 * D_MODEL:3 * D_MODEL]
    o_ref[...] = _mix_out(y, x_ref[...], gate, w_ref, fg_ref, final)


def _sample_c(xs, mod_s, yac, gates, bo, do, w_out_b, final_g, final):
    return pl.pallas_call(
        functools.partial(_sc_kernel, final=final),
        out_shape=jax.ShapeDtypeStruct(xs.shape, F32),
        compiler_params=pltpu.CompilerParams(vmem_limit_bytes=VMEM_LIMIT),
        name="sample_c",
    )(xs, mod_s, yac, gates, bo, do, w_out_b, final_g)


def _band_bias_dil4(slope_lo, slope_hi, variant):
    sub = BLOCK // 4
    qidx = np.arange(BLOCK)[:, None]
    kidx = np.arange(2 * BLOCK)[None, :]
    q_step = 4 * (qidx % sub + (sub if variant == 0 else 0)) + qidx // sub
    k_step = 4 * (kidx % (2 * sub)) + kidx // (2 * sub)
    off = q_step - k_step
    valid = (off >= 0) & (off <= BLOCK)
    out = []
    for s in (slope_lo, slope_hi):
        out.append(np.where(valid, -(np.float32(s) * np.float32(4)) * off.astype(np.float32),
                            np.float32(NEG_INF)).astype(np.float32))
    return np.concatenate(out, axis=0)


def _prompt_bias_tables():
    dil = _alibi_slopes(N_HEADS_DIL)

    def table(hp, d, var):
        if d == 4:
            return _band_bias_dil4(dil[2 * hp], dil[2 * hp + 1], var)
        return _band_bias(dil[2 * hp], dil[2 * hp + 1], d, var)

    dil_bias = np.stack([
        np.stack([np.stack([table(hp, d, var) for var in (0, 1)])
                  for _, d in DIL_CONFIGS]) for hp in range(N_PAIRS_DIL)])
    swa = _alibi_slopes(N_HEADS_SWA)
    swa_bias = np.stack([
        np.stack([_band_bias(swa[SWA_HEAD_PERM[2 * g]], swa[SWA_HEAD_PERM[2 * g + 1]], 1, var)
                  for var in (0, 2)]) for g in range(2)])
    return jnp.asarray(dil_bias), jnp.asarray(swa_bias)


def _sample_bias_tables(r):
    dil = _alibi_slopes(N_HEADS_DIL)
    dist = (r - np.arange(r)).astype(np.float32)
    mult = np.zeros((1, r), np.float32)
    for window, d in DIL_CONFIGS:
        mult[0] += ((dist <= window) & (dist % d == 0)).astype(np.float32)
    dbias = np.zeros((8, r), np.float32)
    for h in range(N_HEADS_DIL):
        dbias[h] = np.where(mult[0] > 0, -np.float32(dil[h]) * dist, np.float32(NEG_INF))
    swa = _alibi_slopes(N_HEADS_SWA)
    sdist = (SWA_WINDOW - np.arange(SWA_WINDOW)).astype(np.float32)
    sbias = np.zeros((8, SWA_WINDOW), np.float32)
    for row, h in enumerate(SWA_HEAD_PERM):
        sbias[row] = -np.float32(swa[h]) * sdist
    return jnp.asarray(dbias), jnp.asarray(mult), jnp.asarray(sbias)


def kernel(x_prompt, x_sample, c_prompt, c_sample, state_pool, cache_dil_k, cache_dil_v, state_conv, cache_swa_k, cache_swa_v, norm_g, w_ada, b_ada, w_in, w_pool, pool_scale, conv_w, swa_sink, w_out, final_g):
    depth = w_in.shape[0]
    nb, l, _ = x_prompt.shape
    ns = x_sample.shape[0]
    assert x_sample.shape[1] == 1 and l % CHUNK == 0 and l >= DIL_MAX and ns % DEC_TILE == 0
    assert cache_dil_k.shape[2] == DIL_MAX and cache_swa_k.shape[2] == SWA_WINDOW

    def perm_heads(w, axis, off):
        take = lambda lo, hi: lax.slice_in_dim(w, lo, hi, axis=axis)
        return [take(off + h * HEAD_DIM, off + (h + 1) * HEAD_DIM) for h in SWA_HEAD_PERM]

    take_in = lambda lo, hi: lax.slice_in_dim(w_in, lo, hi, axis=2)
    w_in_b = jnp.concatenate(
        [take_in(0, OFF_SWA)] + perm_heads(w_in, 2, OFF_SWA)
        + [take_in(OFF_SWA + W_SWA, OFF_SWA + 2 * W_SWA)] + perm_heads(w_in, 2, OFF_SWA + 2 * W_SWA),
        axis=2).astype(BF16)
    w_out_b = jnp.concatenate(
        [lax.slice_in_dim(w_out, 0, D_MIX - W_SWA, axis=1)] + perm_heads(w_out, 1, D_MIX - W_SWA),
        axis=1).astype(BF16)
    eye = jnp.eye(len(POOL_WINDOWS), dtype=F32)
    wbd = jnp.einsum("dgce,gh->dgche", w_pool, eye).reshape(depth, W_POOL, W_POOL).astype(BF16)
    sink_rows = jnp.zeros((depth, 8), F32).at[:, 0:N_HEADS_SWA].set(swa_sink[:, list(SWA_HEAD_PERM)])
    sink_rows = jnp.broadcast_to(sink_rows[:, :, None], (depth, 8, BLOCK))

    dil_bias, swa_bias = _prompt_bias_tables()
    dbias, mult, sbias = _sample_bias_tables(cache_dil_k.shape[2])

    mod = _ada(jnp.concatenate([c_prompt, c_sample], axis=0), w_ada, b_ada)
    fg = final_g.reshape(1, D_MODEL)

    xp = x_prompt
    xs = x_sample.reshape(ns, D_MODEL)
    kt = jnp.transpose(cache_dil_k, (0, 1, 3, 4, 2))
    vt = jnp.transpose(cache_dil_v, (0, 1, 3, 4, 2))
    ckt = jnp.transpose(cache_swa_k, (0, 1, 3, 4, 2))
    cvt = jnp.transpose(cache_swa_v, (0, 1, 3, 4, 2))
    sp_all = jnp.transpose(state_pool, (0, 2, 1, 3))
    sc_all = state_conv.reshape(depth, ns, CONV_BUF * W_CONV)
    pad_heads = lambda a: jnp.pad(a, ((0, 0), (0, 0), (0, 8 - a.shape[2]), (0, 0)))
    outs = [[] for _ in range(12)]
    for i in range(depth):
        final = i == depth - 1
        mod_p = mod[i, 0:nb].reshape(nb, 1, 3 * D_MODEL)
        mod_s = mod[i, nb:nb + ns]
        g = norm_g[i].reshape(1, D_MODEL)
        ps = pool_scale[i].reshape(1, W_POOL)

        q1, k1, v1, qm, km, vm, gm, kc, vc, yacd, pst, cst, skc, svc = _prompt_a(
            xp, mod_p, g, w_in_b[i], wbd[i], ps, conv_w[i], swa_sink[i], swa_bias)
        yb = _prompt_b(q1, k1, v1, qm, km, vm, gm, dil_bias)
        xp = _prompt_c(xp, mod_p, yacd, yb, w_out_b[i], fg, final)

        qkv, sw, yac, gates, pst_s, cst_s = _sample_a(
            xs, mod_s, g, w_in_b[i], wbd[i], ps, conv_w[i], sp_all[i], sc_all[i])
        dq3 = pad_heads(qkv.reshape(ns, 3, N_HEADS_DIL, HEAD_DIM))
        sq4 = sw[:, 0:W_SWA].reshape(ns, 1, N_HEADS_SWA, HEAD_DIM)
        skv = jnp.tile(sw[:, W_SWA:W_SWA + 2 * W_SWA_KV].reshape(ns, 2, 2, HEAD_DIM), (1, 1, 2, 1))
        sq3 = pad_heads(jnp.concatenate([sq4, skv], axis=1))
        bo, do = _sample_b(i, dq3, sq3, dbias, mult, sbias, sink_rows[i], kt, vt, ckt, cvt)
        bo = bo[:, 0:N_HEADS_DIL].reshape(ns, W_DIL)
        do = do[:, 0:N_HEADS_SWA].reshape(ns, W_SWA)
        xs = _sample_c(xs, mod_s, yac, gates, bo, do, w_out_b[i], fg, final)

        unfold = lambda a, h: jnp.transpose(a.reshape(nb, h, HEAD_DIM, a.shape[-1]), (0, 3, 1, 2))
        kc, vc, skc, svc = unfold(kc, N_HEADS_DIL), unfold(vc, N_HEADS_DIL), unfold(skc, 2), unfold(svc, 2)
        outs[0].append(pst)
        outs[1].append(jnp.transpose(pst_s, (1, 0, 2)))
        outs[2].append(kc)
        outs[3].append(vc)
        outs[4].append(qkv[:, W_DIL:2 * W_DIL].reshape(ns, 1, N_HEADS_DIL, HEAD_DIM))
        outs[5].append(qkv[:, 2 * W_DIL:3 * W_DIL].reshape(ns, 1, N_HEADS_DIL, HEAD_DIM))
        outs[6].append(cst)
        outs[7].append(cst_s.reshape(ns, CONV_BUF, W_CONV))
        outs[8].append(skc)
        outs[9].append(svc)
        outs[10].append(sw[:, W_SWA:W_SWA + W_SWA_KV].reshape(ns, 1, 2, HEAD_DIM))
        outs[11].append(sw[:, W_SWA + W_SWA_KV:W_SWA + 2 * W_SWA_KV].reshape(ns, 1, 2, HEAD_DIM))

    return (xp, xs.reshape(ns, 1, D_MODEL)) + tuple(jnp.stack(o) for o in outs)
```

```python
import functools
import math

import numpy as np
import jax
import jax.numpy as jnp
from jax import lax
from jax.experimental import pallas as pl
from jax.experimental.pallas import tpu as pltpu

F32 = jnp.float32
BF16 = jnp.bfloat16

D_MODEL = 1024
HEAD_DIM = 64
BLOCK = 128
POOL_WINDOWS = (2, 4, 8, 16)
POOL_GROUP = 64
W_POOL = 256
POOL_BUF = 15
DIL_CONFIGS = ((128, 1), (512, 4), (2048, 16))
DIL_MAX = 2048
N_HEADS_DIL = 6
N_PAIRS_DIL = 3
N_CLASSES = 16
W_DIL = 384
W_CONV = 256
CONV_BUF = 2
N_HEADS_SWA = 4
W_SWA = 256
W_SWA_KV = 128
SWA_WINDOW = 128
D_MIX = 1152
D_PROJ = 3840
RMS_EPS = 1e-6
QK_SCALE = 1.0 / math.sqrt(HEAD_DIM)

OFF_POOL = 0
OFF_DIL = 512
OFF_CONV = 2048
OFF_SWA = 3072
SWA_HEAD_PERM = (0, 3, 1, 2)

SEQ_TILE = 512
CHUNK = 2048
DEC_TILE = 1
VMEM_LIMIT = 56 * 1024 * 1024

NEG_INF = float("-inf")


def _silu(v):
    return v * jax.nn.sigmoid(v)


def _dot(a, b):
    return jnp.dot(a, b, preferred_element_type=F32)


def _dot_nt(a, b):
    return lax.dot_general(a, b, (((1,), (1,)), ((), ())), preferred_element_type=F32)


def _alibi_slopes(n):
    return [2.0 ** (-8.0 * (h + 1) / n) for h in range(n)]


def _band_bias(slope_lo, slope_hi, dist_scale, variant):
    qi = np.arange(BLOCK)[:, None]
    kj = np.arange(2 * BLOCK)[None, :]
    if variant == 1:
        off = qi - kj
    else:
        off = qi - kj + BLOCK
    valid = (off >= 0) & (off <= BLOCK)
    if variant == 2:
        valid = valid & (kj >= BLOCK)
    out = []
    for s in (slope_lo, slope_hi):
        b = np.where(valid, -(np.float32(s) * np.float32(dist_scale)) * off.astype(np.float32),
                     np.float32(NEG_INF))
        out.append(b.astype(np.float32))
    return np.concatenate(out, axis=0)


def _compiler_params(sem):
    return pltpu.CompilerParams(dimension_semantics=sem, vmem_limit_bytes=VMEM_LIMIT)


def _ada_kernel(c_ref, w_ref, b_ref, o_ref):
    s = _silu(c_ref[...]).astype(BF16)
    o_ref[0] = _dot(s, w_ref[0].astype(BF16)) + b_ref[0]


def _ada(c_all, w_ada_b, b_ada):
    depth = w_ada_b.shape[0]
    rows = c_all.shape[0]
    return pl.pallas_call(
        _ada_kernel,
        grid=(depth, 3),
        in_specs=[
            pl.BlockSpec((rows, D_MODEL), lambda i, j: (0, 0)),
            pl.BlockSpec((1, D_MODEL, D_MODEL), lambda i, j: (i, 0, j)),
            pl.BlockSpec((1, 1, D_MODEL), lambda i, j: (i, 0, j)),
        ],
        out_specs=pl.BlockSpec((1, rows, D_MODEL), lambda i, j: (i, 0, j)),
        out_shape=jax.ShapeDtypeStruct((depth, rows, 3 * D_MODEL), F32),
        compiler_params=_compiler_params(("arbitrary", "arbitrary")),
        name="ada",
    )(c_all, w_ada_b, b_ada.reshape(depth, 1, 3 * D_MODEL))


def _modulated_norm(x, g, mod_row):
    ms = jnp.mean(x * x, axis=-1, keepdims=True)
    y = x * lax.rsqrt(ms + RMS_EPS) * g
    shift = mod_row[:, 0:D_MODEL]
    scale = mod_row[:, D_MODEL:2 * D_MODEL]
    return y * (1.0 + scale) + shift


def _pool_select(sums, lane):
    grp = lane // POOL_GROUP
    sel = jnp.where(grp == 0, sums[2], jnp.where(grp == 1, sums[4],
                                                 jnp.where(grp == 2, sums[8], sums[16])))
    win = jnp.where(grp == 0, 2, jnp.where(grp == 1, 4, jnp.where(grp == 2, 8, 16)))
    return sel, win


def _pa_kernel(sink_ref, x_ref, mod_ref, g_ref, w_ref, wbd_ref, ps_ref, cw_ref, bias_ref,
               dq_ref, sq_ref, dbias_ref, mult_ref, sbias_ref, ssink_ref, kt_ref, vt_ref, ck_ref, cv_ref,
               q1_ref, k1_ref, v1_ref, qm_ref, km_ref, vm_ref, gm_ref, kc_ref, vc_ref, y_ref,
               pst_ref, cst_ref, skc_ref, svc_ref, bo_ref, do_ref,
               uext, zcext, kext, vext, zds):
    tl = SEQ_TILE
    t = pl.program_id(1)

    @pl.when(t == 0)
    def _():
        uext[0:16, :] = jnp.zeros((16, W_POOL), F32)
        zcext[0:8, :] = jnp.zeros((8, W_CONV), F32)
        kext[0:BLOCK, :] = jnp.zeros((BLOCK, W_SWA_KV), BF16)
        vext[0:BLOCK, :] = jnp.zeros((BLOCK, W_SWA_KV), BF16)

    hb = _modulated_norm(x_ref[0], g_ref[...], mod_ref[0]).astype(BF16)

    zp = _dot(hb, w_ref[:, OFF_POOL:OFF_POOL + 2 * W_POOL])
    pu = zp[:, 0:W_POOL]
    pg = zp[:, W_POOL:2 * W_POOL]
    uext[16:16 + tl, :] = pu
    acc = pu
    sums = {}
    for j in range(1, 16):
        acc = acc + uext[16 - j:16 - j + tl, :]
        if j + 1 in POOL_WINDOWS:
            sums[j + 1] = acc
    lane = lax.broadcasted_iota(jnp.int32, (tl, W_POOL), 1)
    gpos = lax.broadcasted_iota(jnp.int32, (tl, W_POOL), 0) + t * tl
    sel, win = _pool_select(sums, lane)
    cnt = jnp.minimum(gpos + 1, win).astype(F32)
    diff = sel / cnt - pu
    a_out = _dot(diff.astype(BF16), wbd_ref[...]) * ps_ref[...]
    y_ref[0, :, 0:W_POOL] = (a_out * _silu(pg)).astype(BF16)
    pst_ref[0] = uext[tl + 1:tl + 16, :]
    uext[0:16, :] = uext[tl:tl + 16, :]

    zc4 = _dot(hb, w_ref[:, OFF_CONV:OFF_CONV + 4 * W_CONV])
    ch = zc4[:, 0:W_CONV]
    cb = zc4[:, W_CONV:2 * W_CONV]
    cc = zc4[:, 2 * W_CONV:3 * W_CONV]
    cg = zc4[:, 3 * W_CONV:4 * W_CONV]
    zc = cc * ch
    zcext[8:8 + tl, :] = zc
    conv = (cw_ref[0:1, :] * zcext[6:6 + tl, :] + cw_ref[1:2, :] * zcext[7:7 + tl, :]
            + cw_ref[2:3, :] * zc)
    y_ref[0, :, W_POOL:W_POOL + W_CONV] = (cb * conv * _silu(cg)).astype(BF16)
    cst_ref[0] = zcext[tl + 6:tl + 8, :]
    zcext[0:8, :] = zcext[tl:tl + 8, :]

    zd = _dot(hb, w_ref[:, OFF_DIL:OFF_DIL + 4 * W_DIL])
    for hp in range(N_PAIRS_DIL):
        lo = hp * BLOCK
        zds[0] = zd[:, lo:lo + BLOCK] * QK_SCALE
        zds[1] = zd[:, W_DIL + lo:W_DIL + lo + BLOCK]
        zds[2] = zd[:, 2 * W_DIL + lo:2 * W_DIL + lo + BLOCK]
        zds[3] = _silu(zd[:, 3 * W_DIL + lo:3 * W_DIL + lo + BLOCK])
        q1_ref[0, hp] = zds[0].astype(BF16)
        k1_ref[0, hp] = zds[1].astype(BF16)
        v1_ref[0, hp] = zds[2].astype(BF16)
        for c in range(N_CLASSES):
            rows = pl.ds(c, tl // N_CLASSES, stride=N_CLASSES)
            qm_ref[0, hp, c] = zds[0, rows, :].astype(BF16)
            km_ref[0, hp, c] = zds[1, rows, :].astype(BF16)
            vm_ref[0, hp, c] = zds[2, rows, :].astype(BF16)
            gm_ref[0, hp, c] = zds[3, rows, :].astype(BF16)
    kc_ref[0] = zd[:, W_DIL:2 * W_DIL].T
    vc_ref[0] = zd[:, 2 * W_DIL:3 * W_DIL].T

    zs = _dot(hb, w_ref[:, OFF_SWA:OFF_SWA + 3 * W_SWA])
    sk = zs[:, W_SWA:W_SWA + W_SWA_KV]
    sv = zs[:, W_SWA + W_SWA_KV:W_SWA + 2 * W_SWA_KV]
    kext[BLOCK:BLOCK + tl, :] = sk.astype(BF16)
    vext[BLOCK:BLOCK + tl, :] = sv.astype(BF16)
    skc_ref[0] = sk[tl - SWA_WINDOW:tl, :].T
    svc_ref[0] = sv[tl - SWA_WINDOW:tl, :].T
    first = jnp.where(t == 0, 1, 0)
    lane_q = lax.broadcasted_iota(jnp.int32, (BLOCK, BLOCK), 1)
    row_s = lax.broadcasted_iota(jnp.int32, (2 * BLOCK, 1), 0)
    for jb in range(tl // BLOCK):
        r0 = jb * BLOCK
        kblk = kext[r0:r0 + 2 * BLOCK, :]
        vblk = vext[r0:r0 + 2 * BLOCK, :]
        for grp in range(2):
            q = (zs[r0:r0 + BLOCK, grp * BLOCK:(grp + 1) * BLOCK] * QK_SCALE).astype(BF16)
            lhs = jnp.concatenate([jnp.where(lane_q < HEAD_DIM, q, jnp.zeros_like(q)),
                                   jnp.where(lane_q >= HEAD_DIM, q, jnp.zeros_like(q))], axis=0)
            s = _dot_nt(lhs, kblk)
            if jb == 0:
                s = s + bias_ref[grp, first]
            else:
                s = s + bias_ref[grp, 0]
            sink = jnp.where(row_s < BLOCK, sink_ref[SWA_HEAD_PERM[2 * grp]],
                             sink_ref[SWA_HEAD_PERM[2 * grp + 1]])
            m = jnp.maximum(jnp.max(s, axis=-1, keepdims=True), sink)
            p = jnp.exp(s - m)
            den = jnp.sum(p, axis=-1, keepdims=True) + jnp.exp(sink - m)
            o = _dot(p.astype(BF16), vblk) / den
            od = jnp.where(lane_q < HEAD_DIM, o[0:BLOCK], o[BLOCK:2 * BLOCK])
            sg = zs[r0:r0 + BLOCK, 2 * W_SWA + grp * BLOCK:2 * W_SWA + (grp + 1) * BLOCK]
            c0 = W_POOL + W_CONV + grp * BLOCK
            y_ref[0, r0:r0 + BLOCK, c0:c0 + BLOCK] = (od * _silu(sg)).astype(BF16)
    kext[0:BLOCK, :] = kext[tl:tl + BLOCK, :]
    vext[0:BLOCK, :] = vext[tl:tl + BLOCK, :]

    _cache_attention(dq_ref, sq_ref, dbias_ref, mult_ref, sbias_ref, ssink_ref, kt_ref, vt_ref,
                     ck_ref, cv_ref, bo_ref, do_ref)


def _cache_attention_specs(layer, row0, nb, nt, r, swr):
    bt = DEC_TILE
    blk0 = row0 // bt
    full = lambda *shape: pl.BlockSpec(shape, lambda n, t: (0,) * len(shape))
    srow_spec = pl.BlockSpec((bt, 3, 8, HEAD_DIM), lambda n, t: (blk0 + n * nt + t, 0, 0, 0))
    dil_spec = pl.BlockSpec((1, bt, N_HEADS_DIL, HEAD_DIM, r),
                            lambda n, t: (layer, blk0 + n * nt + t, 0, 0, 0))
    swa_spec = pl.BlockSpec((1, bt, 2, HEAD_DIM, swr), lambda n, t: (layer, blk0 + n * nt + t, 0, 0, 0))
    sout_spec = pl.BlockSpec((bt, 8, HEAD_DIM), lambda n, t: (n * nt + t, 0, 0))
    sout_shape = jax.ShapeDtypeStruct((nb * nt * bt, 8, HEAD_DIM), F32)
    in_specs = [srow_spec, srow_spec, full(8, r), full(1, r), full(8, swr), full(8, BLOCK),
                dil_spec, dil_spec, swa_spec, swa_spec]
    return in_specs, [sout_spec, sout_spec], [sout_shape, sout_shape]


def _prompt_a(x, mod_p, norm_g, w_in_b, wbd, pool_scale, conv_w, sink, swa_bias, layer, row0, cache_args):
    nb, l, _ = x.shape
    tl = SEQ_TILE
    nt = l // tl
    cache_t0 = (l - DIL_MAX) // tl
    c_in, c_out, c_shape = _cache_attention_specs(layer, row0, nb, nt, cache_args[6].shape[-1],
                                                  cache_args[8].shape[-1])
    pair_spec = pl.BlockSpec((1, N_PAIRS_DIL, tl, BLOCK), lambda n, t: (n, 0, t, 0))
    pair_shape = jax.ShapeDtypeStruct((nb, N_PAIRS_DIL, l, BLOCK), BF16)
    cm_spec = pl.BlockSpec((1, N_PAIRS_DIL, N_CLASSES, tl // N_CLASSES, BLOCK),
                           lambda n, t: (n, 0, 0, t, 0))
    cm_shape = jax.ShapeDtypeStruct((nb, N_PAIRS_DIL, N_CLASSES, l // N_CLASSES, BLOCK), BF16)
    cache_spec = pl.BlockSpec((1, W_DIL, tl), lambda n, t: (n, 0, jnp.maximum(t - cache_t0, 0)))
    cache_shape = jax.ShapeDtypeStruct((nb, W_DIL, DIL_MAX), F32)
    full = lambda *shape: pl.BlockSpec(shape, lambda n, t: (0,) * len(shape))
    return pl.pallas_call(
        _pa_kernel,
        grid=(nb, nt),
        in_specs=[
            pl.BlockSpec(memory_space=pltpu.SMEM),
            pl.BlockSpec((1, tl, D_MODEL), lambda n, t: (n, t, 0)),
            pl.BlockSpec((1, 1, 3 * D_MODEL), lambda n, t: (n, 0, 0)),
            full(1, D_MODEL),
            pl.BlockSpec((D_MODEL, D_PROJ), lambda n, t: (0, 0), pipeline_mode=pl.Buffered(1)),
            full(W_POOL, W_POOL),
            full(1, W_POOL),
            full(3, W_CONV),
            full(2, 2, 2 * BLOCK, 2 * BLOCK),
        ] + c_in,
        out_specs=[
            pair_spec, pair_spec, pair_spec, cm_spec, cm_spec, cm_spec, cm_spec,
            cache_spec, cache_spec,
            pl.BlockSpec((1, tl, 3 * W_POOL), lambda n, t: (n, t, 0)),
            pl.BlockSpec((1, POOL_BUF, W_POOL), lambda n, t: (n, 0, 0)),
            pl.BlockSpec((1, CONV_BUF, W_CONV), lambda n, t: (n, 0, 0)),
            pl.BlockSpec((1, SWA_WINDOW, W_SWA_KV), lambda n, t: (n, 0, 0)),
            pl.BlockSpec((1, SWA_WINDOW, W_SWA_KV), lambda n, t: (n, 0, 0)),
        ] + c_out,
        out_shape=[
            pair_shape, pair_shape, pair_shape, cm_shape, cm_shape, cm_shape, cm_shape,
            cache_shape, cache_shape,
            jax.ShapeDtypeStruct((nb, l, 3 * W_POOL), BF16),
            jax.ShapeDtypeStruct((nb, POOL_BUF, W_POOL), F32),
            jax.ShapeDtypeStruct((nb, CONV_BUF, W_CONV), F32),
            jax.ShapeDtypeStruct((nb, SWA_WINDOW, W_SWA_KV), F32),
            jax.ShapeDtypeStruct((nb, SWA_WINDOW, W_SWA_KV), F32),
        ] + c_shape,
        scratch_shapes=[
            pltpu.VMEM((16 + tl, W_POOL), F32),
            pltpu.VMEM((8 + tl, W_CONV), F32),
            pltpu.VMEM((BLOCK + tl, W_SWA_KV), BF16),
            pltpu.VMEM((BLOCK + tl, W_SWA_KV), BF16),
            pltpu.VMEM((4, tl, BLOCK), F32),
        ],
        compiler_params=_compiler_params(("arbitrary", "arbitrary")),
        name="prompt_a",
    )(sink, x, mod_p, norm_g, w_in_b, wbd, pool_scale, conv_w, swa_bias, *cache_args)


def _band_pair(q, kblk, vblk, bias, lane_q):
    lhs = jnp.concatenate([jnp.where(lane_q < HEAD_DIM, q, jnp.zeros_like(q)),
                           jnp.where(lane_q >= HEAD_DIM, q, jnp.zeros_like(q))], axis=0)
    s = _dot_nt(lhs, kblk) + bias
    m = jnp.max(s, axis=-1, keepdims=True)
    p = jnp.exp(s - m)
    den = jnp.sum(p, axis=-1, keepdims=True)
    o = _dot(p.astype(BF16), vblk) / den
    lse = m + jnp.log(den)
    od = jnp.where(lane_q < HEAD_DIM, o[0:BLOCK], o[BLOCK:2 * BLOCK])
    ld = jnp.where(lane_q < HEAD_DIM, lse[0:BLOCK], lse[BLOCK:2 * BLOCK])
    return od, ld


def _pb_kernel(bias_ref, q1, k1, v1, qm, km, vm, gm, outm, o1s, l1s, o4s, l4s, o16s, l16s):
    c = pl.program_id(2)
    first = jnp.where(c == 0, 1, 0)
    lane_q = lax.broadcasted_iota(jnp.int32, (BLOCK, BLOCK), 1)
    sub = BLOCK // 4

    for j in range(CHUNK // BLOCK):
        r0 = j * BLOCK
        if j == 0:
            start = jnp.maximum(c * (CHUNK // BLOCK) - 1, 0) * BLOCK
            bias = bias_ref[0, 0, first]
        else:
            start = (c * (CHUNK // BLOCK) + (j - 1)) * BLOCK
            bias = bias_ref[0, 0, 0]
        start = pl.multiple_of(start, BLOCK)
        od, ld = _band_pair(q1[0, 0, r0:r0 + BLOCK, :], k1[0, 0, pl.ds(start, 2 * BLOCK), :],
                            v1[0, 0, pl.ds(start, 2 * BLOCK), :], bias, lane_q)
        o1s[0, r0:r0 + BLOCK, :] = od
        l1s[0, r0:r0 + BLOCK, :] = ld

    for c4 in range(4):
        for j in range(4):
            i0 = j * sub
            if j == 0:
                istart = jnp.maximum(c * (CHUNK // N_CLASSES) - sub, 0)
                bias = bias_ref[0, 1, first]
            else:
                istart = c * (CHUNK // N_CLASSES) + i0 - sub
                bias = bias_ref[0, 1, 0]
            istart = pl.multiple_of(istart, sub)
            classes = [4 * cc + c4 for cc in range(4)]
            q = jnp.concatenate([qm[0, 0, cl, i0:i0 + sub, :] for cl in classes], axis=0)
            kblk = jnp.concatenate([km[0, 0, cl, pl.ds(istart, 2 * sub), :] for cl in classes], axis=0)
            vblk = jnp.concatenate([vm[0, 0, cl, pl.ds(istart, 2 * sub), :] for cl in classes], axis=0)
            od, ld = _band_pair(q, kblk, vblk, bias, lane_q)
            for cc, cl in enumerate(classes):
                o4s[cl, i0:i0 + sub, :] = od[cc * sub:(cc + 1) * sub]
                l4s[cl, i0:i0 + sub, :] = ld[cc * sub:(cc + 1) * sub]

    start = pl.multiple_of(jnp.maximum(c - 1, 0) * BLOCK, BLOCK)
    for cl in range(N_CLASSES):
        od, ld = _band_pair(qm[0, 0, cl], km[0, 0, cl, pl.ds(start, 2 * BLOCK), :],
                            vm[0, 0, cl, pl.ds(start, 2 * BLOCK), :], bias_ref[0, 2, first], lane_q)
        o16s[cl] = od
        l16s[cl] = ld

    for cl in range(N_CLASSES):
        oa = o1s[0, pl.ds(cl, BLOCK, stride=N_CLASSES), :]
        la = l1s[0, pl.ds(cl, BLOCK, stride=N_CLASSES), :]
        ob = o4s[cl]
        lb = l4s[cl]
        oc = o16s[cl]
        lc = l16s[cl]
        mx = jnp.maximum(jnp.maximum(la, lb), lc)
        wa = jnp.exp(la - mx)
        wb = jnp.exp(lb - mx)
        wc = jnp.exp(lc - mx)
        mix = (wa * oa + wb * ob + wc * oc) / (wa + wb + wc)
        outm[0, 0, cl] = (mix * gm[0, 0, cl].astype(F32)).astype(BF16)


def _prompt_b(q1, k1, v1, qm, km, vm, gm, dil_bias):
    nb, npair, l, _ = q1.shape
    nc = l // CHUNK
    li = l // N_CLASSES
    ci = CHUNK // N_CLASSES
    q1_spec = pl.BlockSpec((1, 1, CHUNK, BLOCK), lambda n, h, c: (n, h, c, 0))
    kv1_spec = pl.BlockSpec((1, 1, l, BLOCK), lambda n, h, c: (n, h, 0, 0))
    cm_spec = pl.BlockSpec((1, 1, N_CLASSES, ci, BLOCK), lambda n, h, c: (n, h, 0, c, 0))
    kvm_spec = pl.BlockSpec((1, 1, N_CLASSES, li, BLOCK), lambda n, h, c: (n, h, 0, 0, 0))
    cm_scratch = pltpu.VMEM((N_CLASSES, ci, BLOCK), F32)
    return pl.pallas_call(
        _pb_kernel,
        grid=(nb, npair, nc),
        in_specs=[pl.BlockSpec((1, 3, 2, 2 * BLOCK, 2 * BLOCK), lambda n, h, c: (h, 0, 0, 0, 0)),
                  q1_spec, kv1_spec, kv1_spec, cm_spec, kvm_spec, kvm_spec, cm_spec],
        out_specs=cm_spec,
        out_shape=jax.ShapeDtypeStruct((nb, npair, N_CLASSES, li, BLOCK), BF16),
        scratch_shapes=[
            pltpu.VMEM((1, CHUNK, BLOCK), F32), pltpu.VMEM((1, CHUNK, BLOCK), F32),
            cm_scratch, cm_scratch, cm_scratch, cm_scratch,
        ],
        compiler_params=_compiler_params(("arbitrary", "arbitrary", "arbitrary")),
        name="prompt_b",
    )(dil_bias, q1, k1, v1, qm, km, vm, gm)


def _mix_out(y, x, gate, w_ref, fg_ref, final):
    xn = x + gate * _dot(y, w_ref[...])
    if final:
        ms = jnp.mean(xn * xn, axis=-1, keepdims=True)
        xn = xn * lax.rsqrt(ms + RMS_EPS) * fg_ref[...]
    return xn


def _pc_kernel(x_ref, mod_ref, yacd_ref, yb_ref, w_ref, fg_ref,
               dq_ref, sq_ref, dbias_ref, mult_ref, sbias_ref, ssink_ref, kt_ref, vt_ref, ck_ref, cv_ref,
               o_ref, bo_ref, do_ref, ybs, *, final):
    _cache_attention(dq_ref, sq_ref, dbias_ref, mult_ref, sbias_ref, ssink_ref, kt_ref, vt_ref,
                     ck_ref, cv_ref, bo_ref, do_ref)
    for hp in range(N_PAIRS_DIL):
        for c in range(N_CLASSES):
            ybs[hp, pl.ds(c, SEQ_TILE // N_CLASSES, stride=N_CLASSES), :] = yb_ref[0, hp, c].astype(F32)
    yacd = yacd_ref[0]
    y = jnp.concatenate([yacd[:, 0:W_POOL], ybs[0].astype(BF16), ybs[1].astype(BF16),
                         ybs[2].astype(BF16), yacd[:, W_POOL:3 * W_POOL]], axis=-1)
    gate = mod_ref[0][:, 2 * D_MODEL:3 * D_MODEL]
    o_ref[0] = _mix_out(y, x_ref[0], gate, w_ref, fg_ref, final)


def _prompt_c(x, mod_p, yacd, yb, w_out_b, final_g, final, layer, row0, cache_args):
    nb, l, _ = x.shape
    tl = SEQ_TILE
    full = lambda *shape: pl.BlockSpec(shape, lambda n, t: (0,) * len(shape))
    c_in, c_out, c_shape = _cache_attention_specs(layer, row0, nb, l // tl, cache_args[6].shape[-1],
                                                  cache_args[8].shape[-1])
    return pl.pallas_call(
        functools.partial(_pc_kernel, final=final),
        grid=(nb, l // tl),
        in_specs=[
            pl.BlockSpec((1, tl, D_MODEL), lambda n, t: (n, t, 0)),
            pl.BlockSpec((1, 1, 3 * D_MODEL), lambda n, t: (n, 0, 0)),
            pl.BlockSpec((1, tl, 3 * W_POOL), lambda n, t: (n, t, 0)),
            pl.BlockSpec((1, N_PAIRS_DIL, N_CLASSES, tl // N_CLASSES, BLOCK),
                         lambda n, t: (n, 0, 0, t, 0)),
            full(D_MIX, D_MODEL),
            full(1, D_MODEL),
        ] + c_in,
        out_specs=[pl.BlockSpec((1, tl, D_MODEL), lambda n, t: (n, t, 0))] + c_out,
        out_shape=[jax.ShapeDtypeStruct((nb, l, D_MODEL), F32)] + c_shape,
        scratch_shapes=[pltpu.VMEM((N_PAIRS_DIL, tl, BLOCK), F32)],
        compiler_params=_compiler_params(("arbitrary", "arbitrary")),
        name="prompt_c",
    )(x, mod_p, yacd, yb, w_out_b, final_g, *cache_args)


def _sa_kernel(x_ref, mod_ref, g_ref, w_ref, wbd_ref, ps_ref, cw_ref, sp_ref, sc_ref,
               qkv_ref, sw_ref, yac_ref, gates_ref, pst_ref, cst_ref):
    hb = _modulated_norm(x_ref[...], g_ref[...], mod_ref[...]).astype(BF16)
    ns = hb.shape[0]

    zp = _dot(hb, w_ref[:, OFF_POOL:OFF_POOL + 2 * W_POOL])
    pu = zp[:, 0:W_POOL]
    pg = zp[:, W_POOL:2 * W_POOL]
    acc = pu
    sums = {}
    for j in range(1, 16):
        acc = acc + sp_ref[POOL_BUF - j]
        if j + 1 in POOL_WINDOWS:
            sums[j + 1] = acc
    lane = lax.broadcasted_iota(jnp.int32, (ns, W_POOL), 1)
    sel, win = _pool_select(sums, lane)
    diff = sel / win.astype(F32) - pu
    a_out = _dot(diff.astype(BF16), wbd_ref[...]) * ps_ref[...]
    yac_ref[:, 0:W_POOL] = a_out * _silu(pg)
    pst_ref[0:POOL_BUF - 1] = sp_ref[1:POOL_BUF]
    pst_ref[POOL_BUF - 1] = pu

    zc4 = _dot(hb, w_ref[:, OFF_CONV:OFF_CONV + 4 * W_CONV])
    ch = zc4[:, 0:W_CONV]
    cb = zc4[:, W_CONV:2 * W_CONV]
    cc = zc4[:, 2 * W_CONV:3 * W_CONV]
    cg = zc4[:, 3 * W_CONV:4 * W_CONV]
    zc = cc * ch
    conv = (cw_ref[0:1, :] * sc_ref[:, 0:W_CONV] + cw_ref[1:2, :] * sc_ref[:, W_CONV:2 * W_CONV]
            + cw_ref[2:3, :] * zc)
    yac_ref[:, W_POOL:W_POOL + W_CONV] = cb * conv * _silu(cg)
    cst_ref[:, 0:W_CONV] = sc_ref[:, W_CONV:2 * W_CONV]
    cst_ref[:, W_CONV:2 * W_CONV] = zc

    zd = _dot(hb, w_ref[:, OFF_DIL:OFF_DIL + 4 * W_DIL])
    qkv_ref[:, 0:W_DIL] = zd[:, 0:W_DIL] * QK_SCALE
    qkv_ref[:, W_DIL:3 * W_DIL] = zd[:, W_DIL:3 * W_DIL]
    gates_ref[:, 0:W_DIL] = _silu(zd[:, 3 * W_DIL:4 * W_DIL])

    zs = _dot(hb, w_ref[:, OFF_SWA:OFF_SWA + 3 * W_SWA])
    sw_ref[:, 0:W_SWA] = zs[:, 0:W_SWA] * QK_SCALE
    sw_ref[:, W_SWA:2 * W_SWA] = zs[:, W_SWA:2 * W_SWA]
    gates_ref[:, W_DIL:W_DIL + W_SWA] = _silu(zs[:, 2 * W_SWA:3 * W_SWA])


def _sample_a(xs, mod_s, norm_g, w_in_b, wbd, pool_scale, conv_w, sp, sc):
    ns = xs.shape[0]
    shapes = [
        jax.ShapeDtypeStruct((ns, 3 * W_DIL), F32),
        jax.ShapeDtypeStruct((ns, 2 * W_SWA), F32),
        jax.ShapeDtypeStruct((ns, W_POOL + W_CONV), F32),
        jax.ShapeDtypeStruct((ns, W_DIL + W_SWA), F32),
        jax.ShapeDtypeStruct((POOL_BUF, ns, W_POOL), F32),
        jax.ShapeDtypeStruct((ns, CONV_BUF * W_CONV), F32),
    ]
    return pl.pallas_call(
        _sa_kernel,
        out_shape=shapes,
        compiler_params=pltpu.CompilerParams(vmem_limit_bytes=VMEM_LIMIT),
        name="sample_a",
    )(xs, mod_s, norm_g, w_in_b, wbd, pool_scale, conv_w, sp, sc)


def _cache_attention(dq_ref, sq_ref, dbias_ref, mult_ref, sbias_ref, sink_ref, kt_ref, vt_ref,
                     ck_ref, cv_ref, bo_ref, do_ref):
    row_d = lax.broadcasted_iota(jnp.int32, (8, DIL_MAX), 0)
    row_o = lax.broadcasted_iota(jnp.int32, (8, HEAD_DIM), 0)
    row_s = lax.broadcasted_iota(jnp.int32, (8, SWA_WINDOW), 0)
    for j in range(DEC_TILE):
        q = dq_ref[j, 0]
        qb = q.astype(BF16)
        s = jnp.zeros((8, DIL_MAX), F32)
        for h in range(N_HEADS_DIL):
            s = jnp.where(row_d == h, _dot(qb, kt_ref[0, j, h].astype(BF16)), s)
        s = s + dbias_ref[...]
        s_self = jnp.sum(q * dq_ref[j, 1], axis=-1, keepdims=True)
        m = jnp.maximum(jnp.max(s, axis=-1, keepdims=True), s_self)
        p = jnp.exp(s - m) * mult_ref[...]
        p_self = float(len(DIL_CONFIGS)) * jnp.exp(s_self - m)
        den = jnp.sum(p, axis=-1, keepdims=True) + p_self
        pb = p.astype(BF16)
        acc = jnp.zeros((8, HEAD_DIM), F32)
        for h in range(N_HEADS_DIL):
            acc = jnp.where(row_o == h, _dot_nt(pb, vt_ref[0, j, h].astype(BF16)), acc)
        bo_ref[j] = (acc + p_self * dq_ref[j, 2]) / den

        q = sq_ref[j, 0]
        qb = q.astype(BF16)
        s = jnp.where(row_s % 2 == 0, _dot(qb, ck_ref[0, j, 0].astype(BF16)),
                      _dot(qb, ck_ref[0, j, 1].astype(BF16))) + sbias_ref[...]
        w_self = jnp.sum(q * sq_ref[j, 1], axis=-1, keepdims=True)
        sink = sink_ref[...][:, 0:1]
        m = jnp.maximum(jnp.maximum(jnp.max(s, axis=-1, keepdims=True), w_self), sink)
        p = jnp.exp(s - m)
        pw = jnp.exp(w_self - m)
        den = jnp.sum(p, axis=-1, keepdims=True) + pw + jnp.exp(sink - m)
        pb = p.astype(BF16)
        acc = jnp.where(row_o % 2 == 0, _dot_nt(pb, cv_ref[0, j, 0].astype(BF16)),
                        _dot_nt(pb, cv_ref[0, j, 1].astype(BF16)))
        do_ref[j] = (acc + pw * sq_ref[j, 2]) / den


def _sc_kernel(x_ref, mod_ref, yac_ref, gates_ref, bo_ref, do_ref, w_ref, fg_ref, o_ref, *, final):
    yac = yac_ref[...]
    gates = gates_ref[...]
    y = jnp.concatenate([yac[:, 0:W_POOL], bo_ref[...] * gates[:, 0:W_DIL],
                         yac[:, W_POOL:W_POOL + W_CONV], do_ref[...] * gates[:, W_DIL:W_DIL + W_SWA]],
                        axis=-1).astype(BF16)
    gate = mod_ref[...][:, 2 * D_MODEL:3 * D_MODEL]
    o_ref[...] = _mix_out(y, x_ref[...], gate, w_ref, fg_ref, final)


def _sample_c(xs, mod_s, yac, gates, bo, do, w_out_b, final_g, final):
    return pl.pallas_call(
        functools.partial(_sc_kernel, final=final),
        out_shape=jax.ShapeDtypeStruct(xs.shape, F32),
        compiler_params=pltpu.CompilerParams(vmem_limit_bytes=VMEM_LIMIT),
        name="sample_c",
    )(xs, mod_s, yac, gates, bo, do, w_out_b, final_g)


def _band_bias_dil4(slope_lo, slope_hi, variant):
    sub = BLOCK // 4
    qidx = np.arange(BLOCK)[:, None]
    kidx = np.arange(2 * BLOCK)[None, :]
    q_step = 4 * (qidx % sub + (sub if variant == 0 else 0)) + qidx // sub
    k_step = 4 * (kidx % (2 * sub)) + kidx // (2 * sub)
    off = q_step - k_step
    valid = (off >= 0) & (off <= BLOCK)
    out = []
    for s in (slope_lo, slope_hi):
        out.append(np.where(valid, -(np.float32(s) * np.float32(4)) * off.astype(np.float32),
                            np.float32(NEG_INF)).astype(np.float32))
    return np.concatenate(out, axis=0)


def _prompt_bias_tables():
    dil = _alibi_slopes(N_HEADS_DIL)

    def table(hp, d, var):
        if d == 4:
            return _band_bias_dil4(dil[2 * hp], dil[2 * hp + 1], var)
        return _band_bias(dil[2 * hp], dil[2 * hp + 1], d, var)

    dil_bias = np.stack([
        np.stack([np.stack([table(hp, d, var) for var in (0, 1)])
                  for _, d in DIL_CONFIGS]) for hp in range(N_PAIRS_DIL)])
    swa = _alibi_slopes(N_HEADS_SWA)
    swa_bias = np.stack([
        np.stack([_band_bias(swa[SWA_HEAD_PERM[2 * g]], swa[SWA_HEAD_PERM[2 * g + 1]], 1, var)
                  for var in (0, 2)]) for g in range(2)])
    return jnp.asarray(dil_bias), jnp.asarray(swa_bias)


def _sample_bias_tables(r):
    dil = _alibi_slopes(N_HEADS_DIL)
    dist = (r - np.arange(r)).astype(np.float32)
    mult = np.zeros((1, r), np.float32)
    for window, d in DIL_CONFIGS:
        mult[0] += ((dist <= window) & (dist % d == 0)).astype(np.float32)
    dbias = np.zeros((8, r), np.float32)
    for h in range(N_HEADS_DIL):
        dbias[h] = np.where(mult[0] > 0, -np.float32(dil[h]) * dist, np.float32(NEG_INF))
    swa = _alibi_slopes(N_HEADS_SWA)
    sdist = (SWA_WINDOW - np.arange(SWA_WINDOW)).astype(np.float32)
    sbias = np.zeros((8, SWA_WINDOW), np.float32)
    for row, h in enumerate(SWA_HEAD_PERM):
        sbias[row] = -np.float32(swa[h]) * sdist
    return jnp.asarray(dbias), jnp.asarray(mult), jnp.asarray(sbias)


def kernel(x_prompt, x_sample, c_prompt, c_sample, state_pool, cache_dil_k, cache_dil_v, state_conv, cache_swa_k, cache_swa_v, norm_g, w_ada, b_ada, w_in, w_pool, pool_scale, conv_w, swa_sink, w_out, final_g):
    depth = w_in.shape[0]
    nb, l, _ = x_prompt.shape
    ns = x_sample.shape[0]
    assert x_sample.shape[1] == 1 and l % CHUNK == 0 and l >= DIL_MAX
    assert ns == 2 * nb * (l // SEQ_TILE) * DEC_TILE
    assert cache_dil_k.shape[2] == DIL_MAX and cache_swa_k.shape[2] == SWA_WINDOW

    def perm_heads(w, axis, off):
        take = lambda lo, hi: lax.slice_in_dim(w, lo, hi, axis=axis)
        return [take(off + h * HEAD_DIM, off + (h + 1) * HEAD_DIM) for h in SWA_HEAD_PERM]

    take_in = lambda lo, hi: lax.slice_in_dim(w_in, lo, hi, axis=2)
    w_in_b = jnp.concatenate(
        [take_in(0, OFF_SWA)] + perm_heads(w_in, 2, OFF_SWA)
        + [take_in(OFF_SWA + W_SWA, OFF_SWA + 2 * W_SWA)] + perm_heads(w_in, 2, OFF_SWA + 2 * W_SWA),
        axis=2).astype(BF16)
    w_out_b = jnp.concatenate(
        [lax.slice_in_dim(w_out, 0, D_MIX - W_SWA, axis=1)] + perm_heads(w_out, 1, D_MIX - W_SWA),
        axis=1).astype(BF16)
    eye = jnp.eye(len(POOL_WINDOWS), dtype=F32)
    wbd = jnp.einsum("dgce,gh->dgche", w_pool, eye).reshape(depth, W_POOL, W_POOL).astype(BF16)
    sink_rows = jnp.zeros((depth, 8), F32).at[:, 0:N_HEADS_SWA].set(swa_sink[:, list(SWA_HEAD_PERM)])
    sink_rows = jnp.broadcast_to(sink_rows[:, :, None], (depth, 8, BLOCK))

    dil_bias, swa_bias = _prompt_bias_tables()
    dbias, mult, sbias = _sample_bias_tables(cache_dil_k.shape[2])

    mod = _ada(jnp.concatenate([c_prompt, c_sample], axis=0), w_ada, b_ada)
    fg = final_g.reshape(1, D_MODEL)

    xp = x_prompt
    xs = x_sample.reshape(ns, D_MODEL)
    kt = jnp.transpose(cache_dil_k, (0, 1, 3, 4, 2))
    vt = jnp.transpose(cache_dil_v, (0, 1, 3, 4, 2))
    ckt = jnp.transpose(cache_swa_k, (0, 1, 3, 4, 2))
    cvt = jnp.transpose(cache_swa_v, (0, 1, 3, 4, 2))
    sp_all = jnp.transpose(state_pool, (0, 2, 1, 3))
    sc_all = state_conv.reshape(depth, ns, CONV_BUF * W_CONV)
    pad_heads = lambda a: jnp.pad(a, ((0, 0), (0, 0), (0, 8 - a.shape[2]), (0, 0)))
    outs = [[] for _ in range(12)]
    for i in range(depth):
        final = i == depth - 1
        mod_p = mod[i, 0:nb].reshape(nb, 1, 3 * D_MODEL)
        mod_s = mod[i, nb:nb + ns]
        g = norm_g[i].reshape(1, D_MODEL)
        ps = pool_scale[i].reshape(1, W_POOL)

        qkv, sw, yac, gates, pst_s, cst_s = _sample_a(
            xs, mod_s, g, w_in_b[i], wbd[i], ps, conv_w[i], sp_all[i], sc_all[i])
        dq3 = pad_heads(qkv.reshape(ns, 3, N_HEADS_DIL, HEAD_DIM))
        sq4 = sw[:, 0:W_SWA].reshape(ns, 1, N_HEADS_SWA, HEAD_DIM)
        skv = jnp.tile(sw[:, W_SWA:W_SWA + 2 * W_SWA_KV].reshape(ns, 2, 2, HEAD_DIM), (1, 1, 2, 1))
        sq3 = pad_heads(jnp.concatenate([sq4, skv], axis=1))

        cache_args = (dq3, sq3, dbias, mult, sbias, sink_rows[i], kt, vt, ckt, cvt)
        q1, k1, v1, qm, km, vm, gm, kc, vc, yacd, pst, cst, skc, svc, bo_a, do_a = _prompt_a(
            xp, mod_p, g, w_in_b[i], wbd[i], ps, conv_w[i], swa_sink[i], swa_bias, i, 0, cache_args)
        yb = _prompt_b(q1, k1, v1, qm, km, vm, gm, dil_bias)
        xp, bo_c, do_c = _prompt_c(xp, mod_p, yacd, yb, w_out_b[i], fg, final, i, ns // 2, cache_args)

        bo = jnp.concatenate([bo_a, bo_c], axis=0)[:, 0:N_HEADS_DIL].reshape(ns, W_DIL)
        do = jnp.concatenate([do_a, do_c], axis=0)[:, 0:N_HEADS_SWA].reshape(ns, W_SWA)
        xs = _sample_c(xs, mod_s, yac, gates, bo, do, w_out_b[i], fg, final)

        unfold = lambda a, h: jnp.transpose(a.reshape(nb, h, HEAD_DIM, a.shape[-1]), (0, 3, 1, 2))
        kc, vc, skc, svc = unfold(kc, N_HEADS_DIL), unfold(vc, N_HEADS_DIL), unfold(skc, 2), unfold(svc, 2)
        outs[0].append(pst)
        outs[1].append(jnp.transpose(pst_s, (1, 0, 2)))
        outs[2].append(kc)
        outs[3].append(vc)
        outs[4].append(qkv[:, W_DIL:2 * W_DIL].reshape(ns, 1, N_HEADS_DIL, HEAD_DIM))
        outs[5].append(qkv[:, 2 * W_DIL:3 * W_DIL].reshape(ns, 1, N_HEADS_DIL, HEAD_DIM))
        outs[6].append(cst)
        outs[7].append(cst_s.reshape(ns, CONV_BUF, W_CONV))
        outs[8].append(skc)
        outs[9].append(svc)
        outs[10].append(sw[:, W_SWA:W_SWA + W_SWA_KV].reshape(ns, 1, 2, HEAD_DIM))
        outs[11].append(sw[:, W_SWA + W_SWA_KV:W_SWA + 2 * W_SWA_KV].reshape(ns, 1, 2, HEAD_DIM))

    return (xp, xs.reshape(ns, 1, D_MODEL)) + tuple(jnp.stack(o) for o in outs)
```

```python
import functools
import math

import numpy as np
import jax
import jax.numpy as jnp
from jax import lax
from jax.experimental import pallas as pl
from jax.experimental.pallas import tpu as pltpu

F32 = jnp.float32
BF16 = jnp.bfloat16

D_MODEL = 1024
HEAD_DIM = 64
BLOCK = 128
POOL_WINDOWS = (2, 4, 8, 16)
POOL_GROUP = 64
W_POOL = 256
POOL_BUF = 15
DIL_CONFIGS = ((128, 1), (512, 4), (2048, 16))
DIL_MAX = 2048
N_HEADS_DIL = 6
N_PAIRS_DIL = 3
N_CLASSES = 16
W_DIL = 384
W_CONV = 256
CONV_BUF = 2
N_HEADS_SWA = 4
W_SWA = 256
W_SWA_KV = 128
SWA_WINDOW = 128
D_MIX = 1152
D_PROJ = 3840
RMS_EPS = 1e-6
QK_SCALE = 1.0 / math.sqrt(HEAD_DIM)

OFF_POOL = 0
OFF_DIL = 512
OFF_CONV = 2048
OFF_SWA = 3072
Y_WIDTH = W_POOL + W_CONV + 3 * W_SWA
SWA_HEAD_PERM = (0, 3, 1, 2)

SEQ_TILE = 512
CHUNK = 2048
DEC_TILE = 1
VMEM_LIMIT = 56 * 1024 * 1024

NEG_INF = float("-inf")


def _silu(v):
    return v * jax.nn.sigmoid(v)


def _dot(a, b):
    return jnp.dot(a, b, preferred_element_type=F32)


def _dot_nt(a, b):
    return lax.dot_general(a, b, (((1,), (1,)), ((), ())), preferred_element_type=F32)


def _alibi_slopes(n):
    return [2.0 ** (-8.0 * (h + 1) / n) for h in range(n)]


def _band_bias(slope_lo, slope_hi, dist_scale, variant):
    qi = np.arange(BLOCK)[:, None]
    kj = np.arange(2 * BLOCK)[None, :]
    if variant == 1:
        off = qi - kj
    else:
        off = qi - kj + BLOCK
    valid = (off >= 0) & (off <= BLOCK)
    if variant == 2:
        valid = valid & (kj >= BLOCK)
    out = []
    for s in (slope_lo, slope_hi):
        b = np.where(valid, -(np.float32(s) * np.float32(dist_scale)) * off.astype(np.float32),
                     np.float32(NEG_INF))
        out.append(b.astype(np.float32))
    return np.concatenate(out, axis=0)


def _compiler_params(sem):
    return pltpu.CompilerParams(dimension_semantics=sem, vmem_limit_bytes=VMEM_LIMIT)


def _ada_kernel(c_ref, w_ref, b_ref, o_ref):
    s = _silu(c_ref[...]).astype(BF16)
    o_ref[0] = _dot(s, w_ref[0].astype(BF16)) + b_ref[0]


def _ada(c_all, w_ada_b, b_ada):
    depth = w_ada_b.shape[0]
    rows = c_all.shape[0]
    return pl.pallas_call(
        _ada_kernel,
        grid=(depth, 3),
        in_specs=[
            pl.BlockSpec((rows, D_MODEL), lambda i, j: (0, 0)),
            pl.BlockSpec((1, D_MODEL, D_MODEL), lambda i, j: (i, 0, j)),
            pl.BlockSpec((1, 1, D_MODEL), lambda i, j: (i, 0, j)),
        ],
        out_specs=pl.BlockSpec((1, rows, D_MODEL), lambda i, j: (i, 0, j)),
        out_shape=jax.ShapeDtypeStruct((depth, rows, 3 * D_MODEL), F32),
        compiler_params=_compiler_params(("arbitrary", "arbitrary")),
        name="ada",
    )(c_all, w_ada_b, b_ada.reshape(depth, 1, 3 * D_MODEL))


def _modulated_norm(x, g, mod_row):
    ms = jnp.mean(x * x, axis=-1, keepdims=True)
    y = x * lax.rsqrt(ms + RMS_EPS) * g
    shift = mod_row[:, 0:D_MODEL]
    scale = mod_row[:, D_MODEL:2 * D_MODEL]
    return y * (1.0 + scale) + shift


def _pool_select(sums, lane):
    grp = lane // POOL_GROUP
    sel = jnp.where(grp == 0, sums[2], jnp.where(grp == 1, sums[4],
                                                 jnp.where(grp == 2, sums[8], sums[16])))
    win = jnp.where(grp == 0, 2, jnp.where(grp == 1, 4, jnp.where(grp == 2, 8, 16)))
    return sel, win


def _pa_kernel(x_ref, mod_ref, g_ref, w_ref, wbd_ref, ps_ref, cw_ref,
               dq_ref, sq_ref, dbias_ref, mult_ref, sbias_ref, ssink_ref, kt_ref, vt_ref, ck_ref, cv_ref,
               q1_ref, k1_ref, v1_ref, qm_ref, km_ref, vm_ref, gm_ref, kc_ref, vc_ref, y_ref,
               pst_ref, cst_ref, skc_ref, svc_ref, bo_ref, do_ref,
               uext, zcext, zds, z4s):
    tl = SEQ_TILE
    t = pl.program_id(1)

    @pl.when(t == 0)
    def _():
        for k in range(len(POOL_WINDOWS)):
            uext[k, 0:16, :] = jnp.zeros((16, W_POOL), F32)
        zcext[0:8, :] = jnp.zeros((8, W_CONV), F32)

    hb = _modulated_norm(x_ref[0], g_ref[...], mod_ref[0]).astype(BF16)

    zp = _dot(hb, w_ref[:, OFF_POOL:OFF_POOL + 2 * W_POOL])
    zc4 = _dot(hb, w_ref[:, OFF_CONV:OFF_CONV + 4 * W_CONV])

    pu = zp[:, 0:W_POOL]
    pg = zp[:, W_POOL:2 * W_POOL]
    uext[0, 16:16 + tl, :] = pu
    sums = {}
    level = pu
    for k, w in enumerate(POOL_WINDOWS):
        shift = w // 2
        level = level + uext[k, 16 - shift:16 - shift + tl, :]
        sums[w] = level
        if k + 1 < len(POOL_WINDOWS):
            uext[k + 1, 16:16 + tl, :] = level
    lane = lax.broadcasted_iota(jnp.int32, (tl, W_POOL), 1)
    gpos = lax.broadcasted_iota(jnp.int32, (tl, W_POOL), 0) + t * tl
    sel, win = _pool_select(sums, lane)
    cnt = jnp.minimum(gpos + 1, win).astype(F32)
    diff = sel / cnt - pu
    a_out = _dot(diff.astype(BF16), wbd_ref[...]) * ps_ref[...]
    y_ref[0, :, 0:W_POOL] = (a_out * _silu(pg)).astype(BF16)
    pst_ref[0] = uext[0, tl + 1:tl + 16, :]
    for k in range(len(POOL_WINDOWS)):
        uext[k, 0:16, :] = uext[k, tl:tl + 16, :]

    zd = _dot(hb, w_ref[:, OFF_DIL:OFF_DIL + 4 * W_DIL])
    ch = zc4[:, 0:W_CONV]
    cb = zc4[:, W_CONV:2 * W_CONV]
    cc = zc4[:, 2 * W_CONV:3 * W_CONV]
    cg = zc4[:, 3 * W_CONV:4 * W_CONV]
    zc = cc * ch
    zcext[8:8 + tl, :] = zc
    conv = (cw_ref[0:1, :] * zcext[6:6 + tl, :] + cw_ref[1:2, :] * zcext[7:7 + tl, :]
            + cw_ref[2:3, :] * zc)
    y_ref[0, :, W_POOL:W_POOL + W_CONV] = (cb * conv * _silu(cg)).astype(BF16)
    cst_ref[0] = zcext[tl + 6:tl + 8, :]
    zcext[0:8, :] = zcext[tl:tl + 8, :]

    zs = _dot(hb, w_ref[:, OFF_SWA:OFF_SWA + 3 * W_SWA])
    for hp in range(N_PAIRS_DIL):
        lo = hp * BLOCK
        zds[0] = zd[:, lo:lo + BLOCK] * QK_SCALE
        zds[1] = zd[:, W_DIL + lo:W_DIL + lo + BLOCK]
        zds[2] = zd[:, 2 * W_DIL + lo:2 * W_DIL + lo + BLOCK]
        zds[3] = _silu(zd[:, 3 * W_DIL + lo:3 * W_DIL + lo + BLOCK])
        q1_ref[0, hp] = zds[0].astype(BF16)
        k1_ref[0, hp] = zds[1].astype(BF16)
        v1_ref[0, hp] = zds[2].astype(BF16)
        quarter = tl // 4
        for a in range(4):
            for blk in range(4):
                z4s[blk, a] = zds[blk, pl.ds(a, quarter, stride=4), :]
        for b in range(4):
            for a in range(4):
                c = 4 * b + a
                rows = pl.ds(b, tl // N_CLASSES, stride=4)
                qm_ref[0, hp, c] = z4s[0, a, rows, :].astype(BF16)
                km_ref[0, hp, c] = z4s[1, a, rows, :].astype(BF16)
                vm_ref[0, hp, c] = z4s[2, a, rows, :].astype(BF16)
                gm_ref[0, hp, c] = z4s[3, a, rows, :].astype(BF16)
    kc_ref[0] = zd[:, W_DIL:2 * W_DIL].T
    vc_ref[0] = zd[:, 2 * W_DIL:3 * W_DIL].T

    sk = zs[:, W_SWA:W_SWA + W_SWA_KV]
    sv = zs[:, W_SWA + W_SWA_KV:W_SWA + 2 * W_SWA_KV]
    skc_ref[0] = sk[tl - SWA_WINDOW:tl, :].T
    svc_ref[0] = sv[tl - SWA_WINDOW:tl, :].T
    c0 = W_POOL + W_CONV
    y_ref[0, :, c0:c0 + W_SWA] = (zs[:, 0:W_SWA] * QK_SCALE).astype(BF16)
    y_ref[0, :, c0 + W_SWA:c0 + 2 * W_SWA] = zs[:, W_SWA:2 * W_SWA].astype(BF16)
    y_ref[0, :, c0 + 2 * W_SWA:c0 + 3 * W_SWA] = _silu(zs[:, 2 * W_SWA:3 * W_SWA]).astype(BF16)

    _cache_attention(dq_ref, sq_ref, dbias_ref, mult_ref, sbias_ref, ssink_ref, kt_ref, vt_ref,
                     ck_ref, cv_ref, bo_ref, do_ref)


def _cache_attention_specs(layer, row0, nb, nt, r, swr):
    bt = DEC_TILE
    blk0 = row0 // bt
    full = lambda *shape: pl.BlockSpec(shape, lambda n, t: (0,) * len(shape))
    srow_spec = pl.BlockSpec((bt, 3, 8, HEAD_DIM), lambda n, t: (blk0 + n * nt + t, 0, 0, 0))
    dil_spec = pl.BlockSpec((1, bt, N_HEADS_DIL, HEAD_DIM, r),
                            lambda n, t: (layer, blk0 + n * nt + t, 0, 0, 0))
    swa_spec = pl.BlockSpec((1, bt, 2, HEAD_DIM, swr), lambda n, t: (layer, blk0 + n * nt + t, 0, 0, 0))
    sout_spec = pl.BlockSpec((bt, 8, HEAD_DIM), lambda n, t: (n * nt + t, 0, 0))
    sout_shape = jax.ShapeDtypeStruct((nb * nt * bt, 8, HEAD_DIM), F32)
    in_specs = [srow_spec, srow_spec, full(8, r), full(1, r), full(8, swr), full(8, BLOCK),
                dil_spec, dil_spec, swa_spec, swa_spec]
    return in_specs, [sout_spec, sout_spec], [sout_shape, sout_shape]


def _prompt_a(x, mod_p, norm_g, w_in_b, wbd, pool_scale, conv_w, layer, row0, cache_args):
    nb, l, _ = x.shape
    tl = SEQ_TILE
    nt = l // tl
    cache_t0 = (l - DIL_MAX) // tl
    c_in, c_out, c_shape = _cache_attention_specs(layer, row0, nb, nt, cache_args[6].shape[-1],
                                                  cache_args[8].shape[-1])
    pair_spec = pl.BlockSpec((1, N_PAIRS_DIL, tl, BLOCK), lambda n, t: (n, 0, t, 0))
    pair_shape = jax.ShapeDtypeStruct((nb, N_PAIRS_DIL, l, BLOCK), BF16)
    cm_spec = pl.BlockSpec((1, N_PAIRS_DIL, N_CLASSES, tl // N_CLASSES, BLOCK),
                           lambda n, t: (n, 0, 0, t, 0))
    cm_shape = jax.ShapeDtypeStruct((nb, N_PAIRS_DIL, N_CLASSES, l // N_CLASSES, BLOCK), BF16)
    cache_spec = pl.BlockSpec((1, W_DIL, tl), lambda n, t: (n, 0, jnp.maximum(t - cache_t0, 0)))
    cache_shape = jax.ShapeDtypeStruct((nb, W_DIL, DIL_MAX), F32)
    full = lambda *shape: pl.BlockSpec(shape, lambda n, t: (0,) * len(shape))
    return pl.pallas_call(
        _pa_kernel,
        grid=(nb, nt),
        in_specs=[
            pl.BlockSpec((1, tl, D_MODEL), lambda n, t: (n, t, 0)),
            pl.BlockSpec((1, 1, 3 * D_MODEL), lambda n, t: (n, 0, 0)),
            full(1, D_MODEL),
            pl.BlockSpec((D_MODEL, D_PROJ), lambda n, t: (0, 0), pipeline_mode=pl.Buffered(1)),
            full(W_POOL, W_POOL),
            full(1, W_POOL),
            full(3, W_CONV),
        ] + c_in,
        out_specs=[
            pair_spec, pair_spec, pair_spec, cm_spec, cm_spec, cm_spec, cm_spec,
            cache_spec, cache_spec,
            pl.BlockSpec((1, tl, Y_WIDTH), lambda n, t: (n, t, 0)),
            pl.BlockSpec((1, POOL_BUF, W_POOL), lambda n, t: (n, 0, 0)),
            pl.BlockSpec((1, CONV_BUF, W_CONV), lambda n, t: (n, 0, 0)),
            pl.BlockSpec((1, SWA_WINDOW, W_SWA_KV), lambda n, t: (n, 0, 0)),
            pl.BlockSpec((1, SWA_WINDOW, W_SWA_KV), lambda n, t: (n, 0, 0)),
        ] + c_out,
        out_shape=[
            pair_shape, pair_shape, pair_shape, cm_shape, cm_shape, cm_shape, cm_shape,
            cache_shape, cache_shape,
            jax.ShapeDtypeStruct((nb, l, Y_WIDTH), BF16),
            jax.ShapeDtypeStruct((nb, POOL_BUF, W_POOL), F32),
            jax.ShapeDtypeStruct((nb, CONV_BUF, W_CONV), F32),
            jax.ShapeDtypeStruct((nb, SWA_WINDOW, W_SWA_KV), F32),
            jax.ShapeDtypeStruct((nb, SWA_WINDOW, W_SWA_KV), F32),
        ] + c_shape,
        scratch_shapes=[
            pltpu.VMEM((len(POOL_WINDOWS), 16 + tl, W_POOL), F32),
            pltpu.VMEM((8 + tl, W_CONV), F32),
            pltpu.VMEM((4, tl, BLOCK), F32),
            pltpu.VMEM((4, 4, tl // 4, BLOCK), F32),
        ],
        compiler_params=_compiler_params(("arbitrary", "arbitrary")),
        name="prompt_a",
    )(x, mod_p, norm_g, w_in_b, wbd, pool_scale, conv_w, *cache_args)


def _band_pair(q, kblk, vblk, bias, lane_q):
    lhs = jnp.concatenate([jnp.where(lane_q < HEAD_DIM, q, jnp.zeros_like(q)),
                           jnp.where(lane_q >= HEAD_DIM, q, jnp.zeros_like(q))], axis=0)
    s = _dot_nt(lhs, kblk) + bias
    m = jnp.max(s, axis=-1, keepdims=True)
    p = jnp.exp(s - m)
    den = jnp.sum(p, axis=-1, keepdims=True)
    o = _dot(p.astype(BF16), vblk) / den
    lse = m + jnp.log(den)
    od = jnp.where(lane_q < HEAD_DIM, o[0:BLOCK], o[BLOCK:2 * BLOCK])
    ld = jnp.where(lane_q < HEAD_DIM, lse[0:BLOCK], lse[BLOCK:2 * BLOCK])
    return od, ld


def _pb_kernel(bias_ref, q1, k1, v1, qm, km, vm, gm, outm, o1s, l1s, o4s, l4s, o16s, l16s):
    c = pl.program_id(2)
    first = jnp.where(c == 0, 1, 0)
    lane_q = lax.broadcasted_iota(jnp.int32, (BLOCK, BLOCK), 1)
    sub = BLOCK // 4

    for j in range(CHUNK // BLOCK):
        r0 = j * BLOCK
        if j == 0:
            start = jnp.maximum(c * (CHUNK // BLOCK) - 1, 0) * BLOCK
            bias = bias_ref[0, 0, first]
        else:
            start = (c * (CHUNK // BLOCK) + (j - 1)) * BLOCK
            bias = bias_ref[0, 0, 0]
        start = pl.multiple_of(start, BLOCK)
        od, ld = _band_pair(q1[0, 0, r0:r0 + BLOCK, :], k1[0, 0, pl.ds(start, 2 * BLOCK), :],
                            v1[0, 0, pl.ds(start, 2 * BLOCK), :], bias, lane_q)
        o1s[0, r0:r0 + BLOCK, :] = od
        l1s[0, r0:r0 + BLOCK, :] = ld

    for c4 in range(4):
        for j in range(4):
            i0 = j * sub
            if j == 0:
                istart = jnp.maximum(c * (CHUNK // N_CLASSES) - sub, 0)
                bias = bias_ref[0, 1, first]
            else:
                istart = c * (CHUNK // N_CLASSES) + i0 - sub
                bias = bias_ref[0, 1, 0]
            istart = pl.multiple_of(istart, sub)
            classes = [4 * cc + c4 for cc in range(4)]
            q = jnp.concatenate([qm[0, 0, cl, i0:i0 + sub, :] for cl in classes], axis=0)
            kblk = jnp.concatenate([km[0, 0, cl, pl.ds(istart, 2 * sub), :] for cl in classes], axis=0)
            vblk = jnp.concatenate([vm[0, 0, cl, pl.ds(istart, 2 * sub), :] for cl in classes], axis=0)
            od, ld = _band_pair(q, kblk, vblk, bias, lane_q)
            for cc, cl in enumerate(classes):
                o4s[cl, i0:i0 + sub, :] = od[cc * sub:(cc + 1) * sub]
                l4s[cl, i0:i0 + sub, :] = ld[cc * sub:(cc + 1) * sub]

    start = pl.multiple_of(jnp.maximum(c - 1, 0) * BLOCK, BLOCK)
    for cl in range(N_CLASSES):
        od, ld = _band_pair(qm[0, 0, cl], km[0, 0, cl, pl.ds(start, 2 * BLOCK), :],
                            vm[0, 0, cl, pl.ds(start, 2 * BLOCK), :], bias_ref[0, 2, first], lane_q)
        o16s[cl] = od
        l16s[cl] = ld

    for cl in range(N_CLASSES):
        oa = o1s[0, pl.ds(cl, BLOCK, stride=N_CLASSES), :]
        la = l1s[0, pl.ds(cl, BLOCK, stride=N_CLASSES), :]
        ob = o4s[cl]
        lb = l4s[cl]
        oc = o16s[cl]
        lc = l16s[cl]
        mx = jnp.maximum(jnp.maximum(la, lb), lc)
        wa = jnp.exp(la - mx)
        wb = jnp.exp(lb - mx)
        wc = jnp.exp(lc - mx)
        mix = (wa * oa + wb * ob + wc * oc) / (wa + wb + wc)
        outm[0, 0, cl] = (mix * gm[0, 0, cl].astype(F32)).astype(BF16)


def _prompt_b(q1, k1, v1, qm, km, vm, gm, dil_bias):
    nb, npair, l, _ = q1.shape
    nc = l // CHUNK
    li = l // N_CLASSES
    ci = CHUNK // N_CLASSES
    q1_spec = pl.BlockSpec((1, 1, CHUNK, BLOCK), lambda n, h, c: (n, h, c, 0))
    kv1_spec = pl.BlockSpec((1, 1, l, BLOCK), lambda n, h, c: (n, h, 0, 0))
    cm_spec = pl.BlockSpec((1, 1, N_CLASSES, ci, BLOCK), lambda n, h, c: (n, h, 0, c, 0))
    kvm_spec = pl.BlockSpec((1, 1, N_CLASSES, li, BLOCK), lambda n, h, c: (n, h, 0, 0, 0))
    cm_scratch = pltpu.VMEM((N_CLASSES, ci, BLOCK), F32)
    return pl.pallas_call(
        _pb_kernel,
        grid=(nb, npair, nc),
        in_specs=[pl.BlockSpec((1, 3, 2, 2 * BLOCK, 2 * BLOCK), lambda n, h, c: (h, 0, 0, 0, 0)),
                  q1_spec, kv1_spec, kv1_spec, cm_spec, kvm_spec, kvm_spec, cm_spec],
        out_specs=cm_spec,
        out_shape=jax.ShapeDtypeStruct((nb, npair, N_CLASSES, li, BLOCK), BF16),
        scratch_shapes=[
            pltpu.VMEM((1, CHUNK, BLOCK), F32), pltpu.VMEM((1, CHUNK, BLOCK), F32),
            cm_scratch, cm_scratch, cm_scratch, cm_scratch,
        ],
        compiler_params=_compiler_params(("arbitrary", "arbitrary", "arbitrary")),
        name="prompt_b",
    )(dil_bias, q1, k1, v1, qm, km, vm, gm)


def _mix_out(y, x, gate, w_ref, fg_ref, final):
    xn = x + gate * _dot(y, w_ref[...])
    if final:
        ms = jnp.mean(xn * xn, axis=-1, keepdims=True)
        xn = xn * lax.rsqrt(ms + RMS_EPS) * fg_ref[...]
    return xn


def _pc_kernel(sink_ref, x_ref, mod_ref, yacd_ref, yb_ref, w_ref, fg_ref, bias_ref,
               dq_ref, sq_ref, dbias_ref, mult_ref, sbias_ref, ssink_ref, kt_ref, vt_ref, ck_ref, cv_ref,
               o_ref, bo_ref, do_ref, ybs, kext, vext, yds, *, final):
    tl = SEQ_TILE
    t = pl.program_id(1)
    _cache_attention(dq_ref, sq_ref, dbias_ref, mult_ref, sbias_ref, ssink_ref, kt_ref, vt_ref,
                     ck_ref, cv_ref, bo_ref, do_ref)

    @pl.when(t == 0)
    def _():
        kext[0:BLOCK, :] = jnp.zeros((BLOCK, W_SWA_KV), BF16)
        vext[0:BLOCK, :] = jnp.zeros((BLOCK, W_SWA_KV), BF16)

    c0 = W_POOL + W_CONV
    kext[BLOCK:BLOCK + tl, :] = yacd_ref[0, :, c0 + W_SWA:c0 + W_SWA + W_SWA_KV]
    vext[BLOCK:BLOCK + tl, :] = yacd_ref[0, :, c0 + W_SWA + W_SWA_KV:c0 + 2 * W_SWA]
    first = jnp.where(t == 0, 1, 0)
    lane_q = lax.broadcasted_iota(jnp.int32, (BLOCK, BLOCK), 1)
    row_s = lax.broadcasted_iota(jnp.int32, (2 * BLOCK, 1), 0)
    for jb in range(tl // BLOCK):
        r0 = jb * BLOCK
        kblk = kext[r0:r0 + 2 * BLOCK, :]
        vblk = vext[r0:r0 + 2 * BLOCK, :]
        for grp in range(2):
            q = yacd_ref[0, r0:r0 + BLOCK, c0 + grp * BLOCK:c0 + (grp + 1) * BLOCK]
            lhs = jnp.concatenate([jnp.where(lane_q < HEAD_DIM, q, jnp.zeros_like(q)),
                                   jnp.where(lane_q >= HEAD_DIM, q, jnp.zeros_like(q))], axis=0)
            s = _dot_nt(lhs, kblk)
            if jb == 0:
                s = s + bias_ref[grp, first]
            else:
                s = s + bias_ref[grp, 0]
            sink = jnp.where(row_s < BLOCK, sink_ref[SWA_HEAD_PERM[2 * grp]],
                             sink_ref[SWA_HEAD_PERM[2 * grp + 1]])
            m = jnp.maximum(jnp.max(s, axis=-1, keepdims=True), sink)
            p = jnp.exp(s - m)
            den = jnp.sum(p, axis=-1, keepdims=True) + jnp.exp(sink - m)
            o = _dot(p.astype(BF16), vblk) / den
            od = jnp.where(lane_q < HEAD_DIM, o[0:BLOCK], o[BLOCK:2 * BLOCK])
            g0 = c0 + 2 * W_SWA + grp * BLOCK
            gate_s = yacd_ref[0, r0:r0 + BLOCK, g0:g0 + BLOCK].astype(F32)
            yds[r0:r0 + BLOCK, grp * BLOCK:(grp + 1) * BLOCK] = (od * gate_s).astype(BF16)
    kext[0:BLOCK, :] = kext[tl:tl + BLOCK, :]
    vext[0:BLOCK, :] = vext[tl:tl + BLOCK, :]

    for hp in range(N_PAIRS_DIL):
        for c in range(N_CLASSES):
            ybs[hp, pl.ds(c, tl // N_CLASSES, stride=N_CLASSES), :] = yb_ref[0, hp, c].astype(F32)
    y = jnp.concatenate([yacd_ref[0, :, 0:W_POOL], ybs[0].astype(BF16), ybs[1].astype(BF16),
                         ybs[2].astype(BF16), yacd_ref[0, :, W_POOL:W_POOL + W_CONV], yds[...]], axis=-1)
    gate = mod_ref[0][:, 2 * D_MODEL:3 * D_MODEL]
    o_ref[0] = _mix_out(y, x_ref[0], gate, w_ref, fg_ref, final)


def _prompt_c(x, mod_p, yacd, yb, w_out_b, final_g, sink, swa_bias, final, layer, row0, cache_args):
    nb, l, _ = x.shape
    tl = SEQ_TILE
    full = lambda *shape: pl.BlockSpec(shape, lambda n, t: (0,) * len(shape))
    c_in, c_out, c_shape = _cache_attention_specs(layer, row0, nb, l // tl, cache_args[6].shape[-1],
                                                  cache_args[8].shape[-1])
    return pl.pallas_call(
        functools.partial(_pc_kernel, final=final),
        grid=(nb, l // tl),
        in_specs=[
            pl.BlockSpec(memory_space=pltpu.SMEM),
            pl.BlockSpec((1, tl, D_MODEL), lambda n, t: (n, t, 0)),
            pl.BlockSpec((1, 1, 3 * D_MODEL), lambda n, t: (n, 0, 0)),
            pl.BlockSpec((1, tl, Y_WIDTH), lambda n, t: (n, t, 0)),
            pl.BlockSpec((1, N_PAIRS_DIL, N_CLASSES, tl // N_CLASSES, BLOCK),
                         lambda n, t: (n, 0, 0, t, 0)),
            full(D_MIX, D_MODEL),
            full(1, D_MODEL),
            full(2, 2, 2 * BLOCK, 2 * BLOCK),
        ] + c_in,
        out_specs=[pl.BlockSpec((1, tl, D_MODEL), lambda n, t: (n, t, 0))] + c_out,
        out_shape=[jax.ShapeDtypeStruct((nb, l, D_MODEL), F32)] + c_shape,
        scratch_shapes=[
            pltpu.VMEM((N_PAIRS_DIL, tl, BLOCK), F32),
            pltpu.VMEM((BLOCK + tl, W_SWA_KV), BF16),
            pltpu.VMEM((BLOCK + tl, W_SWA_KV), BF16),
            pltpu.VMEM((tl, W_SWA), BF16),
        ],
        compiler_params=_compiler_params(("arbitrary", "arbitrary")),
        name="prompt_c",
    )(sink, x, mod_p, yacd, yb, w_out_b, final_g, swa_bias, *cache_args)


def _sa_kernel(x_ref, mod_ref, g_ref, w_ref, wbd_ref, ps_ref, cw_ref, sp_ref, sc_ref,
               qkv_ref, sw_ref, yac_ref, gates_ref, pst_ref, cst_ref):
    hb = _modulated_norm(x_ref[...], g_ref[...], mod_ref[...]).astype(BF16)
    ns = hb.shape[0]

    zp = _dot(hb, w_ref[:, OFF_POOL:OFF_POOL + 2 * W_POOL])
    pu = zp[:, 0:W_POOL]
    pg = zp[:, W_POOL:2 * W_POOL]
    acc = pu
    sums = {}
    for j in range(1, 16):
        acc = acc + sp_ref[POOL_BUF - j]
        if j + 1 in POOL_WINDOWS:
            sums[j + 1] = acc
    lane = lax.broadcasted_iota(jnp.int32, (ns, W_POOL), 1)
    sel, win = _pool_select(sums, lane)
    diff = sel / win.astype(F32) - pu
    a_out = _dot(diff.astype(BF16), wbd_ref[...]) * ps_ref[...]
    yac_ref[:, 0:W_POOL] = a_out * _silu(pg)
    pst_ref[0:POOL_BUF - 1] = sp_ref[1:POOL_BUF]
    pst_ref[POOL_BUF - 1] = pu

    zc4 = _dot(hb, w_ref[:, OFF_CONV:OFF_CONV + 4 * W_CONV])
    ch = zc4[:, 0:W_CONV]
    cb = zc4[:, W_CONV:2 * W_CONV]
    cc = zc4[:, 2 * W_CONV:3 * W_CONV]
    cg = zc4[:, 3 * W_CONV:4 * W_CONV]
    zc = cc * ch
    conv = (cw_ref[0:1, :] * sc_ref[:, 0:W_CONV] + cw_ref[1:2, :] * sc_ref[:, W_CONV:2 * W_CONV]
            + cw_ref[2:3, :] * zc)
    yac_ref[:, W_POOL:W_POOL + W_CONV] = cb * conv * _silu(cg)
    cst_ref[:, 0:W_CONV] = sc_ref[:, W_CONV:2 * W_CONV]
    cst_ref[:, W_CONV:2 * W_CONV] = zc

    zd = _dot(hb, w_ref[:, OFF_DIL:OFF_DIL + 4 * W_DIL])
    qkv_ref[:, 0:W_DIL] = zd[:, 0:W_DIL] * QK_SCALE
    qkv_ref[:, W_DIL:3 * W_DIL] = zd[:, W_DIL:3 * W_DIL]
    gates_ref[:, 0:W_DIL] = _silu(zd[:, 3 * W_DIL:4 * W_DIL])

    zs = _dot(hb, w_ref[:, OFF_SWA:OFF_SWA + 3 * W_SWA])
    sw_ref[:, 0:W_SWA] = zs[:, 0:W_SWA] * QK_SCALE
    sw_ref[:, W_SWA:2 * W_SWA] = zs[:, W_SWA:2 * W_SWA]
    gates_ref[:, W_DIL:W_DIL + W_SWA] = _silu(zs[:, 2 * W_SWA:3 * W_SWA])


def _sample_a(xs, mod_s, norm_g, w_in_b, wbd, pool_scale, conv_w, sp, sc):
    ns = xs.shape[0]
    shapes = [
        jax.ShapeDtypeStruct((ns, 3 * W_DIL), F32),
        jax.ShapeDtypeStruct((ns, 2 * W_SWA), F32),
        jax.ShapeDtypeStruct((ns, W_POOL + W_CONV), F32),
        jax.ShapeDtypeStruct((ns, W_DIL + W_SWA), F32),
        jax.ShapeDtypeStruct((POOL_BUF, ns, W_POOL), F32),
        jax.ShapeDtypeStruct((ns, CONV_BUF * W_CONV), F32),
    ]
    return pl.pallas_call(
        _sa_kernel,
        out_shape=shapes,
        compiler_params=pltpu.CompilerParams(vmem_limit_bytes=VMEM_LIMIT),
        name="sample_a",
    )(xs, mod_s, norm_g, w_in_b, wbd, pool_scale, conv_w, sp, sc)


def _cache_attention(dq_ref, sq_ref, dbias_ref, mult_ref, sbias_ref, sink_ref, kt_ref, vt_ref,
                     ck_ref, cv_ref, bo_ref, do_ref):
    row_d = lax.broadcasted_iota(jnp.int32, (8, DIL_MAX), 0)
    row_o = lax.broadcasted_iota(jnp.int32, (8, HEAD_DIM), 0)
    row_s = lax.broadcasted_iota(jnp.int32, (8, SWA_WINDOW), 0)
    for j in range(DEC_TILE):
        q = dq_ref[j, 0]
        qb = q.astype(BF16)
        s = jnp.zeros((8, DIL_MAX), F32)
        for h in range(N_HEADS_DIL):
            s = jnp.where(row_d == h, _dot(qb, kt_ref[0, j, h].astype(BF16)), s)
        s = s + dbias_ref[...]
        s_self = jnp.sum(q * dq_ref[j, 1], axis=-1, keepdims=True)
        m = jnp.maximum(jnp.max(s, axis=-1, keepdims=True), s_self)
        p = jnp.exp(s - m) * mult_ref[...]
        p_self = float(len(DIL_CONFIGS)) * jnp.exp(s_self - m)
        den = jnp.sum(p, axis=-1, keepdims=True) + p_self
        pb = p.astype(BF16)
        acc = jnp.zeros((8, HEAD_DIM), F32)
        for h in range(N_HEADS_DIL):
            acc = jnp.where(row_o == h, _dot_nt(pb, vt_ref[0, j, h].astype(BF16)), acc)
        bo_ref[j] = (acc + p_self * dq_ref[j, 2]) / den

        q = sq_ref[j, 0]
        qb = q.astype(BF16)
        s = jnp.where(row_s % 2 == 0, _dot(qb, ck_ref[0, j, 0].astype(BF16)),
                      _dot(qb, ck_ref[0, j, 1].astype(BF16))) + sbias_ref[...]
        w_self = jnp.sum(q * sq_ref[j, 1], axis=-1, keepdims=True)
        sink = sink_ref[...][:, 0:1]
        m = jnp.maximum(jnp.maximum(jnp.max(s, axis=-1, keepdims=True), w_self), sink)
        p = jnp.exp(s - m)
        pw = jnp.exp(w_self - m)
        den = jnp.sum(p, axis=-1, keepdims=True) + pw + jnp.exp(sink - m)
        pb = p.astype(BF16)
        acc = jnp.where(row_o % 2 == 0, _dot_nt(pb, cv_ref[0, j, 0].astype(BF16)),
                        _dot_nt(pb, cv_ref[0, j, 1].astype(BF16)))
        do_ref[j] = (acc + pw * sq_ref[j, 2]) / den


def _sc_kernel(x_ref, mod_ref, yac_ref, gates_ref, bo_ref, do_ref, w_ref, fg_ref, o_ref, *, final):
    yac = yac_ref[...]
    gates = gates_ref[...]
    y = jnp.concatenate([yac[:, 0:W_POOL], bo_ref[...] * gates[:, 0:W_DIL],
                         yac[:, W_POOL:W_POOL + W_CONV], do_ref[...] * gates[:, W_DIL:W_DIL + W_SWA]],
                        axis=-1).astype(BF16)
    gate = mod_ref[...][:, 2 * D_MODEL:3 * D_MODEL]
    o_ref[...] = _mix_out(y, x_ref[...], gate, w_ref, fg_ref, final)


def _sample_c(xs, mod_s, yac, gates, bo, do, w_out_b, final_g, final):
    return pl.pallas_call(
        functools.partial(_sc_kernel, final=final),
        out_shape=jax.ShapeDtypeStruct(xs.shape, F32),
        compiler_params=pltpu.CompilerParams(vmem_limit_bytes=VMEM_LIMIT),
        name="sample_c",
    )(xs, mod_s, yac, gates, bo, do, w_out_b, final_g)


def _band_bias_dil4(slope_lo, slope_hi, variant):
    sub = BLOCK // 4
    qidx = np.arange(BLOCK)[:, None]
    kidx = np.arange(2 * BLOCK)[None, :]
    q_step = 4 * (qidx % sub + (sub if variant == 0 else 0)) + qidx // sub
    k_step = 4 * (kidx % (2 * sub)) + kidx // (2 * sub)
    off = q_step - k_step
    valid = (off >= 0) & (off <= BLOCK)
    out = []
    for s in (slope_lo, slope_hi):
        out.append(np.where(valid, -(np.float32(s) * np.float32(4)) * off.astype(np.float32),
                            np.float32(NEG_INF)).astype(np.float32))
    return np.concatenate(out, axis=0)


def _prompt_bias_tables():
    dil = _alibi_slopes(N_HEADS_DIL)

    def table(hp, d, var):
        if d == 4:
            return _band_bias_dil4(dil[2 * hp], dil[2 * hp + 1], var)
        return _band_bias(dil[2 * hp], dil[2 * hp + 1], d, var)

    dil_bias = np.stack([
        np.stack([np.stack([table(hp, d, var) for var in (0, 1)])
                  for _, d in DIL_CONFIGS]) for hp in range(N_PAIRS_DIL)])
    swa = _alibi_slopes(N_HEADS_SWA)
    swa_bias = np.stack([
        np.stack([_band_bias(swa[SWA_HEAD_PERM[2 * g]], swa[SWA_HEAD_PERM[2 * g + 1]], 1, var)
                  for var in (0, 2)]) for g in range(2)])
    return jnp.asarray(dil_bias), jnp.asarray(swa_bias)


def _sample_bias_tables(r):
    dil = _alibi_slopes(N_HEADS_DIL)
    dist = (r - np.arange(r)).astype(np.float32)
    mult = np.zeros((1, r), np.float32)
    for window, d in DIL_CONFIGS:
        mult[0] += ((dist <= window) & (dist % d == 0)).astype(np.float32)
    dbias = np.zeros((8, r), np.float32)
    for h in range(N_HEADS_DIL):
        dbias[h] = np.where(mult[0] > 0, -np.float32(dil[h]) * dist, np.float32(NEG_INF))
    swa = _alibi_slopes(N_HEADS_SWA)
    sdist = (SWA_WINDOW - np.arange(SWA_WINDOW)).astype(np.float32)
    sbias = np.zeros((8, SWA_WINDOW), np.float32)
    for row, h in enumerate(SWA_HEAD_PERM):
        sbias[row] = -np.float32(swa[h]) * sdist
    return jnp.asarray(dbias), jnp.asarray(mult), jnp.asarray(sbias)


def kernel(x_prompt, x_sample, c_prompt, c_sample, state_pool, cache_dil_k, cache_dil_v, state_conv, cache_swa_k, cache_swa_v, norm_g, w_ada, b_ada, w_in, w_pool, pool_scale, conv_w, swa_sink, w_out, final_g):
    depth = w_in.shape[0]
    nb, l, _ = x_prompt.shape
    ns = x_sample.shape[0]
    assert x_sample.shape[1] == 1 and l % CHUNK == 0 and l >= DIL_MAX
    assert ns == 2 * nb * (l // SEQ_TILE) * DEC_TILE
    assert cache_dil_k.shape[2] == DIL_MAX and cache_swa_k.shape[2] == SWA_WINDOW

    def perm_heads(w, axis, off):
        take = lambda lo, hi: lax.slice_in_dim(w, lo, hi, axis=axis)
        return [take(off + h * HEAD_DIM, off + (h + 1) * HEAD_DIM) for h in SWA_HEAD_PERM]

    take_in = lambda lo, hi: lax.slice_in_dim(w_in, lo, hi, axis=2)
    w_in_b = jnp.concatenate(
        [take_in(0, OFF_SWA)] + perm_heads(w_in, 2, OFF_SWA)
        + [take_in(OFF_SWA + W_SWA, OFF_SWA + 2 * W_SWA)] + perm_heads(w_in, 2, OFF_SWA + 2 * W_SWA),
        axis=2).astype(BF16)
    w_out_b = jnp.concatenate(
        [lax.slice_in_dim(w_out, 0, D_MIX - W_SWA, axis=1)] + perm_heads(w_out, 1, D_MIX - W_SWA),
        axis=1).astype(BF16)
    eye = jnp.eye(len(POOL_WINDOWS), dtype=F32)
    wbd = jnp.einsum("dgce,gh->dgche", w_pool, eye).reshape(depth, W_POOL, W_POOL).astype(BF16)
    sink_rows = jnp.zeros((depth, 8), F32).at[:, 0:N_HEADS_SWA].set(swa_sink[:, list(SWA_HEAD_PERM)])
    sink_rows = jnp.broadcast_to(sink_rows[:, :, None], (depth, 8, BLOCK))

    dil_bias, swa_bias = _prompt_bias_tables()
    dbias, mult, sbias = _sample_bias_tables(cache_dil_k.shape[2])

    mod = _ada(jnp.concatenate([c_prompt, c_sample], axis=0), w_ada, b_ada)
    fg = final_g.reshape(1, D_MODEL)

    xp = x_prompt
    xs = x_sample.reshape(ns, D_MODEL)
    kt = jnp.transpose(cache_dil_k, (0, 1, 3, 4, 2))
    vt = jnp.transpose(cache_dil_v, (0, 1, 3, 4, 2))
    ckt = jnp.transpose(cache_swa_k, (0, 1, 3, 4, 2))
    cvt = jnp.transpose(cache_swa_v, (0, 1, 3, 4, 2))
    sp_all = jnp.transpose(state_pool, (0, 2, 1, 3))
    sc_all = state_conv.reshape(depth, ns, CONV_BUF * W_CONV)
    pad_heads = lambda a: jnp.pad(a, ((0, 0), (0, 0), (0, 8 - a.shape[2]), (0, 0)))
    outs = [[] for _ in range(12)]
    for i in range(depth):
        final = i == depth - 1
        mod_p = mod[i, 0:nb].reshape(nb, 1, 3 * D_MODEL)
        mod_s = mod[i, nb:nb + ns]
        g = norm_g[i].reshape(1, D_MODEL)
        ps = pool_scale[i].reshape(1, W_POOL)

        qkv, sw, yac, gates, pst_s, cst_s = _sample_a(
            xs, mod_s, g, w_in_b[i], wbd[i], ps, conv_w[i], sp_all[i], sc_all[i])
        dq3 = pad_heads(qkv.reshape(ns, 3, N_HEADS_DIL, HEAD_DIM))
        sq4 = sw[:, 0:W_SWA].reshape(ns, 1, N_HEADS_SWA, HEAD_DIM)
        skv = jnp.tile(sw[:, W_SWA:W_SWA + 2 * W_SWA_KV].reshape(ns, 2, 2, HEAD_DIM), (1, 1, 2, 1))
        sq3 = pad_heads(jnp.concatenate([sq4, skv], axis=1))

        cache_args = (dq3, sq3, dbias, mult, sbias, sink_rows[i], kt, vt, ckt, cvt)
        q1, k1, v1, qm, km, vm, gm, kc, vc, yacd, pst, cst, skc, svc, bo_a, do_a = _prompt_a(
            xp, mod_p, g, w_in_b[i], wbd[i], ps, conv_w[i], i, 0, cache_args)
        yb = _prompt_b(q1, k1, v1, qm, km, vm, gm, dil_bias)
        xp, bo_c, do_c = _prompt_c(xp, mod_p, yacd, yb, w_out_b[i], fg, swa_sink[i], swa_bias, final,
                                   i, ns // 2, cache_args)

        bo = jnp.concatenate([bo_a, bo_c], axis=0)[:, 0:N_HEADS_DIL].reshape(ns, W_DIL)
        do = jnp.concatenate([do_a, do_c], axis=0)[:, 0:N_HEADS_SWA].reshape(ns, W_SWA)
        xs = _sample_c(xs, mod_s, yac, gates, bo, do, w_out_b[i], fg, final)

        unfold = lambda a, h: jnp.transpose(a.reshape(nb, h, HEAD_DIM, a.shape[-1]), (0, 3, 1, 2))
        kc, vc, skc, svc = unfold(kc, N_HEADS_DIL), unfold(vc, N_HEADS_DIL), unfold(skc, 2), unfold(svc, 2)
        outs[0].append(pst)
        outs[1].append(jnp.transpose(pst_s, (1, 0, 2)))
        outs[2].append(kc)
        outs[3].append(vc)
        outs[4].append(qkv[:, W_DIL:2 * W_DIL].reshape(ns, 1, N_HEADS_DIL, HEAD_DIM))
        outs[5].append(qkv[:, 2 * W_DIL:3 * W_DIL].reshape(ns, 1, N_HEADS_DIL, HEAD_DIM))
        outs[6].append(cst)
        outs[7].append(cst_s.reshape(ns, CONV_BUF, W_CONV))
        outs[8].append(skc)
        outs[9].append(svc)
        outs[10].append(sw[:, W_SWA:W_SWA + W_SWA_KV].reshape(ns, 1, 2, HEAD_DIM))
        outs[11].append(sw[:, W_SWA + W_SWA_KV:W_SWA + 2 * W_SWA_KV].reshape(ns, 1, 2, HEAD_DIM))

    return (xp, xs.reshape(ns, 1, D_MODEL)) + tuple(jnp.stack(o) for o in outs)
```

```python
import functools
import math

import numpy as np
import jax
import jax.numpy as jnp
from jax import lax
from jax.experimental import pallas as pl
from jax.experimental.pallas import tpu as pltpu

F32 = jnp.float32
BF16 = jnp.bfloat16

D_MODEL = 1024
HEAD_DIM = 64
BLOCK = 128
POOL_WINDOWS = (2, 4, 8, 16)
POOL_GROUP = 64
W_POOL = 256
POOL_BUF = 15
DIL_CONFIGS = ((128, 1), (512, 4), (2048, 16))
DIL_MAX = 2048
N_HEADS_DIL = 6
N_PAIRS_DIL = 3
N_CLASSES = 16
W_DIL = 384
W_CONV = 256
CONV_BUF = 2
N_HEADS_SWA = 4
W_SWA = 256
W_SWA_KV = 128
SWA_WINDOW = 128
D_MIX = 1152
D_PROJ = 3840
RMS_EPS = 1e-6
QK_SCALE = 1.0 / math.sqrt(HEAD_DIM)

OFF_POOL = 0
OFF_DIL = 512
OFF_CONV = 2048
OFF_SWA = 3072
Y_WIDTH = W_POOL + W_CONV + 3 * W_SWA
SWA_HEAD_PERM = (0, 3, 1, 2)

SEQ_TILE = 512
CHUNK = 2048
DEC_TILE = 1
VMEM_LIMIT = 56 * 1024 * 1024

NEG_INF = float("-inf")


def _silu(v):
    return v * jax.nn.sigmoid(v)


def _dot(a, b):
    return jnp.dot(a, b, preferred_element_type=F32)


def _dot_nt(a, b):
    return lax.dot_general(a, b, (((1,), (1,)), ((), ())), preferred_element_type=F32)


def _alibi_slopes(n):
    return [2.0 ** (-8.0 * (h + 1) / n) for h in range(n)]


def _band_bias(slope_lo, slope_hi, dist_scale, variant):
    qi = np.arange(BLOCK)[:, None]
    kj = np.arange(2 * BLOCK)[None, :]
    if variant == 1:
        off = qi - kj
    else:
        off = qi - kj + BLOCK
    valid = (off >= 0) & (off <= BLOCK)
    if variant == 2:
        valid = valid & (kj >= BLOCK)
    out = []
    for s in (slope_lo, slope_hi):
        b = np.where(valid, -(np.float32(s) * np.float32(dist_scale)) * off.astype(np.float32),
                     np.float32(NEG_INF))
        out.append(b.astype(np.float32))
    return np.concatenate(out, axis=0)


def _compiler_params(sem):
    return pltpu.CompilerParams(dimension_semantics=sem, vmem_limit_bytes=VMEM_LIMIT)


def _ada_kernel(c_ref, w_ref, b_ref, o_ref):
    s = _silu(c_ref[...]).astype(BF16)
    o_ref[0] = _dot(s, w_ref[0].astype(BF16)) + b_ref[0]


def _ada(c_all, w_ada_b, b_ada):
    depth = w_ada_b.shape[0]
    rows = c_all.shape[0]
    return pl.pallas_call(
        _ada_kernel,
        grid=(depth, 3),
        in_specs=[
            pl.BlockSpec((rows, D_MODEL), lambda i, j: (0, 0)),
            pl.BlockSpec((1, D_MODEL, D_MODEL), lambda i, j: (i, 0, j)),
            pl.BlockSpec((1, 1, D_MODEL), lambda i, j: (i, 0, j)),
        ],
        out_specs=pl.BlockSpec((1, rows, D_MODEL), lambda i, j: (i, 0, j)),
        out_shape=jax.ShapeDtypeStruct((depth, rows, 3 * D_MODEL), F32),
        compiler_params=_compiler_params(("arbitrary", "arbitrary")),
        name="ada",
    )(c_all, w_ada_b, b_ada.reshape(depth, 1, 3 * D_MODEL))


def _modulated_norm(x, g, mod_row):
    ms = jnp.mean(x * x, axis=-1, keepdims=True)
    y = x * lax.rsqrt(ms + RMS_EPS) * g
    shift = mod_row[:, 0:D_MODEL]
    scale = mod_row[:, D_MODEL:2 * D_MODEL]
    return y * (1.0 + scale) + shift


def _pool_select(sums, lane):
    grp = lane // POOL_GROUP
    sel = jnp.where(grp == 0, sums[2], jnp.where(grp == 1, sums[4],
                                                 jnp.where(grp == 2, sums[8], sums[16])))
    win = jnp.where(grp == 0, 2, jnp.where(grp == 1, 4, jnp.where(grp == 2, 8, 16)))
    return sel, win


def _pa_kernel(x_ref, mod_ref, g_ref, w_ref, wbd_ref, ps_ref, cw_ref,
               dq_ref, sq_ref, dbias_ref, mult_ref, sbias_ref, ssink_ref, kt_ref, vt_ref, ck_ref, cv_ref,
               q1_ref, k1_ref, v1_ref, qm_ref, km_ref, vm_ref, gm_ref, kc_ref, vc_ref, y_ref,
               pst_ref, cst_ref, skc_ref, svc_ref, bo_ref, do_ref,
               uext, zcext, zds, z4s):
    tl = SEQ_TILE
    t = pl.program_id(1)

    @pl.when(t == 0)
    def _():
        for k in range(len(POOL_WINDOWS)):
            uext[k, 0:16, :] = jnp.zeros((16, W_POOL), F32)
        zcext[0:8, :] = jnp.zeros((8, W_CONV), F32)

    hb = _modulated_norm(x_ref[0], g_ref[...], mod_ref[0]).astype(BF16)

    zp = _dot(hb, w_ref[:, OFF_POOL:OFF_POOL + 2 * W_POOL])
    zc4 = _dot(hb, w_ref[:, OFF_CONV:OFF_CONV + 4 * W_CONV])

    pu = zp[:, 0:W_POOL]
    pg = zp[:, W_POOL:2 * W_POOL]
    uext[0, 16:16 + tl, :] = pu
    sums = {}
    level = pu
    for k, w in enumerate(POOL_WINDOWS):
        shift = w // 2
        level = level + uext[k, 16 - shift:16 - shift + tl, :]
        sums[w] = level
        if k + 1 < len(POOL_WINDOWS):
            uext[k + 1, 16:16 + tl, :] = level
    lane = lax.broadcasted_iota(jnp.int32, (tl, W_POOL), 1)
    gpos = lax.broadcasted_iota(jnp.int32, (tl, W_POOL), 0) + t * tl
    sel, win = _pool_select(sums, lane)
    cnt = jnp.minimum(gpos + 1, win).astype(F32)
    diff = sel / cnt - pu
    a_out = _dot(diff.astype(BF16), wbd_ref[...]) * ps_ref[...]
    y_ref[0, :, 0:W_POOL] = (a_out * _silu(pg)).astype(BF16)
    pst_ref[0] = uext[0, tl + 1:tl + 16, :]
    for k in range(len(POOL_WINDOWS)):
        uext[k, 0:16, :] = uext[k, tl:tl + 16, :]

    zd = _dot(hb, w_ref[:, OFF_DIL:OFF_DIL + 4 * W_DIL])
    ch = zc4[:, 0:W_CONV]
    cb = zc4[:, W_CONV:2 * W_CONV]
    cc = zc4[:, 2 * W_CONV:3 * W_CONV]
    cg = zc4[:, 3 * W_CONV:4 * W_CONV]
    zc = cc * ch
    zcext[8:8 + tl, :] = zc
    conv = (cw_ref[0:1, :] * zcext[6:6 + tl, :] + cw_ref[1:2, :] * zcext[7:7 + tl, :]
            + cw_ref[2:3, :] * zc)
    y_ref[0, :, W_POOL:W_POOL + W_CONV] = (cb * conv * _silu(cg)).astype(BF16)
    cst_ref[0] = zcext[tl + 6:tl + 8, :]
    zcext[0:8, :] = zcext[tl:tl + 8, :]

    zs = _dot(hb, w_ref[:, OFF_SWA:OFF_SWA + 3 * W_SWA])
    for hp in range(N_PAIRS_DIL):
        lo = hp * BLOCK
        zds[0] = zd[:, lo:lo + BLOCK] * QK_SCALE
        zds[1] = zd[:, W_DIL + lo:W_DIL + lo + BLOCK]
        zds[2] = zd[:, 2 * W_DIL + lo:2 * W_DIL + lo + BLOCK]
        zds[3] = _silu(zd[:, 3 * W_DIL + lo:3 * W_DIL + lo + BLOCK])
        q1_ref[0, hp] = zds[0].astype(BF16)
        k1_ref[0, hp] = zds[1].astype(BF16)
        v1_ref[0, hp] = zds[2].astype(BF16)
        quarter = tl // 4
        for a in range(4):
            for blk in range(4):
                z4s[blk, a] = zds[blk, pl.ds(a, quarter, stride=4), :]
        for b in range(4):
            for a in range(4):
                c = 4 * b + a
                rows = pl.ds(b, tl // N_CLASSES, stride=4)
                qm_ref[0, hp, c] = z4s[0, a, rows, :].astype(BF16)
                km_ref[0, hp, c] = z4s[1, a, rows, :].astype(BF16)
                vm_ref[0, hp, c] = z4s[2, a, rows, :].astype(BF16)
                gm_ref[0, hp, c] = z4s[3, a, rows, :].astype(BF16)
    kc_ref[0, 0] = zd[:, W_DIL:2 * W_DIL].T
    vc_ref[0, 0] = zd[:, 2 * W_DIL:3 * W_DIL].T

    sk = zs[:, W_SWA:W_SWA + W_SWA_KV]
    sv = zs[:, W_SWA + W_SWA_KV:W_SWA + 2 * W_SWA_KV]
    skc_ref[0] = sk[tl - SWA_WINDOW:tl, :].T
    svc_ref[0] = sv[tl - SWA_WINDOW:tl, :].T
    c0 = W_POOL + W_CONV
    y_ref[0, :, c0:c0 + W_SWA] = (zs[:, 0:W_SWA] * QK_SCALE).astype(BF16)
    y_ref[0, :, c0 + W_SWA:c0 + 2 * W_SWA] = zs[:, W_SWA:2 * W_SWA].astype(BF16)
    y_ref[0, :, c0 + 2 * W_SWA:c0 + 3 * W_SWA] = _silu(zs[:, 2 * W_SWA:3 * W_SWA]).astype(BF16)

    _cache_attention(dq_ref, sq_ref, dbias_ref, mult_ref, sbias_ref, ssink_ref, kt_ref, vt_ref,
                     ck_ref, cv_ref, bo_ref, do_ref)


def _cache_attention_specs(layer, row0, nb, nt, r, swr):
    bt = DEC_TILE
    blk0 = row0 // bt
    full = lambda *shape: pl.BlockSpec(shape, lambda n, t: (0,) * len(shape))
    srow_spec = pl.BlockSpec((bt, 3, 8, HEAD_DIM), lambda n, t: (blk0 + n * nt + t, 0, 0, 0))
    dil_spec = pl.BlockSpec((1, bt, N_HEADS_DIL, HEAD_DIM, r),
                            lambda n, t: (layer, blk0 + n * nt + t, 0, 0, 0))
    swa_spec = pl.BlockSpec((1, bt, 2, HEAD_DIM, swr), lambda n, t: (layer, blk0 + n * nt + t, 0, 0, 0))
    sout_spec = pl.BlockSpec((bt, 8, HEAD_DIM), lambda n, t: (n * nt + t, 0, 0))
    sout_shape = jax.ShapeDtypeStruct((nb * nt * bt, 8, HEAD_DIM), F32)
    in_specs = [srow_spec, srow_spec, full(8, r), full(1, r), full(8, swr), full(8, BLOCK),
                dil_spec, dil_spec, swa_spec, swa_spec]
    return in_specs, [sout_spec, sout_spec], [sout_shape, sout_shape]


N_PA_INPUTS = 17


def _pa_kernel_carry(*refs):
    _pa_kernel(*refs[:N_PA_INPUTS], *refs[N_PA_INPUTS + 2:])


def _prompt_a(x, mod_p, norm_g, w_in_b, wbd, pool_scale, conv_w, layer, depth, row0, cache_args, carried):
    nb, l, _ = x.shape
    tl = SEQ_TILE
    nt = l // tl
    cache_t0 = (l - DIL_MAX) // tl
    c_in, c_out, c_shape = _cache_attention_specs(layer, row0, nb, nt, cache_args[6].shape[-1],
                                                  cache_args[8].shape[-1])
    pair_spec = pl.BlockSpec((1, N_PAIRS_DIL, tl, BLOCK), lambda n, t: (n, 0, t, 0))
    pair_shape = jax.ShapeDtypeStruct((nb, N_PAIRS_DIL, l, BLOCK), BF16)
    cm_spec = pl.BlockSpec((1, N_PAIRS_DIL, N_CLASSES, tl // N_CLASSES, BLOCK),
                           lambda n, t: (n, 0, 0, t, 0))
    cm_shape = jax.ShapeDtypeStruct((nb, N_PAIRS_DIL, N_CLASSES, l // N_CLASSES, BLOCK), BF16)
    cache_spec = pl.BlockSpec((1, 1, W_DIL, tl),
                              lambda n, t: (layer, n, 0, jnp.maximum(t - cache_t0, 0)))
    cache_shape = jax.ShapeDtypeStruct((depth, nb, W_DIL, DIL_MAX), F32)
    full = lambda *shape: pl.BlockSpec(shape, lambda n, t: (0,) * len(shape))
    in_specs = [
        pl.BlockSpec((1, tl, D_MODEL), lambda n, t: (n, t, 0)),
        pl.BlockSpec((1, 1, 3 * D_MODEL), lambda n, t: (n, 0, 0)),
        full(1, D_MODEL),
        pl.BlockSpec((D_MODEL, D_PROJ), lambda n, t: (0, 0), pipeline_mode=pl.Buffered(1)),
        full(W_POOL, W_POOL),
        full(1, W_POOL),
        full(3, W_CONV),
    ] + c_in
    assert len(in_specs) == N_PA_INPUTS
    extra, aliases = (), {}
    if carried is not None:
        in_specs = in_specs + [pl.BlockSpec(memory_space=pl.ANY)] * 2
        extra = tuple(carried)
        aliases = {N_PA_INPUTS: 7, N_PA_INPUTS + 1: 8}
    return pl.pallas_call(
        _pa_kernel if carried is None else _pa_kernel_carry,
        grid=(nb, nt),
        in_specs=in_specs,
        input_output_aliases=aliases,
        out_specs=[
            pair_spec, pair_spec, pair_spec, cm_spec, cm_spec, cm_spec, cm_spec,
            cache_spec, cache_spec,
            pl.BlockSpec((1, tl, Y_WIDTH), lambda n, t: (n, t, 0)),
            pl.BlockSpec((1, POOL_BUF, W_POOL), lambda n, t: (n, 0, 0)),
            pl.BlockSpec((1, CONV_BUF, W_CONV), lambda n, t: (n, 0, 0)),
            pl.BlockSpec((1, SWA_WINDOW, W_SWA_KV), lambda n, t: (n, 0, 0)),
            pl.BlockSpec((1, SWA_WINDOW, W_SWA_KV), lambda n, t: (n, 0, 0)),
        ] + c_out,
        out_shape=[
            pair_shape, pair_shape, pair_shape, cm_shape, cm_shape, cm_shape, cm_shape,
            cache_shape, cache_shape,
            jax.ShapeDtypeStruct((nb, l, Y_WIDTH), BF16),
            jax.ShapeDtypeStruct((nb, POOL_BUF, W_POOL), F32),
            jax.ShapeDtypeStruct((nb, CONV_BUF, W_CONV), F32),
            jax.ShapeDtypeStruct((nb, SWA_WINDOW, W_SWA_KV), F32),
            jax.ShapeDtypeStruct((nb, SWA_WINDOW, W_SWA_KV), F32),
        ] + c_shape,
        scratch_shapes=[
            pltpu.VMEM((len(POOL_WINDOWS), 16 + tl, W_POOL), F32),
            pltpu.VMEM((8 + tl, W_CONV), F32),
            pltpu.VMEM((4, tl, BLOCK), F32),
            pltpu.VMEM((4, 4, tl // 4, BLOCK), F32),
        ],
        compiler_params=_compiler_params(("arbitrary", "arbitrary")),
        name="prompt_a",
    )(x, mod_p, norm_g, w_in_b, wbd, pool_scale, conv_w, *cache_args, *extra)


def _band_pair(q, kblk, vblk, bias, lane_q):
    lhs = jnp.concatenate([jnp.where(lane_q < HEAD_DIM, q, jnp.zeros_like(q)),
                           jnp.where(lane_q >= HEAD_DIM, q, jnp.zeros_like(q))], axis=0)
    s = _dot_nt(lhs, kblk) + bias
    m = jnp.max(s, axis=-1, keepdims=True)
    p = jnp.exp(s - m)
    den = jnp.sum(p, axis=-1, keepdims=True)
    acc = _dot(p.astype(BF16), vblk)
    lo = lane_q < HEAD_DIM
    return (jnp.where(lo, acc[0:BLOCK], acc[BLOCK:2 * BLOCK]),
            jnp.where(lo, m[0:BLOCK], m[BLOCK:2 * BLOCK]),
            jnp.where(lo, den[0:BLOCK], den[BLOCK:2 * BLOCK]))


def _pb_kernel(bias_ref, q1, k1, v1, qm, km, vm, gm, outm, s1, s4, s16):
    c = pl.program_id(2)
    first = jnp.where(c == 0, 1, 0)
    lane_q = lax.broadcasted_iota(jnp.int32, (BLOCK, BLOCK), 1)
    sub = BLOCK // 4

    for j in range(CHUNK // BLOCK):
        r0 = j * BLOCK
        if j == 0:
            start = jnp.maximum(c * (CHUNK // BLOCK) - 1, 0) * BLOCK
            bias = bias_ref[0, 0, first]
        else:
            start = (c * (CHUNK // BLOCK) + (j - 1)) * BLOCK
            bias = bias_ref[0, 0, 0]
        start = pl.multiple_of(start, BLOCK)
        parts = _band_pair(q1[0, 0, r0:r0 + BLOCK, :], k1[0, 0, pl.ds(start, 2 * BLOCK), :],
                           v1[0, 0, pl.ds(start, 2 * BLOCK), :], bias, lane_q)
        for k, part in enumerate(parts):
            s1[k, r0:r0 + BLOCK, :] = part

    for c4 in range(4):
        for j in range(4):
            i0 = j * sub
            if j == 0:
                istart = jnp.maximum(c * (CHUNK // N_CLASSES) - sub, 0)
                bias = bias_ref[0, 1, first]
            else:
                istart = c * (CHUNK // N_CLASSES) + i0 - sub
                bias = bias_ref[0, 1, 0]
            istart = pl.multiple_of(istart, sub)
            classes = [4 * cc + c4 for cc in range(4)]
            q = jnp.concatenate([qm[0, 0, cl, i0:i0 + sub, :] for cl in classes], axis=0)
            kblk = jnp.concatenate([km[0, 0, cl, pl.ds(istart, 2 * sub), :] for cl in classes], axis=0)
            vblk = jnp.concatenate([vm[0, 0, cl, pl.ds(istart, 2 * sub), :] for cl in classes], axis=0)
            parts = _band_pair(q, kblk, vblk, bias, lane_q)
            for k, part in enumerate(parts):
                for cc, cl in enumerate(classes):
                    s4[k, cl, i0:i0 + sub, :] = part[cc * sub:(cc + 1) * sub]

    start = pl.multiple_of(jnp.maximum(c - 1, 0) * BLOCK, BLOCK)
    for cl in range(N_CLASSES):
        parts = _band_pair(qm[0, 0, cl], km[0, 0, cl, pl.ds(start, 2 * BLOCK), :],
                           vm[0, 0, cl, pl.ds(start, 2 * BLOCK), :], bias_ref[0, 2, first], lane_q)
        for k, part in enumerate(parts):
            s16[k, cl] = part

    for cl in range(N_CLASSES):
        rows = pl.ds(cl, BLOCK, stride=N_CLASSES)
        ma, mb, mc = s1[1, rows, :], s4[1, cl], s16[1, cl]
        mx = jnp.maximum(jnp.maximum(ma, mb), mc)
        wa = jnp.exp(ma - mx)
        wb = jnp.exp(mb - mx)
        wc = jnp.exp(mc - mx)
        num = wa * s1[0, rows, :] + wb * s4[0, cl] + wc * s16[0, cl]
        den = wa * s1[2, rows, :] + wb * s4[2, cl] + wc * s16[2, cl]
        outm[0, 0, cl] = (num / den * gm[0, 0, cl].astype(F32)).astype(BF16)


def _prompt_b(q1, k1, v1, qm, km, vm, gm, dil_bias):
    nb, npair, l, _ = q1.shape
    nc = l // CHUNK
    li = l // N_CLASSES
    ci = CHUNK // N_CLASSES
    q1_spec = pl.BlockSpec((1, 1, CHUNK, BLOCK), lambda n, h, c: (n, h, c, 0))
    kv1_spec = pl.BlockSpec((1, 1, l, BLOCK), lambda n, h, c: (n, h, 0, 0))
    cm_spec = pl.BlockSpec((1, 1, N_CLASSES, ci, BLOCK), lambda n, h, c: (n, h, 0, c, 0))
    kvm_spec = pl.BlockSpec((1, 1, N_CLASSES, li, BLOCK), lambda n, h, c: (n, h, 0, 0, 0))
    cm_scratch = pltpu.VMEM((3, N_CLASSES, ci, BLOCK), F32)
    return pl.pallas_call(
        _pb_kernel,
        grid=(nb, npair, nc),
        in_specs=[pl.BlockSpec((1, 3, 2, 2 * BLOCK, 2 * BLOCK), lambda n, h, c: (h, 0, 0, 0, 0)),
                  q1_spec, kv1_spec, kv1_spec, cm_spec, kvm_spec, kvm_spec, cm_spec],
        out_specs=cm_spec,
        out_shape=jax.ShapeDtypeStruct((nb, npair, N_CLASSES, li, BLOCK), BF16),
        scratch_shapes=[
            pltpu.VMEM((3, CHUNK, BLOCK), F32), cm_scratch, cm_scratch,
        ],
        compiler_params=_compiler_params(("arbitrary", "arbitrary", "arbitrary")),
        name="prompt_b",
    )(dil_bias, q1, k1, v1, qm, km, vm, gm)


def _mix_out(y, x, gate, w_ref, fg_ref, final):
    xn = x + gate * _dot(y, w_ref[...])
    if final:
        ms = jnp.mean(xn * xn, axis=-1, keepdims=True)
        xn = xn * lax.rsqrt(ms + RMS_EPS) * fg_ref[...]
    return xn


def _pc_kernel(sink_ref, x_ref, mod_ref, yacd_ref, yb_ref, w_ref, fg_ref, bias_ref,
               dq_ref, sq_ref, dbias_ref, mult_ref, sbias_ref, ssink_ref, kt_ref, vt_ref, ck_ref, cv_ref,
               o_ref, bo_ref, do_ref, ybs, kext, vext, yds, *, final):
    tl = SEQ_TILE
    t = pl.program_id(1)
    _cache_attention(dq_ref, sq_ref, dbias_ref, mult_ref, sbias_ref, ssink_ref, kt_ref, vt_ref,
                     ck_ref, cv_ref, bo_ref, do_ref)

    @pl.when(t == 0)
    def _():
        kext[0:BLOCK, :] = jnp.zeros((BLOCK, W_SWA_KV), BF16)
        vext[0:BLOCK, :] = jnp.zeros((BLOCK, W_SWA_KV), BF16)

    c0 = W_POOL + W_CONV
    kext[BLOCK:BLOCK + tl, :] = yacd_ref[0, :, c0 + W_SWA:c0 + W_SWA + W_SWA_KV]
    vext[BLOCK:BLOCK + tl, :] = yacd_ref[0, :, c0 + W_SWA + W_SWA_KV:c0 + 2 * W_SWA]
    first = jnp.where(t == 0, 1, 0)
    lane_q = lax.broadcasted_iota(jnp.int32, (BLOCK, BLOCK), 1)
    row_s = lax.broadcasted_iota(jnp.int32, (2 * BLOCK, 1), 0)
    for jb in range(tl // BLOCK):
        r0 = jb * BLOCK
        kblk = kext[r0:r0 + 2 * BLOCK, :]
        vblk = vext[r0:r0 + 2 * BLOCK, :]
        for grp in range(2):
            q = yacd_ref[0, r0:r0 + BLOCK, c0 + grp * BLOCK:c0 + (grp + 1) * BLOCK]
            lhs = jnp.concatenate([jnp.where(lane_q < HEAD_DIM, q, jnp.zeros_like(q)),
                                   jnp.where(lane_q >= HEAD_DIM, q, jnp.zeros_like(q))], axis=0)
            s = _dot_nt(lhs, kblk)
            if jb == 0:
                s = s + bias_ref[grp, first]
            else:
                s = s + bias_ref[grp, 0]
            sink = jnp.where(row_s < BLOCK, sink_ref[SWA_HEAD_PERM[2 * grp]],
                             sink_ref[SWA_HEAD_PERM[2 * grp + 1]])
            m = jnp.maximum(jnp.max(s, axis=-1, keepdims=True), sink)
            p = jnp.exp(s - m)
            den = jnp.sum(p, axis=-1, keepdims=True) + jnp.exp(sink - m)
            o = _dot(p.astype(BF16), vblk) / den
            od = jnp.where(lane_q < HEAD_DIM, o[0:BLOCK], o[BLOCK:2 * BLOCK])
            g0 = c0 + 2 * W_SWA + grp * BLOCK
            gate_s = yacd_ref[0, r0:r0 + BLOCK, g0:g0 + BLOCK].astype(F32)
            yds[r0:r0 + BLOCK, grp * BLOCK:(grp + 1) * BLOCK] = (od * gate_s).astype(BF16)
    kext[0:BLOCK, :] = kext[tl:tl + BLOCK, :]
    vext[0:BLOCK, :] = vext[tl:tl + BLOCK, :]

    for hp in range(N_PAIRS_DIL):
        for c in range(N_CLASSES):
            ybs[hp, pl.ds(c, tl // N_CLASSES, stride=N_CLASSES), :] = yb_ref[0, hp, c].astype(F32)
    y = jnp.concatenate([yacd_ref[0, :, 0:W_POOL], ybs[0].astype(BF16), ybs[1].astype(BF16),
                         ybs[2].astype(BF16), yacd_ref[0, :, W_POOL:W_POOL + W_CONV], yds[...]], axis=-1)
    gate = mod_ref[0][:, 2 * D_MODEL:3 * D_MODEL]
    o_ref[0] = _mix_out(y, x_ref[0], gate, w_ref, fg_ref, final)


def _prompt_c(x, mod_p, yacd, yb, w_out_b, final_g, sink, swa_bias, final, layer, row0, cache_args):
    nb, l, _ = x.shape
    tl = SEQ_TILE
    full = lambda *shape: pl.BlockSpec(shape, lambda n, t: (0,) * len(shape))
    c_in, c_out, c_shape = _cache_attention_specs(layer, row0, nb, l // tl, cache_args[6].shape[-1],
                                                  cache_args[8].shape[-1])
    return pl.pallas_call(
        functools.partial(_pc_kernel, final=final),
        grid=(nb, l // tl),
        in_specs=[
            pl.BlockSpec(memory_space=pltpu.SMEM),
            pl.BlockSpec((1, tl, D_MODEL), lambda n, t: (n, t, 0)),
            pl.BlockSpec((1, 1, 3 * D_MODEL), lambda n, t: (n, 0, 0)),
            pl.BlockSpec((1, tl, Y_WIDTH), lambda n, t: (n, t, 0)),
            pl.BlockSpec((1, N_PAIRS_DIL, N_CLASSES, tl // N_CLASSES, BLOCK),
                         lambda n, t: (n, 0, 0, t, 0)),
            full(D_MIX, D_MODEL),
            full(1, D_MODEL),
            full(2, 2, 2 * BLOCK, 2 * BLOCK),
        ] + c_in,
        out_specs=[pl.BlockSpec((1, tl, D_MODEL), lambda n, t: (n, t, 0))] + c_out,
        out_shape=[jax.ShapeDtypeStruct((nb, l, D_MODEL), F32)] + c_shape,
        scratch_shapes=[
            pltpu.VMEM((N_PAIRS_DIL, tl, BLOCK), F32),
            pltpu.VMEM((BLOCK + tl, W_SWA_KV), BF16),
            pltpu.VMEM((BLOCK + tl, W_SWA_KV), BF16),
            pltpu.VMEM((tl, W_SWA), BF16),
        ],
        compiler_params=_compiler_params(("arbitrary", "arbitrary")),
        name="prompt_c",
    )(sink, x, mod_p, yacd, yb, w_out_b, final_g, swa_bias, *cache_args)


def _sa_kernel(x_ref, mod_ref, g_ref, w_ref, wbd_ref, ps_ref, cw_ref, sp_ref, sc_ref,
               qkv_ref, sw_ref, yac_ref, gates_ref, pst_ref, cst_ref):
    hb = _modulated_norm(x_ref[...], g_ref[...], mod_ref[...]).astype(BF16)
    ns = hb.shape[0]

    zp = _dot(hb, w_ref[:, OFF_POOL:OFF_POOL + 2 * W_POOL])
    pu = zp[:, 0:W_POOL]
    pg = zp[:, W_POOL:2 * W_POOL]
    acc = pu
    sums = {}
    for j in range(1, 16):
        acc = acc + sp_ref[POOL_BUF - j]
        if j + 1 in POOL_WINDOWS:
            sums[j + 1] = acc
    lane = lax.broadcasted_iota(jnp.int32, (ns, W_POOL), 1)
    sel, win = _pool_select(sums, lane)
    diff = sel / win.astype(F32) - pu
    a_out = _dot(diff.astype(BF16), wbd_ref[...]) * ps_ref[...]
    yac_ref[:, 0:W_POOL] = a_out * _silu(pg)
    pst_ref[0:POOL_BUF - 1] = sp_ref[1:POOL_BUF]
    pst_ref[POOL_BUF - 1] = pu

    zc4 = _dot(hb, w_ref[:, OFF_CONV:OFF_CONV + 4 * W_CONV])
    ch = zc4[:, 0:W_CONV]
    cb = zc4[:, W_CONV:2 * W_CONV]
    cc = zc4[:, 2 * W_CONV:3 * W_CONV]
    cg = zc4[:, 3 * W_CONV:4 * W_CONV]
    zc = cc * ch
    conv = (cw_ref[0:1, :] * sc_ref[:, 0:W_CONV] + cw_ref[1:2, :] * sc_ref[:, W_CONV:2 * W_CONV]
            + cw_ref[2:3, :] * zc)
    yac_ref[:, W_POOL:W_POOL + W_CONV] = cb * conv * _silu(cg)
    cst_ref[:, 0:W_CONV] = sc_ref[:, W_CONV:2 * W_CONV]
    cst_ref[:, W_CONV:2 * W_CONV] = zc

    zd = _dot(hb, w_ref[:, OFF_DIL:OFF_DIL + 4 * W_DIL])
    qkv_ref[:, 0:W_DIL] = zd[:, 0:W_DIL] * QK_SCALE
    qkv_ref[:, W_DIL:3 * W_DIL] = zd[:, W_DIL:3 * W_DIL]
    gates_ref[:, 0:W_DIL] = _silu(zd[:, 3 * W_DIL:4 * W_DIL])

    zs = _dot(hb, w_ref[:, OFF_SWA:OFF_SWA + 3 * W_SWA])
    sw_ref[:, 0:W_SWA] = zs[:, 0:W_SWA] * QK_SCALE
    sw_ref[:, W_SWA:2 * W_SWA] = zs[:, W_SWA:2 * W_SWA]
    gates_ref[:, W_DIL:W_DIL + W_SWA] = _silu(zs[:, 2 * W_SWA:3 * W_SWA])


def _sample_a(xs, mod_s, norm_g, w_in_b, wbd, pool_scale, conv_w, sp, sc):
    ns = xs.shape[0]
    shapes = [
        jax.ShapeDtypeStruct((ns, 3 * W_DIL), F32),
        jax.ShapeDtypeStruct((ns, 2 * W_SWA), F32),
        jax.ShapeDtypeStruct((ns, W_POOL + W_CONV), F32),
        jax.ShapeDtypeStruct((ns, W_DIL + W_SWA), F32),
        jax.ShapeDtypeStruct((POOL_BUF, ns, W_POOL), F32),
        jax.ShapeDtypeStruct((ns, CONV_BUF * W_CONV), F32),
    ]
    return pl.pallas_call(
        _sa_kernel,
        out_shape=shapes,
        compiler_params=pltpu.CompilerParams(vmem_limit_bytes=VMEM_LIMIT),
        name="sample_a",
    )(xs, mod_s, norm_g, w_in_b, wbd, pool_scale, conv_w, sp, sc)


def _cache_attention(dq_ref, sq_ref, dbias_ref, mult_ref, sbias_ref, sink_ref, kt_ref, vt_ref,
                     ck_ref, cv_ref, bo_ref, do_ref):
    row_d = lax.broadcasted_iota(jnp.int32, (8, DIL_MAX), 0)
    row_o = lax.broadcasted_iota(jnp.int32, (8, HEAD_DIM), 0)
    row_s = lax.broadcasted_iota(jnp.int32, (8, SWA_WINDOW), 0)
    for j in range(DEC_TILE):
        q = dq_ref[j, 0]
        qb = q.astype(BF16)
        s = jnp.zeros((8, DIL_MAX), F32)
        for h in range(N_HEADS_DIL):
            s = jnp.where(row_d == h, _dot(qb, kt_ref[0, j, h].astype(BF16)), s)
        s = s + dbias_ref[...]
        s_self = jnp.sum(q * dq_ref[j, 1], axis=-1, keepdims=True)
        m = jnp.maximum(jnp.max(s, axis=-1, keepdims=True), s_self)
        p = jnp.exp(s - m) * mult_ref[...]
        p_self = float(len(DIL_CONFIGS)) * jnp.exp(s_self - m)
        den = jnp.sum(p, axis=-1, keepdims=True) + p_self
        pb = p.astype(BF16)
        acc = jnp.zeros((8, HEAD_DIM), F32)
        for h in range(N_HEADS_DIL):
            acc = jnp.where(row_o == h, _dot_nt(pb, vt_ref[0, j, h].astype(BF16)), acc)
        bo_ref[j] = (acc + p_self * dq_ref[j, 2]) / den

        q = sq_ref[j, 0]
        qb = q.astype(BF16)
        s = jnp.where(row_s % 2 == 0, _dot(qb, ck_ref[0, j, 0].astype(BF16)),
                      _dot(qb, ck_ref[0, j, 1].astype(BF16))) + sbias_ref[...]
        w_self = jnp.sum(q * sq_ref[j, 1], axis=-1, keepdims=True)
        sink = sink_ref[...][:, 0:1]
        m = jnp.maximum(jnp.maximum(jnp.max(s, axis=-1, keepdims=True), w_self), sink)
        p = jnp.exp(s - m)
        pw = jnp.exp(w_self - m)
        den = jnp.sum(p, axis=-1, keepdims=True) + pw + jnp.exp(sink - m)
        pb = p.astype(BF16)
        acc = jnp.where(row_o % 2 == 0, _dot_nt(pb, cv_ref[0, j, 0].astype(BF16)),
                        _dot_nt(pb, cv_ref[0, j, 1].astype(BF16)))
        do_ref[j] = (acc + pw * sq_ref[j, 2]) / den


def _sc_kernel(x_ref, mod_ref, yac_ref, gates_ref, bo_ref, do_ref, w_ref, fg_ref, o_ref, *, final):
    yac = yac_ref[...]
    gates = gates_ref[...]
    y = jnp.concatenate([yac[:, 0:W_POOL], bo_ref[...] * gates[:, 0:W_DIL],
                         yac[:, W_POOL:W_POOL + W_CONV], do_ref[...] * gates[:, W_DIL:W_DIL + W_SWA]],
                        axis=-1).astype(BF16)
    gate = mod_ref[...][:, 2 * D_MODEL:3 * D_MODEL]
    o_ref[...] = _mix_out(y, x_ref[...], gate, w_ref, fg_ref, final)


def _sample_c(xs, mod_s, yac, gates, bo, do, w_out_b, final_g, final):
    return pl.pallas_call(
        functools.partial(_sc_kernel, final=final),
        out_shape=jax.ShapeDtypeStruct(xs.shape, F32),
        compiler_params=pltpu.CompilerParams(vmem_limit_bytes=VMEM_LIMIT),
        name="sample_c",
    )(xs, mod_s, yac, gates, bo, do, w_out_b, final_g)


def _band_bias_dil4(slope_lo, slope_hi, variant):
    sub = BLOCK // 4
    qidx = np.arange(BLOCK)[:, None]
    kidx = np.arange(2 * BLOCK)[None, :]
    q_step = 4 * (qidx % sub + (sub if variant == 0 else 0)) + qidx // sub
    k_step = 4 * (kidx % (2 * sub)) + kidx // (2 * sub)
    off = q_step - k_step
    valid = (off >= 0) & (off <= BLOCK)
    out = []
    for s in (slope_lo, slope_hi):
        out.append(np.where(valid, -(np.float32(s) * np.float32(4)) * off.astype(np.float32),
                            np.float32(NEG_INF)).astype(np.float32))
    return np.concatenate(out, axis=0)


def _prompt_bias_tables():
    dil = _alibi_slopes(N_HEADS_DIL)

    def table(hp, d, var):
        if d == 4:
            return _band_bias_dil4(dil[2 * hp], dil[2 * hp + 1], var)
        return _band_bias(dil[2 * hp], dil[2 * hp + 1], d, var)

    dil_bias = np.stack([
        np.stack([np.stack([table(hp, d, var) for var in (0, 1)])
                  for _, d in DIL_CONFIGS]) for hp in range(N_PAIRS_DIL)])
    swa = _alibi_slopes(N_HEADS_SWA)
    swa_bias = np.stack([
        np.stack([_band_bias(swa[SWA_HEAD_PERM[2 * g]], swa[SWA_HEAD_PERM[2 * g + 1]], 1, var)
                  for var in (0, 2)]) for g in range(2)])
    return jnp.asarray(dil_bias), jnp.asarray(swa_bias)


def _sample_bias_tables(r):
    dil = _alibi_slopes(N_HEADS_DIL)
    dist = (r - np.arange(r)).astype(np.float32)
    mult = np.zeros((1, r), np.float32)
    for window, d in DIL_CONFIGS:
        mult[0] += ((dist <= window) & (dist % d == 0)).astype(np.float32)
    dbias = np.zeros((8, r), np.float32)
    for h in range(N_HEADS_DIL):
        dbias[h] = np.where(mult[0] > 0, -np.float32(dil[h]) * dist, np.float32(NEG_INF))
    swa = _alibi_slopes(N_HEADS_SWA)
    sdist = (SWA_WINDOW - np.arange(SWA_WINDOW)).astype(np.float32)
    sbias = np.zeros((8, SWA_WINDOW), np.float32)
    for row, h in enumerate(SWA_HEAD_PERM):
        sbias[row] = -np.float32(swa[h]) * sdist
    return jnp.asarray(dbias), jnp.asarray(mult), jnp.asarray(sbias)


def kernel(x_prompt, x_sample, c_prompt, c_sample, state_pool, cache_dil_k, cache_dil_v, state_conv, cache_swa_k, cache_swa_v, norm_g, w_ada, b_ada, w_in, w_pool, pool_scale, conv_w, swa_sink, w_out, final_g):
    depth = w_in.shape[0]
    nb, l, _ = x_prompt.shape
    ns = x_sample.shape[0]
    assert x_sample.shape[1] == 1 and l % CHUNK == 0 and l >= DIL_MAX
    assert ns == 2 * nb * (l // SEQ_TILE) * DEC_TILE
    assert cache_dil_k.shape[2] == DIL_MAX and cache_swa_k.shape[2] == SWA_WINDOW

    def perm_heads(w, axis, off):
        take = lambda lo, hi: lax.slice_in_dim(w, lo, hi, axis=axis)
        return [take(off + h * HEAD_DIM, off + (h + 1) * HEAD_DIM) for h in SWA_HEAD_PERM]

    take_in = lambda lo, hi: lax.slice_in_dim(w_in, lo, hi, axis=2)
    w_in_b = jnp.concatenate(
        [take_in(0, OFF_SWA)] + perm_heads(w_in, 2, OFF_SWA)
        + [take_in(OFF_SWA + W_SWA, OFF_SWA + 2 * W_SWA)] + perm_heads(w_in, 2, OFF_SWA + 2 * W_SWA),
        axis=2).astype(BF16)
    w_out_b = jnp.concatenate(
        [lax.slice_in_dim(w_out, 0, D_MIX - W_SWA, axis=1)] + perm_heads(w_out, 1, D_MIX - W_SWA),
        axis=1).astype(BF16)
    eye = jnp.eye(len(POOL_WINDOWS), dtype=F32)
    wbd = jnp.einsum("dgce,gh->dgche", w_pool, eye).reshape(depth, W_POOL, W_POOL).astype(BF16)
    sink_rows = jnp.zeros((depth, 8), F32).at[:, 0:N_HEADS_SWA].set(swa_sink[:, list(SWA_HEAD_PERM)])
    sink_rows = jnp.broadcast_to(sink_rows[:, :, None], (depth, 8, BLOCK))

    dil_bias, swa_bias = _prompt_bias_tables()
    dbias, mult, sbias = _sample_bias_tables(cache_dil_k.shape[2])

    mod = _ada(jnp.concatenate([c_prompt, c_sample], axis=0), w_ada, b_ada)
    fg = final_g.reshape(1, D_MODEL)

    xp = x_prompt
    xs = x_sample.reshape(ns, D_MODEL)
    kt = jnp.transpose(cache_dil_k, (0, 1, 3, 4, 2))
    vt = jnp.transpose(cache_dil_v, (0, 1, 3, 4, 2))
    ckt = jnp.transpose(cache_swa_k, (0, 1, 3, 4, 2))
    cvt = jnp.transpose(cache_swa_v, (0, 1, 3, 4, 2))
    sp_all = jnp.transpose(state_pool, (0, 2, 1, 3))
    sc_all = state_conv.reshape(depth, ns, CONV_BUF * W_CONV)
    pad_heads = lambda a: jnp.pad(a, ((0, 0), (0, 0), (0, 8 - a.shape[2]), (0, 0)))
    outs = [[] for _ in range(12)]
    carried = None
    for i in range(depth):
        final = i == depth - 1
        mod_p = mod[i, 0:nb].reshape(nb, 1, 3 * D_MODEL)
        mod_s = mod[i, nb:nb + ns]
        g = norm_g[i].reshape(1, D_MODEL)
        ps = pool_scale[i].reshape(1, W_POOL)

        qkv, sw, yac, gates, pst_s, cst_s = _sample_a(
            xs, mod_s, g, w_in_b[i], wbd[i], ps, conv_w[i], sp_all[i], sc_all[i])
        dq3 = pad_heads(qkv.reshape(ns, 3, N_HEADS_DIL, HEAD_DIM))
        sq4 = sw[:, 0:W_SWA].reshape(ns, 1, N_HEADS_SWA, HEAD_DIM)
        skv = jnp.tile(sw[:, W_SWA:W_SWA + 2 * W_SWA_KV].reshape(ns, 2, 2, HEAD_DIM), (1, 1, 2, 1))
        sq3 = pad_heads(jnp.concatenate([sq4, skv], axis=1))

        cache_args = (dq3, sq3, dbias, mult, sbias, sink_rows[i], kt, vt, ckt, cvt)
        q1, k1, v1, qm, km, vm, gm, kc, vc, yacd, pst, cst, skc, svc, bo_a, do_a = _prompt_a(
            xp, mod_p, g, w_in_b[i], wbd[i], ps, conv_w[i], i, depth, 0, cache_args, carried)
        carried = (kc, vc)
        yb = _prompt_b(q1, k1, v1, qm, km, vm, gm, dil_bias)
        xp, bo_c, do_c = _prompt_c(xp, mod_p, yacd, yb, w_out_b[i], fg, swa_sink[i], swa_bias, final,
                                   i, ns // 2, cache_args)

        bo = jnp.concatenate([bo_a, bo_c], axis=0)[:, 0:N_HEADS_DIL].reshape(ns, W_DIL)
        do = jnp.concatenate([do_a, do_c], axis=0)[:, 0:N_HEADS_SWA].reshape(ns, W_SWA)
        xs = _sample_c(xs, mod_s, yac, gates, bo, do, w_out_b[i], fg, final)

        unfold = lambda a, h: jnp.transpose(a.reshape(nb, h, HEAD_DIM, a.shape[-1]), (0, 3, 1, 2))
        skc, svc = unfold(skc, 2), unfold(svc, 2)
        outs[0].append(pst)
        outs[1].append(jnp.transpose(pst_s, (1, 0, 2)))
        outs[4].append(qkv[:, W_DIL:2 * W_DIL].reshape(ns, 1, N_HEADS_DIL, HEAD_DIM))
        outs[5].append(qkv[:, 2 * W_DIL:3 * W_DIL].reshape(ns, 1, N_HEADS_DIL, HEAD_DIM))
        outs[6].append(cst)
        outs[7].append(cst_s.reshape(ns, CONV_BUF, W_CONV))
        outs[8].append(skc)
        outs[9].append(svc)
        outs[10].append(sw[:, W_SWA:W_SWA + W_SWA_KV].reshape(ns, 1, 2, HEAD_DIM))
        outs[11].append(sw[:, W_SWA + W_SWA_KV:W_SWA + 2 * W_SWA_KV].reshape(ns, 1, 2, HEAD_DIM))

    for k, cache in zip((2, 3), carried):
        outs[k] = jnp.transpose(cache.reshape(depth, nb, N_HEADS_DIL, HEAD_DIM, DIL_MAX), (0, 1, 4, 2, 3))
    return (xp, xs.reshape(ns, 1, D_MODEL)) + tuple(
        o if not isinstance(o, list) else jnp.stack(o) for o in outs)
```

```python
import functools
import math

import numpy as np
import jax
import jax.numpy as jnp
from jax import lax
from jax.experimental import pallas as pl
from jax.experimental.pallas import tpu as pltpu

F32 = jnp.float32
BF16 = jnp.bfloat16

D_MODEL = 1024
HEAD_DIM = 64
BLOCK = 128
POOL_WINDOWS = (2, 4, 8, 16)
POOL_GROUP = 64
W_POOL = 256
POOL_BUF = 15
DIL_CONFIGS = ((128, 1), (512, 4), (2048, 16))
DIL_MAX = 2048
N_HEADS_DIL = 6
N_PAIRS_DIL = 3
N_CLASSES = 16
W_DIL = 384
W_CONV = 256
CONV_BUF = 2
N_HEADS_SWA = 4
W_SWA = 256
W_SWA_KV = 128
SWA_WINDOW = 128
D_MIX = 1152
D_PROJ = 3840
RMS_EPS = 1e-6
QK_SCALE = 1.0 / math.sqrt(HEAD_DIM)

OFF_POOL = 0
OFF_DIL = 512
OFF_CONV = 2048
OFF_SWA = 3072
Y_WIDTH = W_POOL + W_CONV + 3 * W_SWA
SWA_HEAD_PERM = (0, 3, 1, 2)

SEQ_TILE = 512
CHUNK = 2048
BAND_SKEW = 1
DEC_TILE = 1
RIDE_STAGES = 15 * DEC_TILE
VMEM_LIMIT = 56 * 1024 * 1024

NEG_INF = float("-inf")


def _silu(v):
    return v * jax.nn.sigmoid(v)


def _dot(a, b):
    return jnp.dot(a, b, preferred_element_type=F32)


def _dot_nt(a, b):
    return lax.dot_general(a, b, (((1,), (1,)), ((), ())), preferred_element_type=F32)


def _alibi_slopes(n):
    return [2.0 ** (-8.0 * (h + 1) / n) for h in range(n)]


def _band_bias(slope_lo, slope_hi, dist_scale, variant):
    qi = np.arange(BLOCK)[:, None]
    kj = np.arange(2 * BLOCK)[None, :]
    if variant == 1:
        off = qi - kj
    else:
        off = qi - kj + BLOCK
    valid = (off >= 0) & (off <= BLOCK)
    if variant == 2:
        valid = valid & (kj >= BLOCK)
    out = []
    for s in (slope_lo, slope_hi):
        b = np.where(valid, -(np.float32(s) * np.float32(dist_scale)) * off.astype(np.float32),
                     np.float32(NEG_INF))
        out.append(b.astype(np.float32))
    return np.concatenate(out, axis=0)


def _compiler_params(sem):
    return pltpu.CompilerParams(dimension_semantics=sem, vmem_limit_bytes=VMEM_LIMIT)


def _ada_kernel(c_ref, w_ref, b_ref, o_ref):
    s = _silu(c_ref[...]).astype(BF16)
    o_ref[0] = _dot(s, w_ref[0].astype(BF16)) + b_ref[0]


def _ada(c_all, w_ada_b, b_ada):
    depth = w_ada_b.shape[0]
    rows = c_all.shape[0]
    return pl.pallas_call(
        _ada_kernel,
        grid=(depth, 3),
        in_specs=[
            pl.BlockSpec((rows, D_MODEL), lambda i, j: (0, 0)),
            pl.BlockSpec((1, D_MODEL, D_MODEL), lambda i, j: (i, 0, j)),
            pl.BlockSpec((1, 1, D_MODEL), lambda i, j: (i, 0, j)),
        ],
        out_specs=pl.BlockSpec((1, rows, D_MODEL), lambda i, j: (i, 0, j)),
        out_shape=jax.ShapeDtypeStruct((depth, rows, 3 * D_MODEL), F32),
        compiler_params=_compiler_params(("arbitrary", "arbitrary")),
        name="ada",
    )(c_all, w_ada_b, b_ada.reshape(depth, 1, 3 * D_MODEL))


def _modulated_norm(x, g, mod_row):
    ms = jnp.mean(x * x, axis=-1, keepdims=True)
    y = x * lax.rsqrt(ms + RMS_EPS) * g
    shift = mod_row[:, 0:D_MODEL]
    scale = mod_row[:, D_MODEL:2 * D_MODEL]
    return y * (1.0 + scale) + shift


def _pair_heads(v):
    left, right = v[:, 0:BLOCK], v[:, BLOCK:2 * BLOCK]
    low = lax.broadcasted_iota(jnp.int32, left.shape, 1) < HEAD_DIM
    return jnp.concatenate([jnp.where(low, left, right),
                            pltpu.roll(jnp.where(low, right, left), HEAD_DIM, axis=1)], axis=-1)


def _unpair_heads(a, b):
    low = lax.broadcasted_iota(jnp.int32, a.shape, 1) < HEAD_DIM
    b_swapped = pltpu.roll(b, HEAD_DIM, axis=1)
    return jnp.where(low, a, b_swapped), jnp.where(low, b_swapped, a)


def _pool_select(sums, lane):
    grp = lane // POOL_GROUP
    sel = jnp.where(grp == 0, sums[2], jnp.where(grp == 1, sums[4],
                                                 jnp.where(grp == 2, sums[8], sums[16])))
    win = jnp.where(grp == 0, 2, jnp.where(grp == 1, 4, jnp.where(grp == 2, 8, 16)))
    return sel, win


def _pa_kernel(x_ref, mod_ref, g_ref, w_ref, wbd_ref, ps_ref, cw_ref,
               dq_ref, sq_ref, dbias_ref, mult_ref, sbias_ref, ssink_ref, kt_ref, vt_ref, ck_ref, cv_ref,
               q1_ref, k1_ref, v1_ref, qm_ref, km_ref, vm_ref, gm_ref, kc_ref, vc_ref, y_ref,
               pst_ref, cst_ref, skc_ref, svc_ref, bo_ref, do_ref,
               uext, zcext, zds, z4s):
    tl = SEQ_TILE
    t = pl.program_id(1)

    @pl.when(t == 0)
    def _():
        for k in range(len(POOL_WINDOWS)):
            uext[k, 0:16, :] = jnp.zeros((16, W_POOL), F32)
        zcext[0:8, :] = jnp.zeros((8, W_CONV), F32)

    ride = _cache_attention(dq_ref, sq_ref, dbias_ref, mult_ref, sbias_ref, ssink_ref, kt_ref, vt_ref,
                            ck_ref, cv_ref, bo_ref, do_ref)

    _advance(ride, 3)
    hb = _modulated_norm(x_ref[0], g_ref[...], mod_ref[0]).astype(BF16)

    zp = _dot(hb, w_ref[:, OFF_POOL:OFF_POOL + 2 * W_POOL])
    zd = _dot(hb, w_ref[:, OFF_DIL:OFF_DIL + 4 * W_DIL])
    _advance(ride, 3)

    pu = zp[:, 0:W_POOL]
    pg = zp[:, W_POOL:2 * W_POOL]
    uext[0, 16:16 + tl, :] = pu
    sums = {}
    level = pu
    for k, w in enumerate(POOL_WINDOWS):
        shift = w // 2
        level = level + uext[k, 16 - shift:16 - shift + tl, :]
        sums[w] = level
        if k + 1 < len(POOL_WINDOWS):
            uext[k + 1, 16:16 + tl, :] = level
    lane = lax.broadcasted_iota(jnp.int32, (tl, W_POOL), 1)
    gpos = lax.broadcasted_iota(jnp.int32, (tl, W_POOL), 0) + t * tl
    sel, win = _pool_select(sums, lane)
    cnt = jnp.minimum(gpos + 1, win).astype(F32)
    diff = sel / cnt - pu
    a_out = _dot(diff.astype(BF16), wbd_ref[...]) * ps_ref[...]
    y_ref[0, :, 0:W_POOL] = (a_out * _silu(pg)).astype(BF16)
    pst_ref[0] = uext[0, tl + 1:tl + 16, :]
    for k in range(len(POOL_WINDOWS)):
        uext[k, 0:16, :] = uext[k, tl:tl + 16, :]

    zs = _dot(hb, w_ref[:, OFF_SWA:OFF_SWA + 3 * W_SWA])
    _advance(ride, 3)
    kc_ref[0, 0] = zd[:, W_DIL:2 * W_DIL].T
    vc_ref[0, 0] = zd[:, 2 * W_DIL:3 * W_DIL].T
    for hp in range(N_PAIRS_DIL):
        _advance(ride, 1)
        lo = hp * BLOCK
        zds[0] = zd[:, lo:lo + BLOCK] * QK_SCALE
        zds[1] = zd[:, W_DIL + lo:W_DIL + lo + BLOCK]
        zds[2] = zd[:, 2 * W_DIL + lo:2 * W_DIL + lo + BLOCK]
        zds[3] = _silu(zd[:, 3 * W_DIL + lo:3 * W_DIL + lo + BLOCK])
        q1_ref[0, hp] = zds[0].astype(BF16)
        k1_ref[0, hp] = zds[1].astype(BF16)
        v1_ref[0, hp] = zds[2].astype(BF16)
        quarter = tl // 4
        for a in range(4):
            for blk in range(4):
                z4s[blk, a] = zds[blk, pl.ds(a, quarter, stride=4), :]
        for b in range(4):
            for a in range(4):
                c = 4 * b + a
                rows = pl.ds(b, tl // N_CLASSES, stride=4)
                qm_ref[0, hp, c] = z4s[0, a, rows, :].astype(BF16)
                km_ref[0, hp, c] = z4s[1, a, rows, :].astype(BF16)
                vm_ref[0, hp, c] = z4s[2, a, rows, :].astype(BF16)
                gm_ref[0, hp, c] = z4s[3, a, rows, :].astype(BF16)

    zc4 = _dot(hb, w_ref[:, OFF_CONV:OFF_CONV + 4 * W_CONV])
    _advance(ride, 3)

    sk = zs[:, W_SWA:W_SWA + W_SWA_KV]
    sv = zs[:, W_SWA + W_SWA_KV:W_SWA + 2 * W_SWA_KV]
    skc_ref[0] = sk[tl - SWA_WINDOW:tl, :].T
    svc_ref[0] = sv[tl - SWA_WINDOW:tl, :].T
    c0 = W_POOL + W_CONV
    y_ref[0, :, c0:c0 + W_SWA] = _pair_heads(zs[:, 0:W_SWA] * QK_SCALE).astype(BF16)
    y_ref[0, :, c0 + W_SWA:c0 + 2 * W_SWA] = zs[:, W_SWA:2 * W_SWA].astype(BF16)
    y_ref[0, :, c0 + 2 * W_SWA:c0 + 3 * W_SWA] = _pair_heads(_silu(zs[:, 2 * W_SWA:3 * W_SWA])).astype(BF16)
    _advance(ride, RIDE_STAGES)

    ch = zc4[:, 0:W_CONV]
    cb = zc4[:, W_CONV:2 * W_CONV]
    cc = zc4[:, 2 * W_CONV:3 * W_CONV]
    cg = zc4[:, 3 * W_CONV:4 * W_CONV]
    zc = cc * ch
    zcext[8:8 + tl, :] = zc
    conv = (cw_ref[0:1, :] * zcext[6:6 + tl, :] + cw_ref[1:2, :] * zcext[7:7 + tl, :]
            + cw_ref[2:3, :] * zc)
    y_ref[0, :, W_POOL:W_POOL + W_CONV] = (cb * conv * _silu(cg)).astype(BF16)
    cst_ref[0] = zcext[tl + 6:tl + 8, :]
    zcext[0:8, :] = zcext[tl:tl + 8, :]


def _cache_attention_specs(layer, row0, nb, nt, r, swr):
    bt = DEC_TILE
    blk0 = row0 // bt
    full = lambda *shape: pl.BlockSpec(shape, lambda n, t: (0,) * len(shape))
    srow_spec = pl.BlockSpec((bt, 3, 8, HEAD_DIM), lambda n, t: (blk0 + n * nt + t, 0, 0, 0))
    dil_spec = pl.BlockSpec((1, bt, N_HEADS_DIL, HEAD_DIM, r),
                            lambda n, t: (layer, blk0 + n * nt + t, 0, 0, 0))
    swa_spec = pl.BlockSpec((1, bt, 2, HEAD_DIM, swr), lambda n, t: (layer, blk0 + n * nt + t, 0, 0, 0))
    sout_spec = pl.BlockSpec((bt, 8, HEAD_DIM), lambda n, t: (n * nt + t, 0, 0))
    sout_shape = jax.ShapeDtypeStruct((nb * nt * bt, 8, HEAD_DIM), F32)
    in_specs = [srow_spec, srow_spec, full(8, r), full(1, r), full(8, swr), full(8, BLOCK),
                dil_spec, dil_spec, swa_spec, swa_spec]
    return in_specs, [sout_spec, sout_spec], [sout_shape, sout_shape]


N_PA_INPUTS = 17


def _pa_kernel_carry(*refs):
    _pa_kernel(*refs[:N_PA_INPUTS], *refs[N_PA_INPUTS + 2:])


def _prompt_a(x, mod_p, norm_g, w_in_b, wbd, pool_scale, conv_w, layer, depth, row0, cache_args, carried):
    nb, l, _ = x.shape
    tl = SEQ_TILE
    nt = l // tl
    cache_t0 = (l - DIL_MAX) // tl
    c_in, c_out, c_shape = _cache_attention_specs(layer, row0, nb, nt, cache_args[6].shape[-1],
                                                  cache_args[8].shape[-1])
    pair_spec = pl.BlockSpec((1, N_PAIRS_DIL, tl, BLOCK), lambda n, t: (n, 0, t, 0))
    pair_shape = jax.ShapeDtypeStruct((nb, N_PAIRS_DIL, l, BLOCK), BF16)
    cm_spec = pl.BlockSpec((1, N_PAIRS_DIL, N_CLASSES, tl // N_CLASSES, BLOCK),
                           lambda n, t: (n, 0, 0, t, 0))
    cm_shape = jax.ShapeDtypeStruct((nb, N_PAIRS_DIL, N_CLASSES, l // N_CLASSES, BLOCK), BF16)
    cache_spec = pl.BlockSpec((1, 1, W_DIL, tl),
                              lambda n, t: (layer, n, 0, jnp.maximum(t - cache_t0, 0)))
    cache_shape = jax.ShapeDtypeStruct((depth, nb, W_DIL, DIL_MAX), F32)
    full = lambda *shape: pl.BlockSpec(shape, lambda n, t: (0,) * len(shape))
    in_specs = [
        pl.BlockSpec((1, tl, D_MODEL), lambda n, t: (n, t, 0)),
        pl.BlockSpec((1, 1, 3 * D_MODEL), lambda n, t: (n, 0, 0)),
        full(1, D_MODEL),
        pl.BlockSpec((None, D_MODEL, D_PROJ), lambda n, t: (layer, 0, 0), pipeline_mode=pl.Buffered(1)),
        full(W_POOL, W_POOL),
        full(1, W_POOL),
        full(3, W_CONV),
    ] + c_in
    assert len(in_specs) == N_PA_INPUTS
    extra, aliases = (), {}
    if carried is not None:
        in_specs = in_specs + [pl.BlockSpec(memory_space=pl.ANY)] * 2
        extra = tuple(carried)
        aliases = {N_PA_INPUTS: 7, N_PA_INPUTS + 1: 8}
    return pl.pallas_call(
        _pa_kernel if carried is None else _pa_kernel_carry,
        grid=(nb, nt),
        in_specs=in_specs,
        input_output_aliases=aliases,
        out_specs=[
            pair_spec, pair_spec, pair_spec, cm_spec, cm_spec, cm_spec, cm_spec,
            cache_spec, cache_spec,
            pl.BlockSpec((1, tl, Y_WIDTH), lambda n, t: (n, t, 0)),
            pl.BlockSpec((1, POOL_BUF, W_POOL), lambda n, t: (n, 0, 0)),
            pl.BlockSpec((1, CONV_BUF, W_CONV), lambda n, t: (n, 0, 0)),
            pl.BlockSpec((1, SWA_WINDOW, W_SWA_KV), lambda n, t: (n, 0, 0)),
            pl.BlockSpec((1, SWA_WINDOW, W_SWA_KV), lambda n, t: (n, 0, 0)),
        ] + c_out,
        out_shape=[
            pair_shape, pair_shape, pair_shape, cm_shape, cm_shape, cm_shape, cm_shape,
            cache_shape, cache_shape,
            jax.ShapeDtypeStruct((nb, l, Y_WIDTH), BF16),
            jax.ShapeDtypeStruct((nb, POOL_BUF, W_POOL), F32),
            jax.ShapeDtypeStruct((nb, CONV_BUF, W_CONV), F32),
            jax.ShapeDtypeStruct((nb, SWA_WINDOW, W_SWA_KV), F32),
            jax.ShapeDtypeStruct((nb, SWA_WINDOW, W_SWA_KV), F32),
        ] + c_shape,
        scratch_shapes=[
            pltpu.VMEM((len(POOL_WINDOWS), 16 + tl, W_POOL), F32),
            pltpu.VMEM((8 + tl, W_CONV), F32),
            pltpu.VMEM((4, tl, BLOCK), F32),
            pltpu.VMEM((4, 4, tl // 4, BLOCK), F32),
        ],
        compiler_params=_compiler_params(("arbitrary", "arbitrary")),
        name="prompt_a",
    )(x, mod_p, norm_g, w_in_b, wbd, pool_scale, conv_w, *cache_args, *extra)


def _band_scores(q, kblk, bias, lane_q):
    lhs = jnp.concatenate([jnp.where(lane_q < HEAD_DIM, q, jnp.zeros_like(q)),
                           jnp.where(lane_q >= HEAD_DIM, q, jnp.zeros_like(q))], axis=0)
    return _dot_nt(lhs, kblk) + bias


def _band_values(s, vblk, lane_q):
    m = jnp.max(s, axis=-1, keepdims=True)
    p = jnp.exp(s - m)
    den = jnp.sum(p, axis=-1, keepdims=True)
    acc = _dot(p.astype(BF16), vblk)
    lo = lane_q < HEAD_DIM
    return (jnp.where(lo, acc[0:BLOCK], acc[BLOCK:2 * BLOCK]),
            jnp.where(lo, m[0:BLOCK], m[BLOCK:2 * BLOCK]),
            jnp.where(lo, den[0:BLOCK], den[BLOCK:2 * BLOCK]))


def _run_skewed(tasks, skew):
    pending = []
    for scores_fn, finish_fn in tasks:
        pending.append((finish_fn, scores_fn()))
        if len(pending) > skew:
            fn, s = pending.pop(0)
            fn(s)
    for fn, s in pending:
        fn(s)


def _pb_kernel(bias_ref, q1, k1, v1, qm, km, vm, gm, outm, s1, s4, s16):
    c = pl.program_id(2)
    first = jnp.where(c == 0, 1, 0)
    lane_q = lax.broadcasted_iota(jnp.int32, (BLOCK, BLOCK), 1)
    sub = BLOCK // 4

    tasks = []

    def dil1_task(j):
        r0 = j * BLOCK
        if j == 0:
            start = jnp.maximum(c * (CHUNK // BLOCK) - 1, 0) * BLOCK
            var = first
        else:
            start = (c * (CHUNK // BLOCK) + (j - 1)) * BLOCK
            var = 0
        start = pl.multiple_of(start, BLOCK)

        def scores():
            return _band_scores(q1[0, 0, r0:r0 + BLOCK, :], k1[0, 0, pl.ds(start, 2 * BLOCK), :],
                                bias_ref[0, 0, var], lane_q)

        def finish(s):
            parts = _band_values(s, v1[0, 0, pl.ds(start, 2 * BLOCK), :], lane_q)
            for k, part in enumerate(parts):
                s1[k, r0:r0 + BLOCK, :] = part

        return scores, finish

    def dil4_task(c4, j):
        i0 = j * sub
        if j == 0:
            istart = jnp.maximum(c * (CHUNK // N_CLASSES) - sub, 0)
            var = first
        else:
            istart = c * (CHUNK // N_CLASSES) + i0 - sub
            var = 0
        istart = pl.multiple_of(istart, sub)
        classes = [4 * cc + c4 for cc in range(4)]

        def scores():
            q = jnp.concatenate([qm[0, 0, cl, i0:i0 + sub, :] for cl in classes], axis=0)
            kblk = jnp.concatenate([km[0, 0, cl, pl.ds(istart, 2 * sub), :] for cl in classes], axis=0)
            return _band_scores(q, kblk, bias_ref[0, 1, var], lane_q)

        def finish(s):
            vblk = jnp.concatenate([vm[0, 0, cl, pl.ds(istart, 2 * sub), :] for cl in classes], axis=0)
            parts = _band_values(s, vblk, lane_q)
            for k, part in enumerate(parts):
                for cc, cl in enumerate(classes):
                    s4[k, cl, i0:i0 + sub, :] = part[cc * sub:(cc + 1) * sub]

        return scores, finish

    start16 = pl.multiple_of(jnp.maximum(c - 1, 0) * BLOCK, BLOCK)

    def dil16_task(cl):
        def scores():
            return _band_scores(qm[0, 0, cl], km[0, 0, cl, pl.ds(start16, 2 * BLOCK), :],
                                bias_ref[0, 2, first], lane_q)

        def finish(s):
            parts = _band_values(s, vm[0, 0, cl, pl.ds(start16, 2 * BLOCK), :], lane_q)
            for k, part in enumerate(parts):
                s16[k, cl] = part

        return scores, finish

    tasks += [dil1_task(j) for j in range(CHUNK // BLOCK)]
    tasks += [dil4_task(c4, j) for c4 in range(4) for j in range(4)]
    tasks += [dil16_task(cl) for cl in range(N_CLASSES)]
    _run_skewed(tasks, BAND_SKEW)

    for cl in range(N_CLASSES):
        rows = pl.ds(cl, BLOCK, stride=N_CLASSES)
        ma, mb, mc = s1[1, rows, :], s4[1, cl], s16[1, cl]
        mx = jnp.maximum(jnp.maximum(ma, mb), mc)
        wa = jnp.exp(ma - mx)
        wb = jnp.exp(mb - mx)
        wc = jnp.exp(mc - mx)
        num = wa * s1[0, rows, :] + wb * s4[0, cl] + wc * s16[0, cl]
        den = wa * s1[2, rows, :] + wb * s4[2, cl] + wc * s16[2, cl]
        outm[0, 0, cl] = (num / den * gm[0, 0, cl].astype(F32)).astype(BF16)


def _prompt_b(q1, k1, v1, qm, km, vm, gm, dil_bias):
    nb, npair, l, _ = q1.shape
    nc = l // CHUNK
    li = l // N_CLASSES
    ci = CHUNK // N_CLASSES
    q1_spec = pl.BlockSpec((1, 1, CHUNK, BLOCK), lambda n, h, c: (n, h, c, 0))
    kv1_spec = pl.BlockSpec((1, 1, l, BLOCK), lambda n, h, c: (n, h, 0, 0))
    cm_spec = pl.BlockSpec((1, 1, N_CLASSES, ci, BLOCK), lambda n, h, c: (n, h, 0, c, 0))
    kvm_spec = pl.BlockSpec((1, 1, N_CLASSES, li, BLOCK), lambda n, h, c: (n, h, 0, 0, 0))
    cm_scratch = pltpu.VMEM((3, N_CLASSES, ci, BLOCK), F32)
    return pl.pallas_call(
        _pb_kernel,
        grid=(nb, npair, nc),
        in_specs=[pl.BlockSpec((1, 3, 2, 2 * BLOCK, 2 * BLOCK), lambda n, h, c: (h, 0, 0, 0, 0)),
                  q1_spec, kv1_spec, kv1_spec, cm_spec, kvm_spec, kvm_spec, cm_spec],
        out_specs=cm_spec,
        out_shape=jax.ShapeDtypeStruct((nb, npair, N_CLASSES, li, BLOCK), BF16),
        scratch_shapes=[
            pltpu.VMEM((3, CHUNK, BLOCK), F32), cm_scratch, cm_scratch,
        ],
        compiler_params=_compiler_params(("arbitrary", "arbitrary", "arbitrary")),
        name="prompt_b",
    )(dil_bias, q1, k1, v1, qm, km, vm, gm)


def _mix_out(y, x, gate, w_ref, fg_ref, final):
    return _residual_out(_dot(y, w_ref[...]), x, gate, fg_ref, final)


def _residual_out(mixed, x, gate, fg_ref, final):
    xn = x + gate * mixed
    if final:
        ms = jnp.mean(xn * xn, axis=-1, keepdims=True)
        xn = xn * lax.rsqrt(ms + RMS_EPS) * fg_ref[...]
    return xn


def _pc_kernel(sink_ref, x_ref, mod_ref, yacd_ref, yb_ref, w_ref, fg_ref, bias_ref,
               dq_ref, sq_ref, dbias_ref, mult_ref, sbias_ref, ssink_ref, kt_ref, vt_ref, ck_ref, cv_ref,
               o_ref, bo_ref, do_ref, ybs, kext, vext, yds, *, final):
    tl = SEQ_TILE
    t = pl.program_id(1)

    @pl.when(t == 0)
    def _():
        kext[0:BLOCK, :] = jnp.zeros((BLOCK, W_SWA_KV), BF16)
        vext[0:BLOCK, :] = jnp.zeros((BLOCK, W_SWA_KV), BF16)

    for hp in range(N_PAIRS_DIL):
        for c in range(N_CLASSES):
            ybs[hp, pl.ds(c, tl // N_CLASSES, stride=N_CLASSES), :] = yb_ref[0, hp, c].astype(F32)
    k_head = D_MIX - W_SWA
    y_head = jnp.concatenate([yacd_ref[0, :, 0:W_POOL], ybs[0].astype(BF16), ybs[1].astype(BF16),
                              ybs[2].astype(BF16), yacd_ref[0, :, W_POOL:W_POOL + W_CONV]], axis=-1)
    n_tile = D_MODEL // (tl // BLOCK)
    proj = []
    ride = _cache_attention(dq_ref, sq_ref, dbias_ref, mult_ref, sbias_ref, ssink_ref, kt_ref, vt_ref,
                            ck_ref, cv_ref, bo_ref, do_ref)

    c0 = W_POOL + W_CONV
    kext[BLOCK:BLOCK + tl, :] = yacd_ref[0, :, c0 + W_SWA:c0 + W_SWA + W_SWA_KV]
    vext[BLOCK:BLOCK + tl, :] = yacd_ref[0, :, c0 + W_SWA + W_SWA_KV:c0 + 2 * W_SWA]
    first = jnp.where(t == 0, 1, 0)
    lane_q = lax.broadcasted_iota(jnp.int32, (BLOCK, BLOCK), 1)
    row_s = lax.broadcasted_iota(jnp.int32, (2 * BLOCK, 1), 0)
    for jb in range(tl // BLOCK):
        proj.append(_dot(y_head, w_ref[0:k_head, jb * n_tile:(jb + 1) * n_tile]))
        r0 = jb * BLOCK
        kblk = kext[r0:r0 + 2 * BLOCK, :]
        vblk = vext[r0:r0 + 2 * BLOCK, :]
        gated = []
        for grp in range(2):
            q = yacd_ref[0, r0:r0 + BLOCK, c0 + grp * BLOCK:c0 + (grp + 1) * BLOCK]
            lhs = jnp.concatenate([jnp.where(lane_q < HEAD_DIM, q, jnp.zeros_like(q)),
                                   jnp.where(lane_q >= HEAD_DIM, q, jnp.zeros_like(q))], axis=0)
            s = _dot_nt(lhs, kblk)
            if jb == 0:
                s = s + bias_ref[grp, first]
            else:
                s = s + bias_ref[grp, 0]
            sink = jnp.where(row_s < BLOCK, sink_ref[SWA_HEAD_PERM[2 * grp]],
                             sink_ref[SWA_HEAD_PERM[2 * grp + 1]])
            m = jnp.maximum(jnp.max(s, axis=-1, keepdims=True), sink)
            p = jnp.exp(s - m)
            den = jnp.sum(p, axis=-1, keepdims=True) + jnp.exp(sink - m)
            o = _dot(p.astype(BF16), vblk) / den
            od = jnp.where(lane_q < HEAD_DIM, o[0:BLOCK], o[BLOCK:2 * BLOCK])
            g0 = c0 + 2 * W_SWA + grp * BLOCK
            gate_s = yacd_ref[0, r0:r0 + BLOCK, g0:g0 + BLOCK].astype(F32)
            gated.append(od * gate_s)
            _advance(ride, 2)
        h01, h23 = _unpair_heads(*gated)
        yds[r0:r0 + BLOCK, 0:BLOCK] = h01.astype(BF16)
        yds[r0:r0 + BLOCK, BLOCK:2 * BLOCK] = h23.astype(BF16)
    kext[0:BLOCK, :] = kext[tl:tl + BLOCK, :]
    vext[0:BLOCK, :] = vext[tl:tl + BLOCK, :]
    _advance(ride, RIDE_STAGES)

    mixed = jnp.concatenate(proj, axis=-1) + _dot(yds[...], w_ref[k_head:D_MIX, :])
    gate = mod_ref[0][:, 2 * D_MODEL:3 * D_MODEL]
    o_ref[0] = _residual_out(mixed, x_ref[0], gate, fg_ref, final)


def _prompt_c(x, mod_p, yacd, yb, w_out_b, final_g, sink, swa_bias, final, layer, row0, cache_args):
    nb, l, _ = x.shape
    tl = SEQ_TILE
    full = lambda *shape: pl.BlockSpec(shape, lambda n, t: (0,) * len(shape))
    c_in, c_out, c_shape = _cache_attention_specs(layer, row0, nb, l // tl, cache_args[6].shape[-1],
                                                  cache_args[8].shape[-1])
    return pl.pallas_call(
        functools.partial(_pc_kernel, final=final),
        grid=(nb, l // tl),
        in_specs=[
            pl.BlockSpec(memory_space=pltpu.SMEM),
            pl.BlockSpec((1, tl, D_MODEL), lambda n, t: (n, t, 0)),
            pl.BlockSpec((1, 1, 3 * D_MODEL), lambda n, t: (n, 0, 0)),
            pl.BlockSpec((1, tl, Y_WIDTH), lambda n, t: (n, t, 0)),
            pl.BlockSpec((1, N_PAIRS_DIL, N_CLASSES, tl // N_CLASSES, BLOCK),
                         lambda n, t: (n, 0, 0, t, 0)),
            pl.BlockSpec((None, D_MIX, D_MODEL), lambda n, t: (layer, 0, 0)),
            full(1, D_MODEL),
            full(2, 2, 2 * BLOCK, 2 * BLOCK),
        ] + c_in,
        out_specs=[pl.BlockSpec((1, tl, D_MODEL), lambda n, t: (n, t, 0))] + c_out,
        out_shape=[jax.ShapeDtypeStruct((nb, l, D_MODEL), F32)] + c_shape,
        scratch_shapes=[
            pltpu.VMEM((N_PAIRS_DIL, tl, BLOCK), F32),
            pltpu.VMEM((BLOCK + tl, W_SWA_KV), BF16),
            pltpu.VMEM((BLOCK + tl, W_SWA_KV), BF16),
            pltpu.VMEM((tl, W_SWA), BF16),
        ],
        compiler_params=_compiler_params(("arbitrary", "arbitrary")),
        name="prompt_c",
    )(sink, x, mod_p, yacd, yb, w_out_b, final_g, swa_bias, *cache_args)


def _sa_kernel(x_ref, mod_ref, g_ref, w_ref, wbd_ref, ps_ref, cw_ref, sp_ref, sc_ref,
               qkv_ref, sw_ref, yac_ref, gates_ref, pst_ref, cst_ref):
    hb = _modulated_norm(x_ref[...], g_ref[...], mod_ref[...]).astype(BF16)
    ns = hb.shape[0]

    zp = _dot(hb, w_ref[:, OFF_POOL:OFF_POOL + 2 * W_POOL])
    pu = zp[:, 0:W_POOL]
    pg = zp[:, W_POOL:2 * W_POOL]
    acc = pu
    sums = {}
    for j in range(1, 16):
        acc = acc + sp_ref[POOL_BUF - j]
        if j + 1 in POOL_WINDOWS:
            sums[j + 1] = acc
    lane = lax.broadcasted_iota(jnp.int32, (ns, W_POOL), 1)
    sel, win = _pool_select(sums, lane)
    diff = sel / win.astype(F32) - pu
    a_out = _dot(diff.astype(BF16), wbd_ref[...]) * ps_ref[...]
    yac_ref[:, 0:W_POOL] = a_out * _silu(pg)
    pst_ref[0:POOL_BUF - 1] = sp_ref[1:POOL_BUF]
    pst_ref[POOL_BUF - 1] = pu

    zc4 = _dot(hb, w_ref[:, OFF_CONV:OFF_CONV + 4 * W_CONV])
    ch = zc4[:, 0:W_CONV]
    cb = zc4[:, W_CONV:2 * W_CONV]
    cc = zc4[:, 2 * W_CONV:3 * W_CONV]
    cg = zc4[:, 3 * W_CONV:4 * W_CONV]
    zc = cc * ch
    conv = (cw_ref[0:1, :] * sc_ref[:, 0:W_CONV] + cw_ref[1:2, :] * sc_ref[:, W_CONV:2 * W_CONV]
            + cw_ref[2:3, :] * zc)
    yac_ref[:, W_POOL:W_POOL + W_CONV] = cb * conv * _silu(cg)
    cst_ref[:, 0:W_CONV] = sc_ref[:, W_CONV:2 * W_CONV]
    cst_ref[:, W_CONV:2 * W_CONV] = zc

    zd = _dot(hb, w_ref[:, OFF_DIL:OFF_DIL + 4 * W_DIL])
    qkv_ref[:, 0:W_DIL] = zd[:, 0:W_DIL] * QK_SCALE
    qkv_ref[:, W_DIL:3 * W_DIL] = zd[:, W_DIL:3 * W_DIL]
    gates_ref[:, 0:W_DIL] = _silu(zd[:, 3 * W_DIL:4 * W_DIL])

    zs = _dot(hb, w_ref[:, OFF_SWA:OFF_SWA + 3 * W_SWA])
    sw_ref[:, 0:W_SWA] = zs[:, 0:W_SWA] * QK_SCALE
    sw_ref[:, W_SWA:2 * W_SWA] = zs[:, W_SWA:2 * W_SWA]
    gates_ref[:, W_DIL:W_DIL + W_SWA] = _silu(zs[:, 2 * W_SWA:3 * W_SWA])


def _whole_spec(shape):
    return pl.BlockSpec(shape, lambda *_: (0,) * len(shape))


def _layer_spec(shape, layer):
    return pl.BlockSpec((None,) + tuple(shape[1:]), lambda *_: (layer,) + (0,) * (len(shape) - 1))


def _sample_a(xs, mod_s, norm_g, w_in_b, wbd, pool_scale, conv_w, sp, sc, layer):
    ns = xs.shape[0]
    shapes = [
        jax.ShapeDtypeStruct((ns, 3 * W_DIL), F32),
        jax.ShapeDtypeStruct((ns, 2 * W_SWA), F32),
        jax.ShapeDtypeStruct((ns, W_POOL + W_CONV), F32),
        jax.ShapeDtypeStruct((ns, W_DIL + W_SWA), F32),
        jax.ShapeDtypeStruct((POOL_BUF, ns, W_POOL), F32),
        jax.ShapeDtypeStruct((ns, CONV_BUF * W_CONV), F32),
    ]
    args = (xs, mod_s, norm_g, w_in_b, wbd, pool_scale, conv_w, sp, sc)
    in_specs = [_whole_spec(a.shape) for a in args]
    in_specs[3] = _layer_spec(w_in_b.shape, layer)
    return pl.pallas_call(
        _sa_kernel,
        grid=(1,),
        in_specs=in_specs,
        out_specs=[_whole_spec(s.shape) for s in shapes],
        out_shape=shapes,
        compiler_params=_compiler_params(("arbitrary",)),
        name="sample_a",
    )(*args)


def _cache_attention(dq_ref, sq_ref, dbias_ref, mult_ref, sbias_ref, sink_ref, kt_ref, vt_ref,
                     ck_ref, cv_ref, bo_ref, do_ref):
    row_d = lax.broadcasted_iota(jnp.int32, (8, DIL_MAX), 0)
    row_o = lax.broadcasted_iota(jnp.int32, (8, HEAD_DIM), 0)
    row_s = lax.broadcasted_iota(jnp.int32, (8, SWA_WINDOW), 0)
    for j in range(DEC_TILE):
        q = dq_ref[j, 0]
        qb = q.astype(BF16)
        s = jnp.zeros((8, DIL_MAX), F32)
        for h in range(N_HEADS_DIL):
            s = jnp.where(row_d == h, _dot(qb, kt_ref[0, j, h].astype(BF16)), s)
            yield
        s = s + dbias_ref[...]
        s_self = jnp.sum(q * dq_ref[j, 1], axis=-1, keepdims=True)
        m = jnp.maximum(jnp.max(s, axis=-1, keepdims=True), s_self)
        p = jnp.exp(s - m) * mult_ref[...]
        p_self = float(len(DIL_CONFIGS)) * jnp.exp(s_self - m)
        den = jnp.sum(p, axis=-1, keepdims=True) + p_self
        pb = p.astype(BF16)
        yield
        acc = jnp.zeros((8, HEAD_DIM), F32)
        for h in range(N_HEADS_DIL):
            acc = jnp.where(row_o == h, _dot_nt(pb, vt_ref[0, j, h].astype(BF16)), acc)
            yield
        bo_ref[j] = (acc + p_self * dq_ref[j, 2]) / den

        q = sq_ref[j, 0]
        qb = q.astype(BF16)
        s = jnp.where(row_s < 2, _dot(qb, ck_ref[0, j, 0].astype(BF16)),
                      _dot(qb, ck_ref[0, j, 1].astype(BF16))) + sbias_ref[...]
        w_self = jnp.sum(q * sq_ref[j, 1], axis=-1, keepdims=True)
        sink = sink_ref[...][:, 0:1]
        m = jnp.maximum(jnp.maximum(jnp.max(s, axis=-1, keepdims=True), w_self), sink)
        p = jnp.exp(s - m)
        pw = jnp.exp(w_self - m)
        den = jnp.sum(p, axis=-1, keepdims=True) + pw + jnp.exp(sink - m)
        pb = p.astype(BF16)
        yield
        acc = jnp.where(row_o < 2, _dot_nt(pb, cv_ref[0, j, 0].astype(BF16)),
                        _dot_nt(pb, cv_ref[0, j, 1].astype(BF16)))
        do_ref[j] = (acc + pw * sq_ref[j, 2]) / den
        yield


def _advance(stages, n):
    for _ in range(n):
        next(stages, None)


def _sc_kernel(x_ref, mod_ref, yac_ref, gates_ref, bo_ref, do_ref, w_ref, fg_ref, o_ref, *, final):
    yac = yac_ref[...]
    gates = gates_ref[...]
    y = jnp.concatenate([yac[:, 0:W_POOL], bo_ref[...] * gates[:, 0:W_DIL],
                         yac[:, W_POOL:W_POOL + W_CONV], do_ref[...] * gates[:, W_DIL:W_DIL + W_SWA]],
                        axis=-1).astype(BF16)
    gate = mod_ref[...][:, 2 * D_MODEL:3 * D_MODEL]
    o_ref[...] = _mix_out(y, x_ref[...], gate, w_ref, fg_ref, final)


def _sample_c(xs, mod_s, yac, gates, bo, do, w_out_b, final_g, final, layer):
    args = (xs, mod_s, yac, gates, bo, do, w_out_b, final_g)
    in_specs = [_whole_spec(a.shape) for a in args]
    in_specs[6] = _layer_spec(w_out_b.shape, layer)
    return pl.pallas_call(
        functools.partial(_sc_kernel, final=final),
        grid=(1,),
        in_specs=in_specs,
        out_specs=_whole_spec(xs.shape),
        out_shape=jax.ShapeDtypeStruct(xs.shape, F32),
        compiler_params=_compiler_params(("arbitrary",)),
        name="sample_c",
    )(*args)


def _band_bias_dil4(slope_lo, slope_hi, variant):
    sub = BLOCK // 4
    qidx = np.arange(BLOCK)[:, None]
    kidx = np.arange(2 * BLOCK)[None, :]
    q_step = 4 * (qidx % sub + (sub if variant == 0 else 0)) + qidx // sub
    k_step = 4 * (kidx % (2 * sub)) + kidx // (2 * sub)
    off = q_step - k_step
    valid = (off >= 0) & (off <= BLOCK)
    out = []
    for s in (slope_lo, slope_hi):
        out.append(np.where(valid, -(np.float32(s) * np.float32(4)) * off.astype(np.float32),
                            np.float32(NEG_INF)).astype(np.float32))
    return np.concatenate(out, axis=0)


def _prompt_bias_tables():
    dil = _alibi_slopes(N_HEADS_DIL)

    def table(hp, d, var):
        if d == 4:
            return _band_bias_dil4(dil[2 * hp], dil[2 * hp + 1], var)
        return _band_bias(dil[2 * hp], dil[2 * hp + 1], d, var)

    dil_bias = np.stack([
        np.stack([np.stack([table(hp, d, var) for var in (0, 1)])
                  for _, d in DIL_CONFIGS]) for hp in range(N_PAIRS_DIL)])
    swa = _alibi_slopes(N_HEADS_SWA)
    swa_bias = np.stack([
        np.stack([_band_bias(swa[SWA_HEAD_PERM[2 * g]], swa[SWA_HEAD_PERM[2 * g + 1]], 1, var)
                  for var in (0, 2)]) for g in range(2)])
    return jnp.asarray(dil_bias), jnp.asarray(swa_bias)


def _sample_bias_tables(r):
    dil = _alibi_slopes(N_HEADS_DIL)
    dist = (r - np.arange(r)).astype(np.float32)
    mult = np.zeros((1, r), np.float32)
    for window, d in DIL_CONFIGS:
        mult[0] += ((dist <= window) & (dist % d == 0)).astype(np.float32)
    dbias = np.zeros((8, r), np.float32)
    for h in range(N_HEADS_DIL):
        dbias[h] = np.where(mult[0] > 0, -np.float32(dil[h]) * dist, np.float32(NEG_INF))
    swa = _alibi_slopes(N_HEADS_SWA)
    sdist = (SWA_WINDOW - np.arange(SWA_WINDOW)).astype(np.float32)
    sbias = np.zeros((8, SWA_WINDOW), np.float32)
    for h in range(N_HEADS_SWA):
        sbias[h] = -np.float32(swa[h]) * sdist
    return jnp.asarray(dbias), jnp.asarray(mult), jnp.asarray(sbias)


def kernel(x_prompt, x_sample, c_prompt, c_sample, state_pool, cache_dil_k, cache_dil_v, state_conv, cache_swa_k, cache_swa_v, norm_g, w_ada, b_ada, w_in, w_pool, pool_scale, conv_w, swa_sink, w_out, final_g):
    depth = w_in.shape[0]
    nb, l, _ = x_prompt.shape
    ns = x_sample.shape[0]
    assert x_sample.shape[1] == 1 and l % CHUNK == 0 and l >= DIL_MAX
    assert ns == 2 * nb * (l // SEQ_TILE) * DEC_TILE
    assert cache_dil_k.shape[2] == DIL_MAX and cache_swa_k.shape[2] == SWA_WINDOW

    w_in_b = w_in.astype(BF16)
    w_out_b = w_out.astype(BF16)
    eye = jnp.eye(len(POOL_WINDOWS), dtype=F32)
    wbd = jnp.einsum("dgce,gh->dgche", w_pool, eye).reshape(depth, W_POOL, W_POOL).astype(BF16)
    sink_rows = jnp.zeros((depth, 8), F32).at[:, 0:N_HEADS_SWA].set(swa_sink)
    sink_rows = jnp.broadcast_to(sink_rows[:, :, None], (depth, 8, BLOCK))

    dil_bias, swa_bias = _prompt_bias_tables()
    dbias, mult, sbias = _sample_bias_tables(cache_dil_k.shape[2])

    mod = _ada(jnp.concatenate([c_prompt, c_sample], axis=0), w_ada, b_ada)
    fg = final_g.reshape(1, D_MODEL)

    xp = x_prompt
    xs = x_sample.reshape(ns, D_MODEL)
    kt = jnp.transpose(cache_dil_k, (0, 1, 3, 4, 2))
    vt = jnp.transpose(cache_dil_v, (0, 1, 3, 4, 2))
    ckt = jnp.transpose(cache_swa_k, (0, 1, 3, 4, 2))
    cvt = jnp.transpose(cache_swa_v, (0, 1, 3, 4, 2))
    sp_all = jnp.transpose(state_pool, (0, 2, 1, 3))
    sc_all = state_conv.reshape(depth, ns, CONV_BUF * W_CONV)
    pad_heads = lambda a: jnp.pad(a, ((0, 0), (0, 0), (0, 8 - a.shape[2]), (0, 0)))
    outs = [[] for _ in range(12)]
    carried = None
    for i in range(depth):
        final = i == depth - 1
        mod_p = mod[i, 0:nb].reshape(nb, 1, 3 * D_MODEL)
        mod_s = mod[i, nb:nb + ns]
        g = norm_g[i].reshape(1, D_MODEL)
        ps = pool_scale[i].reshape(1, W_POOL)

        qkv, sw, yac, gates, pst_s, cst_s = _sample_a(
            xs, mod_s, g, w_in_b, wbd[i], ps, conv_w[i], sp_all[i], sc_all[i], i)
        dq3 = pad_heads(qkv.reshape(ns, 3, N_HEADS_DIL, HEAD_DIM))
        sq4 = sw[:, 0:W_SWA].reshape(ns, 1, N_HEADS_SWA, HEAD_DIM)
        skv = jnp.repeat(sw[:, W_SWA:W_SWA + 2 * W_SWA_KV].reshape(ns, 2, 2, HEAD_DIM), 2, axis=2)
        sq3 = pad_heads(jnp.concatenate([sq4, skv], axis=1))

        cache_args = (dq3, sq3, dbias, mult, sbias, sink_rows[i], kt, vt, ckt, cvt)
        q1, k1, v1, qm, km, vm, gm, kc, vc, yacd, pst, cst, skc, svc, bo_a, do_a = _prompt_a(
            xp, mod_p, g, w_in_b, wbd[i], ps, conv_w[i], i, depth, 0, cache_args, carried)
        carried = (kc, vc)
        yb = _prompt_b(q1, k1, v1, qm, km, vm, gm, dil_bias)
        xp, bo_c, do_c = _prompt_c(xp, mod_p, yacd, yb, w_out_b, fg, swa_sink[i], swa_bias, final,
                                   i, ns // 2, cache_args)

        bo = jnp.concatenate([bo_a, bo_c], axis=0)[:, 0:N_HEADS_DIL].reshape(ns, W_DIL)
        do = jnp.concatenate([do_a, do_c], axis=0)[:, 0:N_HEADS_SWA].reshape(ns, W_SWA)
        xs = _sample_c(xs, mod_s, yac, gates, bo, do, w_out_b, fg, final, i)

        unfold = lambda a, h: jnp.transpose(a.reshape(nb, h, HEAD_DIM, a.shape[-1]), (0, 3, 1, 2))
        skc, svc = unfold(skc, 2), unfold(svc, 2)
        outs[0].append(pst)
        outs[1].append(jnp.transpose(pst_s, (1, 0, 2)))
        outs[4].append(qkv[:, W_DIL:2 * W_DIL].reshape(ns, 1, N_HEADS_DIL, HEAD_DIM))
        outs[5].append(qkv[:, 2 * W_DIL:3 * W_DIL].reshape(ns, 1, N_HEADS_DIL, HEAD_DIM))
        outs[6].append(cst)
        outs[7].append(cst_s.reshape(ns, CONV_BUF, W_CONV))
        outs[8].append(skc)
        outs[9].append(svc)
        outs[10].append(sw[:, W_SWA:W_SWA + W_SWA_KV].reshape(ns, 1, 2, HEAD_DIM))
        outs[11].append(sw[:, W_SWA + W_SWA_KV:W_SWA + 2 * W_SWA_KV].reshape(ns, 1, 2, HEAD_DIM))

    for k, cache in zip((2, 3), carried):
        outs[k] = jnp.transpose(cache.reshape(depth, nb, N_HEADS_DIL, HEAD_DIM, DIL_MAX), (0, 1, 4, 2, 3))
    return (xp, xs.reshape(ns, 1, D_MODEL)) + tuple(
        o if not isinstance(o, list) else jnp.stack(o) for o in outs)
```

```python
import functools
import math

import numpy as np
import jax
import jax.numpy as jnp
from jax import lax
from jax.experimental import pallas as pl
from jax.experimental.pallas import tpu as pltpu

F32 = jnp.float32
BF16 = jnp.bfloat16

D_MODEL = 1024
HEAD_DIM = 64
BLOCK = 128
POOL_WINDOWS = (2, 4, 8, 16)
POOL_GROUP = 64
W_POOL = 256
POOL_BUF = 15
DIL_CONFIGS = ((128, 1), (512, 4), (2048, 16))
DIL_MAX = 2048
N_HEADS_DIL = 6
N_PAIRS_DIL = 3
N_CLASSES = 16
W_DIL = 384
W_CONV = 256
CONV_BUF = 2
N_HEADS_SWA = 4
W_SWA = 256
W_SWA_KV = 128
SWA_WINDOW = 128
D_MIX = 1152
D_PROJ = 3840
RMS_EPS = 1e-6
QK_SCALE = 1.0 / math.sqrt(HEAD_DIM)

OFF_POOL = 0
OFF_DIL = 512
OFF_CONV = 2048
OFF_SWA = 3072
Y_WIDTH = W_POOL + W_CONV + 3 * W_SWA
SWA_HEAD_PERM = (0, 3, 1, 2)

SEQ_TILE = 512
CHUNK = 2048
BAND_SKEW = 1
DEC_TILE = 1
RIDE_STAGES = 15 * DEC_TILE
VMEM_LIMIT = 56 * 1024 * 1024

NEG_INF = float("-inf")


def _silu(v):
    return v * jax.nn.sigmoid(v)


def _dot(a, b):
    return jnp.dot(a, b, preferred_element_type=F32)


def _dot_nt(a, b):
    return lax.dot_general(a, b, (((1,), (1,)), ((), ())), preferred_element_type=F32)


def _alibi_slopes(n):
    return [2.0 ** (-8.0 * (h + 1) / n) for h in range(n)]


def _band_bias(slope_lo, slope_hi, dist_scale, variant):
    qi = np.arange(BLOCK)[:, None]
    kj = np.arange(2 * BLOCK)[None, :]
    if variant == 1:
        off = qi - kj
    else:
        off = qi - kj + BLOCK
    valid = (off >= 0) & (off <= BLOCK)
    if variant == 2:
        valid = valid & (kj >= BLOCK)
    out = []
    for s in (slope_lo, slope_hi):
        b = np.where(valid, -(np.float32(s) * np.float32(dist_scale)) * off.astype(np.float32),
                     np.float32(NEG_INF))
        out.append(b.astype(np.float32))
    return np.concatenate(out, axis=0)


def _compiler_params(sem):
    return pltpu.CompilerParams(dimension_semantics=sem, vmem_limit_bytes=VMEM_LIMIT)


def _ada_kernel(c_ref, w_ref, b_ref, o_ref):
    s = _silu(c_ref[...]).astype(BF16)
    o_ref[0] = _dot(s, w_ref[0].astype(BF16)) + b_ref[0]


def _ada(c_all, w_ada_b, b_ada):
    depth = w_ada_b.shape[0]
    rows = c_all.shape[0]
    return pl.pallas_call(
        _ada_kernel,
        grid=(depth, 3),
        in_specs=[
            pl.BlockSpec((rows, D_MODEL), lambda i, j: (0, 0)),
            pl.BlockSpec((1, D_MODEL, D_MODEL), lambda i, j: (i, 0, j)),
            pl.BlockSpec((1, 1, D_MODEL), lambda i, j: (i, 0, j)),
        ],
        out_specs=pl.BlockSpec((1, rows, D_MODEL), lambda i, j: (i, 0, j)),
        out_shape=jax.ShapeDtypeStruct((depth, rows, 3 * D_MODEL), F32),
        compiler_params=_compiler_params(("arbitrary", "arbitrary")),
        name="ada",
    )(c_all, w_ada_b, b_ada.reshape(depth, 1, 3 * D_MODEL))


def _modulated_norm(x, g, mod_row):
    ms = jnp.mean(x * x, axis=-1, keepdims=True)
    y = x * lax.rsqrt(ms + RMS_EPS) * g
    shift = mod_row[:, 0:D_MODEL]
    scale = mod_row[:, D_MODEL:2 * D_MODEL]
    return y * (1.0 + scale) + shift


def _pair_heads(v):
    left, right = v[:, 0:BLOCK], v[:, BLOCK:2 * BLOCK]
    low = lax.broadcasted_iota(jnp.int32, left.shape, 1) < HEAD_DIM
    return jnp.concatenate([jnp.where(low, left, right),
                            pltpu.roll(jnp.where(low, right, left), HEAD_DIM, axis=1)], axis=-1)


def _unpair_heads(a, b):
    low = lax.broadcasted_iota(jnp.int32, a.shape, 1) < HEAD_DIM
    b_swapped = pltpu.roll(b, HEAD_DIM, axis=1)
    return jnp.where(low, a, b_swapped), jnp.where(low, b_swapped, a)


def _pool_select(sums, lane):
    grp = lane // POOL_GROUP
    sel = jnp.where(grp == 0, sums[2], jnp.where(grp == 1, sums[4],
                                                 jnp.where(grp == 2, sums[8], sums[16])))
    win = jnp.where(grp == 0, 2, jnp.where(grp == 1, 4, jnp.where(grp == 2, 8, 16)))
    return sel, win


def _pa_kernel(x_ref, mod_ref, g_ref, w_ref, wbd_ref, ps_ref, cw_ref,
               dq_ref, sq_ref, dbias_ref, mult_ref, sbias_ref, ssink_ref, kt_ref, vt_ref, ck_ref, cv_ref,
               q1_ref, k1_ref, v1_ref, qm_ref, km_ref, vm_ref, gm_ref, kc_ref, vc_ref, y_ref,
               pst_ref, cst_ref, skc_ref, svc_ref, bo_ref, do_ref,
               uext, zcext, zds, z4s):
    tl = SEQ_TILE
    t = pl.program_id(1)

    @pl.when(t == 0)
    def _():
        for k in range(len(POOL_WINDOWS)):
            uext[k, 0:16, :] = jnp.zeros((16, W_POOL), F32)
        zcext[0:8, :] = jnp.zeros((8, W_CONV), F32)

    ride = _cache_attention(dq_ref, sq_ref, dbias_ref, mult_ref, sbias_ref, ssink_ref, kt_ref, vt_ref,
                            ck_ref, cv_ref, bo_ref, do_ref)

    _advance(ride, 3)
    hb = _modulated_norm(x_ref[0], g_ref[...], mod_ref[0]).astype(BF16)

    zp = _dot(hb, w_ref[:, OFF_POOL:OFF_POOL + 2 * W_POOL])
    zd = _dot(hb, w_ref[:, OFF_DIL:OFF_DIL + 4 * W_DIL])
    _advance(ride, 3)

    pu = zp[:, 0:W_POOL]
    pg = zp[:, W_POOL:2 * W_POOL]
    uext[0, 16:16 + tl, :] = pu
    sums = {}
    level = pu
    for k, w in enumerate(POOL_WINDOWS):
        shift = w // 2
        level = level + uext[k, 16 - shift:16 - shift + tl, :]
        sums[w] = level
        if k + 1 < len(POOL_WINDOWS):
            uext[k + 1, 16:16 + tl, :] = level
    lane = lax.broadcasted_iota(jnp.int32, (tl, W_POOL), 1)
    gpos = lax.broadcasted_iota(jnp.int32, (tl, W_POOL), 0) + t * tl
    sel, win = _pool_select(sums, lane)
    cnt = jnp.minimum(gpos + 1, win).astype(F32)
    diff = sel / cnt - pu
    a_out = _dot(diff.astype(BF16), wbd_ref[...]) * ps_ref[...]
    y_ref[0, :, 0:W_POOL] = (a_out * _silu(pg)).astype(BF16)
    pst_ref[0] = uext[0, tl + 1:tl + 16, :]
    for k in range(len(POOL_WINDOWS)):
        uext[k, 0:16, :] = uext[k, tl:tl + 16, :]

    zs = _dot(hb, w_ref[:, OFF_SWA:OFF_SWA + 3 * W_SWA])
    _advance(ride, 3)
    kc_ref[0, 0] = zd[:, W_DIL:2 * W_DIL].T
    vc_ref[0, 0] = zd[:, 2 * W_DIL:3 * W_DIL].T
    for hp in range(N_PAIRS_DIL):
        _advance(ride, 1)
        lo = hp * BLOCK
        zds[0] = zd[:, lo:lo + BLOCK] * QK_SCALE
        zds[1] = zd[:, W_DIL + lo:W_DIL + lo + BLOCK]
        zds[2] = zd[:, 2 * W_DIL + lo:2 * W_DIL + lo + BLOCK]
        zds[3] = _silu(zd[:, 3 * W_DIL + lo:3 * W_DIL + lo + BLOCK])
        q1_ref[0, hp] = zds[0].astype(BF16)
        k1_ref[0, hp] = zds[1].astype(BF16)
        v1_ref[0, hp] = zds[2].astype(BF16)
        quarter = tl // 4
        for a in range(4):
            for blk in range(4):
                z4s[blk, a] = zds[blk, pl.ds(a, quarter, stride=4), :]
        for b in range(4):
            for a in range(4):
                c = 4 * b + a
                rows = pl.ds(b, tl // N_CLASSES, stride=4)
                qm_ref[0, hp, c] = z4s[0, a, rows, :].astype(BF16)
                km_ref[0, hp, c] = z4s[1, a, rows, :].astype(BF16)
                vm_ref[0, hp, c] = z4s[2, a, rows, :].astype(BF16)
                gm_ref[0, hp, c] = z4s[3, a, rows, :].astype(BF16)

    zc4 = _dot(hb, w_ref[:, OFF_CONV:OFF_CONV + 4 * W_CONV])
    _advance(ride, 3)

    sk = zs[:, W_SWA:W_SWA + W_SWA_KV]
    sv = zs[:, W_SWA + W_SWA_KV:W_SWA + 2 * W_SWA_KV]
    skc_ref[0] = sk[tl - SWA_WINDOW:tl, :].T
    svc_ref[0] = sv[tl - SWA_WINDOW:tl, :].T
    c0 = W_POOL + W_CONV
    y_ref[0, :, c0:c0 + W_SWA] = _pair_heads(zs[:, 0:W_SWA] * QK_SCALE).astype(BF16)
    y_ref[0, :, c0 + W_SWA:c0 + 2 * W_SWA] = zs[:, W_SWA:2 * W_SWA].astype(BF16)
    y_ref[0, :, c0 + 2 * W_SWA:c0 + 3 * W_SWA] = _pair_heads(_silu(zs[:, 2 * W_SWA:3 * W_SWA])).astype(BF16)
    _advance(ride, RIDE_STAGES)

    ch = zc4[:, 0:W_CONV]
    cb = zc4[:, W_CONV:2 * W_CONV]
    cc = zc4[:, 2 * W_CONV:3 * W_CONV]
    cg = zc4[:, 3 * W_CONV:4 * W_CONV]
    zc = cc * ch
    zcext[8:8 + tl, :] = zc
    conv = (cw_ref[0:1, :] * zcext[6:6 + tl, :] + cw_ref[1:2, :] * zcext[7:7 + tl, :]
            + cw_ref[2:3, :] * zc)
    y_ref[0, :, W_POOL:W_POOL + W_CONV] = (cb * conv * _silu(cg)).astype(BF16)
    cst_ref[0] = zcext[tl + 6:tl + 8, :]
    zcext[0:8, :] = zcext[tl:tl + 8, :]


def _cache_attention_specs(layer, row0, nb, nt, r, swr):
    bt = DEC_TILE
    blk0 = row0 // bt
    full = lambda *shape: pl.BlockSpec(shape, lambda n, t: (0,) * len(shape))
    srow_spec = pl.BlockSpec((bt, 3, 8, HEAD_DIM), lambda n, t: (blk0 + n * nt + t, 0, 0, 0))
    dil_spec = pl.BlockSpec((1, bt, N_HEADS_DIL, HEAD_DIM, r),
                            lambda n, t: (layer, blk0 + n * nt + t, 0, 0, 0))
    swa_spec = pl.BlockSpec((1, bt, 2, HEAD_DIM, swr), lambda n, t: (layer, blk0 + n * nt + t, 0, 0, 0))
    sout_spec = pl.BlockSpec((bt, 8, HEAD_DIM), lambda n, t: (n * nt + t, 0, 0))
    sout_shape = jax.ShapeDtypeStruct((nb * nt * bt, 8, HEAD_DIM), F32)
    in_specs = [srow_spec, srow_spec, full(8, r), full(1, r), full(8, swr), full(8, BLOCK),
                dil_spec, dil_spec, swa_spec, swa_spec]
    return in_specs, [sout_spec, sout_spec], [sout_shape, sout_shape]


N_PA_INPUTS = 17


def _pa_kernel_carry(*refs):
    _pa_kernel(*refs[:N_PA_INPUTS], *refs[N_PA_INPUTS + 2:])


def _prompt_a(x, mod_p, norm_g, w_in_b, wbd, pool_scale, conv_w, layer, depth, row0, cache_args, carried):
    nb, l, _ = x.shape
    tl = SEQ_TILE
    nt = l // tl
    cache_t0 = (l - DIL_MAX) // tl
    c_in, c_out, c_shape = _cache_attention_specs(layer, row0, nb, nt, cache_args[6].shape[-1],
                                                  cache_args[8].shape[-1])
    pair_spec = pl.BlockSpec((1, N_PAIRS_DIL, tl, BLOCK), lambda n, t: (n, 0, t, 0))
    pair_shape = jax.ShapeDtypeStruct((nb, N_PAIRS_DIL, l, BLOCK), BF16)
    cm_spec = pl.BlockSpec((1, N_PAIRS_DIL, N_CLASSES, tl // N_CLASSES, BLOCK),
                           lambda n, t: (n, 0, 0, t, 0))
    cm_shape = jax.ShapeDtypeStruct((nb, N_PAIRS_DIL, N_CLASSES, l // N_CLASSES, BLOCK), BF16)
    cache_spec = pl.BlockSpec((1, 1, W_DIL, tl),
                              lambda n, t: (layer, n, 0, jnp.maximum(t - cache_t0, 0)))
    cache_shape = jax.ShapeDtypeStruct((depth, nb, W_DIL, DIL_MAX), F32)
    full = lambda *shape: pl.BlockSpec(shape, lambda n, t: (0,) * len(shape))
    in_specs = [
        pl.BlockSpec((1, tl, D_MODEL), lambda n, t: (n, t, 0)),
        pl.BlockSpec((1, 1, 3 * D_MODEL), lambda n, t: (n, 0, 0)),
        full(1, D_MODEL),
        pl.BlockSpec((None, D_MODEL, D_PROJ), lambda n, t: (layer, 0, 0), pipeline_mode=pl.Buffered(1)),
        full(W_POOL, W_POOL),
        full(1, W_POOL),
        full(3, W_CONV),
    ] + c_in
    assert len(in_specs) == N_PA_INPUTS
    extra, aliases = (), {}
    if carried is not None:
        in_specs = in_specs + [pl.BlockSpec(memory_space=pl.ANY)] * 2
        extra = tuple(carried)
        aliases = {N_PA_INPUTS: 7, N_PA_INPUTS + 1: 8}
    return pl.pallas_call(
        _pa_kernel if carried is None else _pa_kernel_carry,
        grid=(nb, nt),
        in_specs=in_specs,
        input_output_aliases=aliases,
        out_specs=[
            pair_spec, pair_spec, pair_spec, cm_spec, cm_spec, cm_spec, cm_spec,
            cache_spec, cache_spec,
            pl.BlockSpec((1, tl, Y_WIDTH), lambda n, t: (n, t, 0)),
            pl.BlockSpec((1, POOL_BUF, W_POOL), lambda n, t: (n, 0, 0)),
            pl.BlockSpec((1, CONV_BUF, W_CONV), lambda n, t: (n, 0, 0)),
            pl.BlockSpec((1, SWA_WINDOW, W_SWA_KV), lambda n, t: (n, 0, 0)),
            pl.BlockSpec((1, SWA_WINDOW, W_SWA_KV), lambda n, t: (n, 0, 0)),
        ] + c_out,
        out_shape=[
            pair_shape, pair_shape, pair_shape, cm_shape, cm_shape, cm_shape, cm_shape,
            cache_shape, cache_shape,
            jax.ShapeDtypeStruct((nb, l, Y_WIDTH), BF16),
            jax.ShapeDtypeStruct((nb, POOL_BUF, W_POOL), F32),
            jax.ShapeDtypeStruct((nb, CONV_BUF, W_CONV), F32),
            jax.ShapeDtypeStruct((nb, SWA_WINDOW, W_SWA_KV), F32),
            jax.ShapeDtypeStruct((nb, SWA_WINDOW, W_SWA_KV), F32),
        ] + c_shape,
        scratch_shapes=[
            pltpu.VMEM((len(POOL_WINDOWS), 16 + tl, W_POOL), F32),
            pltpu.VMEM((8 + tl, W_CONV), F32),
            pltpu.VMEM((4, tl, BLOCK), F32),
            pltpu.VMEM((4, 4, tl // 4, BLOCK), F32),
        ],
        compiler_params=_compiler_params(("arbitrary", "arbitrary")),
        name="prompt_a",
    )(x, mod_p, norm_g, w_in_b, wbd, pool_scale, conv_w, *cache_args, *extra)


def _band_scores(q, kblk, bias, lane_q):
    lhs = jnp.concatenate([jnp.where(lane_q < HEAD_DIM, q, jnp.zeros_like(q)),
                           jnp.where(lane_q >= HEAD_DIM, q, jnp.zeros_like(q))], axis=0)
    return _dot_nt(lhs, kblk) + bias


def _band_values(s, vblk, lane_q):
    m = jnp.max(s, axis=-1, keepdims=True)
    p = jnp.exp((s - m).astype(BF16))
    acc = _dot(p, jnp.concatenate([vblk, jnp.ones_like(vblk)], axis=-1))
    lo = lane_q < HEAD_DIM
    return (jnp.where(lo, acc[0:BLOCK, 0:BLOCK], acc[BLOCK:2 * BLOCK, 0:BLOCK]),
            jnp.where(lo, m[0:BLOCK], m[BLOCK:2 * BLOCK]),
            jnp.where(lo, acc[0:BLOCK, BLOCK:2 * BLOCK], acc[BLOCK:2 * BLOCK, BLOCK:2 * BLOCK]))


def _run_skewed(tasks, skew):
    pending = []
    for scores_fn, finish_fn in tasks:
        pending.append((finish_fn, scores_fn()))
        if len(pending) > skew:
            fn, s = pending.pop(0)
            fn(s)
    for fn, s in pending:
        fn(s)


def _pb_kernel(bias_ref, q1, k1, v1, qm, km, vm, gm, outm, s1, s4, s16):
    c = pl.program_id(2)
    first = jnp.where(c == 0, 1, 0)
    lane_q = lax.broadcasted_iota(jnp.int32, (BLOCK, BLOCK), 1)
    sub = BLOCK // 4

    tasks = []

    def dil1_task(j):
        r0 = j * BLOCK
        if j == 0:
            start = jnp.maximum(c * (CHUNK // BLOCK) - 1, 0) * BLOCK
            var = first
        else:
            start = (c * (CHUNK // BLOCK) + (j - 1)) * BLOCK
            var = 0
        start = pl.multiple_of(start, BLOCK)

        def scores():
            return _band_scores(q1[0, 0, r0:r0 + BLOCK, :], k1[0, 0, pl.ds(start, 2 * BLOCK), :],
                                bias_ref[0, 0, var], lane_q)

        def finish(s):
            parts = _band_values(s, v1[0, 0, pl.ds(start, 2 * BLOCK), :], lane_q)
            for k, part in enumerate(parts):
                s1[k, r0:r0 + BLOCK, :] = part

        return scores, finish

    def dil4_task(c4, j):
        i0 = j * sub
        if j == 0:
            istart = jnp.maximum(c * (CHUNK // N_CLASSES) - sub, 0)
            var = first
        else:
            istart = c * (CHUNK // N_CLASSES) + i0 - sub
            var = 0
        istart = pl.multiple_of(istart, sub)
        classes = [4 * cc + c4 for cc in range(4)]

        def scores():
            q = jnp.concatenate([qm[0, 0, cl, i0:i0 + sub, :] for cl in classes], axis=0)
            kblk = jnp.concatenate([km[0, 0, cl, pl.ds(istart, 2 * sub), :] for cl in classes], axis=0)
            return _band_scores(q, kblk, bias_ref[0, 1, var], lane_q)

        def finish(s):
            vblk = jnp.concatenate([vm[0, 0, cl, pl.ds(istart, 2 * sub), :] for cl in classes], axis=0)
            parts = _band_values(s, vblk, lane_q)
            for k, part in enumerate(parts):
                for cc, cl in enumerate(classes):
                    s4[k, cl, i0:i0 + sub, :] = part[cc * sub:(cc + 1) * sub]

        return scores, finish

    start16 = pl.multiple_of(jnp.maximum(c - 1, 0) * BLOCK, BLOCK)

    def dil16_task(cl):
        def scores():
            return _band_scores(qm[0, 0, cl], km[0, 0, cl, pl.ds(start16, 2 * BLOCK), :],
                                bias_ref[0, 2, first], lane_q)

        def finish(s):
            parts = _band_values(s, vm[0, 0, cl, pl.ds(start16, 2 * BLOCK), :], lane_q)
            for k, part in enumerate(parts):
                s16[k, cl] = part

        return scores, finish

    tasks += [dil1_task(j) for j in range(CHUNK // BLOCK)]
    tasks += [dil4_task(c4, j) for c4 in range(4) for j in range(4)]
    tasks += [dil16_task(cl) for cl in range(N_CLASSES)]
    _run_skewed(tasks, BAND_SKEW)

    for cl in range(N_CLASSES):
        rows = pl.ds(cl, BLOCK, stride=N_CLASSES)
        ma, mb, mc = s1[1, rows, :], s4[1, cl], s16[1, cl]
        mx = jnp.maximum(jnp.maximum(ma, mb), mc)
        wa = jnp.exp(ma - mx)
        wb = jnp.exp(mb - mx)
        wc = jnp.exp(mc - mx)
        num = wa * s1[0, rows, :] + wb * s4[0, cl] + wc * s16[0, cl]
        den = wa * s1[2, rows, :] + wb * s4[2, cl] + wc * s16[2, cl]
        outm[0, 0, cl] = (num / den * gm[0, 0, cl].astype(F32)).astype(BF16)


def _prompt_b(q1, k1, v1, qm, km, vm, gm, dil_bias):
    nb, npair, l, _ = q1.shape
    nc = l // CHUNK
    li = l // N_CLASSES
    ci = CHUNK // N_CLASSES
    q1_spec = pl.BlockSpec((1, 1, CHUNK, BLOCK), lambda n, h, c: (n, h, c, 0))
    kv1_spec = pl.BlockSpec((1, 1, l, BLOCK), lambda n, h, c: (n, h, 0, 0))
    cm_spec = pl.BlockSpec((1, 1, N_CLASSES, ci, BLOCK), lambda n, h, c: (n, h, 0, c, 0))
    kvm_spec = pl.BlockSpec((1, 1, N_CLASSES, li, BLOCK), lambda n, h, c: (n, h, 0, 0, 0))
    cm_scratch = pltpu.VMEM((3, N_CLASSES, ci, BLOCK), F32)
    return pl.pallas_call(
        _pb_kernel,
        grid=(nb, npair, nc),
        in_specs=[pl.BlockSpec((1, 3, 2, 2 * BLOCK, 2 * BLOCK), lambda n, h, c: (h, 0, 0, 0, 0)),
                  q1_spec, kv1_spec, kv1_spec, cm_spec, kvm_spec, kvm_spec, cm_spec],
        out_specs=cm_spec,
        out_shape=jax.ShapeDtypeStruct((nb, npair, N_CLASSES, li, BLOCK), BF16),
        scratch_shapes=[
            pltpu.VMEM((3, CHUNK, BLOCK), F32), cm_scratch, cm_scratch,
        ],
        compiler_params=_compiler_params(("arbitrary", "arbitrary", "arbitrary")),
        name="prompt_b",
    )(dil_bias, q1, k1, v1, qm, km, vm, gm)


def _mix_out(y, x, gate, w_ref, fg_ref, final):
    return _residual_out(_dot(y, w_ref[...]), x, gate, fg_ref, final)


def _residual_out(mixed, x, gate, fg_ref, final):
    xn = x + gate * mixed
    if final:
        ms = jnp.mean(xn * xn, axis=-1, keepdims=True)
        xn = xn * lax.rsqrt(ms + RMS_EPS) * fg_ref[...]
    return xn


def _pc_kernel(sink_ref, x_ref, mod_ref, yacd_ref, yb_ref, w_ref, fg_ref, bias_ref,
               dq_ref, sq_ref, dbias_ref, mult_ref, sbias_ref, ssink_ref, kt_ref, vt_ref, ck_ref, cv_ref,
               o_ref, bo_ref, do_ref, ybs, kext, vext, yds, *, final):
    tl = SEQ_TILE
    t = pl.program_id(1)

    @pl.when(t == 0)
    def _():
        kext[0:BLOCK, :] = jnp.zeros((BLOCK, W_SWA_KV), BF16)
        vext[0:BLOCK, :] = jnp.zeros((BLOCK, W_SWA_KV), BF16)

    for hp in range(N_PAIRS_DIL):
        for c in range(N_CLASSES):
            ybs[hp, pl.ds(c, tl // N_CLASSES, stride=N_CLASSES), :] = yb_ref[0, hp, c].astype(F32)
    k_head = D_MIX - W_SWA
    y_head = jnp.concatenate([yacd_ref[0, :, 0:W_POOL], ybs[0].astype(BF16), ybs[1].astype(BF16),
                              ybs[2].astype(BF16), yacd_ref[0, :, W_POOL:W_POOL + W_CONV]], axis=-1)
    n_tile = D_MODEL // (tl // BLOCK)
    proj = []
    ride = _cache_attention(dq_ref, sq_ref, dbias_ref, mult_ref, sbias_ref, ssink_ref, kt_ref, vt_ref,
                            ck_ref, cv_ref, bo_ref, do_ref)

    c0 = W_POOL + W_CONV
    kext[BLOCK:BLOCK + tl, :] = yacd_ref[0, :, c0 + W_SWA:c0 + W_SWA + W_SWA_KV]
    vext[BLOCK:BLOCK + tl, :] = yacd_ref[0, :, c0 + W_SWA + W_SWA_KV:c0 + 2 * W_SWA]
    first = jnp.where(t == 0, 1, 0)
    lane_q = lax.broadcasted_iota(jnp.int32, (BLOCK, BLOCK), 1)
    row_s = lax.broadcasted_iota(jnp.int32, (2 * BLOCK, 1), 0)
    for jb in range(tl // BLOCK):
        proj.append(_dot(y_head, w_ref[0:k_head, jb * n_tile:(jb + 1) * n_tile]))
        r0 = jb * BLOCK
        kblk = kext[r0:r0 + 2 * BLOCK, :]
        vblk = vext[r0:r0 + 2 * BLOCK, :]
        gated = []
        for grp in range(2):
            q = yacd_ref[0, r0:r0 + BLOCK, c0 + grp * BLOCK:c0 + (grp + 1) * BLOCK]
            lhs = jnp.concatenate([jnp.where(lane_q < HEAD_DIM, q, jnp.zeros_like(q)),
                                   jnp.where(lane_q >= HEAD_DIM, q, jnp.zeros_like(q))], axis=0)
            s = _dot_nt(lhs, kblk)
            if jb == 0:
                s = s + bias_ref[grp, first]
            else:
                s = s + bias_ref[grp, 0]
            sink = jnp.where(row_s < BLOCK, sink_ref[SWA_HEAD_PERM[2 * grp]],
                             sink_ref[SWA_HEAD_PERM[2 * grp + 1]])
            m = jnp.maximum(jnp.max(s, axis=-1, keepdims=True), sink)
            p = jnp.exp((s - m).astype(BF16))
            acc = _dot(p, jnp.concatenate([vblk, jnp.ones_like(vblk)], axis=-1))
            es = jnp.exp(sink - m)
            lo = lane_q < HEAD_DIM
            den = (jnp.where(lo, acc[0:BLOCK, BLOCK:2 * BLOCK], acc[BLOCK:2 * BLOCK, BLOCK:2 * BLOCK])
                   + jnp.where(lo, es[0:BLOCK], es[BLOCK:2 * BLOCK]))
            od = jnp.where(lo, acc[0:BLOCK, 0:BLOCK], acc[BLOCK:2 * BLOCK, 0:BLOCK]) / den
            g0 = c0 + 2 * W_SWA + grp * BLOCK
            gate_s = yacd_ref[0, r0:r0 + BLOCK, g0:g0 + BLOCK].astype(F32)
            gated.append(od * gate_s)
            _advance(ride, 2)
        h01, h23 = _unpair_heads(*gated)
        yds[r0:r0 + BLOCK, 0:BLOCK] = h01.astype(BF16)
        yds[r0:r0 + BLOCK, BLOCK:2 * BLOCK] = h23.astype(BF16)
    kext[0:BLOCK, :] = kext[tl:tl + BLOCK, :]
    vext[0:BLOCK, :] = vext[tl:tl + BLOCK, :]
    _advance(ride, RIDE_STAGES)

    mixed = jnp.concatenate(proj, axis=-1) + _dot(yds[...], w_ref[k_head:D_MIX, :])
    gate = mod_ref[0][:, 2 * D_MODEL:3 * D_MODEL]
    o_ref[0] = _residual_out(mixed, x_ref[0], gate, fg_ref, final)


def _prompt_c(x, mod_p, yacd, yb, w_out_b, final_g, sink, swa_bias, final, layer, row0, cache_args):
    nb, l, _ = x.shape
    tl = SEQ_TILE
    full = lambda *shape: pl.BlockSpec(shape, lambda n, t: (0,) * len(shape))
    c_in, c_out, c_shape = _cache_attention_specs(layer, row0, nb, l // tl, cache_args[6].shape[-1],
                                                  cache_args[8].shape[-1])
    return pl.pallas_call(
        functools.partial(_pc_kernel, final=final),
        grid=(nb, l // tl),
        in_specs=[
            pl.BlockSpec(memory_space=pltpu.SMEM),
            pl.BlockSpec((1, tl, D_MODEL), lambda n, t: (n, t, 0)),
            pl.BlockSpec((1, 1, 3 * D_MODEL), lambda n, t: (n, 0, 0)),
            pl.BlockSpec((1, tl, Y_WIDTH), lambda n, t: (n, t, 0)),
            pl.BlockSpec((1, N_PAIRS_DIL, N_CLASSES, tl // N_CLASSES, BLOCK),
                         lambda n, t: (n, 0, 0, t, 0)),
            pl.BlockSpec((None, D_MIX, D_MODEL), lambda n, t: (layer, 0, 0)),
            full(1, D_MODEL),
            full(2, 2, 2 * BLOCK, 2 * BLOCK),
        ] + c_in,
        out_specs=[pl.BlockSpec((1, tl, D_MODEL), lambda n, t: (n, t, 0))] + c_out,
        out_shape=[jax.ShapeDtypeStruct((nb, l, D_MODEL), F32)] + c_shape,
        scratch_shapes=[
            pltpu.VMEM((N_PAIRS_DIL, tl, BLOCK), F32),
            pltpu.VMEM((BLOCK + tl, W_SWA_KV), BF16),
            pltpu.VMEM((BLOCK + tl, W_SWA_KV), BF16),
            pltpu.VMEM((tl, W_SWA), BF16),
        ],
        compiler_params=_compiler_params(("arbitrary", "arbitrary")),
        name="prompt_c",
    )(sink, x, mod_p, yacd, yb, w_out_b, final_g, swa_bias, *cache_args)


def _sa_kernel(x_ref, mod_ref, g_ref, w_ref, wbd_ref, ps_ref, cw_ref, sp_ref, sc_ref,
               qkv_ref, sw_ref, yac_ref, gates_ref, pst_ref, cst_ref):
    hb = _modulated_norm(x_ref[...], g_ref[...], mod_ref[...]).astype(BF16)
    ns = hb.shape[0]

    zp = _dot(hb, w_ref[:, OFF_POOL:OFF_POOL + 2 * W_POOL])
    pu = zp[:, 0:W_POOL]
    pg = zp[:, W_POOL:2 * W_POOL]
    acc = pu
    sums = {}
    for j in range(1, 16):
        acc = acc + sp_ref[POOL_BUF - j]
        if j + 1 in POOL_WINDOWS:
            sums[j + 1] = acc
    lane = lax.broadcasted_iota(jnp.int32, (ns, W_POOL), 1)
    sel, win = _pool_select(sums, lane)
    diff = sel / win.astype(F32) - pu
    a_out = _dot(diff.astype(BF16), wbd_ref[...]) * ps_ref[...]
    yac_ref[:, 0:W_POOL] = a_out * _silu(pg)
    pst_ref[0:POOL_BUF - 1] = sp_ref[1:POOL_BUF]
    pst_ref[POOL_BUF - 1] = pu

    zc4 = _dot(hb, w_ref[:, OFF_CONV:OFF_CONV + 4 * W_CONV])
    ch = zc4[:, 0:W_CONV]
    cb = zc4[:, W_CONV:2 * W_CONV]
    cc = zc4[:, 2 * W_CONV:3 * W_CONV]
    cg = zc4[:, 3 * W_CONV:4 * W_CONV]
    zc = cc * ch
    conv = (cw_ref[0:1, :] * sc_ref[:, 0:W_CONV] + cw_ref[1:2, :] * sc_ref[:, W_CONV:2 * W_CONV]
            + cw_ref[2:3, :] * zc)
    yac_ref[:, W_POOL:W_POOL + W_CONV] = cb * conv * _silu(cg)
    cst_ref[:, 0:W_CONV] = sc_ref[:, W_CONV:2 * W_CONV]
    cst_ref[:, W_CONV:2 * W_CONV] = zc

    zd = _dot(hb, w_ref[:, OFF_DIL:OFF_DIL + 4 * W_DIL])
    qkv_ref[:, 0:W_DIL] = zd[:, 0:W_DIL] * QK_SCALE
    qkv_ref[:, W_DIL:3 * W_DIL] = zd[:, W_DIL:3 * W_DIL]
    gates_ref[:, 0:W_DIL] = _silu(zd[:, 3 * W_DIL:4 * W_DIL])

    zs = _dot(hb, w_ref[:, OFF_SWA:OFF_SWA + 3 * W_SWA])
    sw_ref[:, 0:W_SWA] = zs[:, 0:W_SWA] * QK_SCALE
    sw_ref[:, W_SWA:2 * W_SWA] = zs[:, W_SWA:2 * W_SWA]
    gates_ref[:, W_DIL:W_DIL + W_SWA] = _silu(zs[:, 2 * W_SWA:3 * W_SWA])


def _whole_spec(shape):
    return pl.BlockSpec(shape, lambda *_: (0,) * len(shape))


def _layer_spec(shape, layer):
    return pl.BlockSpec((None,) + tuple(shape[1:]), lambda *_: (layer,) + (0,) * (len(shape) - 1))


def _sample_a(xs, mod_s, norm_g, w_in_b, wbd, pool_scale, conv_w, sp, sc, layer):
    ns = xs.shape[0]
    shapes = [
        jax.ShapeDtypeStruct((ns, 3 * W_DIL), F32),
        jax.ShapeDtypeStruct((ns, 2 * W_SWA), F32),
        jax.ShapeDtypeStruct((ns, W_POOL + W_CONV), F32),
        jax.ShapeDtypeStruct((ns, W_DIL + W_SWA), F32),
        jax.ShapeDtypeStruct((POOL_BUF, ns, W_POOL), F32),
        jax.ShapeDtypeStruct((ns, CONV_BUF * W_CONV), F32),
    ]
    args = (xs, mod_s, norm_g, w_in_b, wbd, pool_scale, conv_w, sp, sc)
    in_specs = [_whole_spec(a.shape) for a in args]
    in_specs[3] = _layer_spec(w_in_b.shape, layer)
    return pl.pallas_call(
        _sa_kernel,
        grid=(1,),
        in_specs=in_specs,
        out_specs=[_whole_spec(s.shape) for s in shapes],
        out_shape=shapes,
        compiler_params=_compiler_params(("arbitrary",)),
        name="sample_a",
    )(*args)


def _cache_attention(dq_ref, sq_ref, dbias_ref, mult_ref, sbias_ref, sink_ref, kt_ref, vt_ref,
                     ck_ref, cv_ref, bo_ref, do_ref):
    row_d = lax.broadcasted_iota(jnp.int32, (8, DIL_MAX), 0)
    row_o = lax.broadcasted_iota(jnp.int32, (8, HEAD_DIM), 0)
    row_s = lax.broadcasted_iota(jnp.int32, (8, SWA_WINDOW), 0)
    for j in range(DEC_TILE):
        q = dq_ref[j, 0]
        qb = q.astype(BF16)
        s = jnp.zeros((8, DIL_MAX), F32)
        for h in range(N_HEADS_DIL):
            s = jnp.where(row_d == h, _dot(qb, kt_ref[0, j, h].astype(BF16)), s)
            yield
        s = s + dbias_ref[...]
        s_self = jnp.sum(q * dq_ref[j, 1], axis=-1, keepdims=True)
        m = jnp.maximum(jnp.max(s, axis=-1, keepdims=True), s_self)
        p = jnp.exp(s - m) * mult_ref[...]
        p_self = float(len(DIL_CONFIGS)) * jnp.exp(s_self - m)
        den = jnp.sum(p, axis=-1, keepdims=True) + p_self
        pb = p.astype(BF16)
        yield
        acc = jnp.zeros((8, HEAD_DIM), F32)
        for h in range(N_HEADS_DIL):
            acc = jnp.where(row_o == h, _dot_nt(pb, vt_ref[0, j, h].astype(BF16)), acc)
            yield
        bo_ref[j] = (acc + p_self * dq_ref[j, 2]) / den

        q = sq_ref[j, 0]
        qb = q.astype(BF16)
        s = jnp.where(row_s < 2, _dot(qb, ck_ref[0, j, 0].astype(BF16)),
                      _dot(qb, ck_ref[0, j, 1].astype(BF16))) + sbias_ref[...]
        w_self = jnp.sum(q * sq_ref[j, 1], axis=-1, keepdims=True)
        sink = sink_ref[...][:, 0:1]
        m = jnp.maximum(jnp.maximum(jnp.max(s, axis=-1, keepdims=True), w_self), sink)
        p = jnp.exp(s - m)
        pw = jnp.exp(w_self - m)
        den = jnp.sum(p, axis=-1, keepdims=True) + pw + jnp.exp(sink - m)
        pb = p.astype(BF16)
        yield
        acc = jnp.where(row_o < 2, _dot_nt(pb, cv_ref[0, j, 0].astype(BF16)),
                        _dot_nt(pb, cv_ref[0, j, 1].astype(BF16)))
        do_ref[j] = (acc + pw * sq_ref[j, 2]) / den
        yield


def _advance(stages, n):
    for _ in range(n):
        next(stages, None)


def _sc_kernel(x_ref, mod_ref, yac_ref, gates_ref, bo_ref, do_ref, w_ref, fg_ref, o_ref, *, final):
    yac = yac_ref[...]
    gates = gates_ref[...]
    y = jnp.concatenate([yac[:, 0:W_POOL], bo_ref[...] * gates[:, 0:W_DIL],
                         yac[:, W_POOL:W_POOL + W_CONV], do_ref[...] * gates[:, W_DIL:W_DIL + W_SWA]],
                        axis=-1).astype(BF16)
    gate = mod_ref[...][:, 2 * D_MODEL:3 * D_MODEL]
    o_ref[...] = _mix_out(y, x_ref[...], gate, w_ref, fg_ref, final)


def _sample_c(xs, mod_s, yac, gates, bo, do, w_out_b, final_g, final, layer):
    args = (xs, mod_s, yac, gates, bo, do, w_out_b, final_g)
    in_specs = [_whole_spec(a.shape) for a in args]
    in_specs[6] = _layer_spec(w_out_b.shape, layer)
    return pl.pallas_call(
        functools.partial(_sc_kernel, final=final),
        grid=(1,),
        in_specs=in_specs,
        out_specs=_whole_spec(xs.shape),
        out_shape=jax.ShapeDtypeStruct(xs.shape, F32),
        compiler_params=_compiler_params(("arbitrary",)),
        name="sample_c",
    )(*args)


def _band_bias_dil4(slope_lo, slope_hi, variant):
    sub = BLOCK // 4
    qidx = np.arange(BLOCK)[:, None]
    kidx = np.arange(2 * BLOCK)[None, :]
    q_step = 4 * (qidx % sub + (sub if variant == 0 else 0)) + qidx // sub
    k_step = 4 * (kidx % (2 * sub)) + kidx // (2 * sub)
    off = q_step - k_step
    valid = (off >= 0) & (off <= BLOCK)
    out = []
    for s in (slope_lo, slope_hi):
        out.append(np.where(valid, -(np.float32(s) * np.float32(4)) * off.astype(np.float32),
                            np.float32(NEG_INF)).astype(np.float32))
    return np.concatenate(out, axis=0)


def _prompt_bias_tables():
    dil = _alibi_slopes(N_HEADS_DIL)

    def table(hp, d, var):
        if d == 4:
            return _band_bias_dil4(dil[2 * hp], dil[2 * hp + 1], var)
        return _band_bias(dil[2 * hp], dil[2 * hp + 1], d, var)

    dil_bias = np.stack([
        np.stack([np.stack([table(hp, d, var) for var in (0, 1)])
                  for _, d in DIL_CONFIGS]) for hp in range(N_PAIRS_DIL)])
    swa = _alibi_slopes(N_HEADS_SWA)
    swa_bias = np.stack([
        np.stack([_band_bias(swa[SWA_HEAD_PERM[2 * g]], swa[SWA_HEAD_PERM[2 * g + 1]], 1, var)
                  for var in (0, 2)]) for g in range(2)])
    return jnp.asarray(dil_bias), jnp.asarray(swa_bias)


def _sample_bias_tables(r):
    dil = _alibi_slopes(N_HEADS_DIL)
    dist = (r - np.arange(r)).astype(np.float32)
    mult = np.zeros((1, r), np.float32)
    for window, d in DIL_CONFIGS:
        mult[0] += ((dist <= window) & (dist % d == 0)).astype(np.float32)
    dbias = np.zeros((8, r), np.float32)
    for h in range(N_HEADS_DIL):
        dbias[h] = np.where(mult[0] > 0, -np.float32(dil[h]) * dist, np.float32(NEG_INF))
    swa = _alibi_slopes(N_HEADS_SWA)
    sdist = (SWA_WINDOW - np.arange(SWA_WINDOW)).astype(np.float32)
    sbias = np.zeros((8, SWA_WINDOW), np.float32)
    for h in range(N_HEADS_SWA):
        sbias[h] = -np.float32(swa[h]) * sdist
    return jnp.asarray(dbias), jnp.asarray(mult), jnp.asarray(sbias)


def kernel(x_prompt, x_sample, c_prompt, c_sample, state_pool, cache_dil_k, cache_dil_v, state_conv, cache_swa_k, cache_swa_v, norm_g, w_ada, b_ada, w_in, w_pool, pool_scale, conv_w, swa_sink, w_out, final_g):
    depth = w_in.shape[0]
    nb, l, _ = x_prompt.shape
    ns = x_sample.shape[0]
    assert x_sample.shape[1] == 1 and l % CHUNK == 0 and l >= DIL_MAX
    assert ns == 2 * nb * (l // SEQ_TILE) * DEC_TILE
    assert cache_dil_k.shape[2] == DIL_MAX and cache_swa_k.shape[2] == SWA_WINDOW

    w_in_b = w_in.astype(BF16)
    w_out_b = w_out.astype(BF16)
    eye = jnp.eye(len(POOL_WINDOWS), dtype=F32)
    wbd = jnp.einsum("dgce,gh->dgche", w_pool, eye).reshape(depth, W_POOL, W_POOL).astype(BF16)
    sink_rows = jnp.zeros((depth, 8), F32).at[:, 0:N_HEADS_SWA].set(swa_sink)
    sink_rows = jnp.broadcast_to(sink_rows[:, :, None], (depth, 8, BLOCK))

    dil_bias, swa_bias = _prompt_bias_tables()
    dbias, mult, sbias = _sample_bias_tables(cache_dil_k.shape[2])

    mod = _ada(jnp.concatenate([c_prompt, c_sample], axis=0), w_ada, b_ada)
    fg = final_g.reshape(1, D_MODEL)

    xp = x_prompt
    xs = x_sample.reshape(ns, D_MODEL)
    kt = jnp.transpose(cache_dil_k, (0, 1, 3, 4, 2))
    vt = jnp.transpose(cache_dil_v, (0, 1, 3, 4, 2))
    ckt = jnp.transpose(cache_swa_k, (0, 1, 3, 4, 2))
    cvt = jnp.transpose(cache_swa_v, (0, 1, 3, 4, 2))
    sp_all = jnp.transpose(state_pool, (0, 2, 1, 3))
    sc_all = state_conv.reshape(depth, ns, CONV_BUF * W_CONV)
    pad_heads = lambda a: jnp.pad(a, ((0, 0), (0, 0), (0, 8 - a.shape[2]), (0, 0)))
    outs = [[] for _ in range(12)]
    carried = None
    for i in range(depth):
        final = i == depth - 1
        mod_p = mod[i, 0:nb].reshape(nb, 1, 3 * D_MODEL)
        mod_s = mod[i, nb:nb + ns]
        g = norm_g[i].reshape(1, D_MODEL)
        ps = pool_scale[i].reshape(1, W_POOL)

        qkv, sw, yac, gates, pst_s, cst_s = _sample_a(
            xs, mod_s, g, w_in_b, wbd[i], ps, conv_w[i], sp_all[i], sc_all[i], i)
        dq3 = pad_heads(qkv.reshape(ns, 3, N_HEADS_DIL, HEAD_DIM))
        sq4 = sw[:, 0:W_SWA].reshape(ns, 1, N_HEADS_SWA, HEAD_DIM)
        skv = jnp.repeat(sw[:, W_SWA:W_SWA + 2 * W_SWA_KV].reshape(ns, 2, 2, HEAD_DIM), 2, axis=2)
        sq3 = pad_heads(jnp.concatenate([sq4, skv], axis=1))

        cache_args = (dq3, sq3, dbias, mult, sbias, sink_rows[i], kt, vt, ckt, cvt)
        q1, k1, v1, qm, km, vm, gm, kc, vc, yacd, pst, cst, skc, svc, bo_a, do_a = _prompt_a(
            xp, mod_p, g, w_in_b, wbd[i], ps, conv_w[i], i, depth, 0, cache_args, carried)
        carried = (kc, vc)
        yb = _prompt_b(q1, k1, v1, qm, km, vm, gm, dil_bias)
        xp, bo_c, do_c = _prompt_c(xp, mod_p, yacd, yb, w_out_b, fg, swa_sink[i], swa_bias, final,
                                   i, ns // 2, cache_args)

        bo = jnp.concatenate([bo_a, bo_c], axis=0)[:, 0:N_HEADS_DIL].reshape(ns, W_DIL)
        do = jnp.concatenate([do_a, do_c], axis=0)[:, 0:N_HEADS_SWA].reshape(ns, W_SWA)
        xs = _sample_c(xs, mod_s, yac, gates, bo, do, w_out_b, fg, final, i)

        unfold = lambda a, h: jnp.transpose(a.reshape(nb, h, HEAD_DIM, a.shape[-1]), (0, 3, 1, 2))
        skc, svc = unfold(skc, 2), unfold(svc, 2)
        outs[0].append(pst)
        outs[1].append(jnp.transpose(pst_s, (1, 0, 2)))
        outs[4].append(qkv[:, W_DIL:2 * W_DIL].reshape(ns, 1, N_HEADS_DIL, HEAD_DIM))
        outs[5].append(qkv[:, 2 * W_DIL:3 * W_DIL].reshape(ns, 1, N_HEADS_DIL, HEAD_DIM))
        outs[6].append(cst)
        outs[7].append(cst_s.reshape(ns, CONV_BUF, W_CONV))
        outs[8].append(skc)
        outs[9].append(svc)
        outs[10].append(sw[:, W_SWA:W_SWA + W_SWA_KV].reshape(ns, 1, 2, HEAD_DIM))
        outs[11].append(sw[:, W_SWA + W_SWA_KV:W_SWA + 2 * W_SWA_KV].reshape(ns, 1, 2, HEAD_DIM))

    for k, cache in zip((2, 3), carried):
        outs[k] = jnp.transpose(cache.reshape(depth, nb, N_HEADS_DIL, HEAD_DIM, DIL_MAX), (0, 1, 4, 2, 3))
    return (xp, xs.reshape(ns, 1, D_MODEL)) + tuple(
        o if not isinstance(o, list) else jnp.stack(o) for o in outs)
```

```python
import functools
import math

import numpy as np
import jax
import jax.numpy as jnp
from jax import lax
from jax.experimental import pallas as pl
from jax.experimental.pallas import tpu as pltpu

F32 = jnp.float32
BF16 = jnp.bfloat16

D_MODEL = 1024
HEAD_DIM = 64
BLOCK = 128
POOL_WINDOWS = (2, 4, 8, 16)
POOL_GROUP = 64
W_POOL = 256
POOL_BUF = 15
DIL_CONFIGS = ((128, 1), (512, 4), (2048, 16))
DIL_MAX = 2048
N_HEADS_DIL = 6
N_PAIRS_DIL = 3
N_CLASSES = 16
W_DIL = 384
W_CONV = 256
CONV_BUF = 2
N_HEADS_SWA = 4
W_SWA = 256
W_SWA_KV = 128
SWA_WINDOW = 128
D_MIX = 1152
D_PROJ = 3840
RMS_EPS = 1e-6
QK_SCALE = 1.0 / math.sqrt(HEAD_DIM)

OFF_POOL = 0
OFF_DIL = 512
OFF_CONV = 2048
OFF_SWA = 3072
Y_WIDTH = W_POOL + W_CONV + 3 * W_SWA
SWA_HEAD_PERM = (0, 3, 1, 2)

SEQ_TILE = 512
CHUNK = 2048
BAND_SKEW = 1
DEC_TILE = 1
RIDE_STAGES = 15 * DEC_TILE
VMEM_LIMIT = 56 * 1024 * 1024

NEG_INF = float("-inf")


def _silu(v):
    return v * jax.nn.sigmoid(v)


def _dot(a, b):
    return jnp.dot(a, b, preferred_element_type=F32)


def _dot_nt(a, b):
    return lax.dot_general(a, b, (((1,), (1,)), ((), ())), preferred_element_type=F32)


def _alibi_slopes(n):
    return [2.0 ** (-8.0 * (h + 1) / n) for h in range(n)]


def _band_bias(slope_lo, slope_hi, dist_scale, variant):
    qi = np.arange(BLOCK)[:, None]
    kj = np.arange(2 * BLOCK)[None, :]
    if variant == 1:
        off = qi - kj
    else:
        off = qi - kj + BLOCK
    valid = (off >= 0) & (off <= BLOCK)
    if variant == 2:
        valid = valid & (kj >= BLOCK)
    out = []
    for s in (slope_lo, slope_hi):
        b = np.where(valid, -(np.float32(s) * np.float32(dist_scale)) * off.astype(np.float32),
                     np.float32(NEG_INF))
        out.append(b.astype(np.float32))
    return np.concatenate(out, axis=0)


def _compiler_params(sem):
    return pltpu.CompilerParams(dimension_semantics=sem, vmem_limit_bytes=VMEM_LIMIT)


def _ada_kernel(c_ref, w_ref, b_ref, o_ref):
    s = _silu(c_ref[...]).astype(BF16)
    o_ref[0] = _dot(s, w_ref[0].astype(BF16)) + b_ref[0]


def _ada(c_all, w_ada_b, b_ada):
    depth = w_ada_b.shape[0]
    rows = c_all.shape[0]
    return pl.pallas_call(
        _ada_kernel,
        grid=(depth, 3),
        in_specs=[
            pl.BlockSpec((rows, D_MODEL), lambda i, j: (0, 0)),
            pl.BlockSpec((1, D_MODEL, D_MODEL), lambda i, j: (i, 0, j)),
            pl.BlockSpec((1, 1, D_MODEL), lambda i, j: (i, 0, j)),
        ],
        out_specs=pl.BlockSpec((1, rows, D_MODEL), lambda i, j: (i, 0, j)),
        out_shape=jax.ShapeDtypeStruct((depth, rows, 3 * D_MODEL), F32),
        compiler_params=_compiler_params(("arbitrary", "arbitrary")),
        name="ada",
    )(c_all, w_ada_b, b_ada.reshape(depth, 1, 3 * D_MODEL))


def _modulated_norm(x, g, mod_row):
    ms = jnp.mean(x * x, axis=-1, keepdims=True)
    shift = mod_row[:, 0:D_MODEL]
    scale = mod_row[:, D_MODEL:2 * D_MODEL]
    return x * lax.rsqrt(ms + RMS_EPS) * (g * (1.0 + scale)) + shift


def _pair_heads(v):
    left, right = v[:, 0:BLOCK], v[:, BLOCK:2 * BLOCK]
    low = lax.broadcasted_iota(jnp.int32, left.shape, 1) < HEAD_DIM
    return jnp.concatenate([jnp.where(low, left, right),
                            pltpu.roll(jnp.where(low, right, left), HEAD_DIM, axis=1)], axis=-1)


def _unpair_heads(a, b):
    low = lax.broadcasted_iota(jnp.int32, a.shape, 1) < HEAD_DIM
    b_swapped = pltpu.roll(b, HEAD_DIM, axis=1)
    return jnp.where(low, a, b_swapped), jnp.where(low, b_swapped, a)


def _pool_select(sums, lane):
    grp = lane // POOL_GROUP
    sel = jnp.where(grp == 0, sums[2], jnp.where(grp == 1, sums[4],
                                                 jnp.where(grp == 2, sums[8], sums[16])))
    win = jnp.where(grp == 0, 2, jnp.where(grp == 1, 4, jnp.where(grp == 2, 8, 16)))
    return sel, win


def _pa_kernel(x_ref, mod_ref, g_ref, w_ref, wbd_ref, ps_ref, cw_ref,
               dq_ref, sq_ref, dbias_ref, mult_ref, sbias_ref, ssink_ref, kt_ref, vt_ref, ck_ref, cv_ref,
               q1_ref, k1_ref, v1_ref, qm_ref, km_ref, vm_ref, gm_ref, kc_ref, vc_ref, y_ref,
               pst_ref, cst_ref, skc_ref, svc_ref, bo_ref, do_ref,
               uext, zcext, zds, z4s, *, later_layers=0):
    tl = SEQ_TILE
    t = pl.program_id(1)

    @pl.when(t == 0)
    def _():
        for k in range(len(POOL_WINDOWS)):
            uext[k, 0:16, :] = jnp.zeros((16, W_POOL), F32)
        zcext[0:8, :] = jnp.zeros((8, W_CONV), F32)

    ride = _cache_attention(dq_ref, sq_ref, dbias_ref, mult_ref, sbias_ref, ssink_ref, kt_ref, vt_ref,
                            ck_ref, cv_ref, bo_ref, do_ref)

    _advance(ride, 3)
    hb = _modulated_norm(x_ref[0], g_ref[...], mod_ref[0]).astype(BF16)

    zp = _dot(hb, w_ref[:, OFF_POOL:OFF_POOL + 2 * W_POOL])
    zd = _dot(hb, w_ref[:, OFF_DIL:OFF_DIL + 4 * W_DIL])
    _advance(ride, 3)

    pu = zp[:, 0:W_POOL]
    pg = zp[:, W_POOL:2 * W_POOL]
    uext[0, 16:16 + tl, :] = pu
    sums = {}
    level = pu
    for k, w in enumerate(POOL_WINDOWS):
        shift = w // 2
        level = level + uext[k, 16 - shift:16 - shift + tl, :]
        sums[w] = level
        if k + 1 < len(POOL_WINDOWS):
            uext[k + 1, 16:16 + tl, :] = level
    lane = lax.broadcasted_iota(jnp.int32, (tl, W_POOL), 1)
    gpos = lax.broadcasted_iota(jnp.int32, (tl, W_POOL), 0) + t * tl
    sel, win = _pool_select(sums, lane)
    cnt = jnp.minimum(gpos + 1, win).astype(F32)
    diff = sel / cnt - pu
    a_out = _dot(diff.astype(BF16), wbd_ref[...]) * ps_ref[...]
    y_ref[0, :, 0:W_POOL] = (a_out * _silu(pg)).astype(BF16)
    pst_ref[0] = uext[0, tl + 1:tl + 16, :]
    for k in range(len(POOL_WINDOWS)):
        uext[k, 0:16, :] = uext[k, tl:tl + 16, :]

    zs = _dot(hb, w_ref[:, OFF_SWA:OFF_SWA + 3 * W_SWA])
    _advance(ride, 3)
    kc_ref[0, 0] = zd[:, W_DIL:2 * W_DIL].T
    vc_ref[0, 0] = zd[:, 2 * W_DIL:3 * W_DIL].T
    for d in range(1, 1 + later_layers):
        kc_ref[d, 0] = jnp.zeros((W_DIL, tl), F32)
        vc_ref[d, 0] = jnp.zeros((W_DIL, tl), F32)
    for hp in range(N_PAIRS_DIL):
        _advance(ride, 1)
        lo = hp * BLOCK
        zds[0] = zd[:, lo:lo + BLOCK] * QK_SCALE
        zds[1] = zd[:, W_DIL + lo:W_DIL + lo + BLOCK]
        zds[2] = zd[:, 2 * W_DIL + lo:2 * W_DIL + lo + BLOCK]
        zds[3] = _silu(zd[:, 3 * W_DIL + lo:3 * W_DIL + lo + BLOCK])
        q1_ref[0, hp] = zds[0].astype(BF16)
        k1_ref[0, hp] = zds[1].astype(BF16)
        v1_ref[0, hp] = zds[2].astype(BF16)
        quarter = tl // 4
        for a in range(4):
            for blk in range(4):
                z4s[blk, a] = zds[blk, pl.ds(a, quarter, stride=4), :]
        for b in range(4):
            for a in range(4):
                c = 4 * b + a
                rows = pl.ds(b, tl // N_CLASSES, stride=4)
                qm_ref[0, hp, c] = z4s[0, a, rows, :].astype(BF16)
                km_ref[0, hp, c] = z4s[1, a, rows, :].astype(BF16)
                vm_ref[0, hp, c] = z4s[2, a, rows, :].astype(BF16)
                gm_ref[0, hp, c] = z4s[3, a, rows, :].astype(BF16)

    zc4 = _dot(hb, w_ref[:, OFF_CONV:OFF_CONV + 4 * W_CONV])
    _advance(ride, 3)

    sk = zs[:, W_SWA:W_SWA + W_SWA_KV]
    sv = zs[:, W_SWA + W_SWA_KV:W_SWA + 2 * W_SWA_KV]
    skc_ref[0] = sk[tl - SWA_WINDOW:tl, :].T
    svc_ref[0] = sv[tl - SWA_WINDOW:tl, :].T
    c0 = W_POOL + W_CONV
    y_ref[0, :, c0:c0 + W_SWA] = _pair_heads(zs[:, 0:W_SWA] * QK_SCALE).astype(BF16)
    y_ref[0, :, c0 + W_SWA:c0 + 2 * W_SWA] = zs[:, W_SWA:2 * W_SWA].astype(BF16)
    y_ref[0, :, c0 + 2 * W_SWA:c0 + 3 * W_SWA] = _pair_heads(_silu(zs[:, 2 * W_SWA:3 * W_SWA])).astype(BF16)
    _advance(ride, RIDE_STAGES)

    ch = zc4[:, 0:W_CONV]
    cb = zc4[:, W_CONV:2 * W_CONV]
    cc = zc4[:, 2 * W_CONV:3 * W_CONV]
    cg = zc4[:, 3 * W_CONV:4 * W_CONV]
    zc = cc * ch
    zcext[8:8 + tl, :] = zc
    conv = (cw_ref[0:1, :] * zcext[6:6 + tl, :] + cw_ref[1:2, :] * zcext[7:7 + tl, :]
            + cw_ref[2:3, :] * zc)
    y_ref[0, :, W_POOL:W_POOL + W_CONV] = (cb * conv * _silu(cg)).astype(BF16)
    cst_ref[0] = zcext[tl + 6:tl + 8, :]
    zcext[0:8, :] = zcext[tl:tl + 8, :]


def _cache_attention_specs(layer, row0, nb, nt, r, swr):
    bt = DEC_TILE
    blk0 = row0 // bt
    full = lambda *shape: pl.BlockSpec(shape, lambda n, t: (0,) * len(shape))
    srow_spec = pl.BlockSpec((bt, 3, 8, HEAD_DIM), lambda n, t: (blk0 + n * nt + t, 0, 0, 0))
    dil_spec = pl.BlockSpec((1, bt, N_HEADS_DIL, HEAD_DIM, r),
                            lambda n, t: (layer, blk0 + n * nt + t, 0, 0, 0))
    swa_spec = pl.BlockSpec((1, bt, 2, HEAD_DIM, swr), lambda n, t: (layer, blk0 + n * nt + t, 0, 0, 0))
    sout_spec = pl.BlockSpec((bt, 8, HEAD_DIM), lambda n, t: (n * nt + t, 0, 0))
    sout_shape = jax.ShapeDtypeStruct((nb * nt * bt, 8, HEAD_DIM), F32)
    in_specs = [srow_spec, srow_spec, full(8, r), full(1, r), full(8, swr), full(8, BLOCK),
                dil_spec, dil_spec, swa_spec, swa_spec]
    return in_specs, [sout_spec, sout_spec], [sout_shape, sout_shape]


N_PA_INPUTS = 17


def _pa_kernel_carry(*refs):
    _pa_kernel(*refs[:N_PA_INPUTS], *refs[N_PA_INPUTS + 2:])


def _prompt_a(x, mod_p, norm_g, w_in_b, wbd, pool_scale, conv_w, layer, depth, row0, cache_args, carried):
    nb, l, _ = x.shape
    tl = SEQ_TILE
    nt = l // tl
    cache_t0 = (l - DIL_MAX) // tl
    c_in, c_out, c_shape = _cache_attention_specs(layer, row0, nb, nt, cache_args[6].shape[-1],
                                                  cache_args[8].shape[-1])
    pair_spec = pl.BlockSpec((1, N_PAIRS_DIL, tl, BLOCK), lambda n, t: (n, 0, t, 0))
    pair_shape = jax.ShapeDtypeStruct((nb, N_PAIRS_DIL, l, BLOCK), BF16)
    cm_spec = pl.BlockSpec((1, N_PAIRS_DIL, N_CLASSES, tl // N_CLASSES, BLOCK),
                           lambda n, t: (n, 0, 0, t, 0))
    cm_shape = jax.ShapeDtypeStruct((nb, N_PAIRS_DIL, N_CLASSES, l // N_CLASSES, BLOCK), BF16)
    cache_layers = depth if carried is None else 1
    cache_spec = pl.BlockSpec((cache_layers, 1, W_DIL, tl),
                              lambda n, t: (layer, n, 0, jnp.maximum(t - cache_t0, 0)))
    cache_shape = jax.ShapeDtypeStruct((depth, nb, W_DIL, DIL_MAX), F32)
    full = lambda *shape: pl.BlockSpec(shape, lambda n, t: (0,) * len(shape))
    in_specs = [
        pl.BlockSpec((1, tl, D_MODEL), lambda n, t: (n, t, 0)),
        pl.BlockSpec((1, 1, 3 * D_MODEL), lambda n, t: (n, 0, 0)),
        full(1, D_MODEL),
        pl.BlockSpec((None, D_MODEL, D_PROJ), lambda n, t: (layer, 0, 0), pipeline_mode=pl.Buffered(1)),
        full(W_POOL, W_POOL),
        full(1, W_POOL),
        full(3, W_CONV),
    ] + c_in
    assert len(in_specs) == N_PA_INPUTS
    extra, aliases = (), {}
    if carried is not None:
        in_specs = in_specs + [pl.BlockSpec(memory_space=pl.ANY)] * 2
        extra = tuple(carried)
        aliases = {N_PA_INPUTS: 7, N_PA_INPUTS + 1: 8}
    return pl.pallas_call(
        functools.partial(_pa_kernel, later_layers=depth - 1) if carried is None else _pa_kernel_carry,
        grid=(nb, nt),
        in_specs=in_specs,
        input_output_aliases=aliases,
        out_specs=[
            pair_spec, pair_spec, pair_spec, cm_spec, cm_spec, cm_spec, cm_spec,
            cache_spec, cache_spec,
            pl.BlockSpec((1, tl, Y_WIDTH), lambda n, t: (n, t, 0)),
            pl.BlockSpec((1, POOL_BUF, W_POOL), lambda n, t: (n, 0, 0)),
            pl.BlockSpec((1, CONV_BUF, W_CONV), lambda n, t: (n, 0, 0)),
            pl.BlockSpec((1, SWA_WINDOW, W_SWA_KV), lambda n, t: (n, 0, 0)),
            pl.BlockSpec((1, SWA_WINDOW, W_SWA_KV), lambda n, t: (n, 0, 0)),
        ] + c_out,
        out_shape=[
            pair_shape, pair_shape, pair_shape, cm_shape, cm_shape, cm_shape, cm_shape,
            cache_shape, cache_shape,
            jax.ShapeDtypeStruct((nb, l, Y_WIDTH), BF16),
            jax.ShapeDtypeStruct((nb, POOL_BUF, W_POOL), F32),
            jax.ShapeDtypeStruct((nb, CONV_BUF, W_CONV), F32),
            jax.ShapeDtypeStruct((nb, SWA_WINDOW, W_SWA_KV), F32),
            jax.ShapeDtypeStruct((nb, SWA_WINDOW, W_SWA_KV), F32),
        ] + c_shape,
        scratch_shapes=[
            pltpu.VMEM((len(POOL_WINDOWS), 16 + tl, W_POOL), F32),
            pltpu.VMEM((8 + tl, W_CONV), F32),
            pltpu.VMEM((4, tl, BLOCK), F32),
            pltpu.VMEM((4, 4, tl // 4, BLOCK), F32),
        ],
        compiler_params=_compiler_params(("arbitrary", "arbitrary")),
        name="prompt_a",
    )(x, mod_p, norm_g, w_in_b, wbd, pool_scale, conv_w, *cache_args, *extra)


def _band_scores(q, kblk, bias, lane_q):
    lhs = jnp.concatenate([jnp.where(lane_q < HEAD_DIM, q, jnp.zeros_like(q)),
                           jnp.where(lane_q >= HEAD_DIM, q, jnp.zeros_like(q))], axis=0)
    return _dot_nt(lhs, kblk) + bias


def _band_values(s, vblk, lane_q):
    m = jnp.max(s, axis=-1, keepdims=True)
    p = jnp.exp((s - m).astype(BF16))
    acc = _dot(p, jnp.concatenate([vblk, jnp.ones_like(vblk)], axis=-1))
    lo = lane_q < HEAD_DIM
    return (jnp.where(lo, acc[0:BLOCK, 0:BLOCK], acc[BLOCK:2 * BLOCK, 0:BLOCK]),
            jnp.where(lo, m[0:BLOCK], m[BLOCK:2 * BLOCK]),
            jnp.where(lo, acc[0:BLOCK, BLOCK:2 * BLOCK], acc[BLOCK:2 * BLOCK, BLOCK:2 * BLOCK]))


def _run_skewed(tasks, skew):
    pending = []
    for scores_fn, finish_fn in tasks:
        pending.append((finish_fn, scores_fn()))
        if len(pending) > skew:
            fn, s = pending.pop(0)
            fn(s)
    for fn, s in pending:
        fn(s)


def _pb_kernel(bias_ref, q1, k1, v1, qm, km, vm, gm, outm, s1, s4, s16):
    c = pl.program_id(2)
    first = jnp.where(c == 0, 1, 0)
    lane_q = lax.broadcasted_iota(jnp.int32, (BLOCK, BLOCK), 1)
    sub = BLOCK // 4

    tasks = []

    def dil1_task(j):
        r0 = j * BLOCK
        if j == 0:
            start = jnp.maximum(c * (CHUNK // BLOCK) - 1, 0) * BLOCK
            var = first
        else:
            start = (c * (CHUNK // BLOCK) + (j - 1)) * BLOCK
            var = 0
        start = pl.multiple_of(start, BLOCK)

        def scores():
            return _band_scores(q1[0, 0, r0:r0 + BLOCK, :], k1[0, 0, pl.ds(start, 2 * BLOCK), :],
                                bias_ref[0, 0, var], lane_q)

        def finish(s):
            parts = _band_values(s, v1[0, 0, pl.ds(start, 2 * BLOCK), :], lane_q)
            for k, part in enumerate(parts):
                s1[k, r0:r0 + BLOCK, :] = part

        return scores, finish

    def dil4_task(c4, j):
        i0 = j * sub
        if j == 0:
            istart = jnp.maximum(c * (CHUNK // N_CLASSES) - sub, 0)
            var = first
        else:
            istart = c * (CHUNK // N_CLASSES) + i0 - sub
            var = 0
        istart = pl.multiple_of(istart, sub)
        classes = [4 * cc + c4 for cc in range(4)]

        def scores():
            q = jnp.concatenate([qm[0, 0, cl, i0:i0 + sub, :] for cl in classes], axis=0)
            kblk = jnp.concatenate([km[0, 0, cl, pl.ds(istart, 2 * sub), :] for cl in classes], axis=0)
            return _band_scores(q, kblk, bias_ref[0, 1, var], lane_q)

        def finish(s):
            vblk = jnp.concatenate([vm[0, 0, cl, pl.ds(istart, 2 * sub), :] for cl in classes], axis=0)
            parts = _band_values(s, vblk, lane_q)
            for k, part in enumerate(parts):
                for cc, cl in enumerate(classes):
                    s4[k, cl, i0:i0 + sub, :] = part[cc * sub:(cc + 1) * sub]

        return scores, finish

    start16 = pl.multiple_of(jnp.maximum(c - 1, 0) * BLOCK, BLOCK)

    def dil16_task(cl):
        def scores():
            return _band_scores(qm[0, 0, cl], km[0, 0, cl, pl.ds(start16, 2 * BLOCK), :],
                                bias_ref[0, 2, first], lane_q)

        def finish(s):
            parts = _band_values(s, vm[0, 0, cl, pl.ds(start16, 2 * BLOCK), :], lane_q)
            for k, part in enumerate(parts):
                s16[k, cl] = part

        return scores, finish

    tasks += [dil1_task(j) for j in range(CHUNK // BLOCK)]
    tasks += [dil4_task(c4, j) for c4 in range(4) for j in range(4)]
    tasks += [dil16_task(cl) for cl in range(N_CLASSES)]
    _run_skewed(tasks, BAND_SKEW)

    for cl in range(N_CLASSES):
        rows = pl.ds(cl, BLOCK, stride=N_CLASSES)
        ma, mb, mc = s1[1, rows, :], s4[1, cl], s16[1, cl]
        mx = jnp.maximum(jnp.maximum(ma, mb), mc)
        wa = jnp.exp(ma - mx)
        wb = jnp.exp(mb - mx)
        wc = jnp.exp(mc - mx)
        num = wa * s1[0, rows, :] + wb * s4[0, cl] + wc * s16[0, cl]
        den = wa * s1[2, rows, :] + wb * s4[2, cl] + wc * s16[2, cl]
        outm[0, 0, cl] = (num / den * gm[0, 0, cl].astype(F32)).astype(BF16)


def _prompt_b(q1, k1, v1, qm, km, vm, gm, dil_bias):
    nb, npair, l, _ = q1.shape
    nc = l // CHUNK
    li = l // N_CLASSES
    ci = CHUNK // N_CLASSES
    q1_spec = pl.BlockSpec((1, 1, CHUNK, BLOCK), lambda n, h, c: (n, h, c, 0))
    kv1_spec = pl.BlockSpec((1, 1, l, BLOCK), lambda n, h, c: (n, h, 0, 0))
    cm_spec = pl.BlockSpec((1, 1, N_CLASSES, ci, BLOCK), lambda n, h, c: (n, h, 0, c, 0))
    kvm_spec = pl.BlockSpec((1, 1, N_CLASSES, li, BLOCK), lambda n, h, c: (n, h, 0, 0, 0))
    cm_scratch = pltpu.VMEM((3, N_CLASSES, ci, BLOCK), F32)
    return pl.pallas_call(
        _pb_kernel,
        grid=(nb, npair, nc),
        in_specs=[pl.BlockSpec((1, 3, 2, 2 * BLOCK, 2 * BLOCK), lambda n, h, c: (h, 0, 0, 0, 0)),
                  q1_spec, kv1_spec, kv1_spec, cm_spec, kvm_spec, kvm_spec, cm_spec],
        out_specs=cm_spec,
        out_shape=jax.ShapeDtypeStruct((nb, npair, N_CLASSES, li, BLOCK), BF16),
        scratch_shapes=[
            pltpu.VMEM((3, CHUNK, BLOCK), F32), cm_scratch, cm_scratch,
        ],
        compiler_params=_compiler_params(("arbitrary", "arbitrary", "arbitrary")),
        name="prompt_b",
    )(dil_bias, q1, k1, v1, qm, km, vm, gm)


def _mix_out(y, x, gate, w_ref, fg_ref, final):
    return _residual_out(_dot(y, w_ref[...]), x, gate, fg_ref, final)


def _residual_out(mixed, x, gate, fg_ref, final):
    xn = x + gate * mixed
    if final:
        ms = jnp.mean(xn * xn, axis=-1, keepdims=True)
        xn = xn * lax.rsqrt(ms + RMS_EPS) * fg_ref[...]
    return xn


def _pc_kernel(sink_ref, x_ref, mod_ref, yacd_ref, yb_ref, w_ref, fg_ref, bias_ref,
               dq_ref, sq_ref, dbias_ref, mult_ref, sbias_ref, ssink_ref, kt_ref, vt_ref, ck_ref, cv_ref,
               o_ref, bo_ref, do_ref, ybs, kext, vext, yds, *, final):
    tl = SEQ_TILE
    t = pl.program_id(1)

    @pl.when(t == 0)
    def _():
        kext[0:BLOCK, :] = jnp.zeros((BLOCK, W_SWA_KV), BF16)
        vext[0:BLOCK, :] = jnp.zeros((BLOCK, W_SWA_KV), BF16)

    for hp in range(N_PAIRS_DIL):
        for c in range(N_CLASSES):
            ybs[hp, pl.ds(c, tl // N_CLASSES, stride=N_CLASSES), :] = yb_ref[0, hp, c].astype(F32)
    k_head = D_MIX - W_SWA
    y_head = jnp.concatenate([yacd_ref[0, :, 0:W_POOL], ybs[0].astype(BF16), ybs[1].astype(BF16),
                              ybs[2].astype(BF16), yacd_ref[0, :, W_POOL:W_POOL + W_CONV]], axis=-1)
    n_tile = D_MODEL // (tl // BLOCK)
    proj = []
    ride = _cache_attention(dq_ref, sq_ref, dbias_ref, mult_ref, sbias_ref, ssink_ref, kt_ref, vt_ref,
                            ck_ref, cv_ref, bo_ref, do_ref)

    c0 = W_POOL + W_CONV
    kext[BLOCK:BLOCK + tl, :] = yacd_ref[0, :, c0 + W_SWA:c0 + W_SWA + W_SWA_KV]
    vext[BLOCK:BLOCK + tl, :] = yacd_ref[0, :, c0 + W_SWA + W_SWA_KV:c0 + 2 * W_SWA]
    first = jnp.where(t == 0, 1, 0)
    lane_q = lax.broadcasted_iota(jnp.int32, (BLOCK, BLOCK), 1)
    row_s = lax.broadcasted_iota(jnp.int32, (2 * BLOCK, 1), 0)
    for jb in range(tl // BLOCK):
        proj.append(_dot(y_head, w_ref[0:k_head, jb * n_tile:(jb + 1) * n_tile]))
        r0 = jb * BLOCK
        kblk = kext[r0:r0 + 2 * BLOCK, :]
        vblk = vext[r0:r0 + 2 * BLOCK, :]
        gated = []
        for grp in range(2):
            q = yacd_ref[0, r0:r0 + BLOCK, c0 + grp * BLOCK:c0 + (grp + 1) * BLOCK]
            lhs = jnp.concatenate([jnp.where(lane_q < HEAD_DIM, q, jnp.zeros_like(q)),
                                   jnp.where(lane_q >= HEAD_DIM, q, jnp.zeros_like(q))], axis=0)
            s = _dot_nt(lhs, kblk)
            if jb == 0:
                s = s + bias_ref[grp, first]
            else:
                s = s + bias_ref[grp, 0]
            sink = jnp.where(row_s < BLOCK, sink_ref[SWA_HEAD_PERM[2 * grp]],
                             sink_ref[SWA_HEAD_PERM[2 * grp + 1]])
            m = jnp.maximum(jnp.max(s, axis=-1, keepdims=True), sink)
            p = jnp.exp((s - m).astype(BF16))
            acc = _dot(p, jnp.concatenate([vblk, jnp.ones_like(vblk)], axis=-1))
            es = jnp.exp(sink - m)
            lo = lane_q < HEAD_DIM
            den = (jnp.where(lo, acc[0:BLOCK, BLOCK:2 * BLOCK], acc[BLOCK:2 * BLOCK, BLOCK:2 * BLOCK])
                   + jnp.where(lo, es[0:BLOCK], es[BLOCK:2 * BLOCK]))
            od = jnp.where(lo, acc[0:BLOCK, 0:BLOCK], acc[BLOCK:2 * BLOCK, 0:BLOCK]) / den
            g0 = c0 + 2 * W_SWA + grp * BLOCK
            gate_s = yacd_ref[0, r0:r0 + BLOCK, g0:g0 + BLOCK].astype(F32)
            gated.append(od * gate_s)
            _advance(ride, 2)
        h01, h23 = _unpair_heads(*gated)
        yds[r0:r0 + BLOCK, 0:BLOCK] = h01.astype(BF16)
        yds[r0:r0 + BLOCK, BLOCK:2 * BLOCK] = h23.astype(BF16)
    kext[0:BLOCK, :] = kext[tl:tl + BLOCK, :]
    vext[0:BLOCK, :] = vext[tl:tl + BLOCK, :]
    _advance(ride, RIDE_STAGES)

    mixed = jnp.concatenate(proj, axis=-1) + _dot(yds[...], w_ref[k_head:D_MIX, :])
    gate = mod_ref[0][:, 2 * D_MODEL:3 * D_MODEL]
    o_ref[0] = _residual_out(mixed, x_ref[0], gate, fg_ref, final)


def _prompt_c(x, mod_p, yacd, yb, w_out_b, final_g, sink, swa_bias, final, layer, row0, cache_args):
    nb, l, _ = x.shape
    tl = SEQ_TILE
    full = lambda *shape: pl.BlockSpec(shape, lambda n, t: (0,) * len(shape))
    c_in, c_out, c_shape = _cache_attention_specs(layer, row0, nb, l // tl, cache_args[6].shape[-1],
                                                  cache_args[8].shape[-1])
    return pl.pallas_call(
        functools.partial(_pc_kernel, final=final),
        grid=(nb, l // tl),
        in_specs=[
            pl.BlockSpec(memory_space=pltpu.SMEM),
            pl.BlockSpec((1, tl, D_MODEL), lambda n, t: (n, t, 0)),
            pl.BlockSpec((1, 1, 3 * D_MODEL), lambda n, t: (n, 0, 0)),
            pl.BlockSpec((1, tl, Y_WIDTH), lambda n, t: (n, t, 0)),
            pl.BlockSpec((1, N_PAIRS_DIL, N_CLASSES, tl // N_CLASSES, BLOCK),
                         lambda n, t: (n, 0, 0, t, 0)),
            pl.BlockSpec((None, D_MIX, D_MODEL), lambda n, t: (layer, 0, 0)),
            full(1, D_MODEL),
            full(2, 2, 2 * BLOCK, 2 * BLOCK),
        ] + c_in,
        out_specs=[pl.BlockSpec((1, tl, D_MODEL), lambda n, t: (n, t, 0))] + c_out,
        out_shape=[jax.ShapeDtypeStruct((nb, l, D_MODEL), F32)] + c_shape,
        scratch_shapes=[
            pltpu.VMEM((N_PAIRS_DIL, tl, BLOCK), F32),
            pltpu.VMEM((BLOCK + tl, W_SWA_KV), BF16),
            pltpu.VMEM((BLOCK + tl, W_SWA_KV), BF16),
            pltpu.VMEM((tl, W_SWA), BF16),
        ],
        compiler_params=_compiler_params(("arbitrary", "arbitrary")),
        name="prompt_c",
    )(sink, x, mod_p, yacd, yb, w_out_b, final_g, swa_bias, *cache_args)


def _sa_kernel(x_ref, mod_ref, g_ref, w_ref, wbd_ref, ps_ref, cw_ref, sp_ref, sc_ref,
               qkv_ref, sw_ref, yac_ref, gates_ref, pst_ref, cst_ref):
    hb = _modulated_norm(x_ref[...], g_ref[...], mod_ref[...]).astype(BF16)
    ns = hb.shape[0]

    zp = _dot(hb, w_ref[:, OFF_POOL:OFF_POOL + 2 * W_POOL])
    pu = zp[:, 0:W_POOL]
    pg = zp[:, W_POOL:2 * W_POOL]
    acc = pu
    sums = {}
    for j in range(1, 16):
        acc = acc + sp_ref[POOL_BUF - j]
        if j + 1 in POOL_WINDOWS:
            sums[j + 1] = acc
    lane = lax.broadcasted_iota(jnp.int32, (ns, W_POOL), 1)
    sel, win = _pool_select(sums, lane)
    diff = sel / win.astype(F32) - pu
    a_out = _dot(diff.astype(BF16), wbd_ref[...]) * ps_ref[...]
    yac_ref[:, 0:W_POOL] = a_out * _silu(pg)
    pst_ref[0:POOL_BUF - 1] = sp_ref[1:POOL_BUF]
    pst_ref[POOL_BUF - 1] = pu

    zc4 = _dot(hb, w_ref[:, OFF_CONV:OFF_CONV + 4 * W_CONV])
    ch = zc4[:, 0:W_CONV]
    cb = zc4[:, W_CONV:2 * W_CONV]
    cc = zc4[:, 2 * W_CONV:3 * W_CONV]
    cg = zc4[:, 3 * W_CONV:4 * W_CONV]
    zc = cc * ch
    conv = (cw_ref[0:1, :] * sc_ref[:, 0:W_CONV] + cw_ref[1:2, :] * sc_ref[:, W_CONV:2 * W_CONV]
            + cw_ref[2:3, :] * zc)
    yac_ref[:, W_POOL:W_POOL + W_CONV] = cb * conv * _silu(cg)
    cst_ref[:, 0:W_CONV] = sc_ref[:, W_CONV:2 * W_CONV]
    cst_ref[:, W_CONV:2 * W_CONV] = zc

    zd = _dot(hb, w_ref[:, OFF_DIL:OFF_DIL + 4 * W_DIL])
    qkv_ref[:, 0:W_DIL] = zd[:, 0:W_DIL] * QK_SCALE
    qkv_ref[:, W_DIL:3 * W_DIL] = zd[:, W_DIL:3 * W_DIL]
    gates_ref[:, 0:W_DIL] = _silu(zd[:, 3 * W_DIL:4 * W_DIL])

    zs = _dot(hb, w_ref[:, OFF_SWA:OFF_SWA + 3 * W_SWA])
    sw_ref[:, 0:W_SWA] = zs[:, 0:W_SWA] * QK_SCALE
    sw_ref[:, W_SWA:2 * W_SWA] = zs[:, W_SWA:2 * W_SWA]
    gates_ref[:, W_DIL:W_DIL + W_SWA] = _silu(zs[:, 2 * W_SWA:3 * W_SWA])


def _whole_spec(shape):
    return pl.BlockSpec(shape, lambda *_: (0,) * len(shape))


def _layer_spec(shape, layer):
    return pl.BlockSpec((None,) + tuple(shape[1:]), lambda *_: (layer,) + (0,) * (len(shape) - 1))


def _sample_a(xs, mod_s, norm_g, w_in_b, wbd, pool_scale, conv_w, sp, sc, layer):
    ns = xs.shape[0]
    shapes = [
        jax.ShapeDtypeStruct((ns, 3 * W_DIL), F32),
        jax.ShapeDtypeStruct((ns, 2 * W_SWA), F32),
        jax.ShapeDtypeStruct((ns, W_POOL + W_CONV), F32),
        jax.ShapeDtypeStruct((ns, W_DIL + W_SWA), F32),
        jax.ShapeDtypeStruct((POOL_BUF, ns, W_POOL), F32),
        jax.ShapeDtypeStruct((ns, CONV_BUF * W_CONV), F32),
    ]
    args = (xs, mod_s, norm_g, w_in_b, wbd, pool_scale, conv_w, sp, sc)
    in_specs = [_whole_spec(a.shape) for a in args]
    in_specs[3] = _layer_spec(w_in_b.shape, layer)
    return pl.pallas_call(
        _sa_kernel,
        grid=(1,),
        in_specs=in_specs,
        out_specs=[_whole_spec(s.shape) for s in shapes],
        out_shape=shapes,
        compiler_params=_compiler_params(("arbitrary",)),
        name="sample_a",
    )(*args)


def _cache_attention(dq_ref, sq_ref, dbias_ref, mult_ref, sbias_ref, sink_ref, kt_ref, vt_ref,
                     ck_ref, cv_ref, bo_ref, do_ref):
    row_d = lax.broadcasted_iota(jnp.int32, (8, DIL_MAX), 0)
    row_o = lax.broadcasted_iota(jnp.int32, (8, HEAD_DIM), 0)
    row_s = lax.broadcasted_iota(jnp.int32, (8, SWA_WINDOW), 0)
    for j in range(DEC_TILE):
        q = dq_ref[j, 0]
        qb = q.astype(BF16)
        s = jnp.zeros((8, DIL_MAX), F32)
        for h in range(N_HEADS_DIL):
            s = jnp.where(row_d == h, _dot(qb, kt_ref[0, j, h].astype(BF16)), s)
            yield
        s = s + dbias_ref[...]
        s_self = jnp.sum(q * dq_ref[j, 1], axis=-1, keepdims=True)
        m = jnp.maximum(jnp.max(s, axis=-1, keepdims=True), s_self)
        p = jnp.exp(s - m) * mult_ref[...]
        p_self = float(len(DIL_CONFIGS)) * jnp.exp(s_self - m)
        den = jnp.sum(p, axis=-1, keepdims=True) + p_self
        pb = p.astype(BF16)
        yield
        acc = jnp.zeros((8, HEAD_DIM), F32)
        for h in range(N_HEADS_DIL):
            acc = jnp.where(row_o == h, _dot_nt(pb, vt_ref[0, j, h].astype(BF16)), acc)
            yield
        bo_ref[j] = (acc + p_self * dq_ref[j, 2]) / den

        q = sq_ref[j, 0]
        qb = q.astype(BF16)
        s = jnp.where(row_s < 2, _dot(qb, ck_ref[0, j, 0].astype(BF16)),
                      _dot(qb, ck_ref[0, j, 1].astype(BF16))) + sbias_ref[...]
        w_self = jnp.sum(q * sq_ref[j, 1], axis=-1, keepdims=True)
        sink = sink_ref[...][:, 0:1]
        m = jnp.maximum(jnp.maximum(jnp.max(s, axis=-1, keepdims=True), w_self), sink)
        p = jnp.exp(s - m)
        pw = jnp.exp(w_self - m)
        den = jnp.sum(p, axis=-1, keepdims=True) + pw + jnp.exp(sink - m)
        pb = p.astype(BF16)
        yield
        acc = jnp.where(row_o < 2, _dot_nt(pb, cv_ref[0, j, 0].astype(BF16)),
                        _dot_nt(pb, cv_ref[0, j, 1].astype(BF16)))
        do_ref[j] = (acc + pw * sq_ref[j, 2]) / den
        yield


def _advance(stages, n):
    for _ in range(n):
        next(stages, None)


def _sc_kernel(x_ref, mod_ref, yac_ref, gates_ref, bo_ref, do_ref, w_ref, fg_ref, o_ref, *, final):
    yac = yac_ref[...]
    gates = gates_ref[...]
    y = jnp.concatenate([yac[:, 0:W_POOL], bo_ref[...] * gates[:, 0:W_DIL],
                         yac[:, W_POOL:W_POOL + W_CONV], do_ref[...] * gates[:, W_DIL:W_DIL + W_SWA]],
                        axis=-1).astype(BF16)
    gate = mod_ref[...][:, 2 * D_MODEL:3 * D_MODEL]
    o_ref[...] = _mix_out(y, x_ref[...], gate, w_ref, fg_ref, final)


def _sample_c(xs, mod_s, yac, gates, bo, do, w_out_b, final_g, final, layer):
    args = (xs, mod_s, yac, gates, bo, do, w_out_b, final_g)
    in_specs = [_whole_spec(a.shape) for a in args]
    in_specs[6] = _layer_spec(w_out_b.shape, layer)
    return pl.pallas_call(
        functools.partial(_sc_kernel, final=final),
        grid=(1,),
        in_specs=in_specs,
        out_specs=_whole_spec(xs.shape),
        out_shape=jax.ShapeDtypeStruct(xs.shape, F32),
        compiler_params=_compiler_params(("arbitrary",)),
        name="sample_c",
    )(*args)


def _band_bias_dil4(slope_lo, slope_hi, variant):
    sub = BLOCK // 4
    qidx = np.arange(BLOCK)[:, None]
    kidx = np.arange(2 * BLOCK)[None, :]
    q_step = 4 * (qidx % sub + (sub if variant == 0 else 0)) + qidx // sub
    k_step = 4 * (kidx % (2 * sub)) + kidx // (2 * sub)
    off = q_step - k_step
    valid = (off >= 0) & (off <= BLOCK)
    out = []
    for s in (slope_lo, slope_hi):
        out.append(np.where(valid, -(np.float32(s) * np.float32(4)) * off.astype(np.float32),
                            np.float32(NEG_INF)).astype(np.float32))
    return np.concatenate(out, axis=0)


def _prompt_bias_tables():
    dil = _alibi_slopes(N_HEADS_DIL)

    def table(hp, d, var):
        if d == 4:
            return _band_bias_dil4(dil[2 * hp], dil[2 * hp + 1], var)
        return _band_bias(dil[2 * hp], dil[2 * hp + 1], d, var)

    dil_bias = np.stack([
        np.stack([np.stack([table(hp, d, var) for var in (0, 1)])
                  for _, d in DIL_CONFIGS]) for hp in range(N_PAIRS_DIL)])
    swa = _alibi_slopes(N_HEADS_SWA)
    swa_bias = np.stack([
        np.stack([_band_bias(swa[SWA_HEAD_PERM[2 * g]], swa[SWA_HEAD_PERM[2 * g + 1]], 1, var)
                  for var in (0, 2)]) for g in range(2)])
    return jnp.asarray(dil_bias), jnp.asarray(swa_bias)


def _sample_bias_tables(r):
    dil = _alibi_slopes(N_HEADS_DIL)
    dist = (r - np.arange(r)).astype(np.float32)
    mult = np.zeros((1, r), np.float32)
    for window, d in DIL_CONFIGS:
        mult[0] += ((dist <= window) & (dist % d == 0)).astype(np.float32)
    dbias = np.zeros((8, r), np.float32)
    for h in range(N_HEADS_DIL):
        dbias[h] = np.where(mult[0] > 0, -np.float32(dil[h]) * dist, np.float32(NEG_INF))
    swa = _alibi_slopes(N_HEADS_SWA)
    sdist = (SWA_WINDOW - np.arange(SWA_WINDOW)).astype(np.float32)
    sbias = np.zeros((8, SWA_WINDOW), np.float32)
    for h in range(N_HEADS_SWA):
        sbias[h] = -np.float32(swa[h]) * sdist
    return jnp.asarray(dbias), jnp.asarray(mult), jnp.asarray(sbias)


def kernel(x_prompt, x_sample, c_prompt, c_sample, state_pool, cache_dil_k, cache_dil_v, state_conv, cache_swa_k, cache_swa_v, norm_g, w_ada, b_ada, w_in, w_pool, pool_scale, conv_w, swa_sink, w_out, final_g):
    depth = w_in.shape[0]
    nb, l, _ = x_prompt.shape
    ns = x_sample.shape[0]
    assert x_sample.shape[1] == 1 and l % CHUNK == 0 and l >= DIL_MAX
    assert ns == 2 * nb * (l // SEQ_TILE) * DEC_TILE
    assert cache_dil_k.shape[2] == DIL_MAX and cache_swa_k.shape[2] == SWA_WINDOW

    w_in_b = w_in.astype(BF16)
    w_out_b = w_out.astype(BF16)
    eye = jnp.eye(len(POOL_WINDOWS), dtype=F32)
    wbd = jnp.einsum("dgce,gh->dgche", w_pool, eye).reshape(depth, W_POOL, W_POOL).astype(BF16)
    sink_rows = jnp.zeros((depth, 8), F32).at[:, 0:N_HEADS_SWA].set(swa_sink)
    sink_rows = jnp.broadcast_to(sink_rows[:, :, None], (depth, 8, BLOCK))

    dil_bias, swa_bias = _prompt_bias_tables()
    dbias, mult, sbias = _sample_bias_tables(cache_dil_k.shape[2])

    mod = _ada(jnp.concatenate([c_prompt, c_sample], axis=0), w_ada, b_ada)
    fg = final_g.reshape(1, D_MODEL)

    xp = x_prompt
    xs = x_sample.reshape(ns, D_MODEL)
    kt = jnp.transpose(cache_dil_k, (0, 1, 3, 4, 2))
    vt = jnp.transpose(cache_dil_v, (0, 1, 3, 4, 2))
    ckt = jnp.transpose(cache_swa_k, (0, 1, 3, 4, 2))
    cvt = jnp.transpose(cache_swa_v, (0, 1, 3, 4, 2))
    sp_all = jnp.transpose(state_pool, (0, 2, 1, 3))
    sc_all = state_conv.reshape(depth, ns, CONV_BUF * W_CONV)
    pad_heads = lambda a: jnp.pad(a, ((0, 0), (0, 0), (0, 8 - a.shape[2]), (0, 0)))
    outs = [[] for _ in range(12)]
    carried = None
    for i in range(depth):
        final = i == depth - 1
        mod_p = mod[i, 0:nb].reshape(nb, 1, 3 * D_MODEL)
        mod_s = mod[i, nb:nb + ns]
        g = norm_g[i].reshape(1, D_MODEL)
        ps = pool_scale[i].reshape(1, W_POOL)

        qkv, sw, yac, gates, pst_s, cst_s = _sample_a(
            xs, mod_s, g, w_in_b, wbd[i], ps, conv_w[i], sp_all[i], sc_all[i], i)
        dq3 = pad_heads(qkv.reshape(ns, 3, N_HEADS_DIL, HEAD_DIM))
        sq4 = sw[:, 0:W_SWA].reshape(ns, 1, N_HEADS_SWA, HEAD_DIM)
        skv = jnp.repeat(sw[:, W_SWA:W_SWA + 2 * W_SWA_KV].reshape(ns, 2, 2, HEAD_DIM), 2, axis=2)
        sq3 = pad_heads(jnp.concatenate([sq4, skv], axis=1))

        cache_args = (dq3, sq3, dbias, mult, sbias, sink_rows[i], kt, vt, ckt, cvt)
        q1, k1, v1, qm, km, vm, gm, kc, vc, yacd, pst, cst, skc, svc, bo_a, do_a = _prompt_a(
            xp, mod_p, g, w_in_b, wbd[i], ps, conv_w[i], i, depth, 0, cache_args, carried)
        carried = (kc, vc)
        yb = _prompt_b(q1, k1, v1, qm, km, vm, gm, dil_bias)
        xp, bo_c, do_c = _prompt_c(xp, mod_p, yacd, yb, w_out_b, fg, swa_sink[i], swa_bias, final,
                                   i, ns // 2, cache_args)

        bo = jnp.concatenate([bo_a, bo_c], axis=0)[:, 0:N_HEADS_DIL].reshape(ns, W_DIL)
        do = jnp.concatenate([do_a, do_c], axis=0)[:, 0:N_HEADS_SWA].reshape(ns, W_SWA)
        xs = _sample_c(xs, mod_s, yac, gates, bo, do, w_out_b, fg, final, i)

        unfold = lambda a, h: jnp.transpose(a.reshape(nb, h, HEAD_DIM, a.shape[-1]), (0, 3, 1, 2))
        skc, svc = unfold(skc, 2), unfold(svc, 2)
        outs[0].append(pst)
        outs[1].append(jnp.transpose(pst_s, (1, 0, 2)))
        outs[4].append(qkv[:, W_DIL:2 * W_DIL].reshape(ns, 1, N_HEADS_DIL, HEAD_DIM))
        outs[5].append(qkv[:, 2 * W_DIL:3 * W_DIL].reshape(ns, 1, N_HEADS_DIL, HEAD_DIM))
        outs[6].append(cst)
        outs[7].append(cst_s.reshape(ns, CONV_BUF, W_CONV))
        outs[8].append(skc)
        outs[9].append(svc)
        outs[10].append(sw[:, W_SWA:W_SWA + W_SWA_KV].reshape(ns, 1, 2, HEAD_DIM))
        outs[11].append(sw[:, W_SWA + W_SWA_KV:W_SWA + 2 * W_SWA_KV].reshape(ns, 1, 2, HEAD_DIM))

    for k, cache in zip((2, 3), carried):
        outs[k] = jnp.transpose(cache.reshape(depth, nb, N_HEADS_DIL, HEAD_DIM, DIL_MAX), (0, 1, 4, 2, 3))
    return (xp, xs.reshape(ns, 1, D_MODEL)) + tuple(
        o if not isinstance(o, list) else jnp.stack(o) for o in outs)
```

```python
import functools
import math

import numpy as np
import jax
import jax.numpy as jnp
from jax import lax
from jax.experimental import pallas as pl
from jax.experimental.pallas import tpu as pltpu

F32 = jnp.float32
BF16 = jnp.bfloat16

D_MODEL = 1024
HEAD_DIM = 64
BLOCK = 128
POOL_WINDOWS = (2, 4, 8, 16)
POOL_GROUP = 64
W_POOL = 256
POOL_BUF = 15
DIL_CONFIGS = ((128, 1), (512, 4), (2048, 16))
DIL_MAX = 2048
N_HEADS_DIL = 6
N_PAIRS_DIL = 3
N_CLASSES = 16
W_DIL = 384
W_CONV = 256
CONV_BUF = 2
N_HEADS_SWA = 4
W_SWA = 256
W_SWA_KV = 128
SWA_WINDOW = 128
D_MIX = 1152
D_PROJ = 3840
RMS_EPS = 1e-6
QK_SCALE = 1.0 / math.sqrt(HEAD_DIM)

OFF_POOL = 0
OFF_DIL = 512
OFF_CONV = 2048
OFF_SWA = 3072
Y_WIDTH = W_POOL + W_CONV + 3 * W_SWA
SWA_HEAD_PERM = (0, 3, 1, 2)

SEQ_TILE = 512
CHUNK = 2048
BAND_SKEW = 1
DEC_TILE = 1
RIDE_STAGES = 15 * DEC_TILE
VMEM_LIMIT = 56 * 1024 * 1024

NEG_INF = float("-inf")


def _silu(v):
    return v * jax.nn.sigmoid(v)


def _dot(a, b):
    return jnp.dot(a, b, preferred_element_type=F32)


def _dot_nt(a, b):
    return lax.dot_general(a, b, (((1,), (1,)), ((), ())), preferred_element_type=F32)


def _alibi_slopes(n):
    return [2.0 ** (-8.0 * (h + 1) / n) for h in range(n)]


def _band_bias(slope_lo, slope_hi, dist_scale, variant):
    qi = np.arange(BLOCK)[:, None]
    kj = np.arange(2 * BLOCK)[None, :]
    if variant == 1:
        off = qi - kj
    else:
        off = qi - kj + BLOCK
    valid = (off >= 0) & (off <= BLOCK)
    if variant == 2:
        valid = valid & (kj >= BLOCK)
    out = []
    for s in (slope_lo, slope_hi):
        b = np.where(valid, -(np.float32(s) * np.float32(dist_scale)) * off.astype(np.float32),
                     np.float32(NEG_INF))
        out.append(b.astype(np.float32))
    return np.concatenate(out, axis=0)


def _compiler_params(sem):
    return pltpu.CompilerParams(dimension_semantics=sem, vmem_limit_bytes=VMEM_LIMIT)


def _ada_kernel(c_ref, w_ref, b_ref, o_ref):
    s = _silu(c_ref[...]).astype(BF16)
    o_ref[0] = _dot(s, w_ref[0].astype(BF16)) + b_ref[0]


def _ada(c_all, w_ada_b, b_ada):
    depth = w_ada_b.shape[0]
    rows = c_all.shape[0]
    return pl.pallas_call(
        _ada_kernel,
        grid=(depth, 3),
        in_specs=[
            pl.BlockSpec((rows, D_MODEL), lambda i, j: (0, 0)),
            pl.BlockSpec((1, D_MODEL, D_MODEL), lambda i, j: (i, 0, j)),
            pl.BlockSpec((1, 1, D_MODEL), lambda i, j: (i, 0, j)),
        ],
        out_specs=pl.BlockSpec((1, rows, D_MODEL), lambda i, j: (i, 0, j)),
        out_shape=jax.ShapeDtypeStruct((depth, rows, 3 * D_MODEL), F32),
        compiler_params=_compiler_params(("arbitrary", "arbitrary")),
        name="ada",
    )(c_all, w_ada_b, b_ada.reshape(depth, 1, 3 * D_MODEL))


def _modulated_norm(x, g, mod_row):
    ms = jnp.mean(x * x, axis=-1, keepdims=True)
    shift = mod_row[:, 0:D_MODEL]
    scale = mod_row[:, D_MODEL:2 * D_MODEL]
    return x * lax.rsqrt(ms + RMS_EPS) * (g * (1.0 + scale)) + shift


def _pair_heads(v):
    left, right = v[:, 0:BLOCK], v[:, BLOCK:2 * BLOCK]
    low = lax.broadcasted_iota(jnp.int32, left.shape, 1) < HEAD_DIM
    return jnp.concatenate([jnp.where(low, left, right),
                            pltpu.roll(jnp.where(low, right, left), HEAD_DIM, axis=1)], axis=-1)


def _unpair_heads(a, b):
    low = lax.broadcasted_iota(jnp.int32, a.shape, 1) < HEAD_DIM
    b_swapped = pltpu.roll(b, HEAD_DIM, axis=1)
    return jnp.where(low, a, b_swapped), jnp.where(low, b_swapped, a)


def _pool_select(sums, lane):
    grp = lane // POOL_GROUP
    sel = jnp.where(grp == 0, sums[2], jnp.where(grp == 1, sums[4],
                                                 jnp.where(grp == 2, sums[8], sums[16])))
    win = jnp.where(grp == 0, 2, jnp.where(grp == 1, 4, jnp.where(grp == 2, 8, 16)))
    return sel, win


def _pa_kernel(x_ref, mod_ref, g_ref, w_ref, wbd_ref, ps_ref, cw_ref,
               dq_ref, sq_ref, dbias_ref, mult_ref, sbias_ref, ssink_ref, kt_ref, vt_ref, ck_ref, cv_ref,
               dqt_ref,
               q1_ref, k1_ref, v1_ref, qm_ref, km_ref, vm_ref, gm_ref, kc_ref, vc_ref, y_ref,
               pst_ref, cst_ref, skc_ref, svc_ref, bo_ref, do_ref,
               uext, zcext, zds, z4s, *, later_layers=0):
    tl = SEQ_TILE
    t = pl.program_id(1)

    @pl.when(t == 0)
    def _():
        for k in range(len(POOL_WINDOWS)):
            uext[k, 0:16, :] = jnp.zeros((16, W_POOL), F32)
        zcext[0:8, :] = jnp.zeros((8, W_CONV), F32)

    ride = _cache_attention(dq_ref, sq_ref, dbias_ref, mult_ref, sbias_ref, ssink_ref, kt_ref, vt_ref,
                            ck_ref, cv_ref, dqt_ref, bo_ref, do_ref)

    _advance(ride, 3)
    hb = _modulated_norm(x_ref[0], g_ref[...], mod_ref[0]).astype(BF16)

    zp = _dot(hb, w_ref[:, OFF_POOL:OFF_POOL + 2 * W_POOL])
    zd = _dot(hb, w_ref[:, OFF_DIL:OFF_DIL + 4 * W_DIL])
    _advance(ride, 3)

    pu = zp[:, 0:W_POOL]
    pg = zp[:, W_POOL:2 * W_POOL]
    uext[0, 16:16 + tl, :] = pu
    sums = {}
    level = pu
    for k, w in enumerate(POOL_WINDOWS):
        shift = w // 2
        level = level + uext[k, 16 - shift:16 - shift + tl, :]
        sums[w] = level
        if k + 1 < len(POOL_WINDOWS):
            uext[k + 1, 16:16 + tl, :] = level
    lane = lax.broadcasted_iota(jnp.int32, (tl, W_POOL), 1)
    gpos = lax.broadcasted_iota(jnp.int32, (tl, W_POOL), 0) + t * tl
    sel, win = _pool_select(sums, lane)
    cnt = jnp.minimum(gpos + 1, win).astype(F32)
    diff = sel / cnt - pu
    a_out = _dot(diff.astype(BF16), wbd_ref[...]) * ps_ref[...]
    y_ref[0, :, 0:W_POOL] = (a_out * _silu(pg)).astype(BF16)
    pst_ref[0] = uext[0, tl + 1:tl + 16, :]
    for k in range(len(POOL_WINDOWS)):
        uext[k, 0:16, :] = uext[k, tl:tl + 16, :]

    zs = _dot(hb, w_ref[:, OFF_SWA:OFF_SWA + 3 * W_SWA])
    _advance(ride, 3)
    kc_ref[0, 0] = zd[:, W_DIL:2 * W_DIL].T
    vc_ref[0, 0] = zd[:, 2 * W_DIL:3 * W_DIL].T
    for d in range(1, 1 + later_layers):
        kc_ref[d, 0] = jnp.zeros((W_DIL, tl), F32)
        vc_ref[d, 0] = jnp.zeros((W_DIL, tl), F32)
    for hp in range(N_PAIRS_DIL):
        _advance(ride, 1)
        lo = hp * BLOCK
        zds[0] = zd[:, lo:lo + BLOCK] * QK_SCALE
        zds[1] = zd[:, W_DIL + lo:W_DIL + lo + BLOCK]
        zds[2] = zd[:, 2 * W_DIL + lo:2 * W_DIL + lo + BLOCK]
        zds[3] = _silu(zd[:, 3 * W_DIL + lo:3 * W_DIL + lo + BLOCK])
        q1_ref[0, hp] = zds[0].astype(BF16)
        k1_ref[0, hp] = zds[1].astype(BF16)
        v1_ref[0, hp] = zds[2].astype(BF16)
        quarter = tl // 4
        for a in range(4):
            for blk in range(4):
                z4s[blk, a] = zds[blk, pl.ds(a, quarter, stride=4), :]
        for b in range(4):
            for a in range(4):
                c = 4 * b + a
                rows = pl.ds(b, tl // N_CLASSES, stride=4)
                qm_ref[0, hp, c] = z4s[0, a, rows, :].astype(BF16)
                km_ref[0, hp, c] = z4s[1, a, rows, :].astype(BF16)
                vm_ref[0, hp, c] = z4s[2, a, rows, :].astype(BF16)
                gm_ref[0, hp, c] = z4s[3, a, rows, :].astype(BF16)

    zc4 = _dot(hb, w_ref[:, OFF_CONV:OFF_CONV + 4 * W_CONV])
    _advance(ride, 3)

    sk = zs[:, W_SWA:W_SWA + W_SWA_KV]
    sv = zs[:, W_SWA + W_SWA_KV:W_SWA + 2 * W_SWA_KV]
    skc_ref[0] = sk[tl - SWA_WINDOW:tl, :].T
    svc_ref[0] = sv[tl - SWA_WINDOW:tl, :].T
    c0 = W_POOL + W_CONV
    y_ref[0, :, c0:c0 + W_SWA] = _pair_heads(zs[:, 0:W_SWA] * QK_SCALE).astype(BF16)
    y_ref[0, :, c0 + W_SWA:c0 + 2 * W_SWA] = zs[:, W_SWA:2 * W_SWA].astype(BF16)
    y_ref[0, :, c0 + 2 * W_SWA:c0 + 3 * W_SWA] = _pair_heads(_silu(zs[:, 2 * W_SWA:3 * W_SWA])).astype(BF16)
    _advance(ride, RIDE_STAGES)

    ch = zc4[:, 0:W_CONV]
    cb = zc4[:, W_CONV:2 * W_CONV]
    cc = zc4[:, 2 * W_CONV:3 * W_CONV]
    cg = zc4[:, 3 * W_CONV:4 * W_CONV]
    zc = cc * ch
    zcext[8:8 + tl, :] = zc
    conv = (cw_ref[0:1, :] * zcext[6:6 + tl, :] + cw_ref[1:2, :] * zcext[7:7 + tl, :]
            + cw_ref[2:3, :] * zc)
    y_ref[0, :, W_POOL:W_POOL + W_CONV] = (cb * conv * _silu(cg)).astype(BF16)
    cst_ref[0] = zcext[tl + 6:tl + 8, :]
    zcext[0:8, :] = zcext[tl:tl + 8, :]


def _cache_attention_specs(layer, row0, nb, nt, r, swr):
    bt = DEC_TILE
    blk0 = row0 // bt
    full = lambda *shape: pl.BlockSpec(shape, lambda n, t: (0,) * len(shape))
    srow_spec = pl.BlockSpec((bt, 3, 8, HEAD_DIM), lambda n, t: (blk0 + n * nt + t, 0, 0, 0))
    dil_spec = pl.BlockSpec((1, bt, N_HEADS_DIL, HEAD_DIM, r),
                            lambda n, t: (layer, blk0 + n * nt + t, 0, 0, 0))
    swa_spec = pl.BlockSpec((1, bt, 2, HEAD_DIM, swr), lambda n, t: (layer, blk0 + n * nt + t, 0, 0, 0))
    srow_t_spec = pl.BlockSpec((bt, 2, HEAD_DIM, 8), lambda n, t: (blk0 + n * nt + t, 0, 0, 0))
    sout_spec = pl.BlockSpec((bt, 8, HEAD_DIM), lambda n, t: (n * nt + t, 0, 0))
    sout_shape = jax.ShapeDtypeStruct((nb * nt * bt, 8, HEAD_DIM), F32)
    sout_t_spec = pl.BlockSpec((bt, HEAD_DIM, 8), lambda n, t: (n * nt + t, 0, 0))
    sout_t_shape = jax.ShapeDtypeStruct((nb * nt * bt, HEAD_DIM, 8), F32)
    in_specs = [srow_spec, srow_spec, full(8, r), full(1, r), full(8, swr), full(8, BLOCK),
                dil_spec, dil_spec, swa_spec, swa_spec, srow_t_spec]
    return in_specs, [sout_t_spec, sout_spec], [sout_t_shape, sout_shape]


N_PA_INPUTS = 18


def _pa_kernel_carry(*refs):
    _pa_kernel(*refs[:N_PA_INPUTS], *refs[N_PA_INPUTS + 2:])


def _prompt_a(x, mod_p, norm_g, w_in_b, wbd, pool_scale, conv_w, layer, depth, row0, cache_args, carried):
    nb, l, _ = x.shape
    tl = SEQ_TILE
    nt = l // tl
    cache_t0 = (l - DIL_MAX) // tl
    c_in, c_out, c_shape = _cache_attention_specs(layer, row0, nb, nt, cache_args[6].shape[-1],
                                                  cache_args[8].shape[-1])
    pair_spec = pl.BlockSpec((1, N_PAIRS_DIL, tl, BLOCK), lambda n, t: (n, 0, t, 0))
    pair_shape = jax.ShapeDtypeStruct((nb, N_PAIRS_DIL, l, BLOCK), BF16)
    cm_spec = pl.BlockSpec((1, N_PAIRS_DIL, N_CLASSES, tl // N_CLASSES, BLOCK),
                           lambda n, t: (n, 0, 0, t, 0))
    cm_shape = jax.ShapeDtypeStruct((nb, N_PAIRS_DIL, N_CLASSES, l // N_CLASSES, BLOCK), BF16)
    cache_layers = depth if carried is None else 1
    cache_spec = pl.BlockSpec((cache_layers, 1, W_DIL, tl),
                              lambda n, t: (layer, n, 0, jnp.maximum(t - cache_t0, 0)))
    cache_shape = jax.ShapeDtypeStruct((depth, nb, W_DIL, DIL_MAX), F32)
    full = lambda *shape: pl.BlockSpec(shape, lambda n, t: (0,) * len(shape))
    in_specs = [
        pl.BlockSpec((1, tl, D_MODEL), lambda n, t: (n, t, 0)),
        pl.BlockSpec((1, 1, 3 * D_MODEL), lambda n, t: (n, 0, 0)),
        full(1, D_MODEL),
        pl.BlockSpec((None, D_MODEL, D_PROJ), lambda n, t: (layer, 0, 0), pipeline_mode=pl.Buffered(1)),
        full(W_POOL, W_POOL),
        full(1, W_POOL),
        full(3, W_CONV),
    ] + c_in
    assert len(in_specs) == N_PA_INPUTS
    extra, aliases = (), {}
    if carried is not None:
        in_specs = in_specs + [pl.BlockSpec(memory_space=pl.ANY)] * 2
        extra = tuple(carried)
        aliases = {N_PA_INPUTS: 7, N_PA_INPUTS + 1: 8}
    return pl.pallas_call(
        functools.partial(_pa_kernel, later_layers=depth - 1) if carried is None else _pa_kernel_carry,
        grid=(nb, nt),
        in_specs=in_specs,
        input_output_aliases=aliases,
        out_specs=[
            pair_spec, pair_spec, pair_spec, cm_spec, cm_spec, cm_spec, cm_spec,
            cache_spec, cache_spec,
            pl.BlockSpec((1, tl, Y_WIDTH), lambda n, t: (n, t, 0)),
            pl.BlockSpec((1, POOL_BUF, W_POOL), lambda n, t: (n, 0, 0)),
            pl.BlockSpec((1, CONV_BUF, W_CONV), lambda n, t: (n, 0, 0)),
            pl.BlockSpec((1, SWA_WINDOW, W_SWA_KV), lambda n, t: (n, 0, 0)),
            pl.BlockSpec((1, SWA_WINDOW, W_SWA_KV), lambda n, t: (n, 0, 0)),
        ] + c_out,
        out_shape=[
            pair_shape, pair_shape, pair_shape, cm_shape, cm_shape, cm_shape, cm_shape,
            cache_shape, cache_shape,
            jax.ShapeDtypeStruct((nb, l, Y_WIDTH), BF16),
            jax.ShapeDtypeStruct((nb, POOL_BUF, W_POOL), F32),
            jax.ShapeDtypeStruct((nb, CONV_BUF, W_CONV), F32),
            jax.ShapeDtypeStruct((nb, SWA_WINDOW, W_SWA_KV), F32),
            jax.ShapeDtypeStruct((nb, SWA_WINDOW, W_SWA_KV), F32),
        ] + c_shape,
        scratch_shapes=[
            pltpu.VMEM((len(POOL_WINDOWS), 16 + tl, W_POOL), F32),
            pltpu.VMEM((8 + tl, W_CONV), F32),
            pltpu.VMEM((4, tl, BLOCK), F32),
            pltpu.VMEM((4, 4, tl // 4, BLOCK), F32),
        ],
        compiler_params=_compiler_params(("arbitrary", "arbitrary")),
        name="prompt_a",
    )(x, mod_p, norm_g, w_in_b, wbd, pool_scale, conv_w, *cache_args, *extra)


def _band_scores(q, kblk, bias, lane_q):
    lhs = jnp.concatenate([jnp.where(lane_q < HEAD_DIM, q, jnp.zeros_like(q)),
                           jnp.where(lane_q >= HEAD_DIM, q, jnp.zeros_like(q))], axis=0)
    return _dot_nt(lhs, kblk) + bias


def _band_values(s, vblk, lane_q):
    m = jnp.max(s, axis=-1, keepdims=True)
    p = jnp.exp((s - m).astype(BF16))
    acc = _dot(p, jnp.concatenate([vblk, jnp.ones_like(vblk)], axis=-1))
    lo = lane_q < HEAD_DIM
    return (jnp.where(lo, acc[0:BLOCK, 0:BLOCK], acc[BLOCK:2 * BLOCK, 0:BLOCK]),
            jnp.where(lo, m[0:BLOCK], m[BLOCK:2 * BLOCK]),
            jnp.where(lo, acc[0:BLOCK, BLOCK:2 * BLOCK], acc[BLOCK:2 * BLOCK, BLOCK:2 * BLOCK]))


def _run_skewed(tasks, skew):
    pending = []
    for scores_fn, finish_fn in tasks:
        pending.append((finish_fn, scores_fn()))
        if len(pending) > skew:
            fn, s = pending.pop(0)
            fn(s)
    for fn, s in pending:
        fn(s)


def _pb_kernel(bias_ref, q1, k1, v1, qm, km, vm, gm, outm, s1, s4, s16):
    c = pl.program_id(2)
    first = jnp.where(c == 0, 1, 0)
    lane_q = lax.broadcasted_iota(jnp.int32, (BLOCK, BLOCK), 1)
    sub = BLOCK // 4

    tasks = []

    def dil1_task(j):
        r0 = j * BLOCK
        if j == 0:
            start = jnp.maximum(c * (CHUNK // BLOCK) - 1, 0) * BLOCK
            var = first
        else:
            start = (c * (CHUNK // BLOCK) + (j - 1)) * BLOCK
            var = 0
        start = pl.multiple_of(start, BLOCK)

        def scores():
            return _band_scores(q1[0, 0, r0:r0 + BLOCK, :], k1[0, 0, pl.ds(start, 2 * BLOCK), :],
                                bias_ref[0, 0, var], lane_q)

        def finish(s):
            parts = _band_values(s, v1[0, 0, pl.ds(start, 2 * BLOCK), :], lane_q)
            for k, part in enumerate(parts):
                s1[k, r0:r0 + BLOCK, :] = part

        return scores, finish

    def dil4_task(c4, j):
        i0 = j * sub
        if j == 0:
            istart = jnp.maximum(c * (CHUNK // N_CLASSES) - sub, 0)
            var = first
        else:
            istart = c * (CHUNK // N_CLASSES) + i0 - sub
            var = 0
        istart = pl.multiple_of(istart, sub)
        classes = [4 * cc + c4 for cc in range(4)]

        def scores():
            q = jnp.concatenate([qm[0, 0, cl, i0:i0 + sub, :] for cl in classes], axis=0)
            kblk = jnp.concatenate([km[0, 0, cl, pl.ds(istart, 2 * sub), :] for cl in classes], axis=0)
            return _band_scores(q, kblk, bias_ref[0, 1, var], lane_q)

        def finish(s):
            vblk = jnp.concatenate([vm[0, 0, cl, pl.ds(istart, 2 * sub), :] for cl in classes], axis=0)
            parts = _band_values(s, vblk, lane_q)
            for k, part in enumerate(parts):
                for cc, cl in enumerate(classes):
                    s4[k, cl, i0:i0 + sub, :] = part[cc * sub:(cc + 1) * sub]

        return scores, finish

    start16 = pl.multiple_of(jnp.maximum(c - 1, 0) * BLOCK, BLOCK)

    def dil16_task(cl):
        def scores():
            return _band_scores(qm[0, 0, cl], km[0, 0, cl, pl.ds(start16, 2 * BLOCK), :],
                                bias_ref[0, 2, first], lane_q)

        def finish(s):
            parts = _band_values(s, vm[0, 0, cl, pl.ds(start16, 2 * BLOCK), :], lane_q)
            for k, part in enumerate(parts):
                s16[k, cl] = part

        return scores, finish

    tasks += [dil1_task(j) for j in range(CHUNK // BLOCK)]
    tasks += [dil4_task(c4, j) for c4 in range(4) for j in range(4)]
    tasks += [dil16_task(cl) for cl in range(N_CLASSES)]
    _run_skewed(tasks, BAND_SKEW)

    for cl in range(N_CLASSES):
        rows = pl.ds(cl, BLOCK, stride=N_CLASSES)
        ma, mb, mc = s1[1, rows, :], s4[1, cl], s16[1, cl]
        mx = jnp.maximum(jnp.maximum(ma, mb), mc)
        wa = jnp.exp(ma - mx)
        wb = jnp.exp(mb - mx)
        wc = jnp.exp(mc - mx)
        num = wa * s1[0, rows, :] + wb * s4[0, cl] + wc * s16[0, cl]
        den = wa * s1[2, rows, :] + wb * s4[2, cl] + wc * s16[2, cl]
        outm[0, 0, cl] = (num / den * gm[0, 0, cl].astype(F32)).astype(BF16)


def _prompt_b(q1, k1, v1, qm, km, vm, gm, dil_bias):
    nb, npair, l, _ = q1.shape
    nc = l // CHUNK
    li = l // N_CLASSES
    ci = CHUNK // N_CLASSES
    q1_spec = pl.BlockSpec((1, 1, CHUNK, BLOCK), lambda n, h, c: (n, h, c, 0))
    kv1_spec = pl.BlockSpec((1, 1, l, BLOCK), lambda n, h, c: (n, h, 0, 0))
    cm_spec = pl.BlockSpec((1, 1, N_CLASSES, ci, BLOCK), lambda n, h, c: (n, h, 0, c, 0))
    kvm_spec = pl.BlockSpec((1, 1, N_CLASSES, li, BLOCK), lambda n, h, c: (n, h, 0, 0, 0))
    cm_scratch = pltpu.VMEM((3, N_CLASSES, ci, BLOCK), F32)
    return pl.pallas_call(
        _pb_kernel,
        grid=(nb, npair, nc),
        in_specs=[pl.BlockSpec((1, 3, 2, 2 * BLOCK, 2 * BLOCK), lambda n, h, c: (h, 0, 0, 0, 0)),
                  q1_spec, kv1_spec, kv1_spec, cm_spec, kvm_spec, kvm_spec, cm_spec],
        out_specs=cm_spec,
        out_shape=jax.ShapeDtypeStruct((nb, npair, N_CLASSES, li, BLOCK), BF16),
        scratch_shapes=[
            pltpu.VMEM((3, CHUNK, BLOCK), F32), cm_scratch, cm_scratch,
        ],
        compiler_params=_compiler_params(("arbitrary", "arbitrary", "arbitrary")),
        name="prompt_b",
    )(dil_bias, q1, k1, v1, qm, km, vm, gm)


def _mix_out(y, x, gate, w_ref, fg_ref, final):
    return _residual_out(_dot(y, w_ref[...]), x, gate, fg_ref, final)


def _residual_out(mixed, x, gate, fg_ref, final):
    xn = x + gate * mixed
    if final:
        ms = jnp.mean(xn * xn, axis=-1, keepdims=True)
        xn = xn * lax.rsqrt(ms + RMS_EPS) * fg_ref[...]
    return xn


def _pc_kernel(sink_ref, x_ref, mod_ref, yacd_ref, yb_ref, w_ref, fg_ref, bias_ref,
               dq_ref, sq_ref, dbias_ref, mult_ref, sbias_ref, ssink_ref, kt_ref, vt_ref, ck_ref, cv_ref,
               dqt_ref,
               o_ref, bo_ref, do_ref, ybs, kext, vext, yds, *, final):
    tl = SEQ_TILE
    t = pl.program_id(1)

    @pl.when(t == 0)
    def _():
        kext[0:BLOCK, :] = jnp.zeros((BLOCK, W_SWA_KV), BF16)
        vext[0:BLOCK, :] = jnp.zeros((BLOCK, W_SWA_KV), BF16)

    for hp in range(N_PAIRS_DIL):
        for c in range(N_CLASSES):
            ybs[hp, pl.ds(c, tl // N_CLASSES, stride=N_CLASSES), :] = yb_ref[0, hp, c].astype(F32)
    k_head = D_MIX - W_SWA
    y_head = jnp.concatenate([yacd_ref[0, :, 0:W_POOL], ybs[0].astype(BF16), ybs[1].astype(BF16),
                              ybs[2].astype(BF16), yacd_ref[0, :, W_POOL:W_POOL + W_CONV]], axis=-1)
    n_tile = D_MODEL // (tl // BLOCK)
    proj = []
    ride = _cache_attention(dq_ref, sq_ref, dbias_ref, mult_ref, sbias_ref, ssink_ref, kt_ref, vt_ref,
                            ck_ref, cv_ref, dqt_ref, bo_ref, do_ref)

    c0 = W_POOL + W_CONV
    kext[BLOCK:BLOCK + tl, :] = yacd_ref[0, :, c0 + W_SWA:c0 + W_SWA + W_SWA_KV]
    vext[BLOCK:BLOCK + tl, :] = yacd_ref[0, :, c0 + W_SWA + W_SWA_KV:c0 + 2 * W_SWA]
    first = jnp.where(t == 0, 1, 0)
    lane_q = lax.broadcasted_iota(jnp.int32, (BLOCK, BLOCK), 1)
    row_s = lax.broadcasted_iota(jnp.int32, (2 * BLOCK, 1), 0)
    for jb in range(tl // BLOCK):
        proj.append(_dot(y_head, w_ref[0:k_head, jb * n_tile:(jb + 1) * n_tile]))
        r0 = jb * BLOCK
        kblk = kext[r0:r0 + 2 * BLOCK, :]
        vblk = vext[r0:r0 + 2 * BLOCK, :]
        gated = []
        for grp in range(2):
            q = yacd_ref[0, r0:r0 + BLOCK, c0 + grp * BLOCK:c0 + (grp + 1) * BLOCK]
            lhs = jnp.concatenate([jnp.where(lane_q < HEAD_DIM, q, jnp.zeros_like(q)),
                                   jnp.where(lane_q >= HEAD_DIM, q, jnp.zeros_like(q))], axis=0)
            s = _dot_nt(lhs, kblk)
            if jb == 0:
                s = s + bias_ref[grp, first]
            else:
                s = s + bias_ref[grp, 0]
            sink = jnp.where(row_s < BLOCK, sink_ref[SWA_HEAD_PERM[2 * grp]],
                             sink_ref[SWA_HEAD_PERM[2 * grp + 1]])
            m = jnp.maximum(jnp.max(s, axis=-1, keepdims=True), sink)
            p = jnp.exp((s - m).astype(BF16))
            acc = _dot(p, jnp.concatenate([vblk, jnp.ones_like(vblk)], axis=-1))
            es = jnp.exp(sink - m)
            lo = lane_q < HEAD_DIM
            den = (jnp.where(lo, acc[0:BLOCK, BLOCK:2 * BLOCK], acc[BLOCK:2 * BLOCK, BLOCK:2 * BLOCK])
                   + jnp.where(lo, es[0:BLOCK], es[BLOCK:2 * BLOCK]))
            od = jnp.where(lo, acc[0:BLOCK, 0:BLOCK], acc[BLOCK:2 * BLOCK, 0:BLOCK]) / den
            g0 = c0 + 2 * W_SWA + grp * BLOCK
            gate_s = yacd_ref[0, r0:r0 + BLOCK, g0:g0 + BLOCK].astype(F32)
            gated.append(od * gate_s)
            _advance(ride, 2)
        h01, h23 = _unpair_heads(*gated)
        yds[r0:r0 + BLOCK, 0:BLOCK] = h01.astype(BF16)
        yds[r0:r0 + BLOCK, BLOCK:2 * BLOCK] = h23.astype(BF16)
    kext[0:BLOCK, :] = kext[tl:tl + BLOCK, :]
    vext[0:BLOCK, :] = vext[tl:tl + BLOCK, :]
    _advance(ride, RIDE_STAGES)

    mixed = jnp.concatenate(proj, axis=-1) + _dot(yds[...], w_ref[k_head:D_MIX, :])
    gate = mod_ref[0][:, 2 * D_MODEL:3 * D_MODEL]
    o_ref[0] = _residual_out(mixed, x_ref[0], gate, fg_ref, final)


def _prompt_c(x, mod_p, yacd, yb, w_out_b, final_g, sink, swa_bias, final, layer, row0, cache_args):
    nb, l, _ = x.shape
    tl = SEQ_TILE
    full = lambda *shape: pl.BlockSpec(shape, lambda n, t: (0,) * len(shape))
    c_in, c_out, c_shape = _cache_attention_specs(layer, row0, nb, l // tl, cache_args[6].shape[-1],
                                                  cache_args[8].shape[-1])
    return pl.pallas_call(
        functools.partial(_pc_kernel, final=final),
        grid=(nb, l // tl),
        in_specs=[
            pl.BlockSpec(memory_space=pltpu.SMEM),
            pl.BlockSpec((1, tl, D_MODEL), lambda n, t: (n, t, 0)),
            pl.BlockSpec((1, 1, 3 * D_MODEL), lambda n, t: (n, 0, 0)),
            pl.BlockSpec((1, tl, Y_WIDTH), lambda n, t: (n, t, 0)),
            pl.BlockSpec((1, N_PAIRS_DIL, N_CLASSES, tl // N_CLASSES, BLOCK),
                         lambda n, t: (n, 0, 0, t, 0)),
            pl.BlockSpec((None, D_MIX, D_MODEL), lambda n, t: (layer, 0, 0)),
            full(1, D_MODEL),
            full(2, 2, 2 * BLOCK, 2 * BLOCK),
        ] + c_in,
        out_specs=[pl.BlockSpec((1, tl, D_MODEL), lambda n, t: (n, t, 0))] + c_out,
        out_shape=[jax.ShapeDtypeStruct((nb, l, D_MODEL), F32)] + c_shape,
        scratch_shapes=[
            pltpu.VMEM((N_PAIRS_DIL, tl, BLOCK), F32),
            pltpu.VMEM((BLOCK + tl, W_SWA_KV), BF16),
            pltpu.VMEM((BLOCK + tl, W_SWA_KV), BF16),
            pltpu.VMEM((tl, W_SWA), BF16),
        ],
        compiler_params=_compiler_params(("arbitrary", "arbitrary")),
        name="prompt_c",
    )(sink, x, mod_p, yacd, yb, w_out_b, final_g, swa_bias, *cache_args)


def _sa_kernel(x_ref, mod_ref, g_ref, w_ref, wbd_ref, ps_ref, cw_ref, sp_ref, sc_ref,
               qkv_ref, sw_ref, yac_ref, gates_ref, pst_ref, cst_ref):
    hb = _modulated_norm(x_ref[...], g_ref[...], mod_ref[...]).astype(BF16)
    ns = hb.shape[0]

    zp = _dot(hb, w_ref[:, OFF_POOL:OFF_POOL + 2 * W_POOL])
    pu = zp[:, 0:W_POOL]
    pg = zp[:, W_POOL:2 * W_POOL]
    acc = pu
    sums = {}
    for j in range(1, 16):
        acc = acc + sp_ref[POOL_BUF - j]
        if j + 1 in POOL_WINDOWS:
            sums[j + 1] = acc
    lane = lax.broadcasted_iota(jnp.int32, (ns, W_POOL), 1)
    sel, win = _pool_select(sums, lane)
    diff = sel / win.astype(F32) - pu
    a_out = _dot(diff.astype(BF16), wbd_ref[...]) * ps_ref[...]
    yac_ref[:, 0:W_POOL] = a_out * _silu(pg)
    pst_ref[0:POOL_BUF - 1] = sp_ref[1:POOL_BUF]
    pst_ref[POOL_BUF - 1] = pu

    zc4 = _dot(hb, w_ref[:, OFF_CONV:OFF_CONV + 4 * W_CONV])
    ch = zc4[:, 0:W_CONV]
    cb = zc4[:, W_CONV:2 * W_CONV]
    cc = zc4[:, 2 * W_CONV:3 * W_CONV]
    cg = zc4[:, 3 * W_CONV:4 * W_CONV]
    zc = cc * ch
    conv = (cw_ref[0:1, :] * sc_ref[:, 0:W_CONV] + cw_ref[1:2, :] * sc_ref[:, W_CONV:2 * W_CONV]
            + cw_ref[2:3, :] * zc)
    yac_ref[:, W_POOL:W_POOL + W_CONV] = cb * conv * _silu(cg)
    cst_ref[:, 0:W_CONV] = sc_ref[:, W_CONV:2 * W_CONV]
    cst_ref[:, W_CONV:2 * W_CONV] = zc

    zd = _dot(hb, w_ref[:, OFF_DIL:OFF_DIL + 4 * W_DIL])
    qkv_ref[:, 0:W_DIL] = zd[:, 0:W_DIL] * QK_SCALE
    qkv_ref[:, W_DIL:3 * W_DIL] = zd[:, W_DIL:3 * W_DIL]
    gates_ref[:, 0:W_DIL] = _silu(zd[:, 3 * W_DIL:4 * W_DIL])

    zs = _dot(hb, w_ref[:, OFF_SWA:OFF_SWA + 3 * W_SWA])
    sw_ref[:, 0:W_SWA] = zs[:, 0:W_SWA] * QK_SCALE
    sw_ref[:, W_SWA:2 * W_SWA] = zs[:, W_SWA:2 * W_SWA]
    gates_ref[:, W_DIL:W_DIL + W_SWA] = _silu(zs[:, 2 * W_SWA:3 * W_SWA])


def _whole_spec(shape):
    return pl.BlockSpec(shape, lambda *_: (0,) * len(shape))


def _layer_spec(shape, layer):
    return pl.BlockSpec((None,) + tuple(shape[1:]), lambda *_: (layer,) + (0,) * (len(shape) - 1))


def _sample_a(xs, mod_s, norm_g, w_in_b, wbd, pool_scale, conv_w, sp, sc, layer):
    ns = xs.shape[0]
    shapes = [
        jax.ShapeDtypeStruct((ns, 3 * W_DIL), F32),
        jax.ShapeDtypeStruct((ns, 2 * W_SWA), F32),
        jax.ShapeDtypeStruct((ns, W_POOL + W_CONV), F32),
        jax.ShapeDtypeStruct((ns, W_DIL + W_SWA), F32),
        jax.ShapeDtypeStruct((POOL_BUF, ns, W_POOL), F32),
        jax.ShapeDtypeStruct((ns, CONV_BUF * W_CONV), F32),
    ]
    args = (xs, mod_s, norm_g, w_in_b, wbd, pool_scale, conv_w, sp, sc)
    in_specs = [_whole_spec(a.shape) for a in args]
    in_specs[3] = _layer_spec(w_in_b.shape, layer)
    return pl.pallas_call(
        _sa_kernel,
        grid=(1,),
        in_specs=in_specs,
        out_specs=[_whole_spec(s.shape) for s in shapes],
        out_shape=shapes,
        compiler_params=_compiler_params(("arbitrary",)),
        name="sample_a",
    )(*args)


def _cache_attention(dq_ref, sq_ref, dbias_ref, mult_ref, sbias_ref, sink_ref, kt_ref, vt_ref,
                     ck_ref, cv_ref, dqt_ref, bo_ref, do_ref):
    row_d = lax.broadcasted_iota(jnp.int32, (8, DIL_MAX), 0)
    row_o = lax.broadcasted_iota(jnp.int32, (8, HEAD_DIM), 0)
    row_s = lax.broadcasted_iota(jnp.int32, (8, SWA_WINDOW), 0)
    lane_t = lax.broadcasted_iota(jnp.int32, (HEAD_DIM, 8), 1)
    eye = lax.broadcasted_iota(jnp.int32, (8, 8), 0) == lax.broadcasted_iota(jnp.int32, (8, 8), 1)
    for j in range(DEC_TILE):
        q = dq_ref[j, 0]
        qt = dqt_ref[j, 0]
        s = jnp.zeros((8, DIL_MAX), F32)
        for h in range(N_HEADS_DIL):
            s_h = jnp.sum(kt_ref[0, j, h] * qt[:, h:h + 1], axis=0, keepdims=True)
            s = jnp.where(row_d == h, s_h, s)
            yield
        s = s + dbias_ref[...]
        s_self = jnp.sum(q * dq_ref[j, 1], axis=-1, keepdims=True)
        m = jnp.maximum(jnp.max(s, axis=-1, keepdims=True), s_self)
        p = jnp.exp(s - m) * mult_ref[...]
        p_self = float(len(DIL_CONFIGS)) * jnp.exp(s_self - m)
        den = jnp.sum(p, axis=-1, keepdims=True) + p_self
        yield
        acc_t = jnp.zeros((HEAD_DIM, 8), F32)
        for h in range(N_HEADS_DIL):
            o_h = jnp.sum(vt_ref[0, j, h] * p[h:h + 1, :], axis=1, keepdims=True)
            acc_t = jnp.where(lane_t == h, o_h, acc_t)
            yield
        p_self_t = jnp.sum(jnp.where(eye, p_self, 0.0), axis=0, keepdims=True)
        den_t = jnp.sum(jnp.where(eye, den, 0.0), axis=0, keepdims=True)
        bo_ref[j] = (acc_t + p_self_t * dqt_ref[j, 1]) / den_t

        q = sq_ref[j, 0]
        qb = q.astype(BF16)
        s = jnp.where(row_s < 2, _dot(qb, ck_ref[0, j, 0].astype(BF16)),
                      _dot(qb, ck_ref[0, j, 1].astype(BF16))) + sbias_ref[...]
        w_self = jnp.sum(q * sq_ref[j, 1], axis=-1, keepdims=True)
        sink = sink_ref[...][:, 0:1]
        m = jnp.maximum(jnp.maximum(jnp.max(s, axis=-1, keepdims=True), w_self), sink)
        p = jnp.exp(s - m)
        pw = jnp.exp(w_self - m)
        den = jnp.sum(p, axis=-1, keepdims=True) + pw + jnp.exp(sink - m)
        pb = p.astype(BF16)
        yield
        acc = jnp.where(row_o < 2, _dot_nt(pb, cv_ref[0, j, 0].astype(BF16)),
                        _dot_nt(pb, cv_ref[0, j, 1].astype(BF16)))
        do_ref[j] = (acc + pw * sq_ref[j, 2]) / den
        yield


def _advance(stages, n):
    for _ in range(n):
        next(stages, None)


def _sc_kernel(x_ref, mod_ref, yac_ref, gates_ref, bo_ref, do_ref, w_ref, fg_ref, o_ref, *, final):
    yac = yac_ref[...]
    gates = gates_ref[...]
    y = jnp.concatenate([yac[:, 0:W_POOL], bo_ref[...] * gates[:, 0:W_DIL],
                         yac[:, W_POOL:W_POOL + W_CONV], do_ref[...] * gates[:, W_DIL:W_DIL + W_SWA]],
                        axis=-1).astype(BF16)
    gate = mod_ref[...][:, 2 * D_MODEL:3 * D_MODEL]
    o_ref[...] = _mix_out(y, x_ref[...], gate, w_ref, fg_ref, final)


def _sample_c(xs, mod_s, yac, gates, bo, do, w_out_b, final_g, final, layer):
    args = (xs, mod_s, yac, gates, bo, do, w_out_b, final_g)
    in_specs = [_whole_spec(a.shape) for a in args]
    in_specs[6] = _layer_spec(w_out_b.shape, layer)
    return pl.pallas_call(
        functools.partial(_sc_kernel, final=final),
        grid=(1,),
        in_specs=in_specs,
        out_specs=_whole_spec(xs.shape),
        out_shape=jax.ShapeDtypeStruct(xs.shape, F32),
        compiler_params=_compiler_params(("arbitrary",)),
        name="sample_c",
    )(*args)


def _band_bias_dil4(slope_lo, slope_hi, variant):
    sub = BLOCK // 4
    qidx = np.arange(BLOCK)[:, None]
    kidx = np.arange(2 * BLOCK)[None, :]
    q_step = 4 * (qidx % sub + (sub if variant == 0 else 0)) + qidx // sub
    k_step = 4 * (kidx % (2 * sub)) + kidx // (2 * sub)
    off = q_step - k_step
    valid = (off >= 0) & (off <= BLOCK)
    out = []
    for s in (slope_lo, slope_hi):
        out.append(np.where(valid, -(np.float32(s) * np.float32(4)) * off.astype(np.float32),
                            np.float32(NEG_INF)).astype(np.float32))
    return np.concatenate(out, axis=0)


def _prompt_bias_tables():
    dil = _alibi_slopes(N_HEADS_DIL)

    def table(hp, d, var):
        if d == 4:
            return _band_bias_dil4(dil[2 * hp], dil[2 * hp + 1], var)
        return _band_bias(dil[2 * hp], dil[2 * hp + 1], d, var)

    dil_bias = np.stack([
        np.stack([np.stack([table(hp, d, var) for var in (0, 1)])
                  for _, d in DIL_CONFIGS]) for hp in range(N_PAIRS_DIL)])
    swa = _alibi_slopes(N_HEADS_SWA)
    swa_bias = np.stack([
        np.stack([_band_bias(swa[SWA_HEAD_PERM[2 * g]], swa[SWA_HEAD_PERM[2 * g + 1]], 1, var)
                  for var in (0, 2)]) for g in range(2)])
    return jnp.asarray(dil_bias), jnp.asarray(swa_bias)


def _sample_bias_tables(r):
    dil = _alibi_slopes(N_HEADS_DIL)
    dist = (r - np.arange(r)).astype(np.float32)
    mult = np.zeros((1, r), np.float32)
    for window, d in DIL_CONFIGS:
        mult[0] += ((dist <= window) & (dist % d == 0)).astype(np.float32)
    dbias = np.zeros((8, r), np.float32)
    for h in range(N_HEADS_DIL):
        dbias[h] = np.where(mult[0] > 0, -np.float32(dil[h]) * dist, np.float32(NEG_INF))
    swa = _alibi_slopes(N_HEADS_SWA)
    sdist = (SWA_WINDOW - np.arange(SWA_WINDOW)).astype(np.float32)
    sbias = np.zeros((8, SWA_WINDOW), np.float32)
    for h in range(N_HEADS_SWA):
        sbias[h] = -np.float32(swa[h]) * sdist
    return jnp.asarray(dbias), jnp.asarray(mult), jnp.asarray(sbias)


def kernel(x_prompt, x_sample, c_prompt, c_sample, state_pool, cache_dil_k, cache_dil_v, state_conv, cache_swa_k, cache_swa_v, norm_g, w_ada, b_ada, w_in, w_pool, pool_scale, conv_w, swa_sink, w_out, final_g):
    depth = w_in.shape[0]
    nb, l, _ = x_prompt.shape
    ns = x_sample.shape[0]
    assert x_sample.shape[1] == 1 and l % CHUNK == 0 and l >= DIL_MAX
    assert ns == 2 * nb * (l // SEQ_TILE) * DEC_TILE
    assert cache_dil_k.shape[2] == DIL_MAX and cache_swa_k.shape[2] == SWA_WINDOW

    w_in_b = w_in.astype(BF16)
    w_out_b = w_out.astype(BF16)
    eye = jnp.eye(len(POOL_WINDOWS), dtype=F32)
    wbd = jnp.einsum("dgce,gh->dgche", w_pool, eye).reshape(depth, W_POOL, W_POOL).astype(BF16)
    sink_rows = jnp.zeros((depth, 8), F32).at[:, 0:N_HEADS_SWA].set(swa_sink)
    sink_rows = jnp.broadcast_to(sink_rows[:, :, None], (depth, 8, BLOCK))

    dil_bias, swa_bias = _prompt_bias_tables()
    dbias, mult, sbias = _sample_bias_tables(cache_dil_k.shape[2])

    mod = _ada(jnp.concatenate([c_prompt, c_sample], axis=0), w_ada, b_ada)
    fg = final_g.reshape(1, D_MODEL)

    xp = x_prompt
    xs = x_sample.reshape(ns, D_MODEL)
    kt = jnp.transpose(cache_dil_k, (0, 1, 3, 4, 2))
    vt = jnp.transpose(cache_dil_v, (0, 1, 3, 4, 2))
    ckt = jnp.transpose(cache_swa_k, (0, 1, 3, 4, 2))
    cvt = jnp.transpose(cache_swa_v, (0, 1, 3, 4, 2))
    sp_all = jnp.transpose(state_pool, (0, 2, 1, 3))
    sc_all = state_conv.reshape(depth, ns, CONV_BUF * W_CONV)
    pad_heads = lambda a: jnp.pad(a, ((0, 0), (0, 0), (0, 8 - a.shape[2]), (0, 0)))
    outs = [[] for _ in range(12)]
    carried = None
    for i in range(depth):
        final = i == depth - 1
        mod_p = mod[i, 0:nb].reshape(nb, 1, 3 * D_MODEL)
        mod_s = mod[i, nb:nb + ns]
        g = norm_g[i].reshape(1, D_MODEL)
        ps = pool_scale[i].reshape(1, W_POOL)

        qkv, sw, yac, gates, pst_s, cst_s = _sample_a(
            xs, mod_s, g, w_in_b, wbd[i], ps, conv_w[i], sp_all[i], sc_all[i], i)
        dq3 = pad_heads(qkv.reshape(ns, 3, N_HEADS_DIL, HEAD_DIM))
        sq4 = sw[:, 0:W_SWA].reshape(ns, 1, N_HEADS_SWA, HEAD_DIM)
        skv = jnp.repeat(sw[:, W_SWA:W_SWA + 2 * W_SWA_KV].reshape(ns, 2, 2, HEAD_DIM), 2, axis=2)
        sq3 = pad_heads(jnp.concatenate([sq4, skv], axis=1))

        dqt = jnp.transpose(dq3[:, 0:3:2], (0, 1, 3, 2))
        cache_args = (dq3, sq3, dbias, mult, sbias, sink_rows[i], kt, vt, ckt, cvt, dqt)
        q1, k1, v1, qm, km, vm, gm, kc, vc, yacd, pst, cst, skc, svc, bo_a, do_a = _prompt_a(
            xp, mod_p, g, w_in_b, wbd[i], ps, conv_w[i], i, depth, 0, cache_args, carried)
        carried = (kc, vc)
        yb = _prompt_b(q1, k1, v1, qm, km, vm, gm, dil_bias)
        xp, bo_c, do_c = _prompt_c(xp, mod_p, yacd, yb, w_out_b, fg, swa_sink[i], swa_bias, final,
                                   i, ns // 2, cache_args)

        bo = jnp.transpose(jnp.concatenate([bo_a, bo_c], axis=0), (0, 2, 1))
        bo = bo[:, 0:N_HEADS_DIL].reshape(ns, W_DIL)
        do = jnp.concatenate([do_a, do_c], axis=0)[:, 0:N_HEADS_SWA].reshape(ns, W_SWA)
        xs = _sample_c(xs, mod_s, yac, gates, bo, do, w_out_b, fg, final, i)

        unfold = lambda a, h: jnp.transpose(a.reshape(nb, h, HEAD_DIM, a.shape[-1]), (0, 3, 1, 2))
        skc, svc = unfold(skc, 2), unfold(svc, 2)
        outs[0].append(pst)
        outs[1].append(jnp.transpose(pst_s, (1, 0, 2)))
        outs[4].append(qkv[:, W_DIL:2 * W_DIL].reshape(ns, 1, N_HEADS_DIL, HEAD_DIM))
        outs[5].append(qkv[:, 2 * W_DIL:3 * W_DIL].reshape(ns, 1, N_HEADS_DIL, HEAD_DIM))
        outs[6].append(cst)
        outs[7].append(cst_s.reshape(ns, CONV_BUF, W_CONV))
        outs[8].append(skc)
        outs[9].append(svc)
        outs[10].append(sw[:, W_SWA:W_SWA + W_SWA_KV].reshape(ns, 1, 2, HEAD_DIM))
        outs[11].append(sw[:, W_SWA + W_SWA_KV:W_SWA + 2 * W_SWA_KV].reshape(ns, 1, 2, HEAD_DIM))

    for k, cache in zip((2, 3), carried):
        outs[k] = jnp.transpose(cache.reshape(depth, nb, N_HEADS_DIL, HEAD_DIM, DIL_MAX), (0, 1, 4, 2, 3))
    return (xp, xs.reshape(ns, 1, D_MODEL)) + tuple(
        o if not isinstance(o, list) else jnp.stack(o) for o in outs)
```

```python
import functools
import math

import numpy as np
import jax
import jax.numpy as jnp
from jax import lax
from jax.experimental import pallas as pl
from jax.experimental.pallas import tpu as pltpu

F32 = jnp.float32
BF16 = jnp.bfloat16

D_MODEL = 1024
HEAD_DIM = 64
BLOCK = 128
POOL_WINDOWS = (2, 4, 8, 16)
POOL_GROUP = 64
W_POOL = 256
POOL_BUF = 15
DIL_CONFIGS = ((128, 1), (512, 4), (2048, 16))
DIL_MAX = 2048
N_HEADS_DIL = 6
N_PAIRS_DIL = 3
N_CLASSES = 16
W_DIL = 384
W_CONV = 256
CONV_BUF = 2
N_HEADS_SWA = 4
W_SWA = 256
W_SWA_KV = 128
SWA_WINDOW = 128
D_MIX = 1152
D_PROJ = 3840
RMS_EPS = 1e-6
QK_SCALE = 1.0 / math.sqrt(HEAD_DIM)

OFF_POOL = 0
OFF_DIL = 512
OFF_CONV = 2048
OFF_SWA = 3072
Y_WIDTH = W_POOL + W_CONV + 3 * W_SWA
SWA_HEAD_PERM = (0, 3, 1, 2)

SEQ_TILE = 512
CHUNK = 2048
BAND_SKEW = 1
DEC_TILE = 1
RIDE_STAGES = 15 * DEC_TILE
VMEM_LIMIT = 56 * 1024 * 1024

NEG_INF = float("-inf")


def _silu(v):
    return v * jax.nn.sigmoid(v)


def _dot(a, b):
    return jnp.dot(a, b, preferred_element_type=F32)


def _dot_nt(a, b):
    return lax.dot_general(a, b, (((1,), (1,)), ((), ())), preferred_element_type=F32)


def _alibi_slopes(n):
    return [2.0 ** (-8.0 * (h + 1) / n) for h in range(n)]


def _band_bias(slope_lo, slope_hi, dist_scale, variant):
    qi = np.arange(BLOCK)[:, None]
    kj = np.arange(2 * BLOCK)[None, :]
    if variant == 1:
        off = qi - kj
    else:
        off = qi - kj + BLOCK
    valid = (off >= 0) & (off <= BLOCK)
    if variant == 2:
        valid = valid & (kj >= BLOCK)
    out = []
    for s in (slope_lo, slope_hi):
        b = np.where(valid, -(np.float32(s) * np.float32(dist_scale)) * off.astype(np.float32),
                     np.float32(NEG_INF))
        out.append(b.astype(np.float32))
    return np.concatenate(out, axis=0)


def _compiler_params(sem):
    return pltpu.CompilerParams(dimension_semantics=sem, vmem_limit_bytes=VMEM_LIMIT)


def _ada_kernel(c_ref, w_ref, b_ref, o_ref):
    s = _silu(c_ref[...]).astype(BF16)
    o_ref[0] = _dot(s, w_ref[0].astype(BF16)) + b_ref[0]


def _ada(c_all, w_ada_b, b_ada):
    depth = w_ada_b.shape[0]
    rows = c_all.shape[0]
    return pl.pallas_call(
        _ada_kernel,
        grid=(depth, 3),
        in_specs=[
            pl.BlockSpec((rows, D_MODEL), lambda i, j: (0, 0)),
            pl.BlockSpec((1, D_MODEL, D_MODEL), lambda i, j: (i, 0, j)),
            pl.BlockSpec((1, 1, D_MODEL), lambda i, j: (i, 0, j)),
        ],
        out_specs=pl.BlockSpec((1, rows, D_MODEL), lambda i, j: (i, 0, j)),
        out_shape=jax.ShapeDtypeStruct((depth, rows, 3 * D_MODEL), F32),
        compiler_params=_compiler_params(("arbitrary", "arbitrary")),
        name="ada",
    )(c_all, w_ada_b, b_ada.reshape(depth, 1, 3 * D_MODEL))


def _modulated_norm(x, g, mod_row):
    ms = jnp.mean(x * x, axis=-1, keepdims=True)
    shift = mod_row[:, 0:D_MODEL]
    scale = mod_row[:, D_MODEL:2 * D_MODEL]
    return x * lax.rsqrt(ms + RMS_EPS) * (g * (1.0 + scale)) + shift


def _pair_heads(v):
    left, right = v[:, 0:BLOCK], v[:, BLOCK:2 * BLOCK]
    low = lax.broadcasted_iota(jnp.int32, left.shape, 1) < HEAD_DIM
    return jnp.concatenate([jnp.where(low, left, right),
                            pltpu.roll(jnp.where(low, right, left), HEAD_DIM, axis=1)], axis=-1)


def _unpair_heads(a, b):
    low = lax.broadcasted_iota(jnp.int32, a.shape, 1) < HEAD_DIM
    b_swapped = pltpu.roll(b, HEAD_DIM, axis=1)
    return jnp.where(low, a, b_swapped), jnp.where(low, b_swapped, a)


def _pool_select(sums, lane):
    grp = lane // POOL_GROUP
    sel = jnp.where(grp == 0, sums[2], jnp.where(grp == 1, sums[4],
                                                 jnp.where(grp == 2, sums[8], sums[16])))
    win = jnp.where(grp == 0, 2, jnp.where(grp == 1, 4, jnp.where(grp == 2, 8, 16)))
    return sel, win


def _pa_kernel(x_ref, mod_ref, g_ref, w_ref, wbd_ref, ps_ref, cw_ref,
               dq_ref, sq_ref, dbias_ref, mult_ref, sbias_ref, ssink_ref, kt_ref, vt_ref, ck_ref, cv_ref,
               q1_ref, k1_ref, v1_ref, qm_ref, km_ref, vm_ref, gm_ref, kc_ref, vc_ref, y_ref,
               pst_ref, cst_ref, skc_ref, svc_ref, bo_ref, do_ref,
               uext, zcext, zds, z4s, *, later_layers=0):
    tl = SEQ_TILE
    t = pl.program_id(1)

    @pl.when(t == 0)
    def _():
        for k in range(len(POOL_WINDOWS)):
            uext[k, 0:16, :] = jnp.zeros((16, W_POOL), F32)
        zcext[0:8, :] = jnp.zeros((8, W_CONV), F32)

    ride = _cache_attention(dq_ref, sq_ref, dbias_ref, mult_ref, sbias_ref, ssink_ref, kt_ref, vt_ref,
                            ck_ref, cv_ref, bo_ref, do_ref)

    _advance(ride, 3)
    hb = _modulated_norm(x_ref[0], g_ref[...], mod_ref[0]).astype(BF16)

    zp = _dot(hb, w_ref[:, OFF_POOL:OFF_POOL + 2 * W_POOL])
    zd = _dot(hb, w_ref[:, OFF_DIL:OFF_DIL + 4 * W_DIL])
    _advance(ride, 3)

    pu = zp[:, 0:W_POOL]
    pg = zp[:, W_POOL:2 * W_POOL]
    uext[0, 16:16 + tl, :] = pu
    sums = {}
    level = pu
    for k, w in enumerate(POOL_WINDOWS):
        shift = w // 2
        level = level + uext[k, 16 - shift:16 - shift + tl, :]
        sums[w] = level
        if k + 1 < len(POOL_WINDOWS):
            uext[k + 1, 16:16 + tl, :] = level
    lane = lax.broadcasted_iota(jnp.int32, (tl, W_POOL), 1)
    gpos = lax.broadcasted_iota(jnp.int32, (tl, W_POOL), 0) + t * tl
    sel, win = _pool_select(sums, lane)
    cnt = jnp.minimum(gpos + 1, win).astype(F32)
    diff = sel / cnt - pu
    a_out = _dot(diff.astype(BF16), wbd_ref[...]) * ps_ref[...]
    y_ref[0, :, 0:W_POOL] = (a_out * _silu(pg)).astype(BF16)
    pst_ref[0] = uext[0, tl + 1:tl + 16, :]
    for k in range(len(POOL_WINDOWS)):
        uext[k, 0:16, :] = uext[k, tl:tl + 16, :]

    zs = _dot(hb, w_ref[:, OFF_SWA:OFF_SWA + 3 * W_SWA])
    _advance(ride, 3)
    kc_ref[0, 0] = zd[:, W_DIL:2 * W_DIL].T
    vc_ref[0, 0] = zd[:, 2 * W_DIL:3 * W_DIL].T
    for d in range(1, 1 + later_layers):
        kc_ref[d, 0] = jnp.zeros((W_DIL, tl), F32)
        vc_ref[d, 0] = jnp.zeros((W_DIL, tl), F32)
    for hp in range(N_PAIRS_DIL):
        _advance(ride, 1)
        lo = hp * BLOCK
        zds[0] = zd[:, lo:lo + BLOCK] * QK_SCALE
        zds[1] = zd[:, W_DIL + lo:W_DIL + lo + BLOCK]
        zds[2] = zd[:, 2 * W_DIL + lo:2 * W_DIL + lo + BLOCK]
        zds[3] = _silu(zd[:, 3 * W_DIL + lo:3 * W_DIL + lo + BLOCK])
        q1_ref[0, hp] = zds[0].astype(BF16)
        k1_ref[0, hp] = zds[1].astype(BF16)
        v1_ref[0, hp] = zds[2].astype(BF16)
        quarter = tl // 4
        for a in range(4):
            for blk in range(4):
                z4s[blk, a] = zds[blk, pl.ds(a, quarter, stride=4), :]
        for b in range(4):
            for a in range(4):
                c = 4 * b + a
                rows = pl.ds(b, tl // N_CLASSES, stride=4)
                qm_ref[0, hp, c] = z4s[0, a, rows, :].astype(BF16)
                km_ref[0, hp, c] = z4s[1, a, rows, :].astype(BF16)
                vm_ref[0, hp, c] = z4s[2, a, rows, :].astype(BF16)
                gm_ref[0, hp, c] = z4s[3, a, rows, :].astype(BF16)

    zc4 = _dot(hb, w_ref[:, OFF_CONV:OFF_CONV + 4 * W_CONV])
    _advance(ride, 3)

    sk = zs[:, W_SWA:W_SWA + W_SWA_KV]
    sv = zs[:, W_SWA + W_SWA_KV:W_SWA + 2 * W_SWA_KV]
    skc_ref[0] = sk[tl - SWA_WINDOW:tl, :].T
    svc_ref[0] = sv[tl - SWA_WINDOW:tl, :].T
    c0 = W_POOL + W_CONV
    y_ref[0, :, c0:c0 + W_SWA] = _pair_heads(zs[:, 0:W_SWA] * QK_SCALE).astype(BF16)
    y_ref[0, :, c0 + W_SWA:c0 + 2 * W_SWA] = zs[:, W_SWA:2 * W_SWA].astype(BF16)
    y_ref[0, :, c0 + 2 * W_SWA:c0 + 3 * W_SWA] = _pair_heads(_silu(zs[:, 2 * W_SWA:3 * W_SWA])).astype(BF16)
    _advance(ride, RIDE_STAGES)

    ch = zc4[:, 0:W_CONV]
    cb = zc4[:, W_CONV:2 * W_CONV]
    cc = zc4[:, 2 * W_CONV:3 * W_CONV]
    cg = zc4[:, 3 * W_CONV:4 * W_CONV]
    zc = cc * ch
    zcext[8:8 + tl, :] = zc
    conv = (cw_ref[0:1, :] * zcext[6:6 + tl, :] + cw_ref[1:2, :] * zcext[7:7 + tl, :]
            + cw_ref[2:3, :] * zc)
    y_ref[0, :, W_POOL:W_POOL + W_CONV] = (cb * conv * _silu(cg)).astype(BF16)
    cst_ref[0] = zcext[tl + 6:tl + 8, :]
    zcext[0:8, :] = zcext[tl:tl + 8, :]


def _cache_attention_specs(layer, row0, nb, nt, r, swr):
    bt = DEC_TILE
    blk0 = row0 // bt
    full = lambda *shape: pl.BlockSpec(shape, lambda n, t: (0,) * len(shape))
    srow_spec = pl.BlockSpec((bt, 3, 8, HEAD_DIM), lambda n, t: (blk0 + n * nt + t, 0, 0, 0))
    dil_spec = pl.BlockSpec((1, bt, N_HEADS_DIL, HEAD_DIM, r),
                            lambda n, t: (layer, blk0 + n * nt + t, 0, 0, 0))
    swa_spec = pl.BlockSpec((1, bt, 2, HEAD_DIM, swr), lambda n, t: (layer, blk0 + n * nt + t, 0, 0, 0))
    sout_spec = pl.BlockSpec((bt, 8, HEAD_DIM), lambda n, t: (n * nt + t, 0, 0))
    sout_shape = jax.ShapeDtypeStruct((nb * nt * bt, 8, HEAD_DIM), F32)
    in_specs = [srow_spec, srow_spec, full(8, r), full(1, r), full(8, swr), full(8, BLOCK),
                dil_spec, dil_spec, swa_spec, swa_spec]
    return in_specs, [sout_spec, sout_spec], [sout_shape, sout_shape]


N_PA_INPUTS = 17


def _pa_kernel_carry(*refs):
    _pa_kernel(*refs[:N_PA_INPUTS], *refs[N_PA_INPUTS + 2:])


def _prompt_a(x, mod_p, norm_g, w_in_b, wbd, pool_scale, conv_w, layer, depth, row0, cache_args, carried):
    nb, l, _ = x.shape
    tl = SEQ_TILE
    nt = l // tl
    cache_t0 = (l - DIL_MAX) // tl
    c_in, c_out, c_shape = _cache_attention_specs(layer, row0, nb, nt, cache_args[6].shape[-1],
                                                  cache_args[8].shape[-1])
    pair_spec = pl.BlockSpec((1, N_PAIRS_DIL, tl, BLOCK), lambda n, t: (n, 0, t, 0))
    pair_shape = jax.ShapeDtypeStruct((nb, N_PAIRS_DIL, l, BLOCK), BF16)
    cm_spec = pl.BlockSpec((1, N_PAIRS_DIL, N_CLASSES, tl // N_CLASSES, BLOCK),
                           lambda n, t: (n, 0, 0, t, 0))
    cm_shape = jax.ShapeDtypeStruct((nb, N_PAIRS_DIL, N_CLASSES, l // N_CLASSES, BLOCK), BF16)
    cache_layers = depth if carried is None else 1
    cache_spec = pl.BlockSpec((cache_layers, 1, W_DIL, tl),
                              lambda n, t: (layer, n, 0, jnp.maximum(t - cache_t0, 0)))
    cache_shape = jax.ShapeDtypeStruct((depth, nb, W_DIL, DIL_MAX), F32)
    full = lambda *shape: pl.BlockSpec(shape, lambda n, t: (0,) * len(shape))
    in_specs = [
        pl.BlockSpec((1, tl, D_MODEL), lambda n, t: (n, t, 0)),
        pl.BlockSpec((1, 1, 3 * D_MODEL), lambda n, t: (n, 0, 0)),
        full(1, D_MODEL),
        pl.BlockSpec((None, D_MODEL, D_PROJ), lambda n, t: (layer, 0, 0), pipeline_mode=pl.Buffered(1)),
        full(W_POOL, W_POOL),
        full(1, W_POOL),
        full(3, W_CONV),
    ] + c_in
    assert len(in_specs) == N_PA_INPUTS
    extra, aliases = (), {}
    if carried is not None:
        in_specs = in_specs + [pl.BlockSpec(memory_space=pl.ANY)] * 2
        extra = tuple(carried)
        aliases = {N_PA_INPUTS: 7, N_PA_INPUTS + 1: 8}
    return pl.pallas_call(
        functools.partial(_pa_kernel, later_layers=depth - 1) if carried is None else _pa_kernel_carry,
        grid=(nb, nt),
        in_specs=in_specs,
        input_output_aliases=aliases,
        out_specs=[
            pair_spec, pair_spec, pair_spec, cm_spec, cm_spec, cm_spec, cm_spec,
            cache_spec, cache_spec,
            pl.BlockSpec((1, tl, Y_WIDTH), lambda n, t: (n, t, 0)),
            pl.BlockSpec((1, POOL_BUF, W_POOL), lambda n, t: (n, 0, 0)),
            pl.BlockSpec((1, CONV_BUF, W_CONV), lambda n, t: (n, 0, 0)),
            pl.BlockSpec((1, SWA_WINDOW, W_SWA_KV), lambda n, t: (n, 0, 0)),
            pl.BlockSpec((1, SWA_WINDOW, W_SWA_KV), lambda n, t: (n, 0, 0)),
        ] + c_out,
        out_shape=[
            pair_shape, pair_shape, pair_shape, cm_shape, cm_shape, cm_shape, cm_shape,
            cache_shape, cache_shape,
            jax.ShapeDtypeStruct((nb, l, Y_WIDTH), BF16),
            jax.ShapeDtypeStruct((nb, POOL_BUF, W_POOL), F32),
            jax.ShapeDtypeStruct((nb, CONV_BUF, W_CONV), F32),
            jax.ShapeDtypeStruct((nb, SWA_WINDOW, W_SWA_KV), F32),
            jax.ShapeDtypeStruct((nb, SWA_WINDOW, W_SWA_KV), F32),
        ] + c_shape,
        scratch_shapes=[
            pltpu.VMEM((len(POOL_WINDOWS), 16 + tl, W_POOL), F32),
            pltpu.VMEM((8 + tl, W_CONV), F32),
            pltpu.VMEM((4, tl, BLOCK), F32),
            pltpu.VMEM((4, 4, tl // 4, BLOCK), F32),
        ],
        compiler_params=_compiler_params(("arbitrary", "arbitrary")),
        name="prompt_a",
    )(x, mod_p, norm_g, w_in_b, wbd, pool_scale, conv_w, *cache_args, *extra)


def _band_scores(q, kblk, bias, lane_q):
    lhs = jnp.concatenate([jnp.where(lane_q < HEAD_DIM, q, jnp.zeros_like(q)),
                           jnp.where(lane_q >= HEAD_DIM, q, jnp.zeros_like(q))], axis=0)
    return _dot_nt(lhs, kblk) + bias


def _band_values(s, vblk, lane_q):
    m = jnp.max(s, axis=-1, keepdims=True)
    p = jnp.exp((s - m).astype(BF16))
    acc = _dot(p, jnp.concatenate([vblk, jnp.ones_like(vblk)], axis=-1))
    lo = lane_q < HEAD_DIM
    return (jnp.where(lo, acc[0:BLOCK, 0:BLOCK], acc[BLOCK:2 * BLOCK, 0:BLOCK]),
            jnp.where(lo, m[0:BLOCK], m[BLOCK:2 * BLOCK]),
            jnp.where(lo, acc[0:BLOCK, BLOCK:2 * BLOCK], acc[BLOCK:2 * BLOCK, BLOCK:2 * BLOCK]))


def _run_skewed(tasks, skew):
    pending = []
    for scores_fn, finish_fn in tasks:
        pending.append((finish_fn, scores_fn()))
        if len(pending) > skew:
            fn, s = pending.pop(0)
            fn(s)
    for fn, s in pending:
        fn(s)


def _pb_kernel(bias_ref, q1, k1, v1, qm, km, vm, gm, outm, s1, s4, s16):
    c = pl.program_id(2)
    first = jnp.where(c == 0, 1, 0)
    lane_q = lax.broadcasted_iota(jnp.int32, (BLOCK, BLOCK), 1)
    sub = BLOCK // 4

    tasks = []

    def dil1_task(j):
        r0 = j * BLOCK
        if j == 0:
            start = jnp.maximum(c * (CHUNK // BLOCK) - 1, 0) * BLOCK
            var = first
        else:
            start = (c * (CHUNK // BLOCK) + (j - 1)) * BLOCK
            var = 0
        start = pl.multiple_of(start, BLOCK)

        def scores():
            return _band_scores(q1[0, 0, r0:r0 + BLOCK, :], k1[0, 0, pl.ds(start, 2 * BLOCK), :],
                                bias_ref[0, 0, var], lane_q)

        def finish(s):
            parts = _band_values(s, v1[0, 0, pl.ds(start, 2 * BLOCK), :], lane_q)
            for k, part in enumerate(parts):
                s1[k, r0:r0 + BLOCK, :] = part

        return scores, finish

    def dil4_task(c4, j):
        i0 = j * sub
        if j == 0:
            istart = jnp.maximum(c * (CHUNK // N_CLASSES) - sub, 0)
            var = first
        else:
            istart = c * (CHUNK // N_CLASSES) + i0 - sub
            var = 0
        istart = pl.multiple_of(istart, sub)
        classes = [4 * cc + c4 for cc in range(4)]

        def scores():
            q = jnp.concatenate([qm[0, 0, cl, i0:i0 + sub, :] for cl in classes], axis=0)
            kblk = jnp.concatenate([km[0, 0, cl, pl.ds(istart, 2 * sub), :] for cl in classes], axis=0)
            return _band_scores(q, kblk, bias_ref[0, 1, var], lane_q)

        def finish(s):
            vblk = jnp.concatenate([vm[0, 0, cl, pl.ds(istart, 2 * sub), :] for cl in classes], axis=0)
            parts = _band_values(s, vblk, lane_q)
            for k, part in enumerate(parts):
                for cc, cl in enumerate(classes):
                    s4[k, cl, i0:i0 + sub, :] = part[cc * sub:(cc + 1) * sub]

        return scores, finish

    start16 = pl.multiple_of(jnp.maximum(c - 1, 0) * BLOCK, BLOCK)

    def dil16_task(cl):
        def scores():
            return _band_scores(qm[0, 0, cl], km[0, 0, cl, pl.ds(start16, 2 * BLOCK), :],
                                bias_ref[0, 2, first], lane_q)

        def finish(s):
            parts = _band_values(s, vm[0, 0, cl, pl.ds(start16, 2 * BLOCK), :], lane_q)
            for k, part in enumerate(parts):
                s16[k, cl] = part

        return scores, finish

    tasks += [dil1_task(j) for j in range(CHUNK // BLOCK)]
    tasks += [dil4_task(c4, j) for c4 in range(4) for j in range(4)]
    tasks += [dil16_task(cl) for cl in range(N_CLASSES)]
    _run_skewed(tasks, BAND_SKEW)

    for cl in range(N_CLASSES):
        rows = pl.ds(cl, BLOCK, stride=N_CLASSES)
        ma, mb, mc = s1[1, rows, :], s4[1, cl], s16[1, cl]
        mx = jnp.maximum(jnp.maximum(ma, mb), mc)
        wa = jnp.exp(ma - mx)
        wb = jnp.exp(mb - mx)
        wc = jnp.exp(mc - mx)
        num = wa * s1[0, rows, :] + wb * s4[0, cl] + wc * s16[0, cl]
        den = wa * s1[2, rows, :] + wb * s4[2, cl] + wc * s16[2, cl]
        outm[0, 0, cl] = (num / den * gm[0, 0, cl].astype(F32)).astype(BF16)


def _prompt_b(q1, k1, v1, qm, km, vm, gm, dil_bias):
    nb, npair, l, _ = q1.shape
    nc = l // CHUNK
    li = l // N_CLASSES
    ci = CHUNK // N_CLASSES
    q1_spec = pl.BlockSpec((1, 1, CHUNK, BLOCK), lambda n, h, c: (n, h, c, 0))
    kv1_spec = pl.BlockSpec((1, 1, l, BLOCK), lambda n, h, c: (n, h, 0, 0))
    cm_spec = pl.BlockSpec((1, 1, N_CLASSES, ci, BLOCK), lambda n, h, c: (n, h, 0, c, 0))
    kvm_spec = pl.BlockSpec((1, 1, N_CLASSES, li, BLOCK), lambda n, h, c: (n, h, 0, 0, 0))
    cm_scratch = pltpu.VMEM((3, N_CLASSES, ci, BLOCK), F32)
    return pl.pallas_call(
        _pb_kernel,
        grid=(nb, npair, nc),
        in_specs=[pl.BlockSpec((1, 3, 2, 2 * BLOCK, 2 * BLOCK), lambda n, h, c: (h, 0, 0, 0, 0)),
                  q1_spec, kv1_spec, kv1_spec, cm_spec, kvm_spec, kvm_spec, cm_spec],
        out_specs=cm_spec,
        out_shape=jax.ShapeDtypeStruct((nb, npair, N_CLASSES, li, BLOCK), BF16),
        scratch_shapes=[
            pltpu.VMEM((3, CHUNK, BLOCK), F32), cm_scratch, cm_scratch,
        ],
        compiler_params=_compiler_params(("arbitrary", "arbitrary", "arbitrary")),
        name="prompt_b",
    )(dil_bias, q1, k1, v1, qm, km, vm, gm)


def _mix_out(y, x, gate, w_ref, fg_ref, final):
    return _residual_out(_dot(y, w_ref[...]), x, gate, fg_ref, final)


def _residual_out(mixed, x, gate, fg_ref, final):
    xn = x + gate * mixed
    if final:
        ms = jnp.mean(xn * xn, axis=-1, keepdims=True)
        xn = xn * lax.rsqrt(ms + RMS_EPS) * fg_ref[...]
    return xn


def _pc_kernel(sink_ref, x_ref, mod_ref, yacd_ref, yb_ref, w_ref, fg_ref, bias_ref,
               dq_ref, sq_ref, dbias_ref, mult_ref, sbias_ref, ssink_ref, kt_ref, vt_ref, ck_ref, cv_ref,
               o_ref, bo_ref, do_ref, ybs, kext, vext, yds, *, final):
    tl = SEQ_TILE
    t = pl.program_id(1)

    @pl.when(t == 0)
    def _():
        kext[0:BLOCK, :] = jnp.zeros((BLOCK, W_SWA_KV), BF16)
        vext[0:BLOCK, :] = jnp.zeros((BLOCK, W_SWA_KV), BF16)

    for hp in range(N_PAIRS_DIL):
        for c in range(N_CLASSES):
            ybs[hp, pl.ds(c, tl // N_CLASSES, stride=N_CLASSES), :] = yb_ref[0, hp, c].astype(F32)
    k_head = D_MIX - W_SWA
    y_head = jnp.concatenate([yacd_ref[0, :, 0:W_POOL], ybs[0].astype(BF16), ybs[1].astype(BF16),
                              ybs[2].astype(BF16), yacd_ref[0, :, W_POOL:W_POOL + W_CONV]], axis=-1)
    n_tile = D_MODEL // (tl // BLOCK)
    proj = []
    ride = _cache_attention(dq_ref, sq_ref, dbias_ref, mult_ref, sbias_ref, ssink_ref, kt_ref, vt_ref,
                            ck_ref, cv_ref, bo_ref, do_ref)

    c0 = W_POOL + W_CONV
    kext[BLOCK:BLOCK + tl, :] = yacd_ref[0, :, c0 + W_SWA:c0 + W_SWA + W_SWA_KV]
    vext[BLOCK:BLOCK + tl, :] = yacd_ref[0, :, c0 + W_SWA + W_SWA_KV:c0 + 2 * W_SWA]
    first = jnp.where(t == 0, 1, 0)
    lane_q = lax.broadcasted_iota(jnp.int32, (BLOCK, BLOCK), 1)
    row_s = lax.broadcasted_iota(jnp.int32, (2 * BLOCK, 1), 0)
    for jb in range(tl // BLOCK):
        proj.append(_dot(y_head, w_ref[0:k_head, jb * n_tile:(jb + 1) * n_tile]))
        r0 = jb * BLOCK
        kblk = kext[r0:r0 + 2 * BLOCK, :]
        vblk = vext[r0:r0 + 2 * BLOCK, :]
        gated = []
        for grp in range(2):
            q = yacd_ref[0, r0:r0 + BLOCK, c0 + grp * BLOCK:c0 + (grp + 1) * BLOCK]
            lhs = jnp.concatenate([jnp.where(lane_q < HEAD_DIM, q, jnp.zeros_like(q)),
                                   jnp.where(lane_q >= HEAD_DIM, q, jnp.zeros_like(q))], axis=0)
            s = _dot_nt(lhs, kblk)
            if jb == 0:
                s = s + bias_ref[grp, first]
            else:
                s = s + bias_ref[grp, 0]
            sink = jnp.where(row_s < BLOCK, sink_ref[SWA_HEAD_PERM[2 * grp]],
                             sink_ref[SWA_HEAD_PERM[2 * grp + 1]])
            m = jnp.maximum(jnp.max(s, axis=-1, keepdims=True), sink)
            p = jnp.exp((s - m).astype(BF16))
            acc = _dot(p, jnp.concatenate([vblk, jnp.ones_like(vblk)], axis=-1))
            es = jnp.exp(sink - m)
            lo = lane_q < HEAD_DIM
            den = (jnp.where(lo, acc[0:BLOCK, BLOCK:2 * BLOCK], acc[BLOCK:2 * BLOCK, BLOCK:2 * BLOCK])
                   + jnp.where(lo, es[0:BLOCK], es[BLOCK:2 * BLOCK]))
            od = jnp.where(lo, acc[0:BLOCK, 0:BLOCK], acc[BLOCK:2 * BLOCK, 0:BLOCK]) / den
            g0 = c0 + 2 * W_SWA + grp * BLOCK
            gate_s = yacd_ref[0, r0:r0 + BLOCK, g0:g0 + BLOCK].astype(F32)
            gated.append(od * gate_s)
            _advance(ride, 2)
        h01, h23 = _unpair_heads(*gated)
        yds[r0:r0 + BLOCK, 0:BLOCK] = h01.astype(BF16)
        yds[r0:r0 + BLOCK, BLOCK:2 * BLOCK] = h23.astype(BF16)
    kext[0:BLOCK, :] = kext[tl:tl + BLOCK, :]
    vext[0:BLOCK, :] = vext[tl:tl + BLOCK, :]
    _advance(ride, RIDE_STAGES)

    mixed = jnp.concatenate(proj, axis=-1) + _dot(yds[...], w_ref[k_head:D_MIX, :])
    gate = mod_ref[0][:, 2 * D_MODEL:3 * D_MODEL]
    o_ref[0] = _residual_out(mixed, x_ref[0], gate, fg_ref, final)


def _prompt_c(x, mod_p, yacd, yb, w_out_b, final_g, sink, swa_bias, final, layer, row0, cache_args):
    nb, l, _ = x.shape
    tl = SEQ_TILE
    full = lambda *shape: pl.BlockSpec(shape, lambda n, t: (0,) * len(shape))
    c_in, c_out, c_shape = _cache_attention_specs(layer, row0, nb, l // tl, cache_args[6].shape[-1],
                                                  cache_args[8].shape[-1])
    return pl.pallas_call(
        functools.partial(_pc_kernel, final=final),
        grid=(nb, l // tl),
        in_specs=[
            pl.BlockSpec(memory_space=pltpu.SMEM),
            pl.BlockSpec((1, tl, D_MODEL), lambda n, t: (n, t, 0)),
            pl.BlockSpec((1, 1, 3 * D_MODEL), lambda n, t: (n, 0, 0)),
            pl.BlockSpec((1, tl, Y_WIDTH), lambda n, t: (n, t, 0)),
            pl.BlockSpec((1, N_PAIRS_DIL, N_CLASSES, tl // N_CLASSES, BLOCK),
                         lambda n, t: (n, 0, 0, t, 0)),
            pl.BlockSpec((None, D_MIX, D_MODEL), lambda n, t: (layer, 0, 0)),
            full(1, D_MODEL),
            full(2, 2, 2 * BLOCK, 2 * BLOCK),
        ] + c_in,
        out_specs=[pl.BlockSpec((1, tl, D_MODEL), lambda n, t: (n, t, 0))] + c_out,
        out_shape=[jax.ShapeDtypeStruct((nb, l, D_MODEL), F32)] + c_shape,
        scratch_shapes=[
            pltpu.VMEM((N_PAIRS_DIL, tl, BLOCK), F32),
            pltpu.VMEM((BLOCK + tl, W_SWA_KV), BF16),
            pltpu.VMEM((BLOCK + tl, W_SWA_KV), BF16),
            pltpu.VMEM((tl, W_SWA), BF16),
        ],
        compiler_params=_compiler_params(("arbitrary", "arbitrary")),
        name="prompt_c",
    )(sink, x, mod_p, yacd, yb, w_out_b, final_g, swa_bias, *cache_args)


def _sa_kernel(x_ref, mod_ref, g_ref, w_ref, wbd_ref, ps_ref, cw_ref, sp_ref, sc_ref,
               qkv_ref, sw_ref, yac_ref, gates_ref, pst_ref, cst_ref):
    hb = _modulated_norm(x_ref[...], g_ref[...], mod_ref[...]).astype(BF16)
    ns = hb.shape[0]

    zp = _dot(hb, w_ref[:, OFF_POOL:OFF_POOL + 2 * W_POOL])
    pu = zp[:, 0:W_POOL]
    pg = zp[:, W_POOL:2 * W_POOL]
    acc = pu
    sums = {}
    for j in range(1, 16):
        acc = acc + sp_ref[POOL_BUF - j]
        if j + 1 in POOL_WINDOWS:
            sums[j + 1] = acc
    lane = lax.broadcasted_iota(jnp.int32, (ns, W_POOL), 1)
    sel, win = _pool_select(sums, lane)
    diff = sel / win.astype(F32) - pu
    a_out = _dot(diff.astype(BF16), wbd_ref[...]) * ps_ref[...]
    yac_ref[:, 0:W_POOL] = a_out * _silu(pg)
    pst_ref[0:POOL_BUF - 1] = sp_ref[1:POOL_BUF]
    pst_ref[POOL_BUF - 1] = pu

    zc4 = _dot(hb, w_ref[:, OFF_CONV:OFF_CONV + 4 * W_CONV])
    ch = zc4[:, 0:W_CONV]
    cb = zc4[:, W_CONV:2 * W_CONV]
    cc = zc4[:, 2 * W_CONV:3 * W_CONV]
    cg = zc4[:, 3 * W_CONV:4 * W_CONV]
    zc = cc * ch
    conv = (cw_ref[0:1, :] * sc_ref[:, 0:W_CONV] + cw_ref[1:2, :] * sc_ref[:, W_CONV:2 * W_CONV]
            + cw_ref[2:3, :] * zc)
    yac_ref[:, W_POOL:W_POOL + W_CONV] = cb * conv * _silu(cg)
    cst_ref[:, 0:W_CONV] = sc_ref[:, W_CONV:2 * W_CONV]
    cst_ref[:, W_CONV:2 * W_CONV] = zc

    zd = _dot(hb, w_ref[:, OFF_DIL:OFF_DIL + 4 * W_DIL])
    qkv_ref[:, 0:W_DIL] = zd[:, 0:W_DIL] * QK_SCALE
    qkv_ref[:, W_DIL:3 * W_DIL] = zd[:, W_DIL:3 * W_DIL]
    gates_ref[:, 0:W_DIL] = _silu(zd[:, 3 * W_DIL:4 * W_DIL])

    zs = _dot(hb, w_ref[:, OFF_SWA:OFF_SWA + 3 * W_SWA])
    sw_ref[:, 0:W_SWA] = zs[:, 0:W_SWA] * QK_SCALE
    sw_ref[:, W_SWA:2 * W_SWA] = zs[:, W_SWA:2 * W_SWA]
    gates_ref[:, W_DIL:W_DIL + W_SWA] = _silu(zs[:, 2 * W_SWA:3 * W_SWA])


def _whole_spec(shape):
    return pl.BlockSpec(shape, lambda *_: (0,) * len(shape))


def _layer_spec(shape, layer):
    return pl.BlockSpec((None,) + tuple(shape[1:]), lambda *_: (layer,) + (0,) * (len(shape) - 1))


def _sample_a(xs, mod_s, norm_g, w_in_b, wbd, pool_scale, conv_w, sp, sc, layer):
    ns = xs.shape[0]
    shapes = [
        jax.ShapeDtypeStruct((ns, 3 * W_DIL), F32),
        jax.ShapeDtypeStruct((ns, 2 * W_SWA), F32),
        jax.ShapeDtypeStruct((ns, W_POOL + W_CONV), F32),
        jax.ShapeDtypeStruct((ns, W_DIL + W_SWA), F32),
        jax.ShapeDtypeStruct((POOL_BUF, ns, W_POOL), F32),
        jax.ShapeDtypeStruct((ns, CONV_BUF * W_CONV), F32),
    ]
    args = (xs, mod_s, norm_g, w_in_b, wbd, pool_scale, conv_w, sp, sc)
    in_specs = [_whole_spec(a.shape) for a in args]
    in_specs[3] = _layer_spec(w_in_b.shape, layer)
    return pl.pallas_call(
        _sa_kernel,
        grid=(1,),
        in_specs=in_specs,
        out_specs=[_whole_spec(s.shape) for s in shapes],
        out_shape=shapes,
        compiler_params=_compiler_params(("arbitrary",)),
        name="sample_a",
    )(*args)


def _cache_attention(dq_ref, sq_ref, dbias_ref, mult_ref, sbias_ref, sink_ref, kt_ref, vt_ref,
                     ck_ref, cv_ref, bo_ref, do_ref):
    row_d = lax.broadcasted_iota(jnp.int32, (8, DIL_MAX), 0)
    row_o = lax.broadcasted_iota(jnp.int32, (8, HEAD_DIM), 0)
    row_s = lax.broadcasted_iota(jnp.int32, (8, SWA_WINDOW), 0)
    eye = (lax.broadcasted_iota(jnp.int32, (HEAD_DIM, HEAD_DIM), 0)
           == lax.broadcasted_iota(jnp.int32, (HEAD_DIM, HEAD_DIM), 1))
    for j in range(DEC_TILE):
        q = dq_ref[j, 0]
        s = jnp.zeros((8, DIL_MAX), F32)
        for h in range(N_HEADS_DIL):
            q_col = jnp.sum(jnp.where(eye, q[h:h + 1, :], 0.0), axis=1, keepdims=True)
            s_h = jnp.sum(kt_ref[0, j, h] * q_col, axis=0, keepdims=True)
            s = jnp.where(row_d == h, s_h, s)
            yield
        s = s + dbias_ref[...]
        s_self = jnp.sum(q * dq_ref[j, 1], axis=-1, keepdims=True)
        m = jnp.maximum(jnp.max(s, axis=-1, keepdims=True), s_self)
        p = jnp.exp(s - m) * mult_ref[...]
        p_self = float(len(DIL_CONFIGS)) * jnp.exp(s_self - m)
        den = jnp.sum(p, axis=-1, keepdims=True) + p_self
        yield
        acc = jnp.zeros((8, HEAD_DIM), F32)
        for h in range(N_HEADS_DIL):
            o_col = jnp.sum(vt_ref[0, j, h] * p[h:h + 1, :], axis=1, keepdims=True)
            o_row = jnp.sum(jnp.where(eye, o_col, 0.0), axis=0, keepdims=True)
            acc = jnp.where(row_o == h, o_row, acc)
            yield
        bo_ref[j] = (acc + p_self * dq_ref[j, 2]) / den

        q = sq_ref[j, 0]
        qb = q.astype(BF16)
        s = jnp.where(row_s < 2, _dot(qb, ck_ref[0, j, 0].astype(BF16)),
                      _dot(qb, ck_ref[0, j, 1].astype(BF16))) + sbias_ref[...]
        w_self = jnp.sum(q * sq_ref[j, 1], axis=-1, keepdims=True)
        sink = sink_ref[...][:, 0:1]
        m = jnp.maximum(jnp.maximum(jnp.max(s, axis=-1, keepdims=True), w_self), sink)
        p = jnp.exp(s - m)
        pw = jnp.exp(w_self - m)
        den = jnp.sum(p, axis=-1, keepdims=True) + pw + jnp.exp(sink - m)
        pb = p.astype(BF16)
        yield
        acc = jnp.where(row_o < 2, _dot_nt(pb, cv_ref[0, j, 0].astype(BF16)),
                        _dot_nt(pb, cv_ref[0, j, 1].astype(BF16)))
        do_ref[j] = (acc + pw * sq_ref[j, 2]) / den
        yield


def _advance(stages, n):
    for _ in range(n):
        next(stages, None)


def _sc_kernel(x_ref, mod_ref, yac_ref, gates_ref, bo_ref, do_ref, w_ref, fg_ref, o_ref, *, final):
    yac = yac_ref[...]
    gates = gates_ref[...]
    y = jnp.concatenate([yac[:, 0:W_POOL], bo_ref[...] * gates[:, 0:W_DIL],
                         yac[:, W_POOL:W_POOL + W_CONV], do_ref[...] * gates[:, W_DIL:W_DIL + W_SWA]],
                        axis=-1).astype(BF16)
    gate = mod_ref[...][:, 2 * D_MODEL:3 * D_MODEL]
    o_ref[...] = _mix_out(y, x_ref[...], gate, w_ref, fg_ref, final)


def _sample_c(xs, mod_s, yac, gates, bo, do, w_out_b, final_g, final, layer):
    args = (xs, mod_s, yac, gates, bo, do, w_out_b, final_g)
    in_specs = [_whole_spec(a.shape) for a in args]
    in_specs[6] = _layer_spec(w_out_b.shape, layer)
    return pl.pallas_call(
        functools.partial(_sc_kernel, final=final),
        grid=(1,),
        in_specs=in_specs,
        out_specs=_whole_spec(xs.shape),
        out_shape=jax.ShapeDtypeStruct(xs.shape, F32),
        compiler_params=_compiler_params(("arbitrary",)),
        name="sample_c",
    )(*args)


def _band_bias_dil4(slope_lo, slope_hi, variant):
    sub = BLOCK // 4
    qidx = np.arange(BLOCK)[:, None]
    kidx = np.arange(2 * BLOCK)[None, :]
    q_step = 4 * (qidx % sub + (sub if variant == 0 else 0)) + qidx // sub
    k_step = 4 * (kidx % (2 * sub)) + kidx // (2 * sub)
    off = q_step - k_step
    valid = (off >= 0) & (off <= BLOCK)
    out = []
    for s in (slope_lo, slope_hi):
        out.append(np.where(valid, -(np.float32(s) * np.float32(4)) * off.astype(np.float32),
                            np.float32(NEG_INF)).astype(np.float32))
    return np.concatenate(out, axis=0)


def _prompt_bias_tables():
    dil = _alibi_slopes(N_HEADS_DIL)

    def table(hp, d, var):
        if d == 4:
            return _band_bias_dil4(dil[2 * hp], dil[2 * hp + 1], var)
        return _band_bias(dil[2 * hp], dil[2 * hp + 1], d, var)

    dil_bias = np.stack([
        np.stack([np.stack([table(hp, d, var) for var in (0, 1)])
                  for _, d in DIL_CONFIGS]) for hp in range(N_PAIRS_DIL)])
    swa = _alibi_slopes(N_HEADS_SWA)
    swa_bias = np.stack([
        np.stack([_band_bias(swa[SWA_HEAD_PERM[2 * g]], swa[SWA_HEAD_PERM[2 * g + 1]], 1, var)
                  for var in (0, 2)]) for g in range(2)])
    return jnp.asarray(dil_bias), jnp.asarray(swa_bias)


def _sample_bias_tables(r):
    dil = _alibi_slopes(N_HEADS_DIL)
    dist = (r - np.arange(r)).astype(np.float32)
    mult = np.zeros((1, r), np.float32)
    for window, d in DIL_CONFIGS:
        mult[0] += ((dist <= window) & (dist % d == 0)).astype(np.float32)
    dbias = np.zeros((8, r), np.float32)
    for h in range(N_HEADS_DIL):
        dbias[h] = np.where(mult[0] > 0, -np.float32(dil[h]) * dist, np.float32(NEG_INF))
    swa = _alibi_slopes(N_HEADS_SWA)
    sdist = (SWA_WINDOW - np.arange(SWA_WINDOW)).astype(np.float32)
    sbias = np.zeros((8, SWA_WINDOW), np.float32)
    for h in range(N_HEADS_SWA):
        sbias[h] = -np.float32(swa[h]) * sdist
    return jnp.asarray(dbias), jnp.asarray(mult), jnp.asarray(sbias)


def kernel(x_prompt, x_sample, c_prompt, c_sample, state_pool, cache_dil_k, cache_dil_v, state_conv, cache_swa_k, cache_swa_v, norm_g, w_ada, b_ada, w_in, w_pool, pool_scale, conv_w, swa_sink, w_out, final_g):
    depth = w_in.shape[0]
    nb, l, _ = x_prompt.shape
    ns = x_sample.shape[0]
    assert x_sample.shape[1] == 1 and l % CHUNK == 0 and l >= DIL_MAX
    assert ns == 2 * nb * (l // SEQ_TILE) * DEC_TILE
    assert cache_dil_k.shape[2] == DIL_MAX and cache_swa_k.shape[2] == SWA_WINDOW

    w_in_b = w_in.astype(BF16)
    w_out_b = w_out.astype(BF16)
    eye = jnp.eye(len(POOL_WINDOWS), dtype=F32)
    wbd = jnp.einsum("dgce,gh->dgche", w_pool, eye).reshape(depth, W_POOL, W_POOL).astype(BF16)
    sink_rows = jnp.zeros((depth, 8), F32).at[:, 0:N_HEADS_SWA].set(swa_sink)
    sink_rows = jnp.broadcast_to(sink_rows[:, :, None], (depth, 8, BLOCK))

    dil_bias, swa_bias = _prompt_bias_tables()
    dbias, mult, sbias = _sample_bias_tables(cache_dil_k.shape[2])

    mod = _ada(jnp.concatenate([c_prompt, c_sample], axis=0), w_ada, b_ada)
    fg = final_g.reshape(1, D_MODEL)

    xp = x_prompt
    xs = x_sample.reshape(ns, D_MODEL)
    kt = jnp.transpose(cache_dil_k, (0, 1, 3, 4, 2))
    vt = jnp.transpose(cache_dil_v, (0, 1, 3, 4, 2))
    ckt = jnp.transpose(cache_swa_k, (0, 1, 3, 4, 2))
    cvt = jnp.transpose(cache_swa_v, (0, 1, 3, 4, 2))
    sp_all = jnp.transpose(state_pool, (0, 2, 1, 3))
    sc_all = state_conv.reshape(depth, ns, CONV_BUF * W_CONV)
    pad_heads = lambda a: jnp.pad(a, ((0, 0), (0, 0), (0, 8 - a.shape[2]), (0, 0)))
    outs = [[] for _ in range(12)]
    carried = None
    for i in range(depth):
        final = i == depth - 1
        mod_p = mod[i, 0:nb].reshape(nb, 1, 3 * D_MODEL)
        mod_s = mod[i, nb:nb + ns]
        g = norm_g[i].reshape(1, D_MODEL)
        ps = pool_scale[i].reshape(1, W_POOL)

        qkv, sw, yac, gates, pst_s, cst_s = _sample_a(
            xs, mod_s, g, w_in_b, wbd[i], ps, conv_w[i], sp_all[i], sc_all[i], i)
        dq3 = pad_heads(qkv.reshape(ns, 3, N_HEADS_DIL, HEAD_DIM))
        sq4 = sw[:, 0:W_SWA].reshape(ns, 1, N_HEADS_SWA, HEAD_DIM)
        skv = jnp.repeat(sw[:, W_SWA:W_SWA + 2 * W_SWA_KV].reshape(ns, 2, 2, HEAD_DIM), 2, axis=2)
        sq3 = pad_heads(jnp.concatenate([sq4, skv], axis=1))

        cache_args = (dq3, sq3, dbias, mult, sbias, sink_rows[i], kt, vt, ckt, cvt)
        q1, k1, v1, qm, km, vm, gm, kc, vc, yacd, pst, cst, skc, svc, bo_a, do_a = _prompt_a(
            xp, mod_p, g, w_in_b, wbd[i], ps, conv_w[i], i, depth, 0, cache_args, carried)
        carried = (kc, vc)
        yb = _prompt_b(q1, k1, v1, qm, km, vm, gm, dil_bias)
        xp, bo_c, do_c = _prompt_c(xp, mod_p, yacd, yb, w_out_b, fg, swa_sink[i], swa_bias, final,
                                   i, ns // 2, cache_args)

        bo = jnp.concatenate([bo_a, bo_c], axis=0)[:, 0:N_HEADS_DIL].reshape(ns, W_DIL)
        do = jnp.concatenate([do_a, do_c], axis=0)[:, 0:N_HEADS_SWA].reshape(ns, W_SWA)
        xs = _sample_c(xs, mod_s, yac, gates, bo, do, w_out_b, fg, final, i)

        unfold = lambda a, h: jnp.transpose(a.reshape(nb, h, HEAD_DIM, a.shape[-1]), (0, 3, 1, 2))
        skc, svc = unfold(skc, 2), unfold(svc, 2)
        outs[0].append(pst)
        outs[1].append(jnp.transpose(pst_s, (1, 0, 2)))
        outs[4].append(qkv[:, W_DIL:2 * W_DIL].reshape(ns, 1, N_HEADS_DIL, HEAD_DIM))
        outs[5].append(qkv[:, 2 * W_DIL:3 * W_DIL].reshape(ns, 1, N_HEADS_DIL, HEAD_DIM))
        outs[6].append(cst)
        outs[7].append(cst_s.reshape(ns, CONV_BUF, W_CONV))
        outs[8].append(skc)
        outs[9].append(svc)
        outs[10].append(sw[:, W_SWA:W_SWA + W_SWA_KV].reshape(ns, 1, 2, HEAD_DIM))
        outs[11].append(sw[:, W_SWA + W_SWA_KV:W_SWA + 2 * W_SWA_KV].reshape(ns, 1, 2, HEAD_DIM))

    for k, cache in zip((2, 3), carried):
        outs[k] = jnp.transpose(cache.reshape(depth, nb, N_HEADS_DIL, HEAD_DIM, DIL_MAX), (0, 1, 4, 2, 3))
    return (xp, xs.reshape(ns, 1, D_MODEL)) + tuple(
        o if not isinstance(o, list) else jnp.stack(o) for o in outs)
```

```python
import functools
import math

import numpy as np
import jax
import jax.numpy as jnp
from jax import lax
from jax.experimental import pallas as pl
from jax.experimental.pallas import tpu as pltpu

F32 = jnp.float32
BF16 = jnp.bfloat16

D_MODEL = 1024
HEAD_DIM = 64
BLOCK = 128
POOL_WINDOWS = (2, 4, 8, 16)
POOL_GROUP = 64
W_POOL = 256
POOL_BUF = 15
DIL_CONFIGS = ((128, 1), (512, 4), (2048, 16))
DIL_MAX = 2048
N_HEADS_DIL = 6
N_PAIRS_DIL = 3
N_CLASSES = 16
W_DIL = 384
W_CONV = 256
CONV_BUF = 2
N_HEADS_SWA = 4
W_SWA = 256
W_SWA_KV = 128
SWA_WINDOW = 128
D_MIX = 1152
D_PROJ = 3840
RMS_EPS = 1e-6
QK_SCALE = 1.0 / math.sqrt(HEAD_DIM)

OFF_POOL = 0
OFF_DIL = 512
OFF_CONV = 2048
OFF_SWA = 3072
Y_WIDTH = W_POOL + W_CONV + 3 * W_SWA
SWA_HEAD_PERM = (0, 3, 1, 2)

SEQ_TILE = 512
CHUNK = 2048
BAND_SKEW = 1
DEC_TILE = 1
RIDE_STAGES = 15 * DEC_TILE
VMEM_LIMIT = 56 * 1024 * 1024

NEG_INF = float("-inf")


def _silu(v):
    return v * jax.nn.sigmoid(v)


def _dot(a, b):
    return jnp.dot(a, b, preferred_element_type=F32)


def _dot_nt(a, b):
    return lax.dot_general(a, b, (((1,), (1,)), ((), ())), preferred_element_type=F32)


def _alibi_slopes(n):
    return [2.0 ** (-8.0 * (h + 1) / n) for h in range(n)]


def _band_bias(slope_lo, slope_hi, dist_scale, variant):
    qi = np.arange(BLOCK)[:, None]
    kj = np.arange(2 * BLOCK)[None, :]
    if variant == 1:
        off = qi - kj
    else:
        off = qi - kj + BLOCK
    valid = (off >= 0) & (off <= BLOCK)
    if variant == 2:
        valid = valid & (kj >= BLOCK)
    out = []
    for s in (slope_lo, slope_hi):
        b = np.where(valid, -(np.float32(s) * np.float32(dist_scale)) * off.astype(np.float32),
                     np.float32(NEG_INF))
        out.append(b.astype(np.float32))
    return np.concatenate(out, axis=0)


def _compiler_params(sem):
    return pltpu.CompilerParams(dimension_semantics=sem, vmem_limit_bytes=VMEM_LIMIT)


def _ada_kernel(c_ref, w_ref, b_ref, o_ref):
    s = _silu(c_ref[...]).astype(BF16)
    o_ref[0] = _dot(s, w_ref[0].astype(BF16)) + b_ref[0]


def _ada(c_all, w_ada_b, b_ada):
    depth = w_ada_b.shape[0]
    rows = c_all.shape[0]
    return pl.pallas_call(
        _ada_kernel,
        grid=(depth, 3),
        in_specs=[
            pl.BlockSpec((rows, D_MODEL), lambda i, j: (0, 0)),
            pl.BlockSpec((1, D_MODEL, D_MODEL), lambda i, j: (i, 0, j)),
            pl.BlockSpec((1, 1, D_MODEL), lambda i, j: (i, 0, j)),
        ],
        out_specs=pl.BlockSpec((1, rows, D_MODEL), lambda i, j: (i, 0, j)),
        out_shape=jax.ShapeDtypeStruct((depth, rows, 3 * D_MODEL), F32),
        compiler_params=_compiler_params(("arbitrary", "arbitrary")),
        name="ada",
    )(c_all, w_ada_b, b_ada.reshape(depth, 1, 3 * D_MODEL))


def _modulated_norm(x, g, mod_row):
    ms = jnp.mean(x * x, axis=-1, keepdims=True)
    shift = mod_row[:, 0:D_MODEL]
    scale = mod_row[:, D_MODEL:2 * D_MODEL]
    return x * lax.rsqrt(ms + RMS_EPS) * (g * (1.0 + scale)) + shift


def _pair_heads(v):
    left, right = v[:, 0:BLOCK], v[:, BLOCK:2 * BLOCK]
    low = lax.broadcasted_iota(jnp.int32, left.shape, 1) < HEAD_DIM
    return jnp.concatenate([jnp.where(low, left, right),
                            pltpu.roll(jnp.where(low, right, left), HEAD_DIM, axis=1)], axis=-1)


def _unpair_heads(a, b):
    low = lax.broadcasted_iota(jnp.int32, a.shape, 1) < HEAD_DIM
    b_swapped = pltpu.roll(b, HEAD_DIM, axis=1)
    return jnp.where(low, a, b_swapped), jnp.where(low, b_swapped, a)


def _pool_select(sums, lane):
    grp = lane // POOL_GROUP
    sel = jnp.where(grp == 0, sums[2], jnp.where(grp == 1, sums[4],
                                                 jnp.where(grp == 2, sums[8], sums[16])))
    win = jnp.where(grp == 0, 2, jnp.where(grp == 1, 4, jnp.where(grp == 2, 8, 16)))
    return sel, win


def _pa_kernel(x_ref, mod_ref, g_ref, w_ref, wbd_ref, ps_ref, cw_ref,
               dq_ref, sq_ref, dbias_ref, mult_ref, sbias_ref, ssink_ref, kt_ref, vt_ref, ck_ref, cv_ref,
               q1_ref, k1_ref, v1_ref, qm_ref, km_ref, vm_ref, gm_ref, kc_ref, vc_ref, y_ref,
               pst_ref, cst_ref, skc_ref, svc_ref, bo_ref, do_ref,
               uext, zcext, zds, z4s, *, later_layers=0):
    tl = SEQ_TILE
    t = pl.program_id(1)

    @pl.when(t == 0)
    def _():
        for k in range(len(POOL_WINDOWS)):
            uext[k, 0:16, :] = jnp.zeros((16, W_POOL), F32)
        zcext[0:8, :] = jnp.zeros((8, W_CONV), F32)

    ride = _cache_attention(dq_ref, sq_ref, dbias_ref, mult_ref, sbias_ref, ssink_ref, kt_ref, vt_ref,
                            ck_ref, cv_ref, bo_ref, do_ref)

    _advance(ride, 3)
    hb = _modulated_norm(x_ref[0], g_ref[...], mod_ref[0]).astype(BF16)

    zp = _dot(hb, w_ref[:, OFF_POOL:OFF_POOL + 2 * W_POOL])
    zd = _dot(hb, w_ref[:, OFF_DIL:OFF_DIL + 4 * W_DIL])
    _advance(ride, 3)

    pu = zp[:, 0:W_POOL]
    pg = zp[:, W_POOL:2 * W_POOL]
    uext[0, 16:16 + tl, :] = pu
    sums = {}
    level = pu
    for k, w in enumerate(POOL_WINDOWS):
        shift = w // 2
        level = level + uext[k, 16 - shift:16 - shift + tl, :]
        sums[w] = level
        if k + 1 < len(POOL_WINDOWS):
            uext[k + 1, 16:16 + tl, :] = level
    lane = lax.broadcasted_iota(jnp.int32, (tl, W_POOL), 1)
    gpos = lax.broadcasted_iota(jnp.int32, (tl, W_POOL), 0) + t * tl
    sel, win = _pool_select(sums, lane)
    cnt = jnp.minimum(gpos + 1, win).astype(F32)
    diff = sel / cnt - pu
    a_out = _dot(diff.astype(BF16), wbd_ref[...]) * ps_ref[...]
    y_ref[0, :, 0:W_POOL] = (a_out * _silu(pg)).astype(BF16)
    pst_ref[0] = uext[0, tl + 1:tl + 16, :]
    for k in range(len(POOL_WINDOWS)):
        uext[k, 0:16, :] = uext[k, tl:tl + 16, :]

    zs = _dot(hb, w_ref[:, OFF_SWA:OFF_SWA + 3 * W_SWA])
    _advance(ride, 3)
    kc_ref[0, 0] = zd[:, W_DIL:2 * W_DIL].T
    vc_ref[0, 0] = zd[:, 2 * W_DIL:3 * W_DIL].T
    for d in range(1, 1 + later_layers):
        kc_ref[d, 0] = jnp.zeros((W_DIL, tl), F32)
        vc_ref[d, 0] = jnp.zeros((W_DIL, tl), F32)
    for hp in range(N_PAIRS_DIL):
        _advance(ride, 1)
        lo = hp * BLOCK
        zds[0] = zd[:, lo:lo + BLOCK] * QK_SCALE
        zds[1] = zd[:, W_DIL + lo:W_DIL + lo + BLOCK]
        zds[2] = zd[:, 2 * W_DIL + lo:2 * W_DIL + lo + BLOCK]
        zds[3] = _silu(zd[:, 3 * W_DIL + lo:3 * W_DIL + lo + BLOCK])
        q1_ref[0, hp] = zds[0].astype(BF16)
        k1_ref[0, hp] = zds[1].astype(BF16)
        v1_ref[0, hp] = zds[2].astype(BF16)
        quarter = tl // 4
        for a in range(4):
            for blk in range(4):
                z4s[blk, a] = zds[blk, pl.ds(a, quarter, stride=4), :]
        for b in range(4):
            for a in range(4):
                c = 4 * b + a
                rows = pl.ds(b, tl // N_CLASSES, stride=4)
                qm_ref[0, hp, c] = z4s[0, a, rows, :].astype(BF16)
                km_ref[0, hp, c] = z4s[1, a, rows, :].astype(BF16)
                vm_ref[0, hp, c] = z4s[2, a, rows, :].astype(BF16)
                gm_ref[0, hp, c] = z4s[3, a, rows, :].astype(BF16)

    zc4 = _dot(hb, w_ref[:, OFF_CONV:OFF_CONV + 4 * W_CONV])
    _advance(ride, 3)

    sk = zs[:, W_SWA:W_SWA + W_SWA_KV]
    sv = zs[:, W_SWA + W_SWA_KV:W_SWA + 2 * W_SWA_KV]
    skc_ref[0] = sk[tl - SWA_WINDOW:tl, :].T
    svc_ref[0] = sv[tl - SWA_WINDOW:tl, :].T
    c0 = W_POOL + W_CONV
    y_ref[0, :, c0:c0 + W_SWA] = _pair_heads(zs[:, 0:W_SWA] * QK_SCALE).astype(BF16)
    y_ref[0, :, c0 + W_SWA:c0 + 2 * W_SWA] = zs[:, W_SWA:2 * W_SWA].astype(BF16)
    y_ref[0, :, c0 + 2 * W_SWA:c0 + 3 * W_SWA] = _pair_heads(_silu(zs[:, 2 * W_SWA:3 * W_SWA])).astype(BF16)
    _advance(ride, RIDE_STAGES)

    ch = zc4[:, 0:W_CONV]
    cb = zc4[:, W_CONV:2 * W_CONV]
    cc = zc4[:, 2 * W_CONV:3 * W_CONV]
    cg = zc4[:, 3 * W_CONV:4 * W_CONV]
    zc = cc * ch
    zcext[8:8 + tl, :] = zc
    conv = (cw_ref[0:1, :] * zcext[6:6 + tl, :] + cw_ref[1:2, :] * zcext[7:7 + tl, :]
            + cw_ref[2:3, :] * zc)
    y_ref[0, :, W_POOL:W_POOL + W_CONV] = (cb * conv * _silu(cg)).astype(BF16)
    cst_ref[0] = zcext[tl + 6:tl + 8, :]
    zcext[0:8, :] = zcext[tl:tl + 8, :]


def _cache_attention_specs(layer, row0, n_rows, step_block, r, swr):
    bt = DEC_TILE
    blk0 = row0 // bt
    full = lambda *shape: pl.BlockSpec(shape, lambda *g: (0,) * len(shape))
    srow_spec = pl.BlockSpec((bt, 3, 8, HEAD_DIM), lambda *g: (blk0 + step_block(*g), 0, 0, 0))
    dil_spec = pl.BlockSpec((1, bt, N_HEADS_DIL, HEAD_DIM, r),
                            lambda *g: (layer, blk0 + step_block(*g), 0, 0, 0))
    swa_spec = pl.BlockSpec((1, bt, 2, HEAD_DIM, swr), lambda *g: (layer, blk0 + step_block(*g), 0, 0, 0))
    sout_spec = pl.BlockSpec((bt, 8, HEAD_DIM), lambda *g: (step_block(*g), 0, 0))
    sout_shape = jax.ShapeDtypeStruct((n_rows, 8, HEAD_DIM), F32)
    in_specs = [srow_spec, srow_spec, full(8, r), full(1, r), full(8, swr), full(8, BLOCK),
                dil_spec, dil_spec, swa_spec, swa_spec]
    return in_specs, [sout_spec, sout_spec], [sout_shape, sout_shape]


N_PA_INPUTS = 17


def _pa_kernel_carry(*refs):
    _pa_kernel(*refs[:N_PA_INPUTS], *refs[N_PA_INPUTS + 2:])


def _prompt_a(x, mod_p, norm_g, w_in_b, wbd, pool_scale, conv_w, layer, depth, row0, cache_args, carried):
    nb, l, _ = x.shape
    tl = SEQ_TILE
    nt = l // tl
    cache_t0 = (l - DIL_MAX) // tl
    c_in, c_out, c_shape = _cache_attention_specs(
        layer, row0, nb * nt * DEC_TILE, lambda n, t: n * nt + t,
        cache_args[6].shape[-1], cache_args[8].shape[-1])
    pair_spec = pl.BlockSpec((1, N_PAIRS_DIL, tl, BLOCK), lambda n, t: (n, 0, t, 0))
    pair_shape = jax.ShapeDtypeStruct((nb, N_PAIRS_DIL, l, BLOCK), BF16)
    cm_spec = pl.BlockSpec((1, N_PAIRS_DIL, N_CLASSES, tl // N_CLASSES, BLOCK),
                           lambda n, t: (n, 0, 0, t, 0))
    cm_shape = jax.ShapeDtypeStruct((nb, N_PAIRS_DIL, N_CLASSES, l // N_CLASSES, BLOCK), BF16)
    cache_layers = depth if carried is None else 1
    cache_spec = pl.BlockSpec((cache_layers, 1, W_DIL, tl),
                              lambda n, t: (layer, n, 0, jnp.maximum(t - cache_t0, 0)))
    cache_shape = jax.ShapeDtypeStruct((depth, nb, W_DIL, DIL_MAX), F32)
    full = lambda *shape: pl.BlockSpec(shape, lambda n, t: (0,) * len(shape))
    in_specs = [
        pl.BlockSpec((1, tl, D_MODEL), lambda n, t: (n, t, 0)),
        pl.BlockSpec((1, 1, 3 * D_MODEL), lambda n, t: (n, 0, 0)),
        full(1, D_MODEL),
        pl.BlockSpec((None, D_MODEL, D_PROJ), lambda n, t: (layer, 0, 0), pipeline_mode=pl.Buffered(1)),
        full(W_POOL, W_POOL),
        full(1, W_POOL),
        full(3, W_CONV),
    ] + c_in
    assert len(in_specs) == N_PA_INPUTS
    extra, aliases = (), {}
    if carried is not None:
        in_specs = in_specs + [pl.BlockSpec(memory_space=pl.ANY)] * 2
        extra = tuple(carried)
        aliases = {N_PA_INPUTS: 7, N_PA_INPUTS + 1: 8}
    return pl.pallas_call(
        functools.partial(_pa_kernel, later_layers=depth - 1) if carried is None else _pa_kernel_carry,
        grid=(nb, nt),
        in_specs=in_specs,
        input_output_aliases=aliases,
        out_specs=[
            pair_spec, pair_spec, pair_spec, cm_spec, cm_spec, cm_spec, cm_spec,
            cache_spec, cache_spec,
            pl.BlockSpec((1, tl, Y_WIDTH), lambda n, t: (n, t, 0)),
            pl.BlockSpec((1, POOL_BUF, W_POOL), lambda n, t: (n, 0, 0)),
            pl.BlockSpec((1, CONV_BUF, W_CONV), lambda n, t: (n, 0, 0)),
            pl.BlockSpec((1, SWA_WINDOW, W_SWA_KV), lambda n, t: (n, 0, 0)),
            pl.BlockSpec((1, SWA_WINDOW, W_SWA_KV), lambda n, t: (n, 0, 0)),
        ] + c_out,
        out_shape=[
            pair_shape, pair_shape, pair_shape, cm_shape, cm_shape, cm_shape, cm_shape,
            cache_shape, cache_shape,
            jax.ShapeDtypeStruct((nb, l, Y_WIDTH), BF16),
            jax.ShapeDtypeStruct((nb, POOL_BUF, W_POOL), F32),
            jax.ShapeDtypeStruct((nb, CONV_BUF, W_CONV), F32),
            jax.ShapeDtypeStruct((nb, SWA_WINDOW, W_SWA_KV), F32),
            jax.ShapeDtypeStruct((nb, SWA_WINDOW, W_SWA_KV), F32),
        ] + c_shape,
        scratch_shapes=[
            pltpu.VMEM((len(POOL_WINDOWS), 16 + tl, W_POOL), F32),
            pltpu.VMEM((8 + tl, W_CONV), F32),
            pltpu.VMEM((4, tl, BLOCK), F32),
            pltpu.VMEM((4, 4, tl // 4, BLOCK), F32),
        ],
        compiler_params=_compiler_params(("arbitrary", "arbitrary")),
        name="prompt_a",
    )(x, mod_p, norm_g, w_in_b, wbd, pool_scale, conv_w, *cache_args, *extra)


def _band_scores(q, kblk, bias, lane_q):
    lhs = jnp.concatenate([jnp.where(lane_q < HEAD_DIM, q, jnp.zeros_like(q)),
                           jnp.where(lane_q >= HEAD_DIM, q, jnp.zeros_like(q))], axis=0)
    return _dot_nt(lhs, kblk) + bias


def _band_values(s, vblk, lane_q):
    m = jnp.max(s, axis=-1, keepdims=True)
    p = jnp.exp((s - m).astype(BF16))
    acc = _dot(p, jnp.concatenate([vblk, jnp.ones_like(vblk)], axis=-1))
    lo = lane_q < HEAD_DIM
    return (jnp.where(lo, acc[0:BLOCK, 0:BLOCK], acc[BLOCK:2 * BLOCK, 0:BLOCK]),
            jnp.where(lo, m[0:BLOCK], m[BLOCK:2 * BLOCK]),
            jnp.where(lo, acc[0:BLOCK, BLOCK:2 * BLOCK], acc[BLOCK:2 * BLOCK, BLOCK:2 * BLOCK]))


def _run_skewed(tasks, skew, ride):
    pending = []
    for scores_fn, finish_fn in tasks:
        pending.append((finish_fn, scores_fn()))
        if len(pending) > skew:
            fn, s = pending.pop(0)
            fn(s)
            _advance(ride, 1)
    for fn, s in pending:
        fn(s)


def _pb_kernel(bias_ref, q1, k1, v1, qm, km, vm, gm,
               dq_ref, sq_ref, dbias_ref, mult_ref, sbias_ref, ssink_ref, kt_ref, vt_ref, ck_ref, cv_ref,
               outm, bo_ref, do_ref, s1, s4, s16):
    c = pl.program_id(2)
    first = jnp.where(c == 0, 1, 0)
    lane_q = lax.broadcasted_iota(jnp.int32, (BLOCK, BLOCK), 1)
    sub = BLOCK // 4

    tasks = []

    def dil1_task(j):
        r0 = j * BLOCK
        if j == 0:
            start = jnp.maximum(c * (CHUNK // BLOCK) - 1, 0) * BLOCK
            var = first
        else:
            start = (c * (CHUNK // BLOCK) + (j - 1)) * BLOCK
            var = 0
        start = pl.multiple_of(start, BLOCK)

        def scores():
            return _band_scores(q1[0, 0, r0:r0 + BLOCK, :], k1[0, 0, pl.ds(start, 2 * BLOCK), :],
                                bias_ref[0, 0, var], lane_q)

        def finish(s):
            parts = _band_values(s, v1[0, 0, pl.ds(start, 2 * BLOCK), :], lane_q)
            for k, part in enumerate(parts):
                s1[k, r0:r0 + BLOCK, :] = part

        return scores, finish

    def dil4_task(c4, j):
        i0 = j * sub
        if j == 0:
            istart = jnp.maximum(c * (CHUNK // N_CLASSES) - sub, 0)
            var = first
        else:
            istart = c * (CHUNK // N_CLASSES) + i0 - sub
            var = 0
        istart = pl.multiple_of(istart, sub)
        classes = [4 * cc + c4 for cc in range(4)]

        def scores():
            q = jnp.concatenate([qm[0, 0, cl, i0:i0 + sub, :] for cl in classes], axis=0)
            kblk = jnp.concatenate([km[0, 0, cl, pl.ds(istart, 2 * sub), :] for cl in classes], axis=0)
            return _band_scores(q, kblk, bias_ref[0, 1, var], lane_q)

        def finish(s):
            vblk = jnp.concatenate([vm[0, 0, cl, pl.ds(istart, 2 * sub), :] for cl in classes], axis=0)
            parts = _band_values(s, vblk, lane_q)
            for k, part in enumerate(parts):
                for cc, cl in enumerate(classes):
                    s4[k, cl, i0:i0 + sub, :] = part[cc * sub:(cc + 1) * sub]

        return scores, finish

    start16 = pl.multiple_of(jnp.maximum(c - 1, 0) * BLOCK, BLOCK)

    def dil16_task(cl):
        def scores():
            return _band_scores(qm[0, 0, cl], km[0, 0, cl, pl.ds(start16, 2 * BLOCK), :],
                                bias_ref[0, 2, first], lane_q)

        def finish(s):
            parts = _band_values(s, vm[0, 0, cl, pl.ds(start16, 2 * BLOCK), :], lane_q)
            for k, part in enumerate(parts):
                s16[k, cl] = part

        return scores, finish

    tasks += [dil1_task(j) for j in range(CHUNK // BLOCK)]
    tasks += [dil4_task(c4, j) for c4 in range(4) for j in range(4)]
    tasks += [dil16_task(cl) for cl in range(N_CLASSES)]
    ride = _cache_attention(dq_ref, sq_ref, dbias_ref, mult_ref, sbias_ref, ssink_ref, kt_ref, vt_ref,
                            ck_ref, cv_ref, bo_ref, do_ref)
    _run_skewed(tasks, BAND_SKEW, ride)
    _advance(ride, RIDE_STAGES)

    for cl in range(N_CLASSES):
        rows = pl.ds(cl, BLOCK, stride=N_CLASSES)
        ma, mb, mc = s1[1, rows, :], s4[1, cl], s16[1, cl]
        mx = jnp.maximum(jnp.maximum(ma, mb), mc)
        wa = jnp.exp(ma - mx)
        wb = jnp.exp(mb - mx)
        wc = jnp.exp(mc - mx)
        num = wa * s1[0, rows, :] + wb * s4[0, cl] + wc * s16[0, cl]
        den = wa * s1[2, rows, :] + wb * s4[2, cl] + wc * s16[2, cl]
        outm[0, 0, cl] = (num / den * gm[0, 0, cl].astype(F32)).astype(BF16)


def _prompt_b(q1, k1, v1, qm, km, vm, gm, dil_bias, layer, row0, cache_args):
    nb, npair, l, _ = q1.shape
    nc = l // CHUNK
    li = l // N_CLASSES
    ci = CHUNK // N_CLASSES
    c_in, c_out, c_shape = _cache_attention_specs(
        layer, row0, nb * npair * nc * DEC_TILE, lambda n, h, c: (n * npair + h) * nc + c,
        cache_args[6].shape[-1], cache_args[8].shape[-1])
    q1_spec = pl.BlockSpec((1, 1, CHUNK, BLOCK), lambda n, h, c: (n, h, c, 0))
    kv1_spec = pl.BlockSpec((1, 1, l, BLOCK), lambda n, h, c: (n, h, 0, 0))
    cm_spec = pl.BlockSpec((1, 1, N_CLASSES, ci, BLOCK), lambda n, h, c: (n, h, 0, c, 0))
    kvm_spec = pl.BlockSpec((1, 1, N_CLASSES, li, BLOCK), lambda n, h, c: (n, h, 0, 0, 0))
    cm_scratch = pltpu.VMEM((3, N_CLASSES, ci, BLOCK), F32)
    return pl.pallas_call(
        _pb_kernel,
        grid=(nb, npair, nc),
        in_specs=[pl.BlockSpec((1, 3, 2, 2 * BLOCK, 2 * BLOCK), lambda n, h, c: (h, 0, 0, 0, 0)),
                  q1_spec, kv1_spec, kv1_spec, cm_spec, kvm_spec, kvm_spec, cm_spec] + c_in,
        out_specs=[cm_spec] + c_out,
        out_shape=[jax.ShapeDtypeStruct((nb, npair, N_CLASSES, li, BLOCK), BF16)] + c_shape,
        scratch_shapes=[
            pltpu.VMEM((3, CHUNK, BLOCK), F32), cm_scratch, cm_scratch,
        ],
        compiler_params=_compiler_params(("arbitrary", "arbitrary", "arbitrary")),
        name="prompt_b",
    )(dil_bias, q1, k1, v1, qm, km, vm, gm, *cache_args)


def _mix_out(y, x, gate, w_ref, fg_ref, final):
    return _residual_out(_dot(y, w_ref[...]), x, gate, fg_ref, final)


def _residual_out(mixed, x, gate, fg_ref, final):
    xn = x + gate * mixed
    if final:
        ms = jnp.mean(xn * xn, axis=-1, keepdims=True)
        xn = xn * lax.rsqrt(ms + RMS_EPS) * fg_ref[...]
    return xn


def _pc_kernel(sink_ref, x_ref, mod_ref, yacd_ref, yb_ref, w_ref, fg_ref, bias_ref,
               dq_ref, sq_ref, dbias_ref, mult_ref, sbias_ref, ssink_ref, kt_ref, vt_ref, ck_ref, cv_ref,
               o_ref, bo_ref, do_ref, ybs, kext, vext, yds, *, final, ride_tiles):
    tl = SEQ_TILE
    t = pl.program_id(1)

    @pl.when(t == 0)
    def _():
        kext[0:BLOCK, :] = jnp.zeros((BLOCK, W_SWA_KV), BF16)
        vext[0:BLOCK, :] = jnp.zeros((BLOCK, W_SWA_KV), BF16)

    for hp in range(N_PAIRS_DIL):
        for c in range(N_CLASSES):
            ybs[hp, pl.ds(c, tl // N_CLASSES, stride=N_CLASSES), :] = yb_ref[0, hp, c].astype(F32)
    k_head = D_MIX - W_SWA
    y_head = jnp.concatenate([yacd_ref[0, :, 0:W_POOL], ybs[0].astype(BF16), ybs[1].astype(BF16),
                              ybs[2].astype(BF16), yacd_ref[0, :, W_POOL:W_POOL + W_CONV]], axis=-1)
    n_tile = D_MODEL // (tl // BLOCK)
    proj = []
    @pl.when(t < ride_tiles)
    def _():
        _advance(_cache_attention(dq_ref, sq_ref, dbias_ref, mult_ref, sbias_ref, ssink_ref, kt_ref,
                                  vt_ref, ck_ref, cv_ref, bo_ref, do_ref), RIDE_STAGES)

    c0 = W_POOL + W_CONV
    kext[BLOCK:BLOCK + tl, :] = yacd_ref[0, :, c0 + W_SWA:c0 + W_SWA + W_SWA_KV]
    vext[BLOCK:BLOCK + tl, :] = yacd_ref[0, :, c0 + W_SWA + W_SWA_KV:c0 + 2 * W_SWA]
    first = jnp.where(t == 0, 1, 0)
    lane_q = lax.broadcasted_iota(jnp.int32, (BLOCK, BLOCK), 1)
    row_s = lax.broadcasted_iota(jnp.int32, (2 * BLOCK, 1), 0)
    for jb in range(tl // BLOCK):
        proj.append(_dot(y_head, w_ref[0:k_head, jb * n_tile:(jb + 1) * n_tile]))
        r0 = jb * BLOCK
        kblk = kext[r0:r0 + 2 * BLOCK, :]
        vblk = vext[r0:r0 + 2 * BLOCK, :]
        gated = []
        for grp in range(2):
            q = yacd_ref[0, r0:r0 + BLOCK, c0 + grp * BLOCK:c0 + (grp + 1) * BLOCK]
            lhs = jnp.concatenate([jnp.where(lane_q < HEAD_DIM, q, jnp.zeros_like(q)),
                                   jnp.where(lane_q >= HEAD_DIM, q, jnp.zeros_like(q))], axis=0)
            s = _dot_nt(lhs, kblk)
            if jb == 0:
                s = s + bias_ref[grp, first]
            else:
                s = s + bias_ref[grp, 0]
            sink = jnp.where(row_s < BLOCK, sink_ref[SWA_HEAD_PERM[2 * grp]],
                             sink_ref[SWA_HEAD_PERM[2 * grp + 1]])
            m = jnp.maximum(jnp.max(s, axis=-1, keepdims=True), sink)
            p = jnp.exp((s - m).astype(BF16))
            acc = _dot(p, jnp.concatenate([vblk, jnp.ones_like(vblk)], axis=-1))
            es = jnp.exp(sink - m)
            lo = lane_q < HEAD_DIM
            den = (jnp.where(lo, acc[0:BLOCK, BLOCK:2 * BLOCK], acc[BLOCK:2 * BLOCK, BLOCK:2 * BLOCK])
                   + jnp.where(lo, es[0:BLOCK], es[BLOCK:2 * BLOCK]))
            od = jnp.where(lo, acc[0:BLOCK, 0:BLOCK], acc[BLOCK:2 * BLOCK, 0:BLOCK]) / den
            g0 = c0 + 2 * W_SWA + grp * BLOCK
            gate_s = yacd_ref[0, r0:r0 + BLOCK, g0:g0 + BLOCK].astype(F32)
            gated.append(od * gate_s)
        h01, h23 = _unpair_heads(*gated)
        yds[r0:r0 + BLOCK, 0:BLOCK] = h01.astype(BF16)
        yds[r0:r0 + BLOCK, BLOCK:2 * BLOCK] = h23.astype(BF16)
    kext[0:BLOCK, :] = kext[tl:tl + BLOCK, :]
    vext[0:BLOCK, :] = vext[tl:tl + BLOCK, :]

    mixed = jnp.concatenate(proj, axis=-1) + _dot(yds[...], w_ref[k_head:D_MIX, :])
    gate = mod_ref[0][:, 2 * D_MODEL:3 * D_MODEL]
    o_ref[0] = _residual_out(mixed, x_ref[0], gate, fg_ref, final)


def _prompt_c(x, mod_p, yacd, yb, w_out_b, final_g, sink, swa_bias, final, layer, row0, n_rows, cache_args):
    nb, l, _ = x.shape
    tl = SEQ_TILE
    full = lambda *shape: pl.BlockSpec(shape, lambda n, t: (0,) * len(shape))
    ride_tiles = n_rows // (nb * DEC_TILE)
    assert ride_tiles * nb * DEC_TILE == n_rows and 1 <= ride_tiles <= l // tl
    c_in, c_out, c_shape = _cache_attention_specs(
        layer, row0, n_rows, lambda n, t: n * ride_tiles + jnp.minimum(t, ride_tiles - 1),
        cache_args[6].shape[-1], cache_args[8].shape[-1])
    return pl.pallas_call(
        functools.partial(_pc_kernel, final=final, ride_tiles=ride_tiles),
        grid=(nb, l // tl),
        in_specs=[
            pl.BlockSpec(memory_space=pltpu.SMEM),
            pl.BlockSpec((1, tl, D_MODEL), lambda n, t: (n, t, 0)),
            pl.BlockSpec((1, 1, 3 * D_MODEL), lambda n, t: (n, 0, 0)),
            pl.BlockSpec((1, tl, Y_WIDTH), lambda n, t: (n, t, 0)),
            pl.BlockSpec((1, N_PAIRS_DIL, N_CLASSES, tl // N_CLASSES, BLOCK),
                         lambda n, t: (n, 0, 0, t, 0)),
            pl.BlockSpec((None, D_MIX, D_MODEL), lambda n, t: (layer, 0, 0)),
            full(1, D_MODEL),
            full(2, 2, 2 * BLOCK, 2 * BLOCK),
        ] + c_in,
        out_specs=[pl.BlockSpec((1, tl, D_MODEL), lambda n, t: (n, t, 0))] + c_out,
        out_shape=[jax.ShapeDtypeStruct((nb, l, D_MODEL), F32)] + c_shape,
        scratch_shapes=[
            pltpu.VMEM((N_PAIRS_DIL, tl, BLOCK), F32),
            pltpu.VMEM((BLOCK + tl, W_SWA_KV), BF16),
            pltpu.VMEM((BLOCK + tl, W_SWA_KV), BF16),
            pltpu.VMEM((tl, W_SWA), BF16),
        ],
        compiler_params=_compiler_params(("arbitrary", "arbitrary")),
        name="prompt_c",
    )(sink, x, mod_p, yacd, yb, w_out_b, final_g, swa_bias, *cache_args)


def _sa_kernel(x_ref, mod_ref, g_ref, w_ref, wbd_ref, ps_ref, cw_ref, sp_ref, sc_ref,
               qkv_ref, sw_ref, yac_ref, gates_ref, pst_ref, cst_ref):
    hb = _modulated_norm(x_ref[...], g_ref[...], mod_ref[...]).astype(BF16)
    ns = hb.shape[0]

    zp = _dot(hb, w_ref[:, OFF_POOL:OFF_POOL + 2 * W_POOL])
    pu = zp[:, 0:W_POOL]
    pg = zp[:, W_POOL:2 * W_POOL]
    acc = pu
    sums = {}
    for j in range(1, 16):
        acc = acc + sp_ref[POOL_BUF - j]
        if j + 1 in POOL_WINDOWS:
            sums[j + 1] = acc
    lane = lax.broadcasted_iota(jnp.int32, (ns, W_POOL), 1)
    sel, win = _pool_select(sums, lane)
    diff = sel / win.astype(F32) - pu
    a_out = _dot(diff.astype(BF16), wbd_ref[...]) * ps_ref[...]
    yac_ref[:, 0:W_POOL] = a_out * _silu(pg)
    pst_ref[0:POOL_BUF - 1] = sp_ref[1:POOL_BUF]
    pst_ref[POOL_BUF - 1] = pu

    zc4 = _dot(hb, w_ref[:, OFF_CONV:OFF_CONV + 4 * W_CONV])
    ch = zc4[:, 0:W_CONV]
    cb = zc4[:, W_CONV:2 * W_CONV]
    cc = zc4[:, 2 * W_CONV:3 * W_CONV]
    cg = zc4[:, 3 * W_CONV:4 * W_CONV]
    zc = cc * ch
    conv = (cw_ref[0:1, :] * sc_ref[:, 0:W_CONV] + cw_ref[1:2, :] * sc_ref[:, W_CONV:2 * W_CONV]
            + cw_ref[2:3, :] * zc)
    yac_ref[:, W_POOL:W_POOL + W_CONV] = cb * conv * _silu(cg)
    cst_ref[:, 0:W_CONV] = sc_ref[:, W_CONV:2 * W_CONV]
    cst_ref[:, W_CONV:2 * W_CONV] = zc

    zd = _dot(hb, w_ref[:, OFF_DIL:OFF_DIL + 4 * W_DIL])
    qkv_ref[:, 0:W_DIL] = zd[:, 0:W_DIL] * QK_SCALE
    qkv_ref[:, W_DIL:3 * W_DIL] = zd[:, W_DIL:3 * W_DIL]
    gates_ref[:, 0:W_DIL] = _silu(zd[:, 3 * W_DIL:4 * W_DIL])

    zs = _dot(hb, w_ref[:, OFF_SWA:OFF_SWA + 3 * W_SWA])
    sw_ref[:, 0:W_SWA] = zs[:, 0:W_SWA] * QK_SCALE
    sw_ref[:, W_SWA:2 * W_SWA] = zs[:, W_SWA:2 * W_SWA]
    gates_ref[:, W_DIL:W_DIL + W_SWA] = _silu(zs[:, 2 * W_SWA:3 * W_SWA])


def _whole_spec(shape):
    return pl.BlockSpec(shape, lambda *_: (0,) * len(shape))


def _layer_spec(shape, layer):
    return pl.BlockSpec((None,) + tuple(shape[1:]), lambda *_: (layer,) + (0,) * (len(shape) - 1))


def _sample_a(xs, mod_s, norm_g, w_in_b, wbd, pool_scale, conv_w, sp, sc, layer):
    ns = xs.shape[0]
    shapes = [
        jax.ShapeDtypeStruct((ns, 3 * W_DIL), F32),
        jax.ShapeDtypeStruct((ns, 2 * W_SWA), F32),
        jax.ShapeDtypeStruct((ns, W_POOL + W_CONV), F32),
        jax.ShapeDtypeStruct((ns, W_DIL + W_SWA), F32),
        jax.ShapeDtypeStruct((POOL_BUF, ns, W_POOL), F32),
        jax.ShapeDtypeStruct((ns, CONV_BUF * W_CONV), F32),
    ]
    args = (xs, mod_s, norm_g, w_in_b, wbd, pool_scale, conv_w, sp, sc)
    in_specs = [_whole_spec(a.shape) for a in args]
    in_specs[3] = _layer_spec(w_in_b.shape, layer)
    return pl.pallas_call(
        _sa_kernel,
        grid=(1,),
        in_specs=in_specs,
        out_specs=[_whole_spec(s.shape) for s in shapes],
        out_shape=shapes,
        compiler_params=_compiler_params(("arbitrary",)),
        name="sample_a",
    )(*args)


def _cache_attention(dq_ref, sq_ref, dbias_ref, mult_ref, sbias_ref, sink_ref, kt_ref, vt_ref,
                     ck_ref, cv_ref, bo_ref, do_ref):
    row_d = lax.broadcasted_iota(jnp.int32, (8, DIL_MAX), 0)
    row_o = lax.broadcasted_iota(jnp.int32, (8, HEAD_DIM), 0)
    row_s = lax.broadcasted_iota(jnp.int32, (8, SWA_WINDOW), 0)
    eye = (lax.broadcasted_iota(jnp.int32, (HEAD_DIM, HEAD_DIM), 0)
           == lax.broadcasted_iota(jnp.int32, (HEAD_DIM, HEAD_DIM), 1))
    for j in range(DEC_TILE):
        q = dq_ref[j, 0]
        s = jnp.zeros((8, DIL_MAX), F32)
        for h in range(N_HEADS_DIL):
            q_col = jnp.sum(jnp.where(eye, q[h:h + 1, :], 0.0), axis=1, keepdims=True)
            s_h = jnp.sum(kt_ref[0, j, h] * q_col, axis=0, keepdims=True)
            s = jnp.where(row_d == h, s_h, s)
            yield
        s = s + dbias_ref[...]
        s_self = jnp.sum(q * dq_ref[j, 1], axis=-1, keepdims=True)
        m = jnp.maximum(jnp.max(s, axis=-1, keepdims=True), s_self)
        p = jnp.exp(s - m) * mult_ref[...]
        p_self = float(len(DIL_CONFIGS)) * jnp.exp(s_self - m)
        den = jnp.sum(p, axis=-1, keepdims=True) + p_self
        yield
        acc = jnp.zeros((8, HEAD_DIM), F32)
        for h in range(N_HEADS_DIL):
            o_col = jnp.sum(vt_ref[0, j, h] * p[h:h + 1, :], axis=1, keepdims=True)
            o_row = jnp.sum(jnp.where(eye, o_col, 0.0), axis=0, keepdims=True)
            acc = jnp.where(row_o == h, o_row, acc)
            yield
        bo_ref[j] = (acc + p_self * dq_ref[j, 2]) / den

        q = sq_ref[j, 0]
        qb = q.astype(BF16)
        s = jnp.where(row_s < 2, _dot(qb, ck_ref[0, j, 0].astype(BF16)),
                      _dot(qb, ck_ref[0, j, 1].astype(BF16))) + sbias_ref[...]
        w_self = jnp.sum(q * sq_ref[j, 1], axis=-1, keepdims=True)
        sink = sink_ref[...][:, 0:1]
        m = jnp.maximum(jnp.maximum(jnp.max(s, axis=-1, keepdims=True), w_self), sink)
        p = jnp.exp(s - m)
        pw = jnp.exp(w_self - m)
        den = jnp.sum(p, axis=-1, keepdims=True) + pw + jnp.exp(sink - m)
        pb = p.astype(BF16)
        yield
        acc = jnp.where(row_o < 2, _dot_nt(pb, cv_ref[0, j, 0].astype(BF16)),
                        _dot_nt(pb, cv_ref[0, j, 1].astype(BF16)))
        do_ref[j] = (acc + pw * sq_ref[j, 2]) / den
        yield


def _advance(stages, n):
    for _ in range(n):
        next(stages, None)


def _sc_kernel(x_ref, mod_ref, yac_ref, gates_ref, bo_ref, do_ref, w_ref, fg_ref, o_ref, *, final):
    yac = yac_ref[...]
    gates = gates_ref[...]
    y = jnp.concatenate([yac[:, 0:W_POOL], bo_ref[...] * gates[:, 0:W_DIL],
                         yac[:, W_POOL:W_POOL + W_CONV], do_ref[...] * gates[:, W_DIL:W_DIL + W_SWA]],
                        axis=-1).astype(BF16)
    gate = mod_ref[...][:, 2 * D_MODEL:3 * D_MODEL]
    o_ref[...] = _mix_out(y, x_ref[...], gate, w_ref, fg_ref, final)


def _sample_c(xs, mod_s, yac, gates, bo, do, w_out_b, final_g, final, layer):
    args = (xs, mod_s, yac, gates, bo, do, w_out_b, final_g)
    in_specs = [_whole_spec(a.shape) for a in args]
    in_specs[6] = _layer_spec(w_out_b.shape, layer)
    return pl.pallas_call(
        functools.partial(_sc_kernel, final=final),
        grid=(1,),
        in_specs=in_specs,
        out_specs=_whole_spec(xs.shape),
        out_shape=jax.ShapeDtypeStruct(xs.shape, F32),
        compiler_params=_compiler_params(("arbitrary",)),
        name="sample_c",
    )(*args)


def _band_bias_dil4(slope_lo, slope_hi, variant):
    sub = BLOCK // 4
    qidx = np.arange(BLOCK)[:, None]
    kidx = np.arange(2 * BLOCK)[None, :]
    q_step = 4 * (qidx % sub + (sub if variant == 0 else 0)) + qidx // sub
    k_step = 4 * (kidx % (2 * sub)) + kidx // (2 * sub)
    off = q_step - k_step
    valid = (off >= 0) & (off <= BLOCK)
    out = []
    for s in (slope_lo, slope_hi):
        out.append(np.where(valid, -(np.float32(s) * np.float32(4)) * off.astype(np.float32),
                            np.float32(NEG_INF)).astype(np.float32))
    return np.concatenate(out, axis=0)


def _prompt_bias_tables():
    dil = _alibi_slopes(N_HEADS_DIL)

    def table(hp, d, var):
        if d == 4:
            return _band_bias_dil4(dil[2 * hp], dil[2 * hp + 1], var)
        return _band_bias(dil[2 * hp], dil[2 * hp + 1], d, var)

    dil_bias = np.stack([
        np.stack([np.stack([table(hp, d, var) for var in (0, 1)])
                  for _, d in DIL_CONFIGS]) for hp in range(N_PAIRS_DIL)])
    swa = _alibi_slopes(N_HEADS_SWA)
    swa_bias = np.stack([
        np.stack([_band_bias(swa[SWA_HEAD_PERM[2 * g]], swa[SWA_HEAD_PERM[2 * g + 1]], 1, var)
                  for var in (0, 2)]) for g in range(2)])
    return jnp.asarray(dil_bias), jnp.asarray(swa_bias)


def _sample_bias_tables(r):
    dil = _alibi_slopes(N_HEADS_DIL)
    dist = (r - np.arange(r)).astype(np.float32)
    mult = np.zeros((1, r), np.float32)
    for window, d in DIL_CONFIGS:
        mult[0] += ((dist <= window) & (dist % d == 0)).astype(np.float32)
    dbias = np.zeros((8, r), np.float32)
    for h in range(N_HEADS_DIL):
        dbias[h] = np.where(mult[0] > 0, -np.float32(dil[h]) * dist, np.float32(NEG_INF))
    swa = _alibi_slopes(N_HEADS_SWA)
    sdist = (SWA_WINDOW - np.arange(SWA_WINDOW)).astype(np.float32)
    sbias = np.zeros((8, SWA_WINDOW), np.float32)
    for h in range(N_HEADS_SWA):
        sbias[h] = -np.float32(swa[h]) * sdist
    return jnp.asarray(dbias), jnp.asarray(mult), jnp.asarray(sbias)


def kernel(x_prompt, x_sample, c_prompt, c_sample, state_pool, cache_dil_k, cache_dil_v, state_conv, cache_swa_k, cache_swa_v, norm_g, w_ada, b_ada, w_in, w_pool, pool_scale, conv_w, swa_sink, w_out, final_g):
    depth = w_in.shape[0]
    nb, l, _ = x_prompt.shape
    ns = x_sample.shape[0]
    assert x_sample.shape[1] == 1 and l % CHUNK == 0 and l >= DIL_MAX
    rows_a = nb * (l // SEQ_TILE) * DEC_TILE
    rows_b = nb * N_PAIRS_DIL * (l // CHUNK) * DEC_TILE
    rows_c = ns - rows_a - rows_b
    assert rows_c > 0
    assert cache_dil_k.shape[2] == DIL_MAX and cache_swa_k.shape[2] == SWA_WINDOW

    w_in_b = w_in.astype(BF16)
    w_out_b = w_out.astype(BF16)
    eye = jnp.eye(len(POOL_WINDOWS), dtype=F32)
    wbd = jnp.einsum("dgce,gh->dgche", w_pool, eye).reshape(depth, W_POOL, W_POOL).astype(BF16)
    sink_rows = jnp.zeros((depth, 8), F32).at[:, 0:N_HEADS_SWA].set(swa_sink)
    sink_rows = jnp.broadcast_to(sink_rows[:, :, None], (depth, 8, BLOCK))

    dil_bias, swa_bias = _prompt_bias_tables()
    dbias, mult, sbias = _sample_bias_tables(cache_dil_k.shape[2])

    mod = _ada(jnp.concatenate([c_prompt, c_sample], axis=0), w_ada, b_ada)
    fg = final_g.reshape(1, D_MODEL)

    xp = x_prompt
    xs = x_sample.reshape(ns, D_MODEL)
    kt = jnp.transpose(cache_dil_k, (0, 1, 3, 4, 2))
    vt = jnp.transpose(cache_dil_v, (0, 1, 3, 4, 2))
    ckt = jnp.transpose(cache_swa_k, (0, 1, 3, 4, 2))
    cvt = jnp.transpose(cache_swa_v, (0, 1, 3, 4, 2))
    sp_all = jnp.transpose(state_pool, (0, 2, 1, 3))
    sc_all = state_conv.reshape(depth, ns, CONV_BUF * W_CONV)
    pad_heads = lambda a: jnp.pad(a, ((0, 0), (0, 0), (0, 8 - a.shape[2]), (0, 0)))
    outs = [[] for _ in range(12)]
    carried = None
    for i in range(depth):
        final = i == depth - 1
        mod_p = mod[i, 0:nb].reshape(nb, 1, 3 * D_MODEL)
        mod_s = mod[i, nb:nb + ns]
        g = norm_g[i].reshape(1, D_MODEL)
        ps = pool_scale[i].reshape(1, W_POOL)

        qkv, sw, yac, gates, pst_s, cst_s = _sample_a(
            xs, mod_s, g, w_in_b, wbd[i], ps, conv_w[i], sp_all[i], sc_all[i], i)
        dq3 = pad_heads(qkv.reshape(ns, 3, N_HEADS_DIL, HEAD_DIM))
        sq4 = sw[:, 0:W_SWA].reshape(ns, 1, N_HEADS_SWA, HEAD_DIM)
        skv = jnp.repeat(sw[:, W_SWA:W_SWA + 2 * W_SWA_KV].reshape(ns, 2, 2, HEAD_DIM), 2, axis=2)
        sq3 = pad_heads(jnp.concatenate([sq4, skv], axis=1))

        cache_args = (dq3, sq3, dbias, mult, sbias, sink_rows[i], kt, vt, ckt, cvt)
        q1, k1, v1, qm, km, vm, gm, kc, vc, yacd, pst, cst, skc, svc, bo_a, do_a = _prompt_a(
            xp, mod_p, g, w_in_b, wbd[i], ps, conv_w[i], i, depth, 0, cache_args, carried)
        carried = (kc, vc)
        yb, bo_b, do_b = _prompt_b(q1, k1, v1, qm, km, vm, gm, dil_bias, i, rows_a, cache_args)
        xp, bo_c, do_c = _prompt_c(xp, mod_p, yacd, yb, w_out_b, fg, swa_sink[i], swa_bias, final,
                                   i, rows_a + rows_b, rows_c, cache_args)

        bo = jnp.concatenate([bo_a, bo_b, bo_c], axis=0)[:, 0:N_HEADS_DIL].reshape(ns, W_DIL)
        do = jnp.concatenate([do_a, do_b, do_c], axis=0)[:, 0:N_HEADS_SWA].reshape(ns, W_SWA)
        xs = _sample_c(xs, mod_s, yac, gates, bo, do, w_out_b, fg, final, i)

        unfold = lambda a, h: jnp.transpose(a.reshape(nb, h, HEAD_DIM, a.shape[-1]), (0, 3, 1, 2))
        skc, svc = unfold(skc, 2), unfold(svc, 2)
        outs[0].append(pst)
        outs[1].append(jnp.transpose(pst_s, (1, 0, 2)))
        outs[4].append(qkv[:, W_DIL:2 * W_DIL].reshape(ns, 1, N_HEADS_DIL, HEAD_DIM))
        outs[5].append(qkv[:, 2 * W_DIL:3 * W_DIL].reshape(ns, 1, N_HEADS_DIL, HEAD_DIM))
        outs[6].append(cst)
        outs[7].append(cst_s.reshape(ns, CONV_BUF, W_CONV))
        outs[8].append(skc)
        outs[9].append(svc)
        outs[10].append(sw[:, W_SWA:W_SWA + W_SWA_KV].reshape(ns, 1, 2, HEAD_DIM))
        outs[11].append(sw[:, W_SWA + W_SWA_KV:W_SWA + 2 * W_SWA_KV].reshape(ns, 1, 2, HEAD_DIM))

    for k, cache in zip((2, 3), carried):
        outs[k] = jnp.transpose(cache.reshape(depth, nb, N_HEADS_DIL, HEAD_DIM, DIL_MAX), (0, 1, 4, 2, 3))
    return (xp, xs.reshape(ns, 1, D_MODEL)) + tuple(
        o if not isinstance(o, list) else jnp.stack(o) for o in outs)
```

```python
import functools
import math

import numpy as np
import jax
import jax.numpy as jnp
from jax import lax
from jax.experimental import pallas as pl
from jax.experimental.pallas import tpu as pltpu

F32 = jnp.float32
BF16 = jnp.bfloat16

D_MODEL = 1024
HEAD_DIM = 64
BLOCK = 128
POOL_WINDOWS = (2, 4, 8, 16)
POOL_GROUP = 64
W_POOL = 256
POOL_BUF = 15
DIL_CONFIGS = ((128, 1), (512, 4), (2048, 16))
DIL_MAX = 2048
N_HEADS_DIL = 6
N_PAIRS_DIL = 3
N_CLASSES = 16
W_DIL = 384
W_CONV = 256
CONV_BUF = 2
N_HEADS_SWA = 4
W_SWA = 256
W_SWA_KV = 128
SWA_WINDOW = 128
D_MIX = 1152
D_PROJ = 3840
RMS_EPS = 1e-6
QK_SCALE = 1.0 / math.sqrt(HEAD_DIM)

OFF_POOL = 0
OFF_DIL = 512
OFF_CONV = 2048
OFF_SWA = 3072
Y_WIDTH = W_POOL + W_CONV + 3 * W_SWA
SWA_HEAD_PERM = (0, 3, 1, 2)

SEQ_TILE = 512
CHUNK = 2048
BAND_SKEW = 1
DEC_TILE = 2
RIDE_STAGES = 15 * DEC_TILE
VMEM_LIMIT = 60 * 1024 * 1024

NEG_INF = float("-inf")


def _silu(v):
    return v * jax.nn.sigmoid(v)


def _dot(a, b):
    return jnp.dot(a, b, preferred_element_type=F32)


def _dot_nt(a, b):
    return lax.dot_general(a, b, (((1,), (1,)), ((), ())), preferred_element_type=F32)


def _alibi_slopes(n):
    return [2.0 ** (-8.0 * (h + 1) / n) for h in range(n)]


def _band_bias(slope_lo, slope_hi, dist_scale, variant):
    qi = np.arange(BLOCK)[:, None]
    kj = np.arange(2 * BLOCK)[None, :]
    if variant == 1:
        off = qi - kj
    else:
        off = qi - kj + BLOCK
    valid = (off >= 0) & (off <= BLOCK)
    if variant == 2:
        valid = valid & (kj >= BLOCK)
    out = []
    for s in (slope_lo, slope_hi):
        b = np.where(valid, -(np.float32(s) * np.float32(dist_scale)) * off.astype(np.float32),
                     np.float32(NEG_INF))
        out.append(b.astype(np.float32))
    return np.concatenate(out, axis=0)


def _compiler_params(sem):
    return pltpu.CompilerParams(dimension_semantics=sem, vmem_limit_bytes=VMEM_LIMIT)


def _ada_kernel(c_ref, w_ref, b_ref, o_ref):
    s = _silu(c_ref[...]).astype(BF16)
    o_ref[0] = _dot(s, w_ref[0].astype(BF16)) + b_ref[0]


def _ada(c_all, w_ada_b, b_ada):
    depth = w_ada_b.shape[0]
    rows = c_all.shape[0]
    return pl.pallas_call(
        _ada_kernel,
        grid=(depth, 3),
        in_specs=[
            pl.BlockSpec((rows, D_MODEL), lambda i, j: (0, 0)),
            pl.BlockSpec((1, D_MODEL, D_MODEL), lambda i, j: (i, 0, j)),
            pl.BlockSpec((1, 1, D_MODEL), lambda i, j: (i, 0, j)),
        ],
        out_specs=pl.BlockSpec((1, rows, D_MODEL), lambda i, j: (i, 0, j)),
        out_shape=jax.ShapeDtypeStruct((depth, rows, 3 * D_MODEL), F32),
        compiler_params=_compiler_params(("arbitrary", "arbitrary")),
        name="ada",
    )(c_all, w_ada_b, b_ada.reshape(depth, 1, 3 * D_MODEL))


def _modulated_norm(x, g, mod_row):
    ms = jnp.mean(x * x, axis=-1, keepdims=True)
    shift = mod_row[:, 0:D_MODEL]
    scale = mod_row[:, D_MODEL:2 * D_MODEL]
    return x * lax.rsqrt(ms + RMS_EPS) * (g * (1.0 + scale)) + shift


def _pair_heads(v):
    left, right = v[:, 0:BLOCK], v[:, BLOCK:2 * BLOCK]
    low = lax.broadcasted_iota(jnp.int32, left.shape, 1) < HEAD_DIM
    return jnp.concatenate([jnp.where(low, left, right),
                            pltpu.roll(jnp.where(low, right, left), HEAD_DIM, axis=1)], axis=-1)


def _unpair_heads(a, b):
    low = lax.broadcasted_iota(jnp.int32, a.shape, 1) < HEAD_DIM
    b_swapped = pltpu.roll(b, HEAD_DIM, axis=1)
    return jnp.where(low, a, b_swapped), jnp.where(low, b_swapped, a)


def _pool_select(sums, lane):
    grp = lane // POOL_GROUP
    sel = jnp.where(grp == 0, sums[2], jnp.where(grp == 1, sums[4],
                                                 jnp.where(grp == 2, sums[8], sums[16])))
    win = jnp.where(grp == 0, 2, jnp.where(grp == 1, 4, jnp.where(grp == 2, 8, 16)))
    return sel, win


def _pa_kernel(x_ref, mod_ref, g_ref, w_ref, wbd_ref, ps_ref, cw_ref,
               dq_ref, sq_ref, dbias_ref, mult_ref, sbias_ref, ssink_ref, kt_ref, vt_ref, ck_ref, cv_ref,
               q1_ref, k1_ref, v1_ref, qm_ref, km_ref, vm_ref, gm_ref, kc_ref, vc_ref, y_ref,
               pst_ref, cst_ref, skc_ref, svc_ref, bo_ref, do_ref,
               uext, zcext, zds, z4s, *, later_layers=0):
    tl = SEQ_TILE
    t = pl.program_id(1)

    @pl.when(t == 0)
    def _():
        for k in range(len(POOL_WINDOWS)):
            uext[k, 0:16, :] = jnp.zeros((16, W_POOL), F32)
        zcext[0:8, :] = jnp.zeros((8, W_CONV), F32)

    ride = _cache_attention(dq_ref, sq_ref, dbias_ref, mult_ref, sbias_ref, ssink_ref, kt_ref, vt_ref,
                            ck_ref, cv_ref, bo_ref, do_ref)

    _advance(ride, 3 * DEC_TILE)
    hb = _modulated_norm(x_ref[0], g_ref[...], mod_ref[0]).astype(BF16)

    zp = _dot(hb, w_ref[:, OFF_POOL:OFF_POOL + 2 * W_POOL])
    zd = _dot(hb, w_ref[:, OFF_DIL:OFF_DIL + 4 * W_DIL])
    _advance(ride, 3 * DEC_TILE)

    pu = zp[:, 0:W_POOL]
    pg = zp[:, W_POOL:2 * W_POOL]
    uext[0, 16:16 + tl, :] = pu
    sums = {}
    level = pu
    for k, w in enumerate(POOL_WINDOWS):
        shift = w // 2
        level = level + uext[k, 16 - shift:16 - shift + tl, :]
        sums[w] = level
        if k + 1 < len(POOL_WINDOWS):
            uext[k + 1, 16:16 + tl, :] = level
    lane = lax.broadcasted_iota(jnp.int32, (tl, W_POOL), 1)
    gpos = lax.broadcasted_iota(jnp.int32, (tl, W_POOL), 0) + t * tl
    sel, win = _pool_select(sums, lane)
    cnt = jnp.minimum(gpos + 1, win).astype(F32)
    diff = sel / cnt - pu
    a_out = _dot(diff.astype(BF16), wbd_ref[...]) * ps_ref[...]
    y_ref[0, :, 0:W_POOL] = (a_out * _silu(pg)).astype(BF16)
    pst_ref[0] = uext[0, tl + 1:tl + 16, :]
    for k in range(len(POOL_WINDOWS)):
        uext[k, 0:16, :] = uext[k, tl:tl + 16, :]

    zs = _dot(hb, w_ref[:, OFF_SWA:OFF_SWA + 3 * W_SWA])
    _advance(ride, 3 * DEC_TILE)
    kc_ref[0, 0] = zd[:, W_DIL:2 * W_DIL].T
    vc_ref[0, 0] = zd[:, 2 * W_DIL:3 * W_DIL].T
    for d in range(1, 1 + later_layers):
        kc_ref[d, 0] = jnp.zeros((W_DIL, tl), F32)
        vc_ref[d, 0] = jnp.zeros((W_DIL, tl), F32)
    for hp in range(N_PAIRS_DIL):
        _advance(ride, DEC_TILE)
        lo = hp * BLOCK
        zds[0] = zd[:, lo:lo + BLOCK] * QK_SCALE
        zds[1] = zd[:, W_DIL + lo:W_DIL + lo + BLOCK]
        zds[2] = zd[:, 2 * W_DIL + lo:2 * W_DIL + lo + BLOCK]
        zds[3] = _silu(zd[:, 3 * W_DIL + lo:3 * W_DIL + lo + BLOCK])
        q1_ref[0, hp] = zds[0].astype(BF16)
        k1_ref[0, hp] = zds[1].astype(BF16)
        v1_ref[0, hp] = zds[2].astype(BF16)
        quarter = tl // 4
        for a in range(4):
            for blk in range(4):
                z4s[blk, a] = zds[blk, pl.ds(a, quarter, stride=4), :]
        for b in range(4):
            for a in range(4):
                c = 4 * b + a
                rows = pl.ds(b, tl // N_CLASSES, stride=4)
                qm_ref[0, hp, c] = z4s[0, a, rows, :].astype(BF16)
                km_ref[0, hp, c] = z4s[1, a, rows, :].astype(BF16)
                vm_ref[0, hp, c] = z4s[2, a, rows, :].astype(BF16)
                gm_ref[0, hp, c] = z4s[3, a, rows, :].astype(BF16)

    zc4 = _dot(hb, w_ref[:, OFF_CONV:OFF_CONV + 4 * W_CONV])
    _advance(ride, 3 * DEC_TILE)

    sk = zs[:, W_SWA:W_SWA + W_SWA_KV]
    sv = zs[:, W_SWA + W_SWA_KV:W_SWA + 2 * W_SWA_KV]
    skc_ref[0] = sk[tl - SWA_WINDOW:tl, :].T
    svc_ref[0] = sv[tl - SWA_WINDOW:tl, :].T
    c0 = W_POOL + W_CONV
    y_ref[0, :, c0:c0 + W_SWA] = _pair_heads(zs[:, 0:W_SWA] * QK_SCALE).astype(BF16)
    y_ref[0, :, c0 + W_SWA:c0 + 2 * W_SWA] = zs[:, W_SWA:2 * W_SWA].astype(BF16)
    y_ref[0, :, c0 + 2 * W_SWA:c0 + 3 * W_SWA] = _pair_heads(_silu(zs[:, 2 * W_SWA:3 * W_SWA])).astype(BF16)
    _advance(ride, RIDE_STAGES)

    ch = zc4[:, 0:W_CONV]
    cb = zc4[:, W_CONV:2 * W_CONV]
    cc = zc4[:, 2 * W_CONV:3 * W_CONV]
    cg = zc4[:, 3 * W_CONV:4 * W_CONV]
    zc = cc * ch
    zcext[8:8 + tl, :] = zc
    conv = (cw_ref[0:1, :] * zcext[6:6 + tl, :] + cw_ref[1:2, :] * zcext[7:7 + tl, :]
            + cw_ref[2:3, :] * zc)
    y_ref[0, :, W_POOL:W_POOL + W_CONV] = (cb * conv * _silu(cg)).astype(BF16)
    cst_ref[0] = zcext[tl + 6:tl + 8, :]
    zcext[0:8, :] = zcext[tl:tl + 8, :]


def _cache_attention_specs(layer, row0, nb, nt, r, swr):
    bt = DEC_TILE
    blk0 = row0 // bt
    full = lambda *shape: pl.BlockSpec(shape, lambda n, t: (0,) * len(shape))
    srow_spec = pl.BlockSpec((bt, 3, 8, HEAD_DIM), lambda n, t: (blk0 + n * nt + t, 0, 0, 0))
    dil_spec = pl.BlockSpec((1, bt, N_HEADS_DIL, HEAD_DIM, r),
                            lambda n, t: (layer, blk0 + n * nt + t, 0, 0, 0))
    swa_spec = pl.BlockSpec((1, bt, 2, HEAD_DIM, swr), lambda n, t: (layer, blk0 + n * nt + t, 0, 0, 0))
    sout_spec = pl.BlockSpec((bt, 8, HEAD_DIM), lambda n, t: (n * nt + t, 0, 0))
    sout_shape = jax.ShapeDtypeStruct((nb * nt * bt, 8, HEAD_DIM), F32)
    in_specs = [srow_spec, srow_spec, full(8, r), full(1, r), full(8, swr), full(8, BLOCK),
                dil_spec, dil_spec, swa_spec, swa_spec]
    return in_specs, [sout_spec, sout_spec], [sout_shape, sout_shape]


N_PA_INPUTS = 17


def _pa_kernel_carry(*refs):
    _pa_kernel(*refs[:N_PA_INPUTS], *refs[N_PA_INPUTS + 2:])


def _prompt_a(x, mod_p, norm_g, w_in_b, wbd, pool_scale, conv_w, layer, depth, row0, cache_args, carried):
    nb, l, _ = x.shape
    tl = SEQ_TILE
    nt = l // tl
    cache_t0 = (l - DIL_MAX) // tl
    c_in, c_out, c_shape = _cache_attention_specs(layer, row0, nb, nt, cache_args[6].shape[-1],
                                                  cache_args[8].shape[-1])
    pair_spec = pl.BlockSpec((1, N_PAIRS_DIL, tl, BLOCK), lambda n, t: (n, 0, t, 0))
    pair_shape = jax.ShapeDtypeStruct((nb, N_PAIRS_DIL, l, BLOCK), BF16)
    cm_spec = pl.BlockSpec((1, N_PAIRS_DIL, N_CLASSES, tl // N_CLASSES, BLOCK),
                           lambda n, t: (n, 0, 0, t, 0))
    cm_shape = jax.ShapeDtypeStruct((nb, N_PAIRS_DIL, N_CLASSES, l // N_CLASSES, BLOCK), BF16)
    cache_layers = depth if carried is None else 1
    cache_spec = pl.BlockSpec((cache_layers, 1, W_DIL, tl),
                              lambda n, t: (layer, n, 0, jnp.maximum(t - cache_t0, 0)))
    cache_shape = jax.ShapeDtypeStruct((depth, nb, W_DIL, DIL_MAX), F32)
    full = lambda *shape: pl.BlockSpec(shape, lambda n, t: (0,) * len(shape))
    in_specs = [
        pl.BlockSpec((1, tl, D_MODEL), lambda n, t: (n, t, 0)),
        pl.BlockSpec((1, 1, 3 * D_MODEL), lambda n, t: (n, 0, 0)),
        full(1, D_MODEL),
        pl.BlockSpec((None, D_MODEL, D_PROJ), lambda n, t: (layer, 0, 0), pipeline_mode=pl.Buffered(1)),
        full(W_POOL, W_POOL),
        full(1, W_POOL),
        full(3, W_CONV),
    ] + c_in
    assert len(in_specs) == N_PA_INPUTS
    extra, aliases = (), {}
    if carried is not None:
        in_specs = in_specs + [pl.BlockSpec(memory_space=pl.ANY)] * 2
        extra = tuple(carried)
        aliases = {N_PA_INPUTS: 7, N_PA_INPUTS + 1: 8}
    return pl.pallas_call(
        functools.partial(_pa_kernel, later_layers=depth - 1) if carried is None else _pa_kernel_carry,
        grid=(nb, nt),
        in_specs=in_specs,
        input_output_aliases=aliases,
        out_specs=[
            pair_spec, pair_spec, pair_spec, cm_spec, cm_spec, cm_spec, cm_spec,
            cache_spec, cache_spec,
            pl.BlockSpec((1, tl, Y_WIDTH), lambda n, t: (n, t, 0)),
            pl.BlockSpec((1, POOL_BUF, W_POOL), lambda n, t: (n, 0, 0)),
            pl.BlockSpec((1, CONV_BUF, W_CONV), lambda n, t: (n, 0, 0)),
            pl.BlockSpec((1, SWA_WINDOW, W_SWA_KV), lambda n, t: (n, 0, 0)),
            pl.BlockSpec((1, SWA_WINDOW, W_SWA_KV), lambda n, t: (n, 0, 0)),
        ] + c_out,
        out_shape=[
            pair_shape, pair_shape, pair_shape, cm_shape, cm_shape, cm_shape, cm_shape,
            cache_shape, cache_shape,
            jax.ShapeDtypeStruct((nb, l, Y_WIDTH), BF16),
            jax.ShapeDtypeStruct((nb, POOL_BUF, W_POOL), F32),
            jax.ShapeDtypeStruct((nb, CONV_BUF, W_CONV), F32),
            jax.ShapeDtypeStruct((nb, SWA_WINDOW, W_SWA_KV), F32),
            jax.ShapeDtypeStruct((nb, SWA_WINDOW, W_SWA_KV), F32),
        ] + c_shape,
        scratch_shapes=[
            pltpu.VMEM((len(POOL_WINDOWS), 16 + tl, W_POOL), F32),
            pltpu.VMEM((8 + tl, W_CONV), F32),
            pltpu.VMEM((4, tl, BLOCK), F32),
            pltpu.VMEM((4, 4, tl // 4, BLOCK), F32),
        ],
        compiler_params=_compiler_params(("arbitrary", "arbitrary")),
        name="prompt_a",
    )(x, mod_p, norm_g, w_in_b, wbd, pool_scale, conv_w, *cache_args, *extra)


def _band_scores(q, kblk, bias, lane_q):
    lhs = jnp.concatenate([jnp.where(lane_q < HEAD_DIM, q, jnp.zeros_like(q)),
                           jnp.where(lane_q >= HEAD_DIM, q, jnp.zeros_like(q))], axis=0)
    return _dot_nt(lhs, kblk) + bias


def _band_values(s, vblk, lane_q):
    m = jnp.max(s, axis=-1, keepdims=True)
    p = jnp.exp((s - m).astype(BF16))
    acc = _dot(p, jnp.concatenate([vblk, jnp.ones_like(vblk)], axis=-1))
    lo = lane_q < HEAD_DIM
    return (jnp.where(lo, acc[0:BLOCK, 0:BLOCK], acc[BLOCK:2 * BLOCK, 0:BLOCK]),
            jnp.where(lo, m[0:BLOCK], m[BLOCK:2 * BLOCK]),
            jnp.where(lo, acc[0:BLOCK, BLOCK:2 * BLOCK], acc[BLOCK:2 * BLOCK, BLOCK:2 * BLOCK]))


def _run_skewed(tasks, skew):
    pending = []
    for scores_fn, finish_fn in tasks:
        pending.append((finish_fn, scores_fn()))
        if len(pending) > skew:
            fn, s = pending.pop(0)
            fn(s)
    for fn, s in pending:
        fn(s)


def _pb_kernel(bias_ref, q1, k1, v1, qm, km, vm, gm, outm, s1, s4, s16):
    c = pl.program_id(2)
    first = jnp.where(c == 0, 1, 0)
    lane_q = lax.broadcasted_iota(jnp.int32, (BLOCK, BLOCK), 1)
    sub = BLOCK // 4

    tasks = []

    def dil1_task(j):
        r0 = j * BLOCK
        if j == 0:
            start = jnp.maximum(c * (CHUNK // BLOCK) - 1, 0) * BLOCK
            var = first
        else:
            start = (c * (CHUNK // BLOCK) + (j - 1)) * BLOCK
            var = 0
        start = pl.multiple_of(start, BLOCK)

        def scores():
            return _band_scores(q1[0, 0, r0:r0 + BLOCK, :], k1[0, 0, pl.ds(start, 2 * BLOCK), :],
                                bias_ref[0, 0, var], lane_q)

        def finish(s):
            parts = _band_values(s, v1[0, 0, pl.ds(start, 2 * BLOCK), :], lane_q)
            for k, part in enumerate(parts):
                s1[k, r0:r0 + BLOCK, :] = part

        return scores, finish

    def dil4_task(c4, j):
        i0 = j * sub
        if j == 0:
            istart = jnp.maximum(c * (CHUNK // N_CLASSES) - sub, 0)
            var = first
        else:
            istart = c * (CHUNK // N_CLASSES) + i0 - sub
            var = 0
        istart = pl.multiple_of(istart, sub)
        classes = [4 * cc + c4 for cc in range(4)]

        def scores():
            q = jnp.concatenate([qm[0, 0, cl, i0:i0 + sub, :] for cl in classes], axis=0)
            kblk = jnp.concatenate([km[0, 0, cl, pl.ds(istart, 2 * sub), :] for cl in classes], axis=0)
            return _band_scores(q, kblk, bias_ref[0, 1, var], lane_q)

        def finish(s):
            vblk = jnp.concatenate([vm[0, 0, cl, pl.ds(istart, 2 * sub), :] for cl in classes], axis=0)
            parts = _band_values(s, vblk, lane_q)
            for k, part in enumerate(parts):
                for cc, cl in enumerate(classes):
                    s4[k, cl, i0:i0 + sub, :] = part[cc * sub:(cc + 1) * sub]

        return scores, finish

    start16 = pl.multiple_of(jnp.maximum(c - 1, 0) * BLOCK, BLOCK)

    def dil16_task(cl):
        def scores():
            return _band_scores(qm[0, 0, cl], km[0, 0, cl, pl.ds(start16, 2 * BLOCK), :],
                                bias_ref[0, 2, first], lane_q)

        def finish(s):
            parts = _band_values(s, vm[0, 0, cl, pl.ds(start16, 2 * BLOCK), :], lane_q)
            for k, part in enumerate(parts):
                s16[k, cl] = part

        return scores, finish

    tasks += [dil1_task(j) for j in range(CHUNK // BLOCK)]
    tasks += [dil4_task(c4, j) for c4 in range(4) for j in range(4)]
    tasks += [dil16_task(cl) for cl in range(N_CLASSES)]
    _run_skewed(tasks, BAND_SKEW)

    for cl in range(N_CLASSES):
        rows = pl.ds(cl, BLOCK, stride=N_CLASSES)
        ma, mb, mc = s1[1, rows, :], s4[1, cl], s16[1, cl]
        mx = jnp.maximum(jnp.maximum(ma, mb), mc)
        wa = jnp.exp(ma - mx)
        wb = jnp.exp(mb - mx)
        wc = jnp.exp(mc - mx)
        num = wa * s1[0, rows, :] + wb * s4[0, cl] + wc * s16[0, cl]
        den = wa * s1[2, rows, :] + wb * s4[2, cl] + wc * s16[2, cl]
        outm[0, 0, cl] = (num / den * gm[0, 0, cl].astype(F32)).astype(BF16)


def _prompt_b(q1, k1, v1, qm, km, vm, gm, dil_bias):
    nb, npair, l, _ = q1.shape
    nc = l // CHUNK
    li = l // N_CLASSES
    ci = CHUNK // N_CLASSES
    q1_spec = pl.BlockSpec((1, 1, CHUNK, BLOCK), lambda n, h, c: (n, h, c, 0))
    kv1_spec = pl.BlockSpec((1, 1, l, BLOCK), lambda n, h, c: (n, h, 0, 0))
    cm_spec = pl.BlockSpec((1, 1, N_CLASSES, ci, BLOCK), lambda n, h, c: (n, h, 0, c, 0))
    kvm_spec = pl.BlockSpec((1, 1, N_CLASSES, li, BLOCK), lambda n, h, c: (n, h, 0, 0, 0))
    cm_scratch = pltpu.VMEM((3, N_CLASSES, ci, BLOCK), F32)
    return pl.pallas_call(
        _pb_kernel,
        grid=(nb, npair, nc),
        in_specs=[pl.BlockSpec((1, 3, 2, 2 * BLOCK, 2 * BLOCK), lambda n, h, c: (h, 0, 0, 0, 0)),
                  q1_spec, kv1_spec, kv1_spec, cm_spec, kvm_spec, kvm_spec, cm_spec],
        out_specs=cm_spec,
        out_shape=jax.ShapeDtypeStruct((nb, npair, N_CLASSES, li, BLOCK), BF16),
        scratch_shapes=[
            pltpu.VMEM((3, CHUNK, BLOCK), F32), cm_scratch, cm_scratch,
        ],
        compiler_params=_compiler_params(("arbitrary", "arbitrary", "arbitrary")),
        name="prompt_b",
    )(dil_bias, q1, k1, v1, qm, km, vm, gm)


def _mix_out(y, x, gate, w_ref, fg_ref, final):
    return _residual_out(_dot(y, w_ref[...]), x, gate, fg_ref, final)


def _residual_out(mixed, x, gate, fg_ref, final):
    xn = x + gate * mixed
    if final:
        ms = jnp.mean(xn * xn, axis=-1, keepdims=True)
        xn = xn * lax.rsqrt(ms + RMS_EPS) * fg_ref[...]
    return xn


def _pc_kernel(sink_ref, x_ref, mod_ref, yacd_ref, yb_ref, w_ref, fg_ref, bias_ref,
               o_ref, ybs, kext, vext, yds, *, final):
    tl = SEQ_TILE
    t = pl.program_id(1)

    @pl.when(t == 0)
    def _():
        kext[0:BLOCK, :] = jnp.zeros((BLOCK, W_SWA_KV), BF16)
        vext[0:BLOCK, :] = jnp.zeros((BLOCK, W_SWA_KV), BF16)

    for hp in range(N_PAIRS_DIL):
        for c in range(N_CLASSES):
            ybs[hp, pl.ds(c, tl // N_CLASSES, stride=N_CLASSES), :] = yb_ref[0, hp, c].astype(F32)
    k_head = D_MIX - W_SWA
    y_head = jnp.concatenate([yacd_ref[0, :, 0:W_POOL], ybs[0].astype(BF16), ybs[1].astype(BF16),
                              ybs[2].astype(BF16), yacd_ref[0, :, W_POOL:W_POOL + W_CONV]], axis=-1)
    n_tile = D_MODEL // (tl // BLOCK)
    proj = []

    c0 = W_POOL + W_CONV
    kext[BLOCK:BLOCK + tl, :] = yacd_ref[0, :, c0 + W_SWA:c0 + W_SWA + W_SWA_KV]
    vext[BLOCK:BLOCK + tl, :] = yacd_ref[0, :, c0 + W_SWA + W_SWA_KV:c0 + 2 * W_SWA]
    first = jnp.where(t == 0, 1, 0)
    lane_q = lax.broadcasted_iota(jnp.int32, (BLOCK, BLOCK), 1)
    row_s = lax.broadcasted_iota(jnp.int32, (2 * BLOCK, 1), 0)
    for jb in range(tl // BLOCK):
        proj.append(_dot(y_head, w_ref[0:k_head, jb * n_tile:(jb + 1) * n_tile]))
        r0 = jb * BLOCK
        kblk = kext[r0:r0 + 2 * BLOCK, :]
        vblk = vext[r0:r0 + 2 * BLOCK, :]
        gated = []
        for grp in range(2):
            q = yacd_ref[0, r0:r0 + BLOCK, c0 + grp * BLOCK:c0 + (grp + 1) * BLOCK]
            lhs = jnp.concatenate([jnp.where(lane_q < HEAD_DIM, q, jnp.zeros_like(q)),
                                   jnp.where(lane_q >= HEAD_DIM, q, jnp.zeros_like(q))], axis=0)
            s = _dot_nt(lhs, kblk)
            if jb == 0:
                s = s + bias_ref[grp, first]
            else:
                s = s + bias_ref[grp, 0]
            sink = jnp.where(row_s < BLOCK, sink_ref[SWA_HEAD_PERM[2 * grp]],
                             sink_ref[SWA_HEAD_PERM[2 * grp + 1]])
            m = jnp.maximum(jnp.max(s, axis=-1, keepdims=True), sink)
            p = jnp.exp((s - m).astype(BF16))
            acc = _dot(p, jnp.concatenate([vblk, jnp.ones_like(vblk)], axis=-1))
            es = jnp.exp(sink - m)
            lo = lane_q < HEAD_DIM
            den = (jnp.where(lo, acc[0:BLOCK, BLOCK:2 * BLOCK], acc[BLOCK:2 * BLOCK, BLOCK:2 * BLOCK])
                   + jnp.where(lo, es[0:BLOCK], es[BLOCK:2 * BLOCK]))
            od = jnp.where(lo, acc[0:BLOCK, 0:BLOCK], acc[BLOCK:2 * BLOCK, 0:BLOCK]) / den
            g0 = c0 + 2 * W_SWA + grp * BLOCK
            gate_s = yacd_ref[0, r0:r0 + BLOCK, g0:g0 + BLOCK].astype(F32)
            gated.append(od * gate_s)
        h01, h23 = _unpair_heads(*gated)
        yds[r0:r0 + BLOCK, 0:BLOCK] = h01.astype(BF16)
        yds[r0:r0 + BLOCK, BLOCK:2 * BLOCK] = h23.astype(BF16)
    kext[0:BLOCK, :] = kext[tl:tl + BLOCK, :]
    vext[0:BLOCK, :] = vext[tl:tl + BLOCK, :]

    mixed = jnp.concatenate(proj, axis=-1) + _dot(yds[...], w_ref[k_head:D_MIX, :])
    gate = mod_ref[0][:, 2 * D_MODEL:3 * D_MODEL]
    o_ref[0] = _residual_out(mixed, x_ref[0], gate, fg_ref, final)


def _prompt_c(x, mod_p, yacd, yb, w_out_b, final_g, sink, swa_bias, final, layer):
    nb, l, _ = x.shape
    tl = SEQ_TILE
    full = lambda *shape: pl.BlockSpec(shape, lambda n, t: (0,) * len(shape))
    return pl.pallas_call(
        functools.partial(_pc_kernel, final=final),
        grid=(nb, l // tl),
        in_specs=[
            pl.BlockSpec(memory_space=pltpu.SMEM),
            pl.BlockSpec((1, tl, D_MODEL), lambda n, t: (n, t, 0)),
            pl.BlockSpec((1, 1, 3 * D_MODEL), lambda n, t: (n, 0, 0)),
            pl.BlockSpec((1, tl, Y_WIDTH), lambda n, t: (n, t, 0)),
            pl.BlockSpec((1, N_PAIRS_DIL, N_CLASSES, tl // N_CLASSES, BLOCK),
                         lambda n, t: (n, 0, 0, t, 0)),
            pl.BlockSpec((None, D_MIX, D_MODEL), lambda n, t: (layer, 0, 0)),
            full(1, D_MODEL),
            full(2, 2, 2 * BLOCK, 2 * BLOCK),
        ],
        out_specs=pl.BlockSpec((1, tl, D_MODEL), lambda n, t: (n, t, 0)),
        out_shape=jax.ShapeDtypeStruct((nb, l, D_MODEL), F32),
        scratch_shapes=[
            pltpu.VMEM((N_PAIRS_DIL, tl, BLOCK), F32),
            pltpu.VMEM((BLOCK + tl, W_SWA_KV), BF16),
            pltpu.VMEM((BLOCK + tl, W_SWA_KV), BF16),
            pltpu.VMEM((tl, W_SWA), BF16),
        ],
        compiler_params=_compiler_params(("arbitrary", "arbitrary")),
        name="prompt_c",
    )(sink, x, mod_p, yacd, yb, w_out_b, final_g, swa_bias)


def _sa_kernel(x_ref, mod_ref, g_ref, w_ref, wbd_ref, ps_ref, cw_ref, sp_ref, sc_ref,
               qkv_ref, sw_ref, yac_ref, gates_ref, pst_ref, cst_ref):
    hb = _modulated_norm(x_ref[...], g_ref[...], mod_ref[...]).astype(BF16)
    ns = hb.shape[0]

    zp = _dot(hb, w_ref[:, OFF_POOL:OFF_POOL + 2 * W_POOL])
    pu = zp[:, 0:W_POOL]
    pg = zp[:, W_POOL:2 * W_POOL]
    acc = pu
    sums = {}
    for j in range(1, 16):
        acc = acc + sp_ref[POOL_BUF - j]
        if j + 1 in POOL_WINDOWS:
            sums[j + 1] = acc
    lane = lax.broadcasted_iota(jnp.int32, (ns, W_POOL), 1)
    sel, win = _pool_select(sums, lane)
    diff = sel / win.astype(F32) - pu
    a_out = _dot(diff.astype(BF16), wbd_ref[...]) * ps_ref[...]
    yac_ref[:, 0:W_POOL] = a_out * _silu(pg)
    pst_ref[0:POOL_BUF - 1] = sp_ref[1:POOL_BUF]
    pst_ref[POOL_BUF - 1] = pu

    zc4 = _dot(hb, w_ref[:, OFF_CONV:OFF_CONV + 4 * W_CONV])
    ch = zc4[:, 0:W_CONV]
    cb = zc4[:, W_CONV:2 * W_CONV]
    cc = zc4[:, 2 * W_CONV:3 * W_CONV]
    cg = zc4[:, 3 * W_CONV:4 * W_CONV]
    zc = cc * ch
    conv = (cw_ref[0:1, :] * sc_ref[:, 0:W_CONV] + cw_ref[1:2, :] * sc_ref[:, W_CONV:2 * W_CONV]
            + cw_ref[2:3, :] * zc)
    yac_ref[:, W_POOL:W_POOL + W_CONV] = cb * conv * _silu(cg)
    cst_ref[:, 0:W_CONV] = sc_ref[:, W_CONV:2 * W_CONV]
    cst_ref[:, W_CONV:2 * W_CONV] = zc

    zd = _dot(hb, w_ref[:, OFF_DIL:OFF_DIL + 4 * W_DIL])
    qkv_ref[:, 0:W_DIL] = zd[:, 0:W_DIL] * QK_SCALE
    qkv_ref[:, W_DIL:3 * W_DIL] = zd[:, W_DIL:3 * W_DIL]
    gates_ref[:, 0:W_DIL] = _silu(zd[:, 3 * W_DIL:4 * W_DIL])

    zs = _dot(hb, w_ref[:, OFF_SWA:OFF_SWA + 3 * W_SWA])
    sw_ref[:, 0:W_SWA] = zs[:, 0:W_SWA] * QK_SCALE
    sw_ref[:, W_SWA:2 * W_SWA] = zs[:, W_SWA:2 * W_SWA]
    gates_ref[:, W_DIL:W_DIL + W_SWA] = _silu(zs[:, 2 * W_SWA:3 * W_SWA])


def _whole_spec(shape):
    return pl.BlockSpec(shape, lambda *_: (0,) * len(shape))


def _layer_spec(shape, layer):
    return pl.BlockSpec((None,) + tuple(shape[1:]), lambda *_: (layer,) + (0,) * (len(shape) - 1))


def _sample_a(xs, mod_s, norm_g, w_in_b, wbd, pool_scale, conv_w, sp, sc, layer):
    ns = xs.shape[0]
    shapes = [
        jax.ShapeDtypeStruct((ns, 3 * W_DIL), F32),
        jax.ShapeDtypeStruct((ns, 2 * W_SWA), F32),
        jax.ShapeDtypeStruct((ns, W_POOL + W_CONV), F32),
        jax.ShapeDtypeStruct((ns, W_DIL + W_SWA), F32),
        jax.ShapeDtypeStruct((POOL_BUF, ns, W_POOL), F32),
        jax.ShapeDtypeStruct((ns, CONV_BUF * W_CONV), F32),
    ]
    args = (xs, mod_s, norm_g, w_in_b, wbd, pool_scale, conv_w, sp, sc)
    in_specs = [_whole_spec(a.shape) for a in args]
    in_specs[3] = _layer_spec(w_in_b.shape, layer)
    return pl.pallas_call(
        _sa_kernel,
        grid=(1,),
        in_specs=in_specs,
        out_specs=[_whole_spec(s.shape) for s in shapes],
        out_shape=shapes,
        compiler_params=_compiler_params(("arbitrary",)),
        name="sample_a",
    )(*args)


def _cache_attention(dq_ref, sq_ref, dbias_ref, mult_ref, sbias_ref, sink_ref, kt_ref, vt_ref,
                     ck_ref, cv_ref, bo_ref, do_ref):
    row_d = lax.broadcasted_iota(jnp.int32, (8, DIL_MAX), 0)
    row_o = lax.broadcasted_iota(jnp.int32, (8, HEAD_DIM), 0)
    row_s = lax.broadcasted_iota(jnp.int32, (8, SWA_WINDOW), 0)
    eye = (lax.broadcasted_iota(jnp.int32, (HEAD_DIM, HEAD_DIM), 0)
           == lax.broadcasted_iota(jnp.int32, (HEAD_DIM, HEAD_DIM), 1))
    for j in range(DEC_TILE):
        q = dq_ref[j, 0]
        s = jnp.zeros((8, DIL_MAX), F32)
        for h in range(N_HEADS_DIL):
            q_col = jnp.sum(jnp.where(eye, q[h:h + 1, :], 0.0), axis=1, keepdims=True)
            s_h = jnp.sum(kt_ref[0, j, h] * q_col, axis=0, keepdims=True)
            s = jnp.where(row_d == h, s_h, s)
            yield
        s = s + dbias_ref[...]
        s_self = jnp.sum(q * dq_ref[j, 1], axis=-1, keepdims=True)
        m = jnp.maximum(jnp.max(s, axis=-1, keepdims=True), s_self)
        p = jnp.exp(s - m) * mult_ref[...]
        p_self = float(len(DIL_CONFIGS)) * jnp.exp(s_self - m)
        den = jnp.sum(p, axis=-1, keepdims=True) + p_self
        yield
        acc = jnp.zeros((8, HEAD_DIM), F32)
        for h in range(N_HEADS_DIL):
            o_col = jnp.sum(vt_ref[0, j, h] * p[h:h + 1, :], axis=1, keepdims=True)
            o_row = jnp.sum(jnp.where(eye, o_col, 0.0), axis=0, keepdims=True)
            acc = jnp.where(row_o == h, o_row, acc)
            yield
        bo_ref[j] = (acc + p_self * dq_ref[j, 2]) / den

        q = sq_ref[j, 0]
        qb = q.astype(BF16)
        s = jnp.where(row_s < 2, _dot(qb, ck_ref[0, j, 0].astype(BF16)),
                      _dot(qb, ck_ref[0, j, 1].astype(BF16))) + sbias_ref[...]
        w_self = jnp.sum(q * sq_ref[j, 1], axis=-1, keepdims=True)
        sink = sink_ref[...][:, 0:1]
        m = jnp.maximum(jnp.maximum(jnp.max(s, axis=-1, keepdims=True), w_self), sink)
        p = jnp.exp(s - m)
        pw = jnp.exp(w_self - m)
        den = jnp.sum(p, axis=-1, keepdims=True) + pw + jnp.exp(sink - m)
        pb = p.astype(BF16)
        yield
        acc = jnp.where(row_o < 2, _dot_nt(pb, cv_ref[0, j, 0].astype(BF16)),
                        _dot_nt(pb, cv_ref[0, j, 1].astype(BF16)))
        do_ref[j] = (acc + pw * sq_ref[j, 2]) / den
        yield


def _advance(stages, n):
    for _ in range(n):
        next(stages, None)


def _sc_kernel(x_ref, mod_ref, yac_ref, gates_ref, bo_ref, do_ref, w_ref, fg_ref, o_ref, *, final):
    yac = yac_ref[...]
    gates = gates_ref[...]
    y = jnp.concatenate([yac[:, 0:W_POOL], bo_ref[...] * gates[:, 0:W_DIL],
                         yac[:, W_POOL:W_POOL + W_CONV], do_ref[...] * gates[:, W_DIL:W_DIL + W_SWA]],
                        axis=-1).astype(BF16)
    gate = mod_ref[...][:, 2 * D_MODEL:3 * D_MODEL]
    o_ref[...] = _mix_out(y, x_ref[...], gate, w_ref, fg_ref, final)


def _sample_c(xs, mod_s, yac, gates, bo, do, w_out_b, final_g, final, layer):
    args = (xs, mod_s, yac, gates, bo, do, w_out_b, final_g)
    in_specs = [_whole_spec(a.shape) for a in args]
    in_specs[6] = _layer_spec(w_out_b.shape, layer)
    return pl.pallas_call(
        functools.partial(_sc_kernel, final=final),
        grid=(1,),
        in_specs=in_specs,
        out_specs=_whole_spec(xs.shape),
        out_shape=jax.ShapeDtypeStruct(xs.shape, F32),
        compiler_params=_compiler_params(("arbitrary",)),
        name="sample_c",
    )(*args)


def _band_bias_dil4(slope_lo, slope_hi, variant):
    sub = BLOCK // 4
    qidx = np.arange(BLOCK)[:, None]
    kidx = np.arange(2 * BLOCK)[None, :]
    q_step = 4 * (qidx % sub + (sub if variant == 0 else 0)) + qidx // sub
    k_step = 4 * (kidx % (2 * sub)) + kidx // (2 * sub)
    off = q_step - k_step
    valid = (off >= 0) & (off <= BLOCK)
    out = []
    for s in (slope_lo, slope_hi):
        out.append(np.where(valid, -(np.float32(s) * np.float32(4)) * off.astype(np.float32),
                            np.float32(NEG_INF)).astype(np.float32))
    return np.concatenate(out, axis=0)


def _prompt_bias_tables():
    dil = _alibi_slopes(N_HEADS_DIL)

    def table(hp, d, var):
        if d == 4:
            return _band_bias_dil4(dil[2 * hp], dil[2 * hp + 1], var)
        return _band_bias(dil[2 * hp], dil[2 * hp + 1], d, var)

    dil_bias = np.stack([
        np.stack([np.stack([table(hp, d, var) for var in (0, 1)])
                  for _, d in DIL_CONFIGS]) for hp in range(N_PAIRS_DIL)])
    swa = _alibi_slopes(N_HEADS_SWA)
    swa_bias = np.stack([
        np.stack([_band_bias(swa[SWA_HEAD_PERM[2 * g]], swa[SWA_HEAD_PERM[2 * g + 1]], 1, var)
                  for var in (0, 2)]) for g in range(2)])
    return jnp.asarray(dil_bias), jnp.asarray(swa_bias)


def _sample_bias_tables(r):
    dil = _alibi_slopes(N_HEADS_DIL)
    dist = (r - np.arange(r)).astype(np.float32)
    mult = np.zeros((1, r), np.float32)
    for window, d in DIL_CONFIGS:
        mult[0] += ((dist <= window) & (dist % d == 0)).astype(np.float32)
    dbias = np.zeros((8, r), np.float32)
    for h in range(N_HEADS_DIL):
        dbias[h] = np.where(mult[0] > 0, -np.float32(dil[h]) * dist, np.float32(NEG_INF))
    swa = _alibi_slopes(N_HEADS_SWA)
    sdist = (SWA_WINDOW - np.arange(SWA_WINDOW)).astype(np.float32)
    sbias = np.zeros((8, SWA_WINDOW), np.float32)
    for h in range(N_HEADS_SWA):
        sbias[h] = -np.float32(swa[h]) * sdist
    return jnp.asarray(dbias), jnp.asarray(mult), jnp.asarray(sbias)


def kernel(x_prompt, x_sample, c_prompt, c_sample, state_pool, cache_dil_k, cache_dil_v, state_conv, cache_swa_k, cache_swa_v, norm_g, w_ada, b_ada, w_in, w_pool, pool_scale, conv_w, swa_sink, w_out, final_g):
    depth = w_in.shape[0]
    nb, l, _ = x_prompt.shape
    ns = x_sample.shape[0]
    assert x_sample.shape[1] == 1 and l % CHUNK == 0 and l >= DIL_MAX
    assert ns == nb * (l // SEQ_TILE) * DEC_TILE
    assert cache_dil_k.shape[2] == DIL_MAX and cache_swa_k.shape[2] == SWA_WINDOW

    w_in_b = w_in.astype(BF16)
    w_out_b = w_out.astype(BF16)
    eye = jnp.eye(len(POOL_WINDOWS), dtype=F32)
    wbd = jnp.einsum("dgce,gh->dgche", w_pool, eye).reshape(depth, W_POOL, W_POOL).astype(BF16)
    sink_rows = jnp.zeros((depth, 8), F32).at[:, 0:N_HEADS_SWA].set(swa_sink)
    sink_rows = jnp.broadcast_to(sink_rows[:, :, None], (depth, 8, BLOCK))

    dil_bias, swa_bias = _prompt_bias_tables()
    dbias, mult, sbias = _sample_bias_tables(cache_dil_k.shape[2])

    mod = _ada(jnp.concatenate([c_prompt, c_sample], axis=0), w_ada, b_ada)
    fg = final_g.reshape(1, D_MODEL)

    xp = x_prompt
    xs = x_sample.reshape(ns, D_MODEL)
    kt = jnp.transpose(cache_dil_k, (0, 1, 3, 4, 2))
    vt = jnp.transpose(cache_dil_v, (0, 1, 3, 4, 2))
    ckt = jnp.transpose(cache_swa_k, (0, 1, 3, 4, 2))
    cvt = jnp.transpose(cache_swa_v, (0, 1, 3, 4, 2))
    sp_all = jnp.transpose(state_pool, (0, 2, 1, 3))
    sc_all = state_conv.reshape(depth, ns, CONV_BUF * W_CONV)
    pad_heads = lambda a: jnp.pad(a, ((0, 0), (0, 0), (0, 8 - a.shape[2]), (0, 0)))
    outs = [[] for _ in range(12)]
    carried = None
    for i in range(depth):
        final = i == depth - 1
        mod_p = mod[i, 0:nb].reshape(nb, 1, 3 * D_MODEL)
        mod_s = mod[i, nb:nb + ns]
        g = norm_g[i].reshape(1, D_MODEL)
        ps = pool_scale[i].reshape(1, W_POOL)

        qkv, sw, yac, gates, pst_s, cst_s = _sample_a(
            xs, mod_s, g, w_in_b, wbd[i], ps, conv_w[i], sp_all[i], sc_all[i], i)
        dq3 = pad_heads(qkv.reshape(ns, 3, N_HEADS_DIL, HEAD_DIM))
        sq4 = sw[:, 0:W_SWA].reshape(ns, 1, N_HEADS_SWA, HEAD_DIM)
        skv = jnp.repeat(sw[:, W_SWA:W_SWA + 2 * W_SWA_KV].reshape(ns, 2, 2, HEAD_DIM), 2, axis=2)
        sq3 = pad_heads(jnp.concatenate([sq4, skv], axis=1))

        cache_args = (dq3, sq3, dbias, mult, sbias, sink_rows[i], kt, vt, ckt, cvt)
        q1, k1, v1, qm, km, vm, gm, kc, vc, yacd, pst, cst, skc, svc, bo_a, do_a = _prompt_a(
            xp, mod_p, g, w_in_b, wbd[i], ps, conv_w[i], i, depth, 0, cache_args, carried)
        carried = (kc, vc)
        yb = _prompt_b(q1, k1, v1, qm, km, vm, gm, dil_bias)
        xp = _prompt_c(xp, mod_p, yacd, yb, w_out_b, fg, swa_sink[i], swa_bias, final, i)

        bo = bo_a[:, 0:N_HEADS_DIL].reshape(ns, W_DIL)
        do = do_a[:, 0:N_HEADS_SWA].reshape(ns, W_SWA)
        xs = _sample_c(xs, mod_s, yac, gates, bo, do, w_out_b, fg, final, i)

        unfold = lambda a, h: jnp.transpose(a.reshape(nb, h, HEAD_DIM, a.shape[-1]), (0, 3, 1, 2))
        skc, svc = unfold(skc, 2), unfold(svc, 2)
        outs[0].append(pst)
        outs[1].append(jnp.transpose(pst_s, (1, 0, 2)))
        outs[4].append(qkv[:, W_DIL:2 * W_DIL].reshape(ns, 1, N_HEADS_DIL, HEAD_DIM))
        outs[5].append(qkv[:, 2 * W_DIL:3 * W_DIL].reshape(ns, 1, N_HEADS_DIL, HEAD_DIM))
        outs[6].append(cst)
        outs[7].append(cst_s.reshape(ns, CONV_BUF, W_CONV))
        outs[8].append(skc)
        outs[9].append(svc)
        outs[10].append(sw[:, W_SWA:W_SWA + W_SWA_KV].reshape(ns, 1, 2, HEAD_DIM))
        outs[11].append(sw[:, W_SWA + W_SWA_KV:W_SWA + 2 * W_SWA_KV].reshape(ns, 1, 2, HEAD_DIM))

    for k, cache in zip((2, 3), carried):
        outs[k] = jnp.transpose(cache.reshape(depth, nb, N_HEADS_DIL, HEAD_DIM, DIL_MAX), (0, 1, 4, 2, 3))
    return (xp, xs.reshape(ns, 1, D_MODEL)) + tuple(
        o if not isinstance(o, list) else jnp.stack(o) for o in outs)
```

```python
import functools
import math

import numpy as np
import jax
import jax.numpy as jnp
from jax import lax
from jax.experimental import pallas as pl
from jax.experimental.pallas import tpu as pltpu

F32 = jnp.float32
BF16 = jnp.bfloat16

D_MODEL = 1024
HEAD_DIM = 64
BLOCK = 128
POOL_WINDOWS = (2, 4, 8, 16)
POOL_GROUP = 64
W_POOL = 256
POOL_BUF = 15
DIL_CONFIGS = ((128, 1), (512, 4), (2048, 16))
DIL_MAX = 2048
N_HEADS_DIL = 6
N_PAIRS_DIL = 3
N_CLASSES = 16
W_DIL = 384
W_CONV = 256
CONV_BUF = 2
N_HEADS_SWA = 4
W_SWA = 256
W_SWA_KV = 128
SWA_WINDOW = 128
D_MIX = 1152
D_PROJ = 3840
RMS_EPS = 1e-6
QK_SCALE = 1.0 / math.sqrt(HEAD_DIM)

OFF_POOL = 0
OFF_DIL = 512
OFF_CONV = 2048
OFF_SWA = 3072
Y_WIDTH = W_POOL + W_CONV + 3 * W_SWA
SWA_HEAD_PERM = (0, 3, 1, 2)

SEQ_TILE = 512
CHUNK = 2048
BAND_SKEW = 1
DEC_TILE = 2
RIDE_STAGES = 15 * DEC_TILE
VMEM_LIMIT = 60 * 1024 * 1024

NEG_INF = float("-inf")


def _silu(v):
    return v * jax.nn.sigmoid(v)


def _dot(a, b):
    return jnp.dot(a, b, preferred_element_type=F32)


def _dot_nt(a, b):
    return lax.dot_general(a, b, (((1,), (1,)), ((), ())), preferred_element_type=F32)


def _alibi_slopes(n):
    return [2.0 ** (-8.0 * (h + 1) / n) for h in range(n)]


def _band_bias(slope_lo, slope_hi, dist_scale, variant):
    qi = np.arange(BLOCK)[:, None]
    kj = np.arange(2 * BLOCK)[None, :]
    if variant == 1:
        off = qi - kj
    else:
        off = qi - kj + BLOCK
    valid = (off >= 0) & (off <= BLOCK)
    if variant == 2:
        valid = valid & (kj >= BLOCK)
    out = []
    for s in (slope_lo, slope_hi):
        b = np.where(valid, -(np.float32(s) * np.float32(dist_scale)) * off.astype(np.float32),
                     np.float32(NEG_INF))
        out.append(b.astype(np.float32))
    return np.concatenate(out, axis=0)


def _compiler_params(sem):
    return pltpu.CompilerParams(dimension_semantics=sem, vmem_limit_bytes=VMEM_LIMIT)


def _ada_kernel(c_ref, w_ref, b_ref, o_ref):
    s = _silu(c_ref[...]).astype(BF16)
    o_ref[0] = _dot(s, w_ref[0].astype(BF16)) + b_ref[0]


def _ada(c_all, w_ada_b, b_ada):
    depth = w_ada_b.shape[0]
    rows = c_all.shape[0]
    return pl.pallas_call(
        _ada_kernel,
        grid=(depth, 3),
        in_specs=[
            pl.BlockSpec((rows, D_MODEL), lambda i, j: (0, 0)),
            pl.BlockSpec((1, D_MODEL, D_MODEL), lambda i, j: (i, 0, j)),
            pl.BlockSpec((1, 1, D_MODEL), lambda i, j: (i, 0, j)),
        ],
        out_specs=pl.BlockSpec((1, rows, D_MODEL), lambda i, j: (i, 0, j)),
        out_shape=jax.ShapeDtypeStruct((depth, rows, 3 * D_MODEL), F32),
        compiler_params=_compiler_params(("arbitrary", "arbitrary")),
        name="ada",
    )(c_all, w_ada_b, b_ada.reshape(depth, 1, 3 * D_MODEL))


def _modulated_norm(x, g, mod_row):
    ms = jnp.mean(x * x, axis=-1, keepdims=True)
    shift = mod_row[:, 0:D_MODEL]
    scale = mod_row[:, D_MODEL:2 * D_MODEL]
    return x * lax.rsqrt(ms + RMS_EPS) * (g * (1.0 + scale)) + shift


def _pair_heads(v):
    left, right = v[:, 0:BLOCK], v[:, BLOCK:2 * BLOCK]
    low = lax.broadcasted_iota(jnp.int32, left.shape, 1) < HEAD_DIM
    return jnp.concatenate([jnp.where(low, left, right),
                            pltpu.roll(jnp.where(low, right, left), HEAD_DIM, axis=1)], axis=-1)


def _unpair_heads(a, b):
    low = lax.broadcasted_iota(jnp.int32, a.shape, 1) < HEAD_DIM
    b_swapped = pltpu.roll(b, HEAD_DIM, axis=1)
    return jnp.where(low, a, b_swapped), jnp.where(low, b_swapped, a)


def _pool_select(sums, lane):
    grp = lane // POOL_GROUP
    sel = jnp.where(grp == 0, sums[2], jnp.where(grp == 1, sums[4],
                                                 jnp.where(grp == 2, sums[8], sums[16])))
    win = jnp.where(grp == 0, 2, jnp.where(grp == 1, 4, jnp.where(grp == 2, 8, 16)))
    return sel, win


def _pa_kernel(x_ref, mod_ref, g_ref, w_ref, wbd_ref, ps_ref, cw_ref,
               dq_ref, sq_ref, dbias_ref, mult_ref, sbias_ref, ssink_ref, kt_ref, vt_ref, ck_ref, cv_ref,
               q1_ref, k1_ref, v1_ref, qm_ref, km_ref, vm_ref, gm_ref, kc_ref, vc_ref, y_ref,
               pst_ref, cst_ref, skc_ref, svc_ref, bo_ref, do_ref,
               uext, zcext, zds, z4s, *, later_layers=0):
    tl = SEQ_TILE
    t = pl.program_id(1)

    @pl.when(t == 0)
    def _():
        for k in range(len(POOL_WINDOWS)):
            uext[k, 0:16, :] = jnp.zeros((16, W_POOL), F32)
        zcext[0:8, :] = jnp.zeros((8, W_CONV), F32)

    ride = _cache_attention(dq_ref, sq_ref, dbias_ref, mult_ref, sbias_ref, ssink_ref, kt_ref, vt_ref,
                            ck_ref, cv_ref, bo_ref, do_ref)

    _advance(ride, 3 * DEC_TILE)
    hb = _modulated_norm(x_ref[0], g_ref[...], mod_ref[0]).astype(BF16)

    zp = _dot(hb, w_ref[:, OFF_POOL:OFF_POOL + 2 * W_POOL])
    zd = _dot(hb, w_ref[:, OFF_DIL:OFF_DIL + 4 * W_DIL])
    _advance(ride, 3 * DEC_TILE)

    pu = zp[:, 0:W_POOL]
    pg = zp[:, W_POOL:2 * W_POOL]
    uext[0, 16:16 + tl, :] = pu
    sums = {}
    level = pu
    for k, w in enumerate(POOL_WINDOWS):
        shift = w // 2
        level = level + uext[k, 16 - shift:16 - shift + tl, :]
        sums[w] = level
        if k + 1 < len(POOL_WINDOWS):
            uext[k + 1, 16:16 + tl, :] = level
    lane = lax.broadcasted_iota(jnp.int32, (tl, W_POOL), 1)
    gpos = lax.broadcasted_iota(jnp.int32, (tl, W_POOL), 0) + t * tl
    sel, win = _pool_select(sums, lane)
    cnt = jnp.minimum(gpos + 1, win).astype(F32)
    diff = sel / cnt - pu
    a_out = _dot(diff.astype(BF16), wbd_ref[...]) * ps_ref[...]
    y_ref[0, :, 0:W_POOL] = (a_out * _silu(pg)).astype(BF16)
    pst_ref[0] = uext[0, tl + 1:tl + 16, :]
    for k in range(len(POOL_WINDOWS)):
        uext[k, 0:16, :] = uext[k, tl:tl + 16, :]

    zs = _dot(hb, w_ref[:, OFF_SWA:OFF_SWA + 3 * W_SWA])
    _advance(ride, 3 * DEC_TILE)
    kc_ref[0, 0] = zd[:, W_DIL:2 * W_DIL].T
    vc_ref[0, 0] = zd[:, 2 * W_DIL:3 * W_DIL].T
    for d in range(1, 1 + later_layers):
        kc_ref[d, 0] = jnp.zeros((W_DIL, tl), F32)
        vc_ref[d, 0] = jnp.zeros((W_DIL, tl), F32)
    for hp in range(N_PAIRS_DIL):
        _advance(ride, DEC_TILE)
        lo = hp * BLOCK
        zds[0] = zd[:, lo:lo + BLOCK] * QK_SCALE
        zds[1] = zd[:, W_DIL + lo:W_DIL + lo + BLOCK]
        zds[2] = zd[:, 2 * W_DIL + lo:2 * W_DIL + lo + BLOCK]
        zds[3] = _silu(zd[:, 3 * W_DIL + lo:3 * W_DIL + lo + BLOCK])
        q1_ref[0, hp] = zds[0].astype(BF16)
        k1_ref[0, hp] = zds[1].astype(BF16)
        v1_ref[0, hp] = zds[2].astype(BF16)
        quarter = tl // 4
        for a in range(4):
            for blk in range(4):
                z4s[blk, a] = zds[blk, pl.ds(a, quarter, stride=4), :]
        for b in range(4):
            for a in range(4):
                c = 4 * b + a
                rows = pl.ds(b, tl // N_CLASSES, stride=4)
                qm_ref[0, hp, 0, c] = z4s[0, a, rows, :].astype(BF16)
                km_ref[0, hp, 0, c] = z4s[1, a, rows, :].astype(BF16)
                vm_ref[0, hp, 0, c] = z4s[2, a, rows, :].astype(BF16)
                gm_ref[0, hp, 0, c] = z4s[3, a, rows, :].astype(BF16)

    zc4 = _dot(hb, w_ref[:, OFF_CONV:OFF_CONV + 4 * W_CONV])
    _advance(ride, 3 * DEC_TILE)

    sk = zs[:, W_SWA:W_SWA + W_SWA_KV]
    sv = zs[:, W_SWA + W_SWA_KV:W_SWA + 2 * W_SWA_KV]
    skc_ref[0] = sk[tl - SWA_WINDOW:tl, :].T
    svc_ref[0] = sv[tl - SWA_WINDOW:tl, :].T
    c0 = W_POOL + W_CONV
    y_ref[0, :, c0:c0 + W_SWA] = _pair_heads(zs[:, 0:W_SWA] * QK_SCALE).astype(BF16)
    y_ref[0, :, c0 + W_SWA:c0 + 2 * W_SWA] = zs[:, W_SWA:2 * W_SWA].astype(BF16)
    y_ref[0, :, c0 + 2 * W_SWA:c0 + 3 * W_SWA] = _pair_heads(_silu(zs[:, 2 * W_SWA:3 * W_SWA])).astype(BF16)
    _advance(ride, RIDE_STAGES)

    ch = zc4[:, 0:W_CONV]
    cb = zc4[:, W_CONV:2 * W_CONV]
    cc = zc4[:, 2 * W_CONV:3 * W_CONV]
    cg = zc4[:, 3 * W_CONV:4 * W_CONV]
    zc = cc * ch
    zcext[8:8 + tl, :] = zc
    conv = (cw_ref[0:1, :] * zcext[6:6 + tl, :] + cw_ref[1:2, :] * zcext[7:7 + tl, :]
            + cw_ref[2:3, :] * zc)
    y_ref[0, :, W_POOL:W_POOL + W_CONV] = (cb * conv * _silu(cg)).astype(BF16)
    cst_ref[0] = zcext[tl + 6:tl + 8, :]
    zcext[0:8, :] = zcext[tl:tl + 8, :]


def _cache_attention_specs(layer, row0, nb, nt, r, swr):
    bt = DEC_TILE
    blk0 = row0 // bt
    full = lambda *shape: pl.BlockSpec(shape, lambda n, t: (0,) * len(shape))
    srow_spec = pl.BlockSpec((bt, 3, 8, HEAD_DIM), lambda n, t: (blk0 + n * nt + t, 0, 0, 0))
    dil_spec = pl.BlockSpec((1, bt, N_HEADS_DIL, HEAD_DIM, r),
                            lambda n, t: (layer, blk0 + n * nt + t, 0, 0, 0))
    swa_spec = pl.BlockSpec((1, bt, 2, HEAD_DIM, swr), lambda n, t: (layer, blk0 + n * nt + t, 0, 0, 0))
    sout_spec = pl.BlockSpec((bt, 8, HEAD_DIM), lambda n, t: (n * nt + t, 0, 0))
    sout_shape = jax.ShapeDtypeStruct((nb * nt * bt, 8, HEAD_DIM), F32)
    in_specs = [srow_spec, srow_spec, full(8, r), full(1, r), full(8, swr), full(8, BLOCK),
                dil_spec, dil_spec, swa_spec, swa_spec]
    return in_specs, [sout_spec, sout_spec], [sout_shape, sout_shape]


N_PA_INPUTS = 17


def _pa_kernel_carry(*refs):
    _pa_kernel(*refs[:N_PA_INPUTS], *refs[N_PA_INPUTS + 2:])


def _prompt_a(x, mod_p, norm_g, w_in_b, wbd, pool_scale, conv_w, layer, depth, row0, cache_args, carried):
    nb, l, _ = x.shape
    tl = SEQ_TILE
    nt = l // tl
    cache_t0 = (l - DIL_MAX) // tl
    c_in, c_out, c_shape = _cache_attention_specs(layer, row0, nb, nt, cache_args[6].shape[-1],
                                                  cache_args[8].shape[-1])
    pair_spec = pl.BlockSpec((1, N_PAIRS_DIL, tl, BLOCK), lambda n, t: (n, 0, t, 0))
    pair_shape = jax.ShapeDtypeStruct((nb, N_PAIRS_DIL, l, BLOCK), BF16)
    cm_spec = pl.BlockSpec((1, N_PAIRS_DIL, 1, N_CLASSES, tl // N_CLASSES, BLOCK),
                           lambda n, t: (n, 0, t, 0, 0, 0))
    cm_shape = jax.ShapeDtypeStruct((nb, N_PAIRS_DIL, nt, N_CLASSES, tl // N_CLASSES, BLOCK), BF16)
    cache_layers = depth if carried is None else 1
    cache_spec = pl.BlockSpec((cache_layers, 1, W_DIL, tl),
                              lambda n, t: (layer, n, 0, jnp.maximum(t - cache_t0, 0)))
    cache_shape = jax.ShapeDtypeStruct((depth, nb, W_DIL, DIL_MAX), F32)
    full = lambda *shape: pl.BlockSpec(shape, lambda n, t: (0,) * len(shape))
    in_specs = [
        pl.BlockSpec((1, tl, D_MODEL), lambda n, t: (n, t, 0)),
        pl.BlockSpec((1, 1, 3 * D_MODEL), lambda n, t: (n, 0, 0)),
        full(1, D_MODEL),
        pl.BlockSpec((None, D_MODEL, D_PROJ), lambda n, t: (layer, 0, 0), pipeline_mode=pl.Buffered(1)),
        full(W_POOL, W_POOL),
        full(1, W_POOL),
        full(3, W_CONV),
    ] + c_in
    assert len(in_specs) == N_PA_INPUTS
    extra, aliases = (), {}
    if carried is not None:
        in_specs = in_specs + [pl.BlockSpec(memory_space=pl.ANY)] * 2
        extra = tuple(carried)
        aliases = {N_PA_INPUTS: 7, N_PA_INPUTS + 1: 8}
    return pl.pallas_call(
        functools.partial(_pa_kernel, later_layers=depth - 1) if carried is None else _pa_kernel_carry,
        grid=(nb, nt),
        in_specs=in_specs,
        input_output_aliases=aliases,
        out_specs=[
            pair_spec, pair_spec, pair_spec, cm_spec, cm_spec, cm_spec, cm_spec,
            cache_spec, cache_spec,
            pl.BlockSpec((1, tl, Y_WIDTH), lambda n, t: (n, t, 0)),
            pl.BlockSpec((1, POOL_BUF, W_POOL), lambda n, t: (n, 0, 0)),
            pl.BlockSpec((1, CONV_BUF, W_CONV), lambda n, t: (n, 0, 0)),
            pl.BlockSpec((1, SWA_WINDOW, W_SWA_KV), lambda n, t: (n, 0, 0)),
            pl.BlockSpec((1, SWA_WINDOW, W_SWA_KV), lambda n, t: (n, 0, 0)),
        ] + c_out,
        out_shape=[
            pair_shape, pair_shape, pair_shape, cm_shape, cm_shape, cm_shape, cm_shape,
            cache_shape, cache_shape,
            jax.ShapeDtypeStruct((nb, l, Y_WIDTH), BF16),
            jax.ShapeDtypeStruct((nb, POOL_BUF, W_POOL), F32),
            jax.ShapeDtypeStruct((nb, CONV_BUF, W_CONV), F32),
            jax.ShapeDtypeStruct((nb, SWA_WINDOW, W_SWA_KV), F32),
            jax.ShapeDtypeStruct((nb, SWA_WINDOW, W_SWA_KV), F32),
        ] + c_shape,
        scratch_shapes=[
            pltpu.VMEM((len(POOL_WINDOWS), 16 + tl, W_POOL), F32),
            pltpu.VMEM((8 + tl, W_CONV), F32),
            pltpu.VMEM((4, tl, BLOCK), F32),
            pltpu.VMEM((4, 4, tl // 4, BLOCK), F32),
        ],
        compiler_params=_compiler_params(("arbitrary", "arbitrary")),
        name="prompt_a",
    )(x, mod_p, norm_g, w_in_b, wbd, pool_scale, conv_w, *cache_args, *extra)


def _band_scores(q, kblk, bias, lane_q):
    lhs = jnp.concatenate([jnp.where(lane_q < HEAD_DIM, q, jnp.zeros_like(q)),
                           jnp.where(lane_q >= HEAD_DIM, q, jnp.zeros_like(q))], axis=0)
    return _dot_nt(lhs, kblk) + bias


def _band_values(s, vblk, lane_q):
    m = jnp.max(s, axis=-1, keepdims=True)
    p = jnp.exp((s - m).astype(BF16))
    acc = _dot(p, jnp.concatenate([vblk, jnp.ones_like(vblk)], axis=-1))
    lo = lane_q < HEAD_DIM
    return (jnp.where(lo, acc[0:BLOCK, 0:BLOCK], acc[BLOCK:2 * BLOCK, 0:BLOCK]),
            jnp.where(lo, m[0:BLOCK], m[BLOCK:2 * BLOCK]),
            jnp.where(lo, acc[0:BLOCK, BLOCK:2 * BLOCK], acc[BLOCK:2 * BLOCK, BLOCK:2 * BLOCK]))


def _run_skewed(tasks, skew):
    pending = []
    for scores_fn, finish_fn in tasks:
        pending.append((finish_fn, scores_fn()))
        if len(pending) > skew:
            fn, s = pending.pop(0)
            fn(s)
    for fn, s in pending:
        fn(s)


def _pb_kernel(bias_ref, q1, k1, v1, qm, km, vm, gm, outm, s1, s4, s16):
    c = pl.program_id(2)
    first = jnp.where(c == 0, 1, 0)
    lane_q = lax.broadcasted_iota(jnp.int32, (BLOCK, BLOCK), 1)
    sub = BLOCK // 4
    assert sub == SEQ_TILE // N_CLASSES
    chunk_tiles = CHUNK // SEQ_TILE

    tasks = []

    def dil1_task(j):
        r0 = j * BLOCK
        if j == 0:
            start = jnp.maximum(c * (CHUNK // BLOCK) - 1, 0) * BLOCK
            var = first
        else:
            start = (c * (CHUNK // BLOCK) + (j - 1)) * BLOCK
            var = 0
        start = pl.multiple_of(start, BLOCK)

        def scores():
            return _band_scores(q1[0, 0, r0:r0 + BLOCK, :], k1[0, 0, pl.ds(start, 2 * BLOCK), :],
                                bias_ref[0, 0, var], lane_q)

        def finish(s):
            parts = _band_values(s, v1[0, 0, pl.ds(start, 2 * BLOCK), :], lane_q)
            for k, part in enumerate(parts):
                s1[k, r0:r0 + BLOCK, :] = part

        return scores, finish

    def cm_rows(ref, cl, tile0, ntiles):
        return jnp.concatenate([ref[0, 0, tile0 + k, cl] for k in range(ntiles)], axis=0)

    def dil4_task(c4, j):
        i0 = j * sub
        if j == 0:
            ktile = jnp.maximum(c * chunk_tiles - 1, 0)
            var = first
        else:
            ktile = c * chunk_tiles + j - 1
            var = 0
        classes = [4 * cc + c4 for cc in range(4)]

        def scores():
            q = jnp.concatenate([qm[0, 0, j, cl] for cl in classes], axis=0)
            kblk = jnp.concatenate([cm_rows(km, cl, ktile, 2) for cl in classes], axis=0)
            return _band_scores(q, kblk, bias_ref[0, 1, var], lane_q)

        def finish(s):
            vblk = jnp.concatenate([cm_rows(vm, cl, ktile, 2) for cl in classes], axis=0)
            parts = _band_values(s, vblk, lane_q)
            for k, part in enumerate(parts):
                for cc, cl in enumerate(classes):
                    s4[k, cl, i0:i0 + sub, :] = part[cc * sub:(cc + 1) * sub]

        return scores, finish

    tile16 = jnp.maximum(c - 1, 0) * chunk_tiles

    def dil16_task(cl):
        def scores():
            return _band_scores(cm_rows(qm, cl, 0, chunk_tiles), cm_rows(km, cl, tile16, 2 * chunk_tiles),
                                bias_ref[0, 2, first], lane_q)

        def finish(s):
            parts = _band_values(s, cm_rows(vm, cl, tile16, 2 * chunk_tiles), lane_q)
            for k, part in enumerate(parts):
                s16[k, cl] = part

        return scores, finish

    tasks += [dil1_task(j) for j in range(CHUNK // BLOCK)]
    tasks += [dil4_task(c4, j) for c4 in range(4) for j in range(4)]
    tasks += [dil16_task(cl) for cl in range(N_CLASSES)]
    _run_skewed(tasks, BAND_SKEW)

    for cl in range(N_CLASSES):
        rows = pl.ds(cl, BLOCK, stride=N_CLASSES)
        ma, mb, mc = s1[1, rows, :], s4[1, cl], s16[1, cl]
        mx = jnp.maximum(jnp.maximum(ma, mb), mc)
        wa = jnp.exp(ma - mx)
        wb = jnp.exp(mb - mx)
        wc = jnp.exp(mc - mx)
        num = wa * s1[0, rows, :] + wb * s4[0, cl] + wc * s16[0, cl]
        den = wa * s1[2, rows, :] + wb * s4[2, cl] + wc * s16[2, cl]
        mixed = (num / den * cm_rows(gm, cl, 0, chunk_tiles).astype(F32)).astype(BF16)
        for k in range(chunk_tiles):
            outm[0, 0, k, cl] = mixed[k * sub:(k + 1) * sub]


def _prompt_b(q1, k1, v1, qm, km, vm, gm, dil_bias):
    nb, npair, l, _ = q1.shape
    nc = l // CHUNK
    li = l // N_CLASSES
    ci = CHUNK // N_CLASSES
    q1_spec = pl.BlockSpec((1, 1, CHUNK, BLOCK), lambda n, h, c: (n, h, c, 0))
    kv1_spec = pl.BlockSpec((1, 1, l, BLOCK), lambda n, h, c: (n, h, 0, 0))
    ti = SEQ_TILE // N_CLASSES
    cm_spec = pl.BlockSpec((1, 1, CHUNK // SEQ_TILE, N_CLASSES, ti, BLOCK), lambda n, h, c: (n, h, c, 0, 0, 0))
    kvm_spec = pl.BlockSpec((1, 1, l // SEQ_TILE, N_CLASSES, ti, BLOCK), lambda n, h, c: (n, h, 0, 0, 0, 0))
    cm_scratch = pltpu.VMEM((3, N_CLASSES, ci, BLOCK), F32)
    return pl.pallas_call(
        _pb_kernel,
        grid=(nb, npair, nc),
        in_specs=[pl.BlockSpec((1, 3, 2, 2 * BLOCK, 2 * BLOCK), lambda n, h, c: (h, 0, 0, 0, 0)),
                  q1_spec, kv1_spec, kv1_spec, cm_spec, kvm_spec, kvm_spec, cm_spec],
        out_specs=cm_spec,
        out_shape=jax.ShapeDtypeStruct((nb, npair, l // SEQ_TILE, N_CLASSES, ti, BLOCK), BF16),
        scratch_shapes=[
            pltpu.VMEM((3, CHUNK, BLOCK), F32), cm_scratch, cm_scratch,
        ],
        compiler_params=_compiler_params(("arbitrary", "arbitrary", "arbitrary")),
        name="prompt_b",
    )(dil_bias, q1, k1, v1, qm, km, vm, gm)


def _mix_out(y, x, gate, w_ref, fg_ref, final):
    return _residual_out(_dot(y, w_ref[...]), x, gate, fg_ref, final)


def _residual_out(mixed, x, gate, fg_ref, final):
    xn = x + gate * mixed
    if final:
        ms = jnp.mean(xn * xn, axis=-1, keepdims=True)
        xn = xn * lax.rsqrt(ms + RMS_EPS) * fg_ref[...]
    return xn


def _pc_kernel(sink_ref, x_ref, mod_ref, yacd_ref, yb_ref, w_ref, fg_ref, bias_ref,
               o_ref, ybs, kext, vext, yds, *, final):
    tl = SEQ_TILE
    t = pl.program_id(1)

    @pl.when(t == 0)
    def _():
        kext[0:BLOCK, :] = jnp.zeros((BLOCK, W_SWA_KV), BF16)
        vext[0:BLOCK, :] = jnp.zeros((BLOCK, W_SWA_KV), BF16)

    for hp in range(N_PAIRS_DIL):
        for c in range(N_CLASSES):
            ybs[hp, pl.ds(c, tl // N_CLASSES, stride=N_CLASSES), :] = yb_ref[0, hp, 0, c].astype(F32)
    k_head = D_MIX - W_SWA
    y_head = jnp.concatenate([yacd_ref[0, :, 0:W_POOL], ybs[0].astype(BF16), ybs[1].astype(BF16),
                              ybs[2].astype(BF16), yacd_ref[0, :, W_POOL:W_POOL + W_CONV]], axis=-1)
    n_tile = D_MODEL // (tl // BLOCK)
    proj = []

    c0 = W_POOL + W_CONV
    kext[BLOCK:BLOCK + tl, :] = yacd_ref[0, :, c0 + W_SWA:c0 + W_SWA + W_SWA_KV]
    vext[BLOCK:BLOCK + tl, :] = yacd_ref[0, :, c0 + W_SWA + W_SWA_KV:c0 + 2 * W_SWA]
    first = jnp.where(t == 0, 1, 0)
    lane_q = lax.broadcasted_iota(jnp.int32, (BLOCK, BLOCK), 1)
    row_s = lax.broadcasted_iota(jnp.int32, (2 * BLOCK, 1), 0)
    for jb in range(tl // BLOCK):
        proj.append(_dot(y_head, w_ref[0:k_head, jb * n_tile:(jb + 1) * n_tile]))
        r0 = jb * BLOCK
        kblk = kext[r0:r0 + 2 * BLOCK, :]
        vblk = vext[r0:r0 + 2 * BLOCK, :]
        gated = []
        for grp in range(2):
            q = yacd_ref[0, r0:r0 + BLOCK, c0 + grp * BLOCK:c0 + (grp + 1) * BLOCK]
            lhs = jnp.concatenate([jnp.where(lane_q < HEAD_DIM, q, jnp.zeros_like(q)),
                                   jnp.where(lane_q >= HEAD_DIM, q, jnp.zeros_like(q))], axis=0)
            s = _dot_nt(lhs, kblk)
            if jb == 0:
                s = s + bias_ref[grp, first]
            else:
                s = s + bias_ref[grp, 0]
            sink = jnp.where(row_s < BLOCK, sink_ref[SWA_HEAD_PERM[2 * grp]],
                             sink_ref[SWA_HEAD_PERM[2 * grp + 1]])
            m = jnp.maximum(jnp.max(s, axis=-1, keepdims=True), sink)
            p = jnp.exp((s - m).astype(BF16))
            acc = _dot(p, jnp.concatenate([vblk, jnp.ones_like(vblk)], axis=-1))
            es = jnp.exp(sink - m)
            lo = lane_q < HEAD_DIM
            den = (jnp.where(lo, acc[0:BLOCK, BLOCK:2 * BLOCK], acc[BLOCK:2 * BLOCK, BLOCK:2 * BLOCK])
                   + jnp.where(lo, es[0:BLOCK], es[BLOCK:2 * BLOCK]))
            od = jnp.where(lo, acc[0:BLOCK, 0:BLOCK], acc[BLOCK:2 * BLOCK, 0:BLOCK]) / den
            g0 = c0 + 2 * W_SWA + grp * BLOCK
            gate_s = yacd_ref[0, r0:r0 + BLOCK, g0:g0 + BLOCK].astype(F32)
            gated.append(od * gate_s)
        h01, h23 = _unpair_heads(*gated)
        yds[r0:r0 + BLOCK, 0:BLOCK] = h01.astype(BF16)
        yds[r0:r0 + BLOCK, BLOCK:2 * BLOCK] = h23.astype(BF16)
    kext[0:BLOCK, :] = kext[tl:tl + BLOCK, :]
    vext[0:BLOCK, :] = vext[tl:tl + BLOCK, :]

    mixed = jnp.concatenate(proj, axis=-1) + _dot(yds[...], w_ref[k_head:D_MIX, :])
    gate = mod_ref[0][:, 2 * D_MODEL:3 * D_MODEL]
    o_ref[0] = _residual_out(mixed, x_ref[0], gate, fg_ref, final)


def _prompt_c(x, mod_p, yacd, yb, w_out_b, final_g, sink, swa_bias, final, layer):
    nb, l, _ = x.shape
    tl = SEQ_TILE
    full = lambda *shape: pl.BlockSpec(shape, lambda n, t: (0,) * len(shape))
    return pl.pallas_call(
        functools.partial(_pc_kernel, final=final),
        grid=(nb, l // tl),
        in_specs=[
            pl.BlockSpec(memory_space=pltpu.SMEM),
            pl.BlockSpec((1, tl, D_MODEL), lambda n, t: (n, t, 0)),
            pl.BlockSpec((1, 1, 3 * D_MODEL), lambda n, t: (n, 0, 0)),
            pl.BlockSpec((1, tl, Y_WIDTH), lambda n, t: (n, t, 0)),
            pl.BlockSpec((1, N_PAIRS_DIL, 1, N_CLASSES, tl // N_CLASSES, BLOCK),
                         lambda n, t: (n, 0, t, 0, 0, 0)),
            pl.BlockSpec((None, D_MIX, D_MODEL), lambda n, t: (layer, 0, 0)),
            full(1, D_MODEL),
            full(2, 2, 2 * BLOCK, 2 * BLOCK),
        ],
        out_specs=pl.BlockSpec((1, tl, D_MODEL), lambda n, t: (n, t, 0)),
        out_shape=jax.ShapeDtypeStruct((nb, l, D_MODEL), F32),
        scratch_shapes=[
            pltpu.VMEM((N_PAIRS_DIL, tl, BLOCK), F32),
            pltpu.VMEM((BLOCK + tl, W_SWA_KV), BF16),
            pltpu.VMEM((BLOCK + tl, W_SWA_KV), BF16),
            pltpu.VMEM((tl, W_SWA), BF16),
        ],
        compiler_params=_compiler_params(("arbitrary", "arbitrary")),
        name="prompt_c",
    )(sink, x, mod_p, yacd, yb, w_out_b, final_g, swa_bias)


def _sa_kernel(x_ref, mod_ref, g_ref, w_ref, wbd_ref, ps_ref, cw_ref, sp_ref, sc_ref,
               qkv_ref, sw_ref, yac_ref, gates_ref, pst_ref, cst_ref):
    hb = _modulated_norm(x_ref[...], g_ref[...], mod_ref[...]).astype(BF16)
    ns = hb.shape[0]

    zp = _dot(hb, w_ref[:, OFF_POOL:OFF_POOL + 2 * W_POOL])
    pu = zp[:, 0:W_POOL]
    pg = zp[:, W_POOL:2 * W_POOL]
    acc = pu
    sums = {}
    for j in range(1, 16):
        acc = acc + sp_ref[POOL_BUF - j]
        if j + 1 in POOL_WINDOWS:
            sums[j + 1] = acc
    lane = lax.broadcasted_iota(jnp.int32, (ns, W_POOL), 1)
    sel, win = _pool_select(sums, lane)
    diff = sel / win.astype(F32) - pu
    a_out = _dot(diff.astype(BF16), wbd_ref[...]) * ps_ref[...]
    yac_ref[:, 0:W_POOL] = a_out * _silu(pg)
    pst_ref[0:POOL_BUF - 1] = sp_ref[1:POOL_BUF]
    pst_ref[POOL_BUF - 1] = pu

    zc4 = _dot(hb, w_ref[:, OFF_CONV:OFF_CONV + 4 * W_CONV])
    ch = zc4[:, 0:W_CONV]
    cb = zc4[:, W_CONV:2 * W_CONV]
    cc = zc4[:, 2 * W_CONV:3 * W_CONV]
    cg = zc4[:, 3 * W_CONV:4 * W_CONV]
    zc = cc * ch
    conv = (cw_ref[0:1, :] * sc_ref[:, 0:W_CONV] + cw_ref[1:2, :] * sc_ref[:, W_CONV:2 * W_CONV]
            + cw_ref[2:3, :] * zc)
    yac_ref[:, W_POOL:W_POOL + W_CONV] = cb * conv * _silu(cg)
    cst_ref[:, 0:W_CONV] = sc_ref[:, W_CONV:2 * W_CONV]
    cst_ref[:, W_CONV:2 * W_CONV] = zc

    zd = _dot(hb, w_ref[:, OFF_DIL:OFF_DIL + 4 * W_DIL])
    qkv_ref[:, 0:W_DIL] = zd[:, 0:W_DIL] * QK_SCALE
    qkv_ref[:, W_DIL:3 * W_DIL] = zd[:, W_DIL:3 * W_DIL]
    gates_ref[:, 0:W_DIL] = _silu(zd[:, 3 * W_DIL:4 * W_DIL])

    zs = _dot(hb, w_ref[:, OFF_SWA:OFF_SWA + 3 * W_SWA])
    sw_ref[:, 0:W_SWA] = zs[:, 0:W_SWA] * QK_SCALE
    sw_ref[:, W_SWA:2 * W_SWA] = zs[:, W_SWA:2 * W_SWA]
    gates_ref[:, W_DIL:W_DIL + W_SWA] = _silu(zs[:, 2 * W_SWA:3 * W_SWA])


def _whole_spec(shape):
    return pl.BlockSpec(shape, lambda *_: (0,) * len(shape))


def _layer_spec(shape, layer):
    return pl.BlockSpec((None,) + tuple(shape[1:]), lambda *_: (layer,) + (0,) * (len(shape) - 1))


def _sample_a(xs, mod_s, norm_g, w_in_b, wbd, pool_scale, conv_w, sp, sc, layer):
    ns = xs.shape[0]
    shapes = [
        jax.ShapeDtypeStruct((ns, 3 * W_DIL), F32),
        jax.ShapeDtypeStruct((ns, 2 * W_SWA), F32),
        jax.ShapeDtypeStruct((ns, W_POOL + W_CONV), F32),
        jax.ShapeDtypeStruct((ns, W_DIL + W_SWA), F32),
        jax.ShapeDtypeStruct((POOL_BUF, ns, W_POOL), F32),
        jax.ShapeDtypeStruct((ns, CONV_BUF * W_CONV), F32),
    ]
    args = (xs, mod_s, norm_g, w_in_b, wbd, pool_scale, conv_w, sp, sc)
    in_specs = [_whole_spec(a.shape) for a in args]
    in_specs[3] = _layer_spec(w_in_b.shape, layer)
    return pl.pallas_call(
        _sa_kernel,
        grid=(1,),
        in_specs=in_specs,
        out_specs=[_whole_spec(s.shape) for s in shapes],
        out_shape=shapes,
        compiler_params=_compiler_params(("arbitrary",)),
        name="sample_a",
    )(*args)


def _cache_attention(dq_ref, sq_ref, dbias_ref, mult_ref, sbias_ref, sink_ref, kt_ref, vt_ref,
                     ck_ref, cv_ref, bo_ref, do_ref):
    row_d = lax.broadcasted_iota(jnp.int32, (8, DIL_MAX), 0)
    row_o = lax.broadcasted_iota(jnp.int32, (8, HEAD_DIM), 0)
    row_s = lax.broadcasted_iota(jnp.int32, (8, SWA_WINDOW), 0)
    eye = (lax.broadcasted_iota(jnp.int32, (HEAD_DIM, HEAD_DIM), 0)
           == lax.broadcasted_iota(jnp.int32, (HEAD_DIM, HEAD_DIM), 1))
    for j in range(DEC_TILE):
        q = dq_ref[j, 0]
        s = jnp.zeros((8, DIL_MAX), F32)
        for h in range(N_HEADS_DIL):
            q_col = jnp.sum(jnp.where(eye, q[h:h + 1, :], 0.0), axis=1, keepdims=True)
            s_h = jnp.sum(kt_ref[0, j, h] * q_col, axis=0, keepdims=True)
            s = jnp.where(row_d == h, s_h, s)
            yield
        s = s + dbias_ref[...]
        s_self = jnp.sum(q * dq_ref[j, 1], axis=-1, keepdims=True)
        m = jnp.maximum(jnp.max(s, axis=-1, keepdims=True), s_self)
        p = jnp.exp(s - m) * mult_ref[...]
        p_self = float(len(DIL_CONFIGS)) * jnp.exp(s_self - m)
        den = jnp.sum(p, axis=-1, keepdims=True) + p_self
        yield
        acc = jnp.zeros((8, HEAD_DIM), F32)
        for h in range(N_HEADS_DIL):
            o_col = jnp.sum(vt_ref[0, j, h] * p[h:h + 1, :], axis=1, keepdims=True)
            o_row = jnp.sum(jnp.where(eye, o_col, 0.0), axis=0, keepdims=True)
            acc = jnp.where(row_o == h, o_row, acc)
            yield
        bo_ref[j] = (acc + p_self * dq_ref[j, 2]) / den

        q = sq_ref[j, 0]
        qb = q.astype(BF16)
        s = jnp.where(row_s < 2, _dot(qb, ck_ref[0, j, 0].astype(BF16)),
                      _dot(qb, ck_ref[0, j, 1].astype(BF16))) + sbias_ref[...]
        w_self = jnp.sum(q * sq_ref[j, 1], axis=-1, keepdims=True)
        sink = sink_ref[...][:, 0:1]
        m = jnp.maximum(jnp.maximum(jnp.max(s, axis=-1, keepdims=True), w_self), sink)
        p = jnp.exp(s - m)
        pw = jnp.exp(w_self - m)
        den = jnp.sum(p, axis=-1, keepdims=True) + pw + jnp.exp(sink - m)
        pb = p.astype(BF16)
        yield
        acc = jnp.where(row_o < 2, _dot_nt(pb, cv_ref[0, j, 0].astype(BF16)),
                        _dot_nt(pb, cv_ref[0, j, 1].astype(BF16)))
        do_ref[j] = (acc + pw * sq_ref[j, 2]) / den
        yield


def _advance(stages, n):
    for _ in range(n):
        next(stages, None)


def _sc_kernel(x_ref, mod_ref, yac_ref, gates_ref, bo_ref, do_ref, w_ref, fg_ref, o_ref, *, final):
    yac = yac_ref[...]
    gates = gates_ref[...]
    y = jnp.concatenate([yac[:, 0:W_POOL], bo_ref[...] * gates[:, 0:W_DIL],
                         yac[:, W_POOL:W_POOL + W_CONV], do_ref[...] * gates[:, W_DIL:W_DIL + W_SWA]],
                        axis=-1).astype(BF16)
    gate = mod_ref[...][:, 2 * D_MODEL:3 * D_MODEL]
    o_ref[...] = _mix_out(y, x_ref[...], gate, w_ref, fg_ref, final)


def _sample_c(xs, mod_s, yac, gates, bo, do, w_out_b, final_g, final, layer):
    args = (xs, mod_s, yac, gates, bo, do, w_out_b, final_g)
    in_specs = [_whole_spec(a.shape) for a in args]
    in_specs[6] = _layer_spec(w_out_b.shape, layer)
    return pl.pallas_call(
        functools.partial(_sc_kernel, final=final),
        grid=(1,),
        in_specs=in_specs,
        out_specs=_whole_spec(xs.shape),
        out_shape=jax.ShapeDtypeStruct(xs.shape, F32),
        compiler_params=_compiler_params(("arbitrary",)),
        name="sample_c",
    )(*args)


def _band_bias_dil4(slope_lo, slope_hi, variant):
    sub = BLOCK // 4
    qidx = np.arange(BLOCK)[:, None]
    kidx = np.arange(2 * BLOCK)[None, :]
    q_step = 4 * (qidx % sub + (sub if variant == 0 else 0)) + qidx // sub
    k_step = 4 * (kidx % (2 * sub)) + kidx // (2 * sub)
    off = q_step - k_step
    valid = (off >= 0) & (off <= BLOCK)
    out = []
    for s in (slope_lo, slope_hi):
        out.append(np.where(valid, -(np.float32(s) * np.float32(4)) * off.astype(np.float32),
                            np.float32(NEG_INF)).astype(np.float32))
    return np.concatenate(out, axis=0)


def _prompt_bias_tables():
    dil = _alibi_slopes(N_HEADS_DIL)

    def table(hp, d, var):
        if d == 4:
            return _band_bias_dil4(dil[2 * hp], dil[2 * hp + 1], var)
        return _band_bias(dil[2 * hp], dil[2 * hp + 1], d, var)

    dil_bias = np.stack([
        np.stack([np.stack([table(hp, d, var) for var in (0, 1)])
                  for _, d in DIL_CONFIGS]) for hp in range(N_PAIRS_DIL)])
    swa = _alibi_slopes(N_HEADS_SWA)
    swa_bias = np.stack([
        np.stack([_band_bias(swa[SWA_HEAD_PERM[2 * g]], swa[SWA_HEAD_PERM[2 * g + 1]], 1, var)
                  for var in (0, 2)]) for g in range(2)])
    return jnp.asarray(dil_bias), jnp.asarray(swa_bias)


def _sample_bias_tables(r):
    dil = _alibi_slopes(N_HEADS_DIL)
    dist = (r - np.arange(r)).astype(np.float32)
    mult = np.zeros((1, r), np.float32)
    for window, d in DIL_CONFIGS:
        mult[0] += ((dist <= window) & (dist % d == 0)).astype(np.float32)
    dbias = np.zeros((8, r), np.float32)
    for h in range(N_HEADS_DIL):
        dbias[h] = np.where(mult[0] > 0, -np.float32(dil[h]) * dist, np.float32(NEG_INF))
    swa = _alibi_slopes(N_HEADS_SWA)
    sdist = (SWA_WINDOW - np.arange(SWA_WINDOW)).astype(np.float32)
    sbias = np.zeros((8, SWA_WINDOW), np.float32)
    for h in range(N_HEADS_SWA):
        sbias[h] = -np.float32(swa[h]) * sdist
    return jnp.asarray(dbias), jnp.asarray(mult), jnp.asarray(sbias)


def kernel(x_prompt, x_sample, c_prompt, c_sample, state_pool, cache_dil_k, cache_dil_v, state_conv, cache_swa_k, cache_swa_v, norm_g, w_ada, b_ada, w_in, w_pool, pool_scale, conv_w, swa_sink, w_out, final_g):
    depth = w_in.shape[0]
    nb, l, _ = x_prompt.shape
    ns = x_sample.shape[0]
    assert x_sample.shape[1] == 1 and l % CHUNK == 0 and l >= DIL_MAX
    assert ns == nb * (l // SEQ_TILE) * DEC_TILE
    assert cache_dil_k.shape[2] == DIL_MAX and cache_swa_k.shape[2] == SWA_WINDOW

    w_in_b = w_in.astype(BF16)
    w_out_b = w_out.astype(BF16)
    eye = jnp.eye(len(POOL_WINDOWS), dtype=F32)
    wbd = jnp.einsum("dgce,gh->dgche", w_pool, eye).reshape(depth, W_POOL, W_POOL).astype(BF16)
    sink_rows = jnp.zeros((depth, 8), F32).at[:, 0:N_HEADS_SWA].set(swa_sink)
    sink_rows = jnp.broadcast_to(sink_rows[:, :, None], (depth, 8, BLOCK))

    dil_bias, swa_bias = _prompt_bias_tables()
    dbias, mult, sbias = _sample_bias_tables(cache_dil_k.shape[2])

    mod = _ada(jnp.concatenate([c_prompt, c_sample], axis=0), w_ada, b_ada)
    fg = final_g.reshape(1, D_MODEL)

    xp = x_prompt
    xs = x_sample.reshape(ns, D_MODEL)
    kt = jnp.transpose(cache_dil_k, (0, 1, 3, 4, 2))
    vt = jnp.transpose(cache_dil_v, (0, 1, 3, 4, 2))
    ckt = jnp.transpose(cache_swa_k, (0, 1, 3, 4, 2))
    cvt = jnp.transpose(cache_swa_v, (0, 1, 3, 4, 2))
    sp_all = jnp.transpose(state_pool, (0, 2, 1, 3))
    sc_all = state_conv.reshape(depth, ns, CONV_BUF * W_CONV)
    pad_heads = lambda a: jnp.pad(a, ((0, 0), (0, 0), (0, 8 - a.shape[2]), (0, 0)))
    outs = [[] for _ in range(12)]
    carried = None
    for i in range(depth):
        final = i == depth - 1
        mod_p = mod[i, 0:nb].reshape(nb, 1, 3 * D_MODEL)
        mod_s = mod[i, nb:nb + ns]
        g = norm_g[i].reshape(1, D_MODEL)
        ps = pool_scale[i].reshape(1, W_POOL)

        qkv, sw, yac, gates, pst_s, cst_s = _sample_a(
            xs, mod_s, g, w_in_b, wbd[i], ps, conv_w[i], sp_all[i], sc_all[i], i)
        dq3 = pad_heads(qkv.reshape(ns, 3, N_HEADS_DIL, HEAD_DIM))
        sq4 = sw[:, 0:W_SWA].reshape(ns, 1, N_HEADS_SWA, HEAD_DIM)
        skv = jnp.repeat(sw[:, W_SWA:W_SWA + 2 * W_SWA_KV].reshape(ns, 2, 2, HEAD_DIM), 2, axis=2)
        sq3 = pad_heads(jnp.concatenate([sq4, skv], axis=1))

        cache_args = (dq3, sq3, dbias, mult, sbias, sink_rows[i], kt, vt, ckt, cvt)
        q1, k1, v1, qm, km, vm, gm, kc, vc, yacd, pst, cst, skc, svc, bo_a, do_a = _prompt_a(
            xp, mod_p, g, w_in_b, wbd[i], ps, conv_w[i], i, depth, 0, cache_args, carried)
        carried = (kc, vc)
        yb = _prompt_b(q1, k1, v1, qm, km, vm, gm, dil_bias)
        xp = _prompt_c(xp, mod_p, yacd, yb, w_out_b, fg, swa_sink[i], swa_bias, final, i)

        bo = bo_a[:, 0:N_HEADS_DIL].reshape(ns, W_DIL)
        do = do_a[:, 0:N_HEADS_SWA].reshape(ns, W_SWA)
        xs = _sample_c(xs, mod_s, yac, gates, bo, do, w_out_b, fg, final, i)

        unfold = lambda a, h: jnp.transpose(a.reshape(nb, h, HEAD_DIM, a.shape[-1]), (0, 3, 1, 2))
        skc, svc = unfold(skc, 2), unfold(svc, 2)
        outs[0].append(pst)
        outs[1].append(jnp.transpose(pst_s, (1, 0, 2)))
        outs[4].append(qkv[:, W_DIL:2 * W_DIL].reshape(ns, 1, N_HEADS_DIL, HEAD_DIM))
        outs[5].append(qkv[:, 2 * W_DIL:3 * W_DIL].reshape(ns, 1, N_HEADS_DIL, HEAD_DIM))
        outs[6].append(cst)
        outs[7].append(cst_s.reshape(ns, CONV_BUF, W_CONV))
        outs[8].append(skc)
        outs[9].append(svc)
        outs[10].append(sw[:, W_SWA:W_SWA + W_SWA_KV].reshape(ns, 1, 2, HEAD_DIM))
        outs[11].append(sw[:, W_SWA + W_SWA_KV:W_SWA + 2 * W_SWA_KV].reshape(ns, 1, 2, HEAD_DIM))

    for k, cache in zip((2, 3), carried):
        outs[k] = jnp.transpose(cache.reshape(depth, nb, N_HEADS_DIL, HEAD_DIM, DIL_MAX), (0, 1, 4, 2, 3))
    return (xp, xs.reshape(ns, 1, D_MODEL)) + tuple(
        o if not isinstance(o, list) else jnp.stack(o) for o in outs)
```
